```python
import jax
import jax.numpy as jnp
from jax import lax
import numpy as np

D_MODEL = 1024
BATCH = 4
SEQ = 4096
DEPTH = 2

HEAD_DIM = 64
BLOCK = 128
SWA_HEADS = D_MODEL // (2 * HEAD_DIM)
SWA_KV_HEADS = SWA_HEADS // 4
SWA_GROUP = SWA_HEADS // SWA_KV_HEADS
WINDOW = 128
RWKV_HEADS = D_MODEL // (2 * HEAD_DIM)
RWKV_DIM = RWKV_HEADS * HEAD_DIM
DECAY_LORA = 64
ICLR_LORA = 64
GATE_LORA = 128
FOX_HEADS = D_MODEL // HEAD_DIM
FOX_DIM = FOX_HEADS * HEAD_DIM
D_FF = 2816
PLE_DIM = 256
N_EVEN = (DEPTH + 1) // 2
N_ODD = DEPTH // 2
NORM_EPS = 1e-6
GN_EPS = 64e-5
L2_EPS = 1e-12

SWA_Q = SWA_HEADS * HEAD_DIM
SWA_KV = SWA_KV_HEADS * HEAD_DIM
SWA_COLS = SWA_Q + 2 * SWA_KV
RWKV_COLS = 3 * RWKV_DIM + DECAY_LORA + ICLR_LORA + GATE_LORA
RWKV_SPLITS = [RWKV_DIM, 2 * RWKV_DIM, 3 * RWKV_DIM,
               3 * RWKV_DIM + DECAY_LORA, 3 * RWKV_DIM + DECAY_LORA + ICLR_LORA]
EVEN_IN = SWA_COLS + RWKV_COLS
EVEN_OUT = SWA_Q + RWKV_DIM
FOX_IN = 3 * FOX_DIM + FOX_HEADS

kernel_name = "hybrid_swa_rwkv7_fox_macaron"


def rms_norm(x, g):
    xf = x.astype(jnp.float32)
    y = xf * lax.rsqrt(jnp.mean(xf * xf, axis=-1, keepdims=True) + NORM_EPS)
    return (y * g.astype(jnp.float32)).astype(x.dtype)


def swiglu(h, w_gu, w_down):
    g, u = jnp.split(h @ w_gu, 2, axis=-1)
    return (jax.nn.silu(g) * u) @ w_down


def alibi_slopes(n):
    return 2.0 ** (-8.0 * jnp.arange(1, n + 1, dtype=jnp.float32) / n)


def sliding_window_attention(q, k, v, sinks):
    b, s = q.shape[:2]
    nb = s // BLOCK
    scale = HEAD_DIM ** -0.5
    qb = q.reshape(b, nb, BLOCK, SWA_KV_HEADS, SWA_GROUP, HEAD_DIM)
    pad = ((0, 0), (BLOCK, 0), (0, 0), (0, 0))
    kp = jnp.pad(k, pad).reshape(b, nb + 1, BLOCK, SWA_KV_HEADS, HEAD_DIM)
    vp = jnp.pad(v, pad).reshape(b, nb + 1, BLOCK, SWA_KV_HEADS, HEAD_DIM)
    kb = jnp.concatenate([kp[:, :-1], kp[:, 1:]], axis=2)
    vb = jnp.concatenate([vp[:, :-1], vp[:, 1:]], axis=2)
    logits = jnp.einsum('bnqhgd,bnkhd->bnhgqk', qb, kb).astype(jnp.float32) * scale
    qi = jnp.arange(BLOCK)[:, None]
    ki = jnp.arange(2 * BLOCK)[None, :]
    dist = qi + BLOCK - ki
    blk = jnp.arange(nb)[:, None, None]
    valid = (dist >= 0) & (dist < WINDOW) & (blk * BLOCK - BLOCK + ki >= 0)
    slopes = alibi_slopes(SWA_HEADS).reshape(SWA_KV_HEADS, SWA_GROUP)
    logits = logits - slopes[:, :, None, None] * dist.astype(jnp.float32)
    logits = jnp.where(valid[None, :, None, None], logits, -jnp.inf)
    sink = sinks.astype(jnp.float32).reshape(SWA_KV_HEADS, SWA_GROUP)[:, :, None]
    m = jnp.maximum(logits.max(axis=-1), sink)
    pr = jnp.exp(logits - m[..., None])
    denom = pr.sum(axis=-1) + jnp.exp(sink - m)
    pr = pr / denom[..., None]
    out = jnp.einsum('bnhgqk,bnkhd->bnqhgd', pr.astype(v.dtype), vb)
    return out.reshape(b, s, SWA_Q)


def rwkv7_time_mix(h, mu, w0, w2, a0, a2, g2, k_k, k_a, r_k, ln_w, ln_b):
    b, s, _ = h.shape
    f32 = jnp.float32
    shifted = jnp.pad(h[:, :-1], ((0, 0), (1, 0), (0, 0)))
    h = h + (shifted - h) * mu
    r, k, v, xw, xa, xg = jnp.split(h, RWKV_SPLITS, axis=-1)
    wlog = -jax.nn.softplus(-(w0 + jnp.tanh(xw) @ w2)) - 0.5
    a = jax.nn.sigmoid(a0 + xa @ a2)
    g = jax.nn.sigmoid(xg) @ g2
    hd = lambda t: t.astype(f32).reshape(b, s, RWKV_HEADS, HEAD_DIM)
    r, k, v, wlog, a = hd(r), hd(k), hd(v), hd(wlog), hd(a)
    kk = k * k_k.astype(f32).reshape(RWKV_HEADS, HEAD_DIM)
    kk = kk / jnp.maximum(jnp.sqrt(jnp.sum(kk * kk, axis=-1, keepdims=True)), L2_EPS)
    k = k * (1.0 + (a - 1.0) * k_a.astype(f32).reshape(RWKV_HEADS, HEAD_DIM))
    decay = jnp.exp(-jnp.exp(wlog))

    def step(state, inp):
        r_t, w_t, k_t, v_t, kk_t, a_t = inp
        s_kk = jnp.einsum('bhij,bhj->bhi', state, kk_t)
        state = (state * w_t[:, :, None, :]
                 - s_kk[..., None] * (kk_t * a_t)[:, :, None, :]
                 + v_t[..., None] * k_t[:, :, None, :])
        return state, jnp.einsum('bhij,bhj->bhi', state, r_t)

    xs = tuple(jnp.moveaxis(t, 1, 0) for t in (r, decay, k, v, kk, a))
    state0 = jnp.zeros((b, RWKV_HEADS, HEAD_DIM, HEAD_DIM), f32)
    _, y = lax.scan(step, state0, xs)
    y = jnp.moveaxis(y, 0, 1)
    mean = jnp.mean(y, axis=-1, keepdims=True)
    var = jnp.mean(jnp.square(y - mean), axis=-1, keepdims=True)
    y = ((y - mean) * lax.rsqrt(var + GN_EPS) * ln_w.astype(f32).reshape(RWKV_HEADS, HEAD_DIM)
         + ln_b.astype(f32).reshape(RWKV_HEADS, HEAD_DIM))
    y = y + jnp.sum(r * k * r_k.astype(f32), axis=-1, keepdims=True) * v
    return (y.reshape(b, s, RWKV_DIM) * g.astype(f32)).astype(h.dtype)


def forgetting_attention(q, k, v, log_f):
    b, s = q.shape[:2]
    nb = s // BLOCK
    scale = HEAD_DIM ** -0.5
    c = jnp.moveaxis(jnp.cumsum(log_f, axis=1), 1, 2)
    key_pos = jnp.arange(s)

    def one_block(n):
        start = n * BLOCK
        qb = lax.dynamic_slice_in_dim(q, start, BLOCK, axis=1)
        cq = lax.dynamic_slice_in_dim(c, start, BLOCK, axis=2)
        logits = (jnp.einsum('bqhd,bkhd->bhqk', qb, k).astype(jnp.float32) * scale
                  + cq[..., None] - c[:, :, None, :])
        qpos = start + jnp.arange(BLOCK)
        logits = jnp.where(key_pos[None, :] <= qpos[:, None], logits, -jnp.inf)
        pr = jax.nn.softmax(logits, axis=-1)
        return jnp.einsum('bhqk,bkhd->bqhd', pr.astype(v.dtype), v)

    out = lax.map(one_block, jnp.arange(nb))
    return jnp.moveaxis(out, 0, 1).reshape(b, s, FOX_DIM)


def setup_inputs(seed: int = 0) -> dict:
    key = jax.random.key(seed)
    k = jax.random.split(key, 30)
    f32 = jnp.float32
    nrm = lambda kk, shape, scale: jax.random.normal(kk, shape, f32) * scale
    gain = lambda kk, shape: 1.0 + 0.05 * jax.random.normal(kk, shape, f32)
    unif = lambda kk, shape, lo, hi: jax.random.uniform(kk, shape, f32, lo, hi)
    return {
        'x': nrm(k[0], (BATCH, SEQ, D_MODEL), 1.0),
        'p': nrm(k[1], (DEPTH, BATCH, SEQ, PLE_DIM), 1.0),
        'ffn1_norm': gain(k[2], (DEPTH, D_MODEL)),
        'ffn1_w_gu': nrm(k[3], (DEPTH, D_MODEL, 2 * D_FF), D_MODEL ** -0.5),
        'ffn1_w_down': nrm(k[4], (DEPTH, D_FF, D_MODEL), D_FF ** -0.5),
        'mix_norm': gain(k[5], (DEPTH, D_MODEL)),
        'ffn2_norm': gain(k[6], (DEPTH, D_MODEL)),
        'ffn2_w_gu': nrm(k[7], (DEPTH, D_MODEL, 2 * D_FF), D_MODEL ** -0.5),
        'ffn2_w_down': nrm(k[8], (DEPTH, D_FF, D_MODEL), D_FF ** -0.5),
        'ple_norm': gain(k[9], (DEPTH, D_MODEL)),
        'ple_w_gate': nrm(k[10], (DEPTH, D_MODEL, D_MODEL), D_MODEL ** -0.5),
        'ple_w_proj': nrm(k[11], (DEPTH, PLE_DIM, D_MODEL), 0.5 * PLE_DIM ** -0.5),
        'even_w_in': nrm(k[12], (N_EVEN, D_MODEL, EVEN_IN), D_MODEL ** -0.5),
        'even_w_out': nrm(k[13], (N_EVEN, EVEN_OUT, D_MODEL), EVEN_OUT ** -0.5),
        'swa_sinks': nrm(k[14], (N_EVEN, SWA_HEADS), 0.5),
        'rwkv_mu': unif(k[15], (N_EVEN, RWKV_COLS), 0.0, 1.0),
        'rwkv_w0': unif(k[16], (N_EVEN, RWKV_DIM), -5.0, -0.5),
        'rwkv_w2': nrm(k[17], (N_EVEN, DECAY_LORA, RWKV_DIM), 0.5 * DECAY_LORA ** -0.5),
        'rwkv_a0': nrm(k[18], (N_EVEN, RWKV_DIM), 0.1),
        'rwkv_a2': nrm(k[19], (N_EVEN, ICLR_LORA, RWKV_DIM), 0.5 * ICLR_LORA ** -0.5),
        'rwkv_g2': nrm(k[20], (N_EVEN, GATE_LORA, RWKV_DIM), GATE_LORA ** -0.5),
        'rwkv_k_k': 0.85 + nrm(k[21], (N_EVEN, RWKV_DIM), 0.05),
        'rwkv_k_a': 1.0 + nrm(k[22], (N_EVEN, RWKV_DIM), 0.05),
        'rwkv_r_k': nrm(k[23], (N_EVEN, RWKV_HEADS, HEAD_DIM), 0.1),
        'rwkv_ln_w': gain(k[24], (N_EVEN, RWKV_DIM)),
        'rwkv_ln_b': nrm(k[25], (N_EVEN, RWKV_DIM), 0.01),
        'fox_w_in': nrm(k[26], (N_ODD, D_MODEL, FOX_IN), D_MODEL ** -0.5),
        'fox_b_f': unif(k[27], (N_ODD, FOX_HEADS), 1.0, 6.0),
        'fox_w_out': nrm(k[28], (N_ODD, FOX_DIM, D_MODEL), FOX_DIM ** -0.5),
        'final_norm': gain(k[29], (D_MODEL,)),
    }


def reference(x, p, ffn1_norm, ffn1_w_gu, ffn1_w_down, mix_norm, ffn2_norm, ffn2_w_gu,
              ffn2_w_down, ple_norm, ple_w_gate, ple_w_proj, even_w_in, even_w_out,
              swa_sinks, rwkv_mu, rwkv_w0, rwkv_w2, rwkv_a0, rwkv_a2, rwkv_g2, rwkv_k_k,
              rwkv_k_a, rwkv_r_k, rwkv_ln_w, rwkv_ln_b, fox_w_in, fox_b_f, fox_w_out,
              final_norm):
    b, s, _ = x.shape
    for i in range(DEPTH):
        j = i // 2
        x = x + 0.5 * swiglu(rms_norm(x, ffn1_norm[i]), ffn1_w_gu[i], ffn1_w_down[i])
        hn = rms_norm(x, mix_norm[i])
        if i % 2 == 0:
            proj = hn @ even_w_in[j]
            qa, ka, va, hb = jnp.split(proj, [SWA_Q, SWA_Q + SWA_KV, SWA_COLS], axis=-1)
            ya = sliding_window_attention(
                qa.reshape(b, s, SWA_HEADS, HEAD_DIM),
                ka.reshape(b, s, SWA_KV_HEADS, HEAD_DIM),
                va.reshape(b, s, SWA_KV_HEADS, HEAD_DIM),
                swa_sinks[j])
            yb = rwkv7_time_mix(hb, rwkv_mu[j], rwkv_w0[j], rwkv_w2[j], rwkv_a0[j], rwkv_a2[j],
                                rwkv_g2[j], rwkv_k_k[j], rwkv_k_a[j], rwkv_r_k[j],
                                rwkv_ln_w[j], rwkv_ln_b[j])
            mixed = jnp.concatenate([ya, yb], axis=-1) @ even_w_out[j]
        else:
            proj = hn @ fox_w_in[j]
            qc, kc, vc, fz = jnp.split(proj, [FOX_DIM, 2 * FOX_DIM, 3 * FOX_DIM], axis=-1)
            log_f = jax.nn.log_sigmoid(fz.astype(jnp.float32) + fox_b_f[j].astype(jnp.float32))
            yc = forgetting_attention(
                qc.reshape(b, s, FOX_HEADS, HEAD_DIM),
                kc.reshape(b, s, FOX_HEADS, HEAD_DIM),
                vc.reshape(b, s, FOX_HEADS, HEAD_DIM),
                log_f)
            mixed = yc @ fox_w_out[j]
        x = x + mixed
        x = x + 0.5 * swiglu(rms_norm(x, ffn2_norm[i]), ffn2_w_gu[i], ffn2_w_down[i])
        gate = jax.nn.sigmoid(rms_norm(x, ple_norm[i]) @ ple_w_gate[i])
        x = x + gate * (p[i] @ ple_w_proj[i])
    return rms_norm(x, final_norm)
```

```python
import functools

import jax
import jax.numpy as jnp
import numpy as np
from jax import lax
from jax.experimental import pallas as pl
from jax.experimental.pallas import tpu as pltpu

F32 = jnp.float32
BF16 = jnp.bfloat16

LANES = 128
HEAD_DIM = 64
SWA_HEADS = 8
SWA_GROUP = 4
SWA_BLOCK = 128
RWKV_CHUNK = 64
NORM_EPS = 1e-6
GN_EPS = 64e-5
L2_EPS = 1e-12
VMEM_LIMIT = 56 * 1024 * 1024

NN = (((1,), (0,)), ((), ()))
NT = (((1,), (1,)), ((), ()))
TN = (((0,), (0,)), ((), ()))


def _dot(a, b, dims=NN):
    return lax.dot_general(a, b, dims, preferred_element_type=F32)


def _dot_bf(a, b, dims=NN):
    return _dot(a.astype(BF16), b.astype(BF16), dims)


def _hi_lo(x):
    hi = x.astype(BF16)
    lo = (x - hi.astype(F32)).astype(BF16)
    return hi, lo


def _dot_x3(a, b, dims=NN):
    ah, al = _hi_lo(a)
    bh, bl = _hi_lo(b)
    return _dot(ah, bh, dims) + (_dot(ah, bl, dims) + _dot(al, bh, dims))


def _split3(x):
    hi = x.astype(BF16)
    r1 = x - hi.astype(F32)
    mid = r1.astype(BF16)
    lo = (r1 - mid.astype(F32)).astype(BF16)
    return hi, mid, lo


def _sel_dot(sel, x):
    hi, mid, lo = _split3(x)
    return _dot(sel, hi) + (_dot(sel, mid) + _dot(sel, lo))


def _dot_sel(x, sel):
    hi, mid, lo = _split3(x)
    return _dot(hi, sel) + (_dot(mid, sel) + _dot(lo, sel))


def _rms(x, g):
    ms = jnp.mean(x * x, axis=-1, keepdims=True)
    return x * lax.rsqrt(ms + NORM_EPS) * g


def _params(*sem):
    return pltpu.CompilerParams(dimension_semantics=sem, vmem_limit_bytes=VMEM_LIMIT)


def _ffn_kernel(x_ref, g_ref, wg_ref, wu_ref, wd_ref, o_ref, hn_ref, acc_ref):
    f = pl.program_id(1)

    @pl.when(f == 0)
    def _():
        hn_ref[...] = _rms(x_ref[...], g_ref[...]).astype(BF16)
        acc_ref[...] = jnp.zeros_like(acc_ref)

    hn = hn_ref[...]
    gate = _dot(hn, wg_ref[...])
    up = _dot(hn, wu_ref[...])
    act = (gate * jax.nn.sigmoid(gate) * up).astype(BF16)
    acc_ref[...] += _dot(act, wd_ref[...])

    @pl.when(f == pl.num_programs(1) - 1)
    def _():
        o_ref[...] = x_ref[...] + 0.5 * acc_ref[...]


def _ffn(x, g, w_gu, w_down, *, tm=512, tf=1408):
    t, d = x.shape
    dff = w_down.shape[0]
    nf = dff // tf
    return pl.pallas_call(
        _ffn_kernel,
        out_shape=jax.ShapeDtypeStruct((t, d), F32),
        grid=(t // tm, nf),
        in_specs=[
            pl.BlockSpec((tm, d), lambda i, f: (i, 0)),
            pl.BlockSpec((1, d), lambda i, f: (0, 0)),
            pl.BlockSpec((d, tf), lambda i, f: (0, f)),
            pl.BlockSpec((d, tf), lambda i, f: (0, nf + f)),
            pl.BlockSpec((tf, d), lambda i, f: (f, 0)),
        ],
        out_specs=pl.BlockSpec((tm, d), lambda i, f: (i, 0)),
        scratch_shapes=[pltpu.VMEM((tm, d), BF16), pltpu.VMEM((tm, d), F32)],
        compiler_params=_params("parallel", "arbitrary"),
        name="ffn",
    )(x, g.reshape(1, d), w_gu, w_gu, w_down)


def _rms_matmul_kernel(x_ref, g_ref, w_ref, o_ref, hn_ref):
    @pl.when(pl.program_id(1) == 0)
    def _():
        hn_ref[...] = _rms(x_ref[...], g_ref[...]).astype(BF16)

    o_ref[...] = _dot(hn_ref[...], w_ref[...]).astype(o_ref.dtype)


def _rms_matmul(x, g, w, *, tm=512, tn=None, out_dtype=F32):
    t, d = x.shape
    n = w.shape[1]
    tn = n if tn is None else tn
    return pl.pallas_call(
        _rms_matmul_kernel,
        out_shape=jax.ShapeDtypeStruct((t, n), out_dtype),
        grid=(t // tm, n // tn),
        in_specs=[
            pl.BlockSpec((tm, d), lambda i, j: (i, 0)),
            pl.BlockSpec((1, d), lambda i, j: (0, 0)),
            pl.BlockSpec((d, tn), lambda i, j: (0, j)),
        ],
        out_specs=pl.BlockSpec((tm, tn), lambda i, j: (i, j)),
        scratch_shapes=[pltpu.VMEM((tm, d), BF16)],
        compiler_params=_params("parallel", "arbitrary"),
        name="rms_matmul",
    )(x, g.reshape(1, d), w)


def _proj_res_kernel(n_in, *refs):
    a_refs = refs[:n_in]
    w_refs = refs[n_in:2 * n_in]
    x_ref, o_ref = refs[2 * n_in], refs[2 * n_in + 1]
    acc = x_ref[...]
    for a_ref, w_ref in zip(a_refs, w_refs):
        acc = acc + _dot(a_ref[...].astype(BF16), w_ref[...])
    o_ref[...] = acc


def _proj_res(a_list, w_list, x, *, tm=512):
    t, d = x.shape
    n_in = len(a_list)
    in_specs = [pl.BlockSpec((tm, a.shape[1]), lambda i: (i, 0)) for a in a_list]
    in_specs += [pl.BlockSpec(w.shape, lambda i: (0, 0)) for w in w_list]
    in_specs += [pl.BlockSpec((tm, d), lambda i: (i, 0))]
    return pl.pallas_call(
        functools.partial(_proj_res_kernel, n_in),
        out_shape=jax.ShapeDtypeStruct((t, d), F32),
        grid=(t // tm,),
        in_specs=in_specs,
        out_specs=pl.BlockSpec((tm, d), lambda i: (i, 0)),
        compiler_params=_params("parallel"),
        name="proj_res",
    )(*a_list, *w_list, x)


def _ple_kernel(final, x_ref, g_ref, wg_ref, p_ref, wp_ref, fn_ref, o_ref):
    x = x_ref[...]
    hn = _rms(x, g_ref[...]).astype(BF16)
    gate = jax.nn.sigmoid(_dot(hn, wg_ref[...]))
    y = x + gate * _dot(p_ref[...].astype(BF16), wp_ref[...])
    if final:
        y = _rms(y, fn_ref[...])
    o_ref[...] = y


def _ple(x, g, w_gate, p, w_proj, final_g, *, final, tm=512):
    t, d = x.shape
    pd = p.shape[1]
    return pl.pallas_call(
        functools.partial(_ple_kernel, final),
        out_shape=jax.ShapeDtypeStruct((t, d), F32),
        grid=(t // tm,),
        in_specs=[
            pl.BlockSpec((tm, d), lambda i: (i, 0)),
            pl.BlockSpec((1, d), lambda i: (0, 0)),
            pl.BlockSpec((d, d), lambda i: (0, 0)),
            pl.BlockSpec((tm, pd), lambda i: (i, 0)),
            pl.BlockSpec((pd, d), lambda i: (0, 0)),
            pl.BlockSpec((1, d), lambda i: (0, 0)),
        ],
        out_specs=pl.BlockSpec((tm, d), lambda i: (i, 0)),
        compiler_params=_params("parallel"),
        name="ple",
    )(x, g.reshape(1, d), w_gate, p, w_proj, final_g.reshape(1, d))


def _swa_kernel(sink_ref, q_ref, kp_ref, kc_ref, vp_ref, vc_ref, o_ref):
    n = pl.program_id(1)
    blk = SWA_BLOCK
    scale = HEAD_DIM ** -0.5
    k = jnp.concatenate([kp_ref[...], kc_ref[...]], axis=0)
    v = jnp.concatenate([vp_ref[...], vc_ref[...]], axis=0)
    kr = pltpu.roll(k, HEAD_DIM, 1)
    vr = pltpu.roll(v, HEAD_DIM, 1)
    lo_kv = lax.broadcasted_iota(jnp.int32, k.shape, 1) < HEAD_DIM
    kdup = [jnp.where(lo_kv, k, kr).astype(BF16), jnp.where(lo_kv, kr, k).astype(BF16)]
    vdup = [jnp.where(lo_kv, v, vr).astype(BF16), jnp.where(lo_kv, vr, v).astype(BF16)]

    qi = lax.broadcasted_iota(jnp.int32, (blk, 2 * blk), 0)
    ki = lax.broadcasted_iota(jnp.int32, (blk, 2 * blk), 1)
    dist = qi + blk - ki
    valid = (dist >= 0) & (dist < blk) & ((n > 0) | (ki >= blk))
    distf = dist.astype(F32)
    lo_q = lax.broadcasted_iota(jnp.int32, (blk, LANES), 1) < HEAD_DIM

    for j in range(SWA_HEADS // 2):
        g = (2 * j) // SWA_GROUP
        q2 = q_ref[:, j * LANES:(j + 1) * LANES] * scale
        outs = []
        for e in range(2):
            h = 2 * j + e
            qm = jnp.where(lo_q if e == 0 else ~lo_q, q2, 0.0).astype(BF16)
            s = _dot(qm, kdup[g], NT)
            slope = 2.0 ** (-8.0 * (h + 1) / SWA_HEADS)
            s = jnp.where(valid, s - slope * distf, -jnp.inf)
            sink = sink_ref[h]
            m = jnp.maximum(jnp.max(s, axis=-1, keepdims=True), sink)
            p = jnp.exp(s - m)
            denom = jnp.sum(p, axis=-1, keepdims=True) + jnp.exp(sink - m)
            p = p / denom
            outs.append(_dot(p.astype(BF16), vdup[g]))
        o_ref[:, j * LANES:(j + 1) * LANES] = jnp.where(lo_q, outs[0], outs[1])


def _swa(qkv, sinks):
    b, s, _ = qkv.shape
    blk = SWA_BLOCK
    nq = SWA_HEADS * HEAD_DIM
    kcol = nq // LANES
    vcol = kcol + 1
    prev = lambda bi, n: jnp.maximum(n - 1, 0)
    return pl.pallas_call(
        _swa_kernel,
        out_shape=jax.ShapeDtypeStruct((b, s, nq), F32),
        grid=(b, s // blk),
        in_specs=[
            pl.BlockSpec(memory_space=pltpu.SMEM),
            pl.BlockSpec((None, blk, nq), lambda bi, n: (bi, n, 0)),
            pl.BlockSpec((None, blk, LANES), lambda bi, n: (bi, prev(bi, n), kcol)),
            pl.BlockSpec((None, blk, LANES), lambda bi, n: (bi, n, kcol)),
            pl.BlockSpec((None, blk, LANES), lambda bi, n: (bi, prev(bi, n), vcol)),
            pl.BlockSpec((None, blk, LANES), lambda bi, n: (bi, n, vcol)),
        ],
        out_specs=pl.BlockSpec((None, blk, nq), lambda bi, n: (bi, n, 0)),
        compiler_params=_params("parallel", "arbitrary"),
        name="swa",
    )(sinks, qkv, qkv, qkv, qkv, qkv)


def _tri_inverse(low, ri, ci):
    eye = (ri == ci).astype(F32)
    base = 8
    same = lambda w: (ri ^ ci) < w
    xd = jnp.where(same(base), -low, 0.0)
    inv = eye + xd
    p2 = _dot_x3(xd, xd)
    inv = inv + _dot_x3(inv, p2)
    p4 = _dot_x3(p2, p2)
    inv = inv + _dot_x3(inv, p4)
    w = base * 2
    while w <= RWKV_CHUNK:
        off = jnp.where(same(w) & ~same(w // 2), low, 0.0)
        inv = inv - _dot_x3(_dot_x3(inv, off), inv)
        w *= 2
    return inv


def _rwkv_kernel(h_ref, mu_ref, w0_ref, w2_ref, a0_ref, a2_ref, g2_ref, kk_ref, ka_ref,
                 rk_ref, lnw_ref, lnb_ref, o_ref, state_ref, last_ref):
    c = RWKV_CHUNK
    dim = o_ref.shape[-1]
    npair = dim // LANES

    @pl.when(pl.program_id(1) == 0)
    def _():
        state_ref[...] = jnp.zeros_like(state_ref)
        last_ref[...] = jnp.zeros_like(last_ref)

    h = h_ref[...]
    row = lax.broadcasted_iota(jnp.int32, h.shape, 0)
    shifted = jnp.where(row == 0, last_ref[...], pltpu.roll(h, 1, 0))
    last_ref[...] = h[c - 1:c, :]
    hs = h + (shifted - h) * mu_ref[...]
    r = hs[:, 0:dim]
    k = hs[:, dim:2 * dim]
    v = hs[:, 2 * dim:3 * dim]
    xwa = hs[:, 3 * dim:3 * dim + LANES]
    xg = hs[:, 3 * dim + LANES:3 * dim + 2 * LANES]

    wl = w0_ref[...] + _dot_x3(jnp.tanh(xwa), w2_ref[...])
    logw = -jax.nn.sigmoid(wl) * float(np.exp(-0.5))
    a = jax.nn.sigmoid(a0_ref[...] + _dot_x3(xwa, a2_ref[...]))
    gate = _dot_bf(jax.nn.sigmoid(xg), g2_ref[...])

    ri = lax.broadcasted_iota(jnp.int32, (2 * c, 2 * c), 0)
    ci = lax.broadcasted_iota(jnp.int32, (2 * c, 2 * c), 1)
    ones_bd = ((ri ^ ci) < HEAD_DIM).astype(BF16)

    def head_sum(x):
        return jnp.concatenate(
            [_dot_sel(x[:, j * LANES:(j + 1) * LANES], ones_bd) for j in range(npair)], axis=1)

    kk = k * kk_ref[...]
    kk = kk / jnp.maximum(jnp.sqrt(head_sum(kk * kk)), L2_EPS)
    k2 = k * (1.0 + (a - 1.0) * ka_ref[...])
    bvec = kk * a

    ti = lax.broadcasted_iota(jnp.int32, (c, c), 0)
    si = lax.broadcasted_iota(jnp.int32, (c, c), 1)
    cum = _sel_dot((si <= ti).astype(BF16), logw)
    cum_end = cum[c - 1:c, :]
    e_neg = jnp.exp(-cum)
    alpha = kk * jnp.exp(cum - logw)
    beta = bvec * e_neg
    kappa = k2 * e_neg
    rho = r * jnp.exp(cum)
    to_end = jnp.exp(cum_end - cum)
    beta_e = bvec * to_end
    kappa_e = k2 * to_end
    w_end = jnp.exp(cum_end)

    same_head = (ri ^ ci) < c
    strict = same_head & (ci < ri)
    incl = same_head & (ci <= ri)
    lo = lax.broadcasted_iota(jnp.int32, (c, LANES), 1) < HEAD_DIM

    def stack(x2):
        return jnp.concatenate([jnp.where(lo, x2, 0.0), jnp.where(lo, 0.0, x2)], axis=0)

    ys = []
    for j in range(npair):
        sl = slice(j * LANES, (j + 1) * LANES)
        a_s, rho_s, v_s = stack(alpha[:, sl]), stack(rho[:, sl]), stack(v[:, sl])
        be_s, ke_s = stack(beta_e[:, sl]), stack(kappa_e[:, sl])
        b_cat = jnp.concatenate([beta[:, sl], beta[:, sl]], axis=0)
        k_cat = jnp.concatenate([kappa[:, sl], kappa[:, sl]], axis=0)
        l_ab = jnp.where(strict, _dot_x3(a_s, b_cat, NT), 0.0)
        l_ak = jnp.where(strict, _dot_x3(a_s, k_cat, NT), 0.0)
        r_b = jnp.where(incl, _dot_x3(rho_s, b_cat, NT), 0.0)
        r_k = jnp.where(incl, _dot_x3(rho_s, k_cat, NT), 0.0)
        inv = _tri_inverse(l_ab, ri, ci)

        s0 = state_ref[j]
        u = -_dot_x3(inv, _dot_x3(a_s, s0, NT) + _dot_x3(l_ak, v_s))
        y = _dot_x3(rho_s, s0, NT) + _dot_x3(r_b, u) + _dot_x3(r_k, v_s)
        ys.append(y[:c] + y[c:])
        state_ref[j] = (s0 * w_end[:, sl]
                        + _dot_x3(jnp.concatenate([u, v_s], axis=0),
                                  jnp.concatenate([be_s, ke_s], axis=0), TN))
    y = jnp.concatenate(ys, axis=1)

    mean = head_sum(y) * (1.0 / HEAD_DIM)
    d = y - mean
    var = head_sum(d * d) * (1.0 / HEAD_DIM)
    y = d * lax.rsqrt(var + GN_EPS) * lnw_ref[...] + lnb_ref[...]
    y = y + head_sum(r * k2 * rk_ref[...]) * v
    o_ref[...] = y * gate


def _rwkv(hb, mu, w0, w2p, a0, a2p, g2, k_k, k_a, r_k, ln_w, ln_b):
    b, s, cols = hb.shape
    dim = w0.shape[-1]
    c = RWKV_CHUNK
    row = lambda x: x.reshape(1, -1)
    vec = lambda n: pl.BlockSpec((1, n), lambda bi, t: (0, 0))
    mat = lambda m: pl.BlockSpec(m.shape, lambda bi, t: (0, 0))
    return pl.pallas_call(
        _rwkv_kernel,
        out_shape=jax.ShapeDtypeStruct((b, s, dim), F32),
        grid=(b, s // c),
        in_specs=[
            pl.BlockSpec((None, c, cols), lambda bi, t: (bi, t, 0)),
            vec(cols), vec(dim), mat(w2p), vec(dim), mat(a2p), mat(g2),
            vec(dim), vec(dim), vec(dim), vec(dim), vec(dim),
        ],
        out_specs=pl.BlockSpec((None, c, dim), lambda bi, t: (bi, t, 0)),
        scratch_shapes=[pltpu.VMEM((dim // LANES, LANES, LANES), F32), pltpu.VMEM((1, cols), F32)],
        compiler_params=_params("parallel", "arbitrary"),
        name="rwkv7",
    )(hb, row(mu), row(w0), w2p, row(a0), a2p, g2, row(k_k), row(k_a), row(r_k), row(ln_w), row(ln_b))


FOX_AUG = 3


def _fox_prep_kernel(q_ref, k_ref, v_ref, fz_ref, bf_ref, sel_ref, qa_ref, ka_ref, vo_ref, carry_ref):
    tc = q_ref.shape[0]
    nheads = qa_ref.shape[-1] // LANES
    scale = HEAD_DIM ** -0.5

    @pl.when(pl.program_id(1) == 0)
    def _():
        carry_ref[...] = jnp.zeros_like(carry_ref)

    z = fz_ref[...] + bf_ref[...]
    logf = jnp.minimum(z, 0.0) - jnp.log(1.0 + jnp.exp(-jnp.abs(z)))
    lane = lax.broadcasted_iota(jnp.int32, z.shape, 1)
    logf = jnp.where(lane < nheads, logf, 0.0)
    ti = lax.broadcasted_iota(jnp.int32, (tc, tc), 0)
    si = lax.broadcasted_iota(jnp.int32, (tc, tc), 1)
    cg = _sel_dot((si <= ti).astype(BF16), logf) + carry_ref[...]
    carry_ref[...] = cg[tc - 1:tc, :]

    hi, mid, low = _split3(cg)
    q_c = _dot(hi, sel_ref[0]) + _dot(mid, sel_ref[1]) + _dot(low, sel_ref[2])
    k_c = -(_dot(hi, sel_ref[3]) + _dot(mid, sel_ref[4]) + _dot(low, sel_ref[5]))

    lane = lax.broadcasted_iota(jnp.int32, (tc, LANES), 1)
    lo = lane < HEAD_DIM
    q_one = ((lane >= HEAD_DIM + FOX_AUG) & (lane < HEAD_DIM + 2 * FOX_AUG)).astype(F32)
    k_one = ((lane >= HEAD_DIM) & (lane < HEAD_DIM + FOX_AUG)).astype(F32)
    for j in range(nheads // 2):
        sl = slice(j * LANES, (j + 1) * LANES)
        for x_ref, c_aug, one, o_ref in ((q_ref, q_c, q_one, qa_ref), (k_ref, k_c, k_one, ka_ref)):
            x = x_ref[:, sl]
            if x_ref is q_ref:
                x = x * scale
            xr = pltpu.roll(x, HEAD_DIM, 1)
            for e, xe in ((0, x), (1, xr)):
                hs = slice((2 * j + e) * LANES, (2 * j + e + 1) * LANES)
                o_ref[:, hs] = jnp.where(lo, xe, c_aug[:, hs] + one).astype(BF16)
    vo_ref[...] = v_ref[...].astype(BF16)


def _fox_select_matrices(nheads):
    sel = np.zeros((2 * FOX_AUG, LANES, nheads * LANES), np.float32)
    for h in range(nheads):
        for i in range(FOX_AUG):
            sel[i, h, h * LANES + HEAD_DIM + i] = 1.0
            sel[FOX_AUG + i, h, h * LANES + HEAD_DIM + FOX_AUG + i] = 1.0
    return jnp.asarray(sel, BF16)


def _fox_prep(proj, b_f, nheads, *, tc=256):
    b, s, _ = proj.shape
    dim = nheads * HEAD_DIM
    ncol = dim // LANES
    bf = jnp.zeros((1, LANES), F32).at[0, :nheads].set(b_f)
    sel = _fox_select_matrices(nheads)
    wide = nheads * LANES
    return pl.pallas_call(
        _fox_prep_kernel,
        out_shape=(jax.ShapeDtypeStruct((b, s, wide), BF16),
                   jax.ShapeDtypeStruct((b, s, wide), BF16),
                   jax.ShapeDtypeStruct((b, s, dim), BF16)),
        grid=(b, s // tc),
        in_specs=[
            pl.BlockSpec((None, tc, dim), lambda bi, t: (bi, t, 0)),
            pl.BlockSpec((None, tc, dim), lambda bi, t: (bi, t, 1)),
            pl.BlockSpec((None, tc, dim), lambda bi, t: (bi, t, 2)),
            pl.BlockSpec((None, tc, LANES), lambda bi, t: (bi, t, 3 * ncol)),
            pl.BlockSpec((1, LANES), lambda bi, t: (0, 0)),
            pl.BlockSpec(sel.shape, lambda bi, t: (0, 0, 0)),
        ],
        out_specs=(pl.BlockSpec((None, tc, wide), lambda bi, t: (bi, t, 0)),
                   pl.BlockSpec((None, tc, wide), lambda bi, t: (bi, t, 0)),
                   pl.BlockSpec((None, tc, dim), lambda bi, t: (bi, t, 0))),
        scratch_shapes=[pltpu.VMEM((1, LANES), F32)],
        compiler_params=_params("parallel", "arbitrary"),
        name="fox_prep",
    )(proj, proj, proj, proj, bf, sel)


def _fox_attn_kernel(q_ref, k_ref, v_ref, o_ref, m_ref, l_ref, acc_ref):
    tq = q_ref.shape[0]
    qi = pl.program_id(2)
    m_ref[...] = jnp.full_like(m_ref, -jnp.inf)
    l_ref[...] = jnp.zeros_like(l_ref)
    acc_ref[...] = jnp.zeros_like(acc_ref)
    qs = [q_ref[:, 0:LANES], q_ref[:, LANES:2 * LANES]]
    rows = lax.broadcasted_iota(jnp.int32, (tq, tq), 0)
    cols = lax.broadcasted_iota(jnp.int32, (tq, tq), 1)

    def block(kb, diagonal):
        start = pl.multiple_of(kb * tq, tq)
        kblk = k_ref[pl.ds(start, tq), :]
        vblk = v_ref[pl.ds(start, tq), :]
        for e in range(2):
            s = _dot(qs[e], kblk[:, e * LANES:(e + 1) * LANES], NT)
            if diagonal:
                s = jnp.where(cols <= rows, s, -jnp.inf)
            m_prev = m_ref[e]
            m_new = jnp.maximum(m_prev, jnp.max(s, axis=-1, keepdims=True))
            corr = jnp.exp(m_prev - m_new)
            p = jnp.exp(s - m_new)
            l_ref[e] = corr * l_ref[e] + jnp.sum(p, axis=-1, keepdims=True)
            acc_ref[e] = corr * acc_ref[e] + _dot(p.astype(BF16), vblk)
            m_ref[e] = m_new

    def body(kb, carry):
        block(kb, False)
        return carry

    lax.fori_loop(0, qi, body, 0)
    block(qi, True)
    lo = lax.broadcasted_iota(jnp.int32, (tq, LANES), 1) < HEAD_DIM
    o_ref[...] = jnp.where(lo, acc_ref[0] / l_ref[0], acc_ref[1] / l_ref[1])


def _fox_attn(q_aug, k_aug, v, *, tq=256):
    b, s, wide = q_aug.shape
    npair = wide // (2 * LANES)
    return pl.pallas_call(
        _fox_attn_kernel,
        out_shape=jax.ShapeDtypeStruct((b, s, npair * LANES), F32),
        grid=(b, npair, s // tq),
        in_specs=[
            pl.BlockSpec((None, tq, 2 * LANES), lambda bi, j, qi: (bi, qi, j)),
            pl.BlockSpec((None, s, 2 * LANES), lambda bi, j, qi: (bi, 0, j)),
            pl.BlockSpec((None, s, LANES), lambda bi, j, qi: (bi, 0, j)),
        ],
        out_specs=pl.BlockSpec((None, tq, LANES), lambda bi, j, qi: (bi, qi, j)),
        scratch_shapes=[pltpu.VMEM((2, tq, 1), F32), pltpu.VMEM((2, tq, 1), F32),
                        pltpu.VMEM((2, tq, LANES), F32)],
        compiler_params=_params("parallel", "parallel", "arbitrary"),
        name="fox_attn",
    )(q_aug, k_aug, v)


def kernel(x, p, ffn1_norm, ffn1_w_gu, ffn1_w_down, mix_norm, ffn2_norm, ffn2_w_gu, ffn2_w_down, ple_norm, ple_w_gate, ple_w_proj, even_w_in, even_w_out, swa_sinks, rwkv_mu, rwkv_w0, rwkv_w2, rwkv_a0, rwkv_a2, rwkv_g2, rwkv_k_k, rwkv_k_a, rwkv_r_k, rwkv_ln_w, rwkv_ln_b, fox_w_in, fox_b_f, fox_w_out, final_norm):
    b, s, d = x.shape
    depth = p.shape[0]
    t = b * s
    bf = lambda w: w.astype(BF16)
    swa_q = SWA_HEADS * HEAD_DIM
    swa_cols = swa_q + 2 * (SWA_HEADS // SWA_GROUP) * HEAD_DIM
    rwkv_dim = rwkv_w0.shape[-1]
    lora = rwkv_w2.shape[1]
    fox_heads = fox_b_f.shape[-1]
    fox_dim = fox_heads * HEAD_DIM

    x = x.reshape(t, d)
    for i in range(depth):
        j = i // 2
        x = _ffn(x, ffn1_norm[i], bf(ffn1_w_gu[i]), bf(ffn1_w_down[i]))
        if i % 2 == 0:
            w_in = even_w_in[j]
            qkv = _rms_matmul(x, mix_norm[i], bf(w_in[:, :swa_cols]))
            hb = _rms_matmul(x, mix_norm[i], bf(w_in[:, swa_cols:]))
            ya = _swa(qkv.reshape(b, s, swa_cols), swa_sinks[j])
            zeros = jnp.zeros((lora, rwkv_dim), F32)
            w2p = jnp.concatenate([rwkv_w2[j], zeros], axis=0)
            a2p = jnp.concatenate([zeros, rwkv_a2[j]], axis=0)
            yb = _rwkv(hb.reshape(b, s, -1), rwkv_mu[j], rwkv_w0[j], w2p, rwkv_a0[j], a2p,
                       bf(rwkv_g2[j]), rwkv_k_k[j], rwkv_k_a[j], rwkv_r_k[j].reshape(-1),
                       rwkv_ln_w[j], rwkv_ln_b[j])
            w_out = bf(even_w_out[j])
            x = _proj_res([ya.reshape(t, swa_q), yb.reshape(t, rwkv_dim)],
                          [w_out[:swa_q], w_out[swa_q:]], x)
        else:
            w_in = jnp.pad(fox_w_in[j], ((0, 0), (0, LANES - fox_heads)))
            proj = _rms_matmul(x, mix_norm[i], bf(w_in), tn=640)
            q_aug, k_aug, v = _fox_prep(proj.reshape(b, s, -1), fox_b_f[j], fox_heads)
            yc = _fox_attn(q_aug, k_aug, v)
            x = _proj_res([yc.reshape(t, fox_dim)], [bf(fox_w_out[j])], x)
        x = _ffn(x, ffn2_norm[i], bf(ffn2_w_gu[i]), bf(ffn2_w_down[i]))
        x = _ple(x, ple_norm[i], bf(ple_w_gate[i]), p[i].reshape(t, -1), bf(ple_w_proj[i]),
                 final_norm, final=(i == depth - 1))
    return x.reshape(b, s, d)
```

```python
import functools

import jax
import jax.numpy as jnp
import numpy as np
from jax import lax
from jax.experimental import pallas as pl
from jax.experimental.pallas import tpu as pltpu

F32 = jnp.float32
BF16 = jnp.bfloat16

LANES = 128
HEAD_DIM = 64
SWA_HEADS = 8
SWA_GROUP = 4
SWA_BLOCK = 128
RWKV_CHUNK = 64
NORM_EPS = 1e-6
GN_EPS = 64e-5
L2_EPS = 1e-12
VMEM_LIMIT = 56 * 1024 * 1024

NN = (((1,), (0,)), ((), ()))
NT = (((1,), (1,)), ((), ()))
TN = (((0,), (0,)), ((), ()))


def _dot(a, b, dims=NN):
    return lax.dot_general(a, b, dims, preferred_element_type=F32)


def _dot_bf(a, b, dims=NN):
    return _dot(a.astype(BF16), b.astype(BF16), dims)


def _hi_lo(x):
    hi = x.astype(BF16)
    lo = (x - hi.astype(F32)).astype(BF16)
    return hi, lo


def _dot_x3(a, b, dims=NN):
    ah, al = _hi_lo(a)
    bh, bl = _hi_lo(b)
    return _dot(ah, bh, dims) + (_dot(ah, bl, dims) + _dot(al, bh, dims))


def _split3(x):
    hi = x.astype(BF16)
    r1 = x - hi.astype(F32)
    mid = r1.astype(BF16)
    lo = (r1 - mid.astype(F32)).astype(BF16)
    return hi, mid, lo


def _sel_dot(sel, x):
    hi, mid, lo = _split3(x)
    return _dot(sel, hi) + (_dot(sel, mid) + _dot(sel, lo))


def _dot_sel(x, sel):
    hi, mid, lo = _split3(x)
    return _dot(hi, sel) + (_dot(mid, sel) + _dot(lo, sel))


def _rms(x, g):
    ms = jnp.mean(x * x, axis=-1, keepdims=True)
    return x * lax.rsqrt(ms + NORM_EPS) * g


def _params(*sem):
    return pltpu.CompilerParams(dimension_semantics=sem, vmem_limit_bytes=VMEM_LIMIT)


def _ffn_kernel(x_ref, g_ref, wg_ref, wu_ref, wd_ref, o_ref, hn_ref, acc_ref):
    f = pl.program_id(1)

    @pl.when(f == 0)
    def _():
        hn_ref[...] = _rms(x_ref[...], g_ref[...]).astype(BF16)
        acc_ref[...] = jnp.zeros_like(acc_ref)

    hn = hn_ref[...]
    gate = _dot(hn, wg_ref[...])
    up = _dot(hn, wu_ref[...])
    act = (gate * jax.nn.sigmoid(gate) * up).astype(BF16)
    acc_ref[...] += _dot(act, wd_ref[...])

    @pl.when(f == pl.num_programs(1) - 1)
    def _():
        o_ref[...] = x_ref[...] + 0.5 * acc_ref[...]


def _ffn(x, g, w_gu, w_down, *, tm=512, tf=1408):
    t, d = x.shape
    dff = w_down.shape[0]
    nf = dff // tf
    return pl.pallas_call(
        _ffn_kernel,
        out_shape=jax.ShapeDtypeStruct((t, d), F32),
        grid=(t // tm, nf),
        in_specs=[
            pl.BlockSpec((tm, d), lambda i, f: (i, 0)),
            pl.BlockSpec((1, d), lambda i, f: (0, 0)),
            pl.BlockSpec((d, tf), lambda i, f: (0, f)),
            pl.BlockSpec((d, tf), lambda i, f: (0, nf + f)),
            pl.BlockSpec((tf, d), lambda i, f: (f, 0)),
        ],
        out_specs=pl.BlockSpec((tm, d), lambda i, f: (i, 0)),
        scratch_shapes=[pltpu.VMEM((tm, d), BF16), pltpu.VMEM((tm, d), F32)],
        compiler_params=_params("parallel", "arbitrary"),
        name="ffn",
    )(x, g.reshape(1, d), w_gu, w_gu, w_down)


def _rms_matmul_kernel(x_ref, g_ref, w_ref, o_ref, hn_ref):
    @pl.when(pl.program_id(1) == 0)
    def _():
        hn_ref[...] = _rms(x_ref[...], g_ref[...]).astype(BF16)

    o_ref[...] = _dot(hn_ref[...], w_ref[...]).astype(o_ref.dtype)


def _rms_matmul(x, g, w, *, tm=512, tn=None, out_dtype=F32):
    t, d = x.shape
    n = w.shape[1]
    tn = n if tn is None else tn
    return pl.pallas_call(
        _rms_matmul_kernel,
        out_shape=jax.ShapeDtypeStruct((t, n), out_dtype),
        grid=(t // tm, n // tn),
        in_specs=[
            pl.BlockSpec((tm, d), lambda i, j: (i, 0)),
            pl.BlockSpec((1, d), lambda i, j: (0, 0)),
            pl.BlockSpec((d, tn), lambda i, j: (0, j)),
        ],
        out_specs=pl.BlockSpec((tm, tn), lambda i, j: (i, j)),
        scratch_shapes=[pltpu.VMEM((tm, d), BF16)],
        compiler_params=_params("parallel", "arbitrary"),
        name="rms_matmul",
    )(x, g.reshape(1, d), w)


def _proj_res_kernel(n_in, *refs):
    a_refs = refs[:n_in]
    w_refs = refs[n_in:2 * n_in]
    x_ref, o_ref = refs[2 * n_in], refs[2 * n_in + 1]
    acc = x_ref[...]
    for a_ref, w_ref in zip(a_refs, w_refs):
        acc = acc + _dot(a_ref[...].astype(BF16), w_ref[...])
    o_ref[...] = acc


def _proj_res(a_list, w_list, x, *, tm=512):
    t, d = x.shape
    n_in = len(a_list)
    in_specs = [pl.BlockSpec((tm, a.shape[1]), lambda i: (i, 0)) for a in a_list]
    in_specs += [pl.BlockSpec(w.shape, lambda i: (0, 0)) for w in w_list]
    in_specs += [pl.BlockSpec((tm, d), lambda i: (i, 0))]
    return pl.pallas_call(
        functools.partial(_proj_res_kernel, n_in),
        out_shape=jax.ShapeDtypeStruct((t, d), F32),
        grid=(t // tm,),
        in_specs=in_specs,
        out_specs=pl.BlockSpec((tm, d), lambda i: (i, 0)),
        compiler_params=_params("parallel"),
        name="proj_res",
    )(*a_list, *w_list, x)


def _ple_kernel(final, x_ref, g_ref, wg_ref, p_ref, wp_ref, fn_ref, o_ref):
    x = x_ref[...]
    hn = _rms(x, g_ref[...]).astype(BF16)
    gate = jax.nn.sigmoid(_dot(hn, wg_ref[...]))
    y = x + gate * _dot(p_ref[...].astype(BF16), wp_ref[...])
    if final:
        y = _rms(y, fn_ref[...])
    o_ref[...] = y


def _ple(x, g, w_gate, p, w_proj, final_g, *, final, tm=512):
    t, d = x.shape
    pd = p.shape[1]
    return pl.pallas_call(
        functools.partial(_ple_kernel, final),
        out_shape=jax.ShapeDtypeStruct((t, d), F32),
        grid=(t // tm,),
        in_specs=[
            pl.BlockSpec((tm, d), lambda i: (i, 0)),
            pl.BlockSpec((1, d), lambda i: (0, 0)),
            pl.BlockSpec((d, d), lambda i: (0, 0)),
            pl.BlockSpec((tm, pd), lambda i: (i, 0)),
            pl.BlockSpec((pd, d), lambda i: (0, 0)),
            pl.BlockSpec((1, d), lambda i: (0, 0)),
        ],
        out_specs=pl.BlockSpec((tm, d), lambda i: (i, 0)),
        compiler_params=_params("parallel"),
        name="ple",
    )(x, g.reshape(1, d), w_gate, p, w_proj, final_g.reshape(1, d))


def _swa_kernel(sink_ref, q_ref, kp_ref, kc_ref, vp_ref, vc_ref, o_ref):
    n = pl.program_id(1)
    blk = SWA_BLOCK
    scale = HEAD_DIM ** -0.5
    k = jnp.concatenate([kp_ref[...], kc_ref[...]], axis=0)
    v = jnp.concatenate([vp_ref[...], vc_ref[...]], axis=0)
    kr = pltpu.roll(k, HEAD_DIM, 1)
    vr = pltpu.roll(v, HEAD_DIM, 1)
    lo_kv = lax.broadcasted_iota(jnp.int32, k.shape, 1) < HEAD_DIM
    kdup = [jnp.where(lo_kv, k, kr).astype(BF16), jnp.where(lo_kv, kr, k).astype(BF16)]
    vdup = [jnp.where(lo_kv, v, vr).astype(BF16), jnp.where(lo_kv, vr, v).astype(BF16)]

    qi = lax.broadcasted_iota(jnp.int32, (blk, 2 * blk), 0)
    ki = lax.broadcasted_iota(jnp.int32, (blk, 2 * blk), 1)
    dist = qi + blk - ki
    valid = (dist >= 0) & (dist < blk) & ((n > 0) | (ki >= blk))
    distf = dist.astype(F32)
    lo_q = lax.broadcasted_iota(jnp.int32, (blk, LANES), 1) < HEAD_DIM

    for j in range(SWA_HEADS // 2):
        g = (2 * j) // SWA_GROUP
        q2 = q_ref[:, j * LANES:(j + 1) * LANES] * scale
        outs = []
        for e in range(2):
            h = 2 * j + e
            qm = jnp.where(lo_q if e == 0 else ~lo_q, q2, 0.0).astype(BF16)
            s = _dot(qm, kdup[g], NT)
            slope = 2.0 ** (-8.0 * (h + 1) / SWA_HEADS)
            s = jnp.where(valid, s - slope * distf, -jnp.inf)
            sink = sink_ref[h]
            m = jnp.maximum(jnp.max(s, axis=-1, keepdims=True), sink)
            p = jnp.exp(s - m)
            denom = jnp.sum(p, axis=-1, keepdims=True) + jnp.exp(sink - m)
            p = p / denom
            outs.append(_dot(p.astype(BF16), vdup[g]))
        o_ref[:, j * LANES:(j + 1) * LANES] = jnp.where(lo_q, outs[0], outs[1])


def _swa(qkv, sinks):
    b, s, _ = qkv.shape
    blk = SWA_BLOCK
    nq = SWA_HEADS * HEAD_DIM
    kcol = nq // LANES
    vcol = kcol + 1
    prev = lambda bi, n: jnp.maximum(n - 1, 0)
    return pl.pallas_call(
        _swa_kernel,
        out_shape=jax.ShapeDtypeStruct((b, s, nq), F32),
        grid=(b, s // blk),
        in_specs=[
            pl.BlockSpec(memory_space=pltpu.SMEM),
            pl.BlockSpec((None, blk, nq), lambda bi, n: (bi, n, 0)),
            pl.BlockSpec((None, blk, LANES), lambda bi, n: (bi, prev(bi, n), kcol)),
            pl.BlockSpec((None, blk, LANES), lambda bi, n: (bi, n, kcol)),
            pl.BlockSpec((None, blk, LANES), lambda bi, n: (bi, prev(bi, n), vcol)),
            pl.BlockSpec((None, blk, LANES), lambda bi, n: (bi, n, vcol)),
        ],
        out_specs=pl.BlockSpec((None, blk, nq), lambda bi, n: (bi, n, 0)),
        compiler_params=_params("parallel", "arbitrary"),
        name="swa",
    )(sinks, qkv, qkv, qkv, qkv, qkv)


def _tri_inverse(low, ri, ci):
    eye = (ri == ci).astype(F32)
    base = 8
    same = lambda w: (ri ^ ci) < w
    xd = jnp.where(same(base), -low, 0.0)
    inv = eye + xd
    p2 = _dot_x3(xd, xd)
    inv = inv + _dot_x3(inv, p2)
    p4 = _dot_x3(p2, p2)
    inv = inv + _dot_x3(inv, p4)
    w = base * 2
    while w <= RWKV_CHUNK:
        off = jnp.where(same(w) & ~same(w // 2), low, 0.0)
        inv = inv - _dot_x3(_dot_x3(inv, off), inv)
        w *= 2
    return inv


def _rwkv_kernel(h_ref, mu_ref, w0_ref, w2_ref, a0_ref, a2_ref, g2_ref, kk_ref, ka_ref,
                 rk_ref, lnw_ref, lnb_ref, o_ref, state_ref, last_ref):
    c = RWKV_CHUNK
    dim = o_ref.shape[-1]
    npair = dim // LANES

    @pl.when(pl.program_id(1) == 0)
    def _():
        state_ref[...] = jnp.zeros_like(state_ref)
        last_ref[...] = jnp.zeros_like(last_ref)

    h = h_ref[...]
    row = lax.broadcasted_iota(jnp.int32, h.shape, 0)
    shifted = jnp.where(row == 0, last_ref[...], pltpu.roll(h, 1, 0))
    last_ref[...] = h[c - 1:c, :]
    hs = h + (shifted - h) * mu_ref[...]
    r = hs[:, 0:dim]
    k = hs[:, dim:2 * dim]
    v = hs[:, 2 * dim:3 * dim]
    xwa = hs[:, 3 * dim:3 * dim + LANES]
    xg = hs[:, 3 * dim + LANES:3 * dim + 2 * LANES]

    wl = w0_ref[...] + _dot_x3(jnp.tanh(xwa), w2_ref[...])
    logw = -jax.nn.sigmoid(wl) * float(np.exp(-0.5))
    a = jax.nn.sigmoid(a0_ref[...] + _dot_x3(xwa, a2_ref[...]))
    gate = _dot_bf(jax.nn.sigmoid(xg), g2_ref[...])

    ri = lax.broadcasted_iota(jnp.int32, (2 * c, 2 * c), 0)
    ci = lax.broadcasted_iota(jnp.int32, (2 * c, 2 * c), 1)
    ones_bd = ((ri ^ ci) < HEAD_DIM).astype(BF16)

    def head_sum(x):
        return jnp.concatenate(
            [_dot_sel(x[:, j * LANES:(j + 1) * LANES], ones_bd) for j in range(npair)], axis=1)

    kk = k * kk_ref[...]
    kk = kk / jnp.maximum(jnp.sqrt(head_sum(kk * kk)), L2_EPS)
    k2 = k * (1.0 + (a - 1.0) * ka_ref[...])
    bvec = kk * a

    ti = lax.broadcasted_iota(jnp.int32, (c, c), 0)
    si = lax.broadcasted_iota(jnp.int32, (c, c), 1)
    cum = _sel_dot((si <= ti).astype(BF16), logw)
    cum_end = cum[c - 1:c, :]
    e_neg = jnp.exp(-cum)
    alpha = kk * jnp.exp(cum - logw)
    beta = bvec * e_neg
    kappa = k2 * e_neg
    rho = r * jnp.exp(cum)
    to_end = jnp.exp(cum_end - cum)
    beta_e = bvec * to_end
    kappa_e = k2 * to_end
    w_end = jnp.exp(cum_end)

    same_head = (ri ^ ci) < c
    strict = same_head & (ci < ri)
    incl = same_head & (ci <= ri)
    lo = lax.broadcasted_iota(jnp.int32, (c, LANES), 1) < HEAD_DIM

    def stack(x2):
        return jnp.concatenate([jnp.where(lo, x2, 0.0), jnp.where(lo, 0.0, x2)], axis=0)

    ys = []
    for j in range(npair):
        sl = slice(j * LANES, (j + 1) * LANES)
        a_s, rho_s, v_s = stack(alpha[:, sl]), stack(rho[:, sl]), stack(v[:, sl])
        be_s, ke_s = stack(beta_e[:, sl]), stack(kappa_e[:, sl])
        b_cat = jnp.concatenate([beta[:, sl], beta[:, sl]], axis=0)
        k_cat = jnp.concatenate([kappa[:, sl], kappa[:, sl]], axis=0)
        l_ab = jnp.where(strict, _dot_x3(a_s, b_cat, NT), 0.0)
        l_ak = jnp.where(strict, _dot_x3(a_s, k_cat, NT), 0.0)
        r_b = jnp.where(incl, _dot_x3(rho_s, b_cat, NT), 0.0)
        r_k = jnp.where(incl, _dot_x3(rho_s, k_cat, NT), 0.0)
        inv = _tri_inverse(l_ab, ri, ci)

        s0 = state_ref[j]
        u = -_dot_x3(inv, _dot_x3(a_s, s0, NT) + _dot_x3(l_ak, v_s))
        y = _dot_x3(rho_s, s0, NT) + _dot_x3(r_b, u) + _dot_x3(r_k, v_s)
        ys.append(y[:c] + y[c:])
        state_ref[j] = (s0 * w_end[:, sl]
                        + _dot_x3(jnp.concatenate([u, v_s], axis=0),
                                  jnp.concatenate([be_s, ke_s], axis=0), TN))
    y = jnp.concatenate(ys, axis=1)

    mean = head_sum(y) * (1.0 / HEAD_DIM)
    d = y - mean
    var = head_sum(d * d) * (1.0 / HEAD_DIM)
    y = d * lax.rsqrt(var + GN_EPS) * lnw_ref[...] + lnb_ref[...]
    y = y + head_sum(r * k2 * rk_ref[...]) * v
    o_ref[...] = y * gate


def _rwkv(hb, mu, w0, w2p, a0, a2p, g2, k_k, k_a, r_k, ln_w, ln_b):
    b, s, cols = hb.shape
    dim = w0.shape[-1]
    c = RWKV_CHUNK
    row = lambda x: x.reshape(1, -1)
    vec = lambda n: pl.BlockSpec((1, n), lambda bi, t: (0, 0))
    mat = lambda m: pl.BlockSpec(m.shape, lambda bi, t: (0, 0))
    return pl.pallas_call(
        _rwkv_kernel,
        out_shape=jax.ShapeDtypeStruct((b, s, dim), F32),
        grid=(b, s // c),
        in_specs=[
            pl.BlockSpec((None, c, cols), lambda bi, t: (bi, t, 0)),
            vec(cols), vec(dim), mat(w2p), vec(dim), mat(a2p), mat(g2),
            vec(dim), vec(dim), vec(dim), vec(dim), vec(dim),
        ],
        out_specs=pl.BlockSpec((None, c, dim), lambda bi, t: (bi, t, 0)),
        scratch_shapes=[pltpu.VMEM((dim // LANES, LANES, LANES), F32), pltpu.VMEM((1, cols), F32)],
        compiler_params=_params("parallel", "arbitrary"),
        name="rwkv7",
    )(hb, row(mu), row(w0), w2p, row(a0), a2p, g2, row(k_k), row(k_a), row(r_k), row(ln_w), row(ln_b))


FOX_AUG = 3
LOG2E = float(np.log2(np.e))


def _fox_prep_kernel(q_ref, k_ref, v_ref, fz_ref, bf_ref, sel_ref, qa_ref, ka_ref, va_ref, carry_ref):
    tc = q_ref.shape[0]
    nheads = qa_ref.shape[-1] // LANES
    scale = HEAD_DIM ** -0.5 * LOG2E

    @pl.when(pl.program_id(1) == 0)
    def _():
        carry_ref[...] = jnp.zeros_like(carry_ref)

    z = fz_ref[...] + bf_ref[...]
    logf = jnp.minimum(z, 0.0) - jnp.log(1.0 + jnp.exp(-jnp.abs(z)))
    lane = lax.broadcasted_iota(jnp.int32, z.shape, 1)
    logf = jnp.where(lane < nheads, logf, 0.0)
    ti = lax.broadcasted_iota(jnp.int32, (tc, tc), 0)
    si = lax.broadcasted_iota(jnp.int32, (tc, tc), 1)
    cg = _sel_dot((si <= ti).astype(BF16), logf) + carry_ref[...]
    carry_ref[...] = cg[tc - 1:tc, :]

    hi, mid, low = _split3(cg * LOG2E)
    q_c = _dot(hi, sel_ref[0]) + _dot(mid, sel_ref[1]) + _dot(low, sel_ref[2])
    k_c = -(_dot(hi, sel_ref[3]) + _dot(mid, sel_ref[4]) + _dot(low, sel_ref[5]))

    lane = lax.broadcasted_iota(jnp.int32, (tc, LANES), 1)
    lo = lane < HEAD_DIM
    q_one = ((lane >= HEAD_DIM + FOX_AUG) & (lane < HEAD_DIM + 2 * FOX_AUG)).astype(F32)
    k_one = ((lane >= HEAD_DIM) & (lane < HEAD_DIM + FOX_AUG)).astype(F32)
    v_one = (lane == HEAD_DIM).astype(F32)
    sources = ((q_ref, q_c, q_one, qa_ref), (k_ref, k_c, k_one, ka_ref), (v_ref, None, v_one, va_ref))
    for j in range(nheads // 2):
        sl = slice(j * LANES, (j + 1) * LANES)
        for x_ref, c_aug, one, o_ref in sources:
            x = x_ref[:, sl]
            if x_ref is q_ref:
                x = x * scale
            xr = pltpu.roll(x, HEAD_DIM, 1)
            for e, xe in ((0, x), (1, xr)):
                hs = slice((2 * j + e) * LANES, (2 * j + e + 1) * LANES)
                fill = one if c_aug is None else c_aug[:, hs] + one
                o_ref[:, hs] = jnp.where(lo, xe, fill).astype(BF16)


def _fox_select_matrices(nheads):
    sel = np.zeros((2 * FOX_AUG, LANES, nheads * LANES), np.float32)
    for h in range(nheads):
        for i in range(FOX_AUG):
            sel[i, h, h * LANES + HEAD_DIM + i] = 1.0
            sel[FOX_AUG + i, h, h * LANES + HEAD_DIM + FOX_AUG + i] = 1.0
    return jnp.asarray(sel, BF16)


def _fox_prep(proj, b_f, nheads, *, tc=256):
    b, s, _ = proj.shape
    dim = nheads * HEAD_DIM
    ncol = dim // LANES
    bf = jnp.zeros((1, LANES), F32).at[0, :nheads].set(b_f)
    sel = _fox_select_matrices(nheads)
    wide = nheads * LANES
    return pl.pallas_call(
        _fox_prep_kernel,
        out_shape=(jax.ShapeDtypeStruct((b, s, wide), BF16),) * 3,
        grid=(b, s // tc),
        in_specs=[
            pl.BlockSpec((None, tc, dim), lambda bi, t: (bi, t, 0)),
            pl.BlockSpec((None, tc, dim), lambda bi, t: (bi, t, 1)),
            pl.BlockSpec((None, tc, dim), lambda bi, t: (bi, t, 2)),
            pl.BlockSpec((None, tc, LANES), lambda bi, t: (bi, t, 3 * ncol)),
            pl.BlockSpec((1, LANES), lambda bi, t: (0, 0)),
            pl.BlockSpec(sel.shape, lambda bi, t: (0, 0, 0)),
        ],
        out_specs=(pl.BlockSpec((None, tc, wide), lambda bi, t: (bi, t, 0)),) * 3,
        scratch_shapes=[pltpu.VMEM((1, LANES), F32)],
        compiler_params=_params("parallel", "arbitrary"),
        name="fox_prep",
    )(proj, proj, proj, proj, bf, sel)


def _fox_attn_kernel(q_ref, k_ref, v_ref, o_ref, m_ref, acc_ref):
    tq = q_ref.shape[0]
    qi = pl.program_id(2)
    m_ref[...] = jnp.full_like(m_ref, -jnp.inf)
    acc_ref[...] = jnp.zeros_like(acc_ref)
    rows = lax.broadcasted_iota(jnp.int32, (tq, tq), 0)
    cols = lax.broadcasted_iota(jnp.int32, (tq, tq), 1)

    def block(kb, diagonal):
        start = pl.multiple_of(kb * tq, tq)
        for e in range(2):
            hs = slice(e * LANES, (e + 1) * LANES)
            s = _dot(q_ref[:, hs], k_ref[pl.ds(start, tq), hs], NT)
            if diagonal:
                s = jnp.where(cols <= rows, s, -jnp.inf)
            chunks = [s[:, i * LANES:(i + 1) * LANES] for i in range(tq // LANES)]
            m_prev = m_ref[e]
            blk_max = functools.reduce(jnp.maximum, chunks)
            m_new = jnp.maximum(m_prev, jnp.max(blk_max, axis=-1, keepdims=True))
            p = jnp.concatenate([jnp.exp2(ch - m_new) for ch in chunks], axis=1).astype(BF16)
            acc_ref[e] = jnp.exp2(m_prev - m_new) * acc_ref[e] + _dot(p, v_ref[pl.ds(start, tq), hs])
            m_ref[e] = m_new

    def body(kb, carry):
        block(kb, False)
        return carry

    lax.fori_loop(0, qi, body, 0)
    block(qi, True)
    outs = []
    for e in range(2):
        acc = acc_ref[e]
        outs.append(acc / acc[:, HEAD_DIM:HEAD_DIM + 1])
    lo = lax.broadcasted_iota(jnp.int32, (tq, LANES), 1) < HEAD_DIM
    o_ref[...] = jnp.where(lo, outs[0], pltpu.roll(outs[1], HEAD_DIM, 1))


def _fox_attn(q_aug, k_aug, v_aug, *, tq=512):
    b, s, wide = q_aug.shape
    npair = wide // (2 * LANES)
    return pl.pallas_call(
        _fox_attn_kernel,
        out_shape=jax.ShapeDtypeStruct((b, s, npair * LANES), F32),
        grid=(b, npair, s // tq),
        in_specs=[
            pl.BlockSpec((None, tq, 2 * LANES), lambda bi, j, qi: (bi, qi, j)),
            pl.BlockSpec((None, s, 2 * LANES), lambda bi, j, qi: (bi, 0, j)),
            pl.BlockSpec((None, s, 2 * LANES), lambda bi, j, qi: (bi, 0, j)),
        ],
        out_specs=pl.BlockSpec((None, tq, LANES), lambda bi, j, qi: (bi, qi, j)),
        scratch_shapes=[pltpu.VMEM((2, tq, LANES), F32), pltpu.VMEM((2, tq, LANES), F32)],
        compiler_params=_params("parallel", "parallel", "arbitrary"),
        name="fox_attn",
    )(q_aug, k_aug, v_aug)


def kernel(x, p, ffn1_norm, ffn1_w_gu, ffn1_w_down, mix_norm, ffn2_norm, ffn2_w_gu, ffn2_w_down, ple_norm, ple_w_gate, ple_w_proj, even_w_in, even_w_out, swa_sinks, rwkv_mu, rwkv_w0, rwkv_w2, rwkv_a0, rwkv_a2, rwkv_g2, rwkv_k_k, rwkv_k_a, rwkv_r_k, rwkv_ln_w, rwkv_ln_b, fox_w_in, fox_b_f, fox_w_out, final_norm):
    b, s, d = x.shape
    depth = p.shape[0]
    t = b * s
    bf = lambda w: w.astype(BF16)
    swa_q = SWA_HEADS * HEAD_DIM
    swa_cols = swa_q + 2 * (SWA_HEADS // SWA_GROUP) * HEAD_DIM
    rwkv_dim = rwkv_w0.shape[-1]
    lora = rwkv_w2.shape[1]
    fox_heads = fox_b_f.shape[-1]
    fox_dim = fox_heads * HEAD_DIM

    x = x.reshape(t, d)
    for i in range(depth):
        j = i // 2
        x = _ffn(x, ffn1_norm[i], bf(ffn1_w_gu[i]), bf(ffn1_w_down[i]))
        if i % 2 == 0:
            w_in = even_w_in[j]
            qkv = _rms_matmul(x, mix_norm[i], bf(w_in[:, :swa_cols]))
            hb = _rms_matmul(x, mix_norm[i], bf(w_in[:, swa_cols:]))
            ya = _swa(qkv.reshape(b, s, swa_cols), swa_sinks[j])
            zeros = jnp.zeros((lora, rwkv_dim), F32)
            w2p = jnp.concatenate([rwkv_w2[j], zeros], axis=0)
            a2p = jnp.concatenate([zeros, rwkv_a2[j]], axis=0)
            yb = _rwkv(hb.reshape(b, s, -1), rwkv_mu[j], rwkv_w0[j], w2p, rwkv_a0[j], a2p,
                       bf(rwkv_g2[j]), rwkv_k_k[j], rwkv_k_a[j], rwkv_r_k[j].reshape(-1),
                       rwkv_ln_w[j], rwkv_ln_b[j])
            w_out = bf(even_w_out[j])
            x = _proj_res([ya.reshape(t, swa_q), yb.reshape(t, rwkv_dim)],
                          [w_out[:swa_q], w_out[swa_q:]], x)
        else:
            w_in = jnp.pad(fox_w_in[j], ((0, 0), (0, LANES - fox_heads)))
            proj = _rms_matmul(x, mix_norm[i], bf(w_in), tn=640)
            q_aug, k_aug, v = _fox_prep(proj.reshape(b, s, -1), fox_b_f[j], fox_heads)
            yc = _fox_attn(q_aug, k_aug, v)
            x = _proj_res([yc.reshape(t, fox_dim)], [bf(fox_w_out[j])], x)
        x = _ffn(x, ffn2_norm[i], bf(ffn2_w_gu[i]), bf(ffn2_w_down[i]))
        x = _ple(x, ple_norm[i], bf(ple_w_gate[i]), p[i].reshape(t, -1), bf(ple_w_proj[i]),
                 final_norm, final=(i == depth - 1))
    return x.reshape(b, s, d)
```

```python
import functools

import jax
import jax.numpy as jnp
import numpy as np
from jax import lax
from jax.experimental import pallas as pl
from jax.experimental.pallas import tpu as pltpu

F32 = jnp.float32
BF16 = jnp.bfloat16

LANES = 128
HEAD_DIM = 64
SWA_HEADS = 8
SWA_GROUP = 4
SWA_BLOCK = 128
RWKV_CHUNK = 64
NORM_EPS = 1e-6
GN_EPS = 64e-5
L2_EPS = 1e-12
VMEM_LIMIT = 56 * 1024 * 1024

NN = (((1,), (0,)), ((), ()))
NT = (((1,), (1,)), ((), ()))
TN = (((0,), (0,)), ((), ()))


def _dot(a, b, dims=NN):
    return lax.dot_general(a, b, dims, preferred_element_type=F32)


def _dot_bf(a, b, dims=NN):
    return _dot(a.astype(BF16), b.astype(BF16), dims)


def _hi_lo(x):
    hi = x.astype(BF16)
    lo = (x - hi.astype(F32)).astype(BF16)
    return hi, lo


def _dot_x3(a, b, dims=NN):
    ah, al = _hi_lo(a)
    bh, bl = _hi_lo(b)
    return _dot(ah, bh, dims) + (_dot(ah, bl, dims) + _dot(al, bh, dims))


def _split3(x):
    hi = x.astype(BF16)
    r1 = x - hi.astype(F32)
    mid = r1.astype(BF16)
    lo = (r1 - mid.astype(F32)).astype(BF16)
    return hi, mid, lo


def _sel_dot(sel, x):
    hi, mid, lo = _split3(x)
    return _dot(sel, hi) + (_dot(sel, mid) + _dot(sel, lo))


def _dot_sel(x, sel):
    hi, mid, lo = _split3(x)
    return _dot(hi, sel) + (_dot(mid, sel) + _dot(lo, sel))


def _rms(x, g):
    ms = jnp.mean(x * x, axis=-1, keepdims=True)
    return x * lax.rsqrt(ms + NORM_EPS) * g


def _params(*sem):
    return pltpu.CompilerParams(dimension_semantics=sem, vmem_limit_bytes=VMEM_LIMIT)


def _ffn_kernel(x_ref, g_ref, wg_ref, wu_ref, wd_ref, o_ref, hn_ref, acc_ref):
    f = pl.program_id(1)

    @pl.when(f == 0)
    def _():
        hn_ref[...] = _rms(x_ref[...], g_ref[...]).astype(BF16)
        acc_ref[...] = jnp.zeros_like(acc_ref)

    hn = hn_ref[...]
    gate = _dot(hn, wg_ref[...])
    up = _dot(hn, wu_ref[...])
    act = (gate * jax.nn.sigmoid(gate) * up).astype(BF16)
    acc_ref[...] += _dot(act, wd_ref[...])

    @pl.when(f == pl.num_programs(1) - 1)
    def _():
        o_ref[...] = x_ref[...] + 0.5 * acc_ref[...]


def _ffn(x, g, w_gu, w_down, *, tm=512, tf=1408):
    t, d = x.shape
    dff = w_down.shape[0]
    nf = dff // tf
    return pl.pallas_call(
        _ffn_kernel,
        out_shape=jax.ShapeDtypeStruct((t, d), F32),
        grid=(t // tm, nf),
        in_specs=[
            pl.BlockSpec((tm, d), lambda i, f: (i, 0)),
            pl.BlockSpec((1, d), lambda i, f: (0, 0)),
            pl.BlockSpec((d, tf), lambda i, f: (0, f)),
            pl.BlockSpec((d, tf), lambda i, f: (0, nf + f)),
            pl.BlockSpec((tf, d), lambda i, f: (f, 0)),
        ],
        out_specs=pl.BlockSpec((tm, d), lambda i, f: (i, 0)),
        scratch_shapes=[pltpu.VMEM((tm, d), BF16), pltpu.VMEM((tm, d), F32)],
        compiler_params=_params("parallel", "arbitrary"),
        name="ffn",
    )(x, g.reshape(1, d), w_gu, w_gu, w_down)


def _rms_matmul_kernel(x_ref, g_ref, w_ref, o_ref, hn_ref):
    @pl.when(pl.program_id(1) == 0)
    def _():
        hn_ref[...] = _rms(x_ref[...], g_ref[...]).astype(BF16)

    o_ref[...] = _dot(hn_ref[...], w_ref[...]).astype(o_ref.dtype)


def _rms_matmul(x, g, w, *, tm=512, tn=None, out_dtype=F32):
    t, d = x.shape
    n = w.shape[1]
    tn = n if tn is None else tn
    return pl.pallas_call(
        _rms_matmul_kernel,
        out_shape=jax.ShapeDtypeStruct((t, n), out_dtype),
        grid=(t // tm, n // tn),
        in_specs=[
            pl.BlockSpec((tm, d), lambda i, j: (i, 0)),
            pl.BlockSpec((1, d), lambda i, j: (0, 0)),
            pl.BlockSpec((d, tn), lambda i, j: (0, j)),
        ],
        out_specs=pl.BlockSpec((tm, tn), lambda i, j: (i, j)),
        scratch_shapes=[pltpu.VMEM((tm, d), BF16)],
        compiler_params=_params("parallel", "arbitrary"),
        name="rms_matmul",
    )(x, g.reshape(1, d), w)


def _proj_res_kernel(n_in, *refs):
    a_refs = refs[:n_in]
    w_refs = refs[n_in:2 * n_in]
    x_ref, o_ref = refs[2 * n_in], refs[2 * n_in + 1]
    acc = x_ref[...]
    for a_ref, w_ref in zip(a_refs, w_refs):
        acc = acc + _dot(a_ref[...].astype(BF16), w_ref[...])
    o_ref[...] = acc


def _proj_res(a_list, w_list, x, *, tm=512):
    t, d = x.shape
    n_in = len(a_list)
    in_specs = [pl.BlockSpec((tm, a.shape[1]), lambda i: (i, 0)) for a in a_list]
    in_specs += [pl.BlockSpec(w.shape, lambda i: (0, 0)) for w in w_list]
    in_specs += [pl.BlockSpec((tm, d), lambda i: (i, 0))]
    return pl.pallas_call(
        functools.partial(_proj_res_kernel, n_in),
        out_shape=jax.ShapeDtypeStruct((t, d), F32),
        grid=(t // tm,),
        in_specs=in_specs,
        out_specs=pl.BlockSpec((tm, d), lambda i: (i, 0)),
        compiler_params=_params("parallel"),
        name="proj_res",
    )(*a_list, *w_list, x)


def _ple_kernel(final, x_ref, g_ref, wg_ref, p_ref, wp_ref, fn_ref, o_ref):
    x = x_ref[...]
    hn = _rms(x, g_ref[...]).astype(BF16)
    gate = jax.nn.sigmoid(_dot(hn, wg_ref[...]))
    y = x + gate * _dot(p_ref[...].astype(BF16), wp_ref[...])
    if final:
        y = _rms(y, fn_ref[...])
    o_ref[...] = y


def _ple(x, g, w_gate, p, w_proj, final_g, *, final, tm=512):
    t, d = x.shape
    pd = p.shape[1]
    return pl.pallas_call(
        functools.partial(_ple_kernel, final),
        out_shape=jax.ShapeDtypeStruct((t, d), F32),
        grid=(t // tm,),
        in_specs=[
            pl.BlockSpec((tm, d), lambda i: (i, 0)),
            pl.BlockSpec((1, d), lambda i: (0, 0)),
            pl.BlockSpec((d, d), lambda i: (0, 0)),
            pl.BlockSpec((tm, pd), lambda i: (i, 0)),
            pl.BlockSpec((pd, d), lambda i: (0, 0)),
            pl.BlockSpec((1, d), lambda i: (0, 0)),
        ],
        out_specs=pl.BlockSpec((tm, d), lambda i: (i, 0)),
        compiler_params=_params("parallel"),
        name="ple",
    )(x, g.reshape(1, d), w_gate, p, w_proj, final_g.reshape(1, d))


def _swa_kernel(sink_ref, q_ref, kp_ref, kc_ref, vp_ref, vc_ref, o_ref):
    n = pl.program_id(1)
    blk = SWA_BLOCK
    scale = HEAD_DIM ** -0.5
    k = jnp.concatenate([kp_ref[...], kc_ref[...]], axis=0)
    v = jnp.concatenate([vp_ref[...], vc_ref[...]], axis=0)
    kr = pltpu.roll(k, HEAD_DIM, 1)
    vr = pltpu.roll(v, HEAD_DIM, 1)
    lo_kv = lax.broadcasted_iota(jnp.int32, k.shape, 1) < HEAD_DIM
    kdup = [jnp.where(lo_kv, k, kr).astype(BF16), jnp.where(lo_kv, kr, k).astype(BF16)]
    vdup = [jnp.where(lo_kv, v, vr).astype(BF16), jnp.where(lo_kv, vr, v).astype(BF16)]

    qi = lax.broadcasted_iota(jnp.int32, (blk, 2 * blk), 0)
    ki = lax.broadcasted_iota(jnp.int32, (blk, 2 * blk), 1)
    dist = qi + blk - ki
    valid = (dist >= 0) & (dist < blk) & ((n > 0) | (ki >= blk))
    distf = dist.astype(F32)
    lo_q = lax.broadcasted_iota(jnp.int32, (blk, LANES), 1) < HEAD_DIM

    for j in range(SWA_HEADS // 2):
        g = (2 * j) // SWA_GROUP
        q2 = q_ref[:, j * LANES:(j + 1) * LANES] * scale
        outs = []
        for e in range(2):
            h = 2 * j + e
            qm = jnp.where(lo_q if e == 0 else ~lo_q, q2, 0.0).astype(BF16)
            s = _dot(qm, kdup[g], NT)
            slope = 2.0 ** (-8.0 * (h + 1) / SWA_HEADS)
            s = jnp.where(valid, s - slope * distf, -jnp.inf)
            sink = sink_ref[h]
            m = jnp.maximum(jnp.max(s, axis=-1, keepdims=True), sink)
            p = jnp.exp(s - m)
            denom = jnp.sum(p, axis=-1, keepdims=True) + jnp.exp(sink - m)
            p = p / denom
            outs.append(_dot(p.astype(BF16), vdup[g]))
        o_ref[:, j * LANES:(j + 1) * LANES] = jnp.where(lo_q, outs[0], outs[1])


def _swa(qkv, sinks):
    b, s, _ = qkv.shape
    blk = SWA_BLOCK
    nq = SWA_HEADS * HEAD_DIM
    kcol = nq // LANES
    vcol = kcol + 1
    prev = lambda bi, n: jnp.maximum(n - 1, 0)
    return pl.pallas_call(
        _swa_kernel,
        out_shape=jax.ShapeDtypeStruct((b, s, nq), F32),
        grid=(b, s // blk),
        in_specs=[
            pl.BlockSpec(memory_space=pltpu.SMEM),
            pl.BlockSpec((None, blk, nq), lambda bi, n: (bi, n, 0)),
            pl.BlockSpec((None, blk, LANES), lambda bi, n: (bi, prev(bi, n), kcol)),
            pl.BlockSpec((None, blk, LANES), lambda bi, n: (bi, n, kcol)),
            pl.BlockSpec((None, blk, LANES), lambda bi, n: (bi, prev(bi, n), vcol)),
            pl.BlockSpec((None, blk, LANES), lambda bi, n: (bi, n, vcol)),
        ],
        out_specs=pl.BlockSpec((None, blk, nq), lambda bi, n: (bi, n, 0)),
        compiler_params=_params("parallel", "arbitrary"),
        name="swa",
    )(sinks, qkv, qkv, qkv, qkv, qkv)


def _tri_inverse_minus_eye(lows, ri, ci):
    same = lambda w: (ri ^ ci) < w
    base = 8
    x = [jnp.where(same(base), -low, 0.0) for low in lows]
    p2 = [_dot_bf(xi, xi) for xi in x]
    e = [xi + pi + _dot_bf(xi, pi) for xi, pi in zip(x, p2)]
    p4 = [_dot_bf(pi, pi) for pi in p2]
    e = [ei + pi + _dot_bf(ei, pi) for ei, pi in zip(e, p4)]
    w = base * 2
    while w <= RWKV_CHUNK:
        off = [jnp.where(same(w) & ~same(w // 2), low, 0.0) for low in lows]
        wm = [oi + _dot_bf(ei, oi) for ei, oi in zip(e, off)]
        e = [ei - wi - _dot_bf(wi, ei) for ei, wi in zip(e, wm)]
        w *= 2
    return e


def _rwkv_kernel(h_ref, mu_ref, w0_ref, w2_ref, a0_ref, a2_ref, g2_ref, kk_ref, ka_ref,
                 rk_ref, lnw_ref, lnb_ref, o_ref, state_ref, last_ref):
    c = RWKV_CHUNK
    tb = h_ref.shape[0]
    nchunk = tb // c
    dim = o_ref.shape[-1]
    npair = dim // LANES

    @pl.when(pl.program_id(1) == 0)
    def _():
        state_ref[...] = jnp.zeros_like(state_ref)
        last_ref[...] = jnp.zeros_like(last_ref)

    h = h_ref[...]
    row = lax.broadcasted_iota(jnp.int32, h.shape, 0)
    shifted = jnp.where(row == 0, last_ref[...], pltpu.roll(h, 1, 0))
    last_ref[...] = h[tb - 1:tb, :]
    hs = h + (shifted - h) * mu_ref[...]
    r = hs[:, 0:dim]
    k = hs[:, dim:2 * dim]
    v = hs[:, 2 * dim:3 * dim]
    xwa = hs[:, 3 * dim:3 * dim + LANES]
    xg = hs[:, 3 * dim + LANES:3 * dim + 2 * LANES]

    wl = w0_ref[...] + _dot_x3(jnp.tanh(xwa), w2_ref[...])
    logw = -jax.nn.sigmoid(wl) * float(np.exp(-0.5))
    a = jax.nn.sigmoid(a0_ref[...] + _dot_x3(xwa, a2_ref[...]))
    gate = _dot_bf(jax.nn.sigmoid(xg), g2_ref[...])

    ri = lax.broadcasted_iota(jnp.int32, (LANES, LANES), 0)
    ci = lax.broadcasted_iota(jnp.int32, (LANES, LANES), 1)
    ones_bd = ((ri ^ ci) < HEAD_DIM).astype(BF16)
    ones_bd2 = jnp.concatenate([ones_bd, ones_bd], axis=0)

    def head_sum(x):
        cols = []
        for j in range(npair):
            hi, lo = _hi_lo(x[:, j * LANES:(j + 1) * LANES])
            cols.append(_dot(jnp.concatenate([hi, lo], axis=1), ones_bd2))
        return jnp.concatenate(cols, axis=1)

    kk = k * kk_ref[...]
    kk = kk / jnp.maximum(jnp.sqrt(head_sum(kk * kk)), L2_EPS)
    k2 = k * (1.0 + (a - 1.0) * ka_ref[...])
    bvec = kk * a

    ti = lax.broadcasted_iota(jnp.int32, (tb, tb), 0)
    si = lax.broadcasted_iota(jnp.int32, (tb, tb), 1)
    tri = (((ti ^ si) < c) & (si <= ti)).astype(BF16)
    cum = _dot(jnp.concatenate([tri] * 3, axis=1), jnp.concatenate(_split3(logw), axis=0))
    e_neg = jnp.exp(-cum)
    alpha = kk * jnp.exp(cum - logw)
    beta = bvec * e_neg
    kappa = k2 * e_neg
    rho = r * jnp.exp(cum)

    same_head = (ri ^ ci) < c
    strict = same_head & (ci < ri)
    incl = same_head & (ci <= ri)
    lo = lax.broadcasted_iota(jnp.int32, (c, LANES), 1) < HEAD_DIM

    def stack(x2):
        return jnp.concatenate([jnp.where(lo, x2, 0.0), jnp.where(lo, 0.0, x2)], axis=0)

    units = [(n, j) for n in range(nchunk) for j in range(npair)]
    blk = lambda x, n, j: x[n * c:(n + 1) * c, j * LANES:(j + 1) * LANES]
    a_s = [stack(blk(alpha, n, j)) for n, j in units]
    rho_s = [stack(blk(rho, n, j)) for n, j in units]
    v_s = [stack(blk(v, n, j)) for n, j in units]
    bk = [jnp.concatenate([blk(beta, n, j)] * 2 + [blk(kappa, n, j)] * 2, axis=0) for n, j in units]
    sc = [_dot_bf(jnp.concatenate([ai, ri_], axis=0), bi, NT) for ai, ri_, bi in zip(a_s, rho_s, bk)]
    l_ab = [jnp.where(strict, s[:LANES, :LANES], 0.0) for s in sc]
    l_ak = [jnp.where(strict, s[:LANES, LANES:], 0.0) for s in sc]
    r_bk = [jnp.concatenate([jnp.where(incl, s[LANES:, :LANES], 0.0),
                             jnp.where(incl, s[LANES:, LANES:], 0.0)], axis=1) for s in sc]
    e_inv = _tri_inverse_minus_eye(l_ab, ri, ci)
    lkv = [_dot_bf(li, vi) for li, vi in zip(l_ak, v_s)]

    ys = []
    for n in range(nchunk):
        cum_n = cum[n * c:(n + 1) * c, :]
        cum_end = cum_n[c - 1:c, :]
        to_end = jnp.exp(cum_end - cum_n)
        beta_e = bvec[n * c:(n + 1) * c, :] * to_end
        kappa_e = k2[n * c:(n + 1) * c, :] * to_end
        w_end = jnp.exp(cum_end)
        idx = [n * npair + j for j in range(npair)]
        s0 = [state_ref[j] for j in range(npair)]
        rhs = [_dot_x3(a_s[i], s0[j], NT) + lkv[i] for j, i in enumerate(idx)]
        u = [-(x + _dot_bf(e_inv[i], x)) for x, i in zip(rhs, idx)]
        y = [_dot_x3(rho_s[i], s0[j], NT) + _dot_bf(r_bk[i], jnp.concatenate([u[j], v_s[i]], axis=0))
             for j, i in enumerate(idx)]
        ys.append(jnp.concatenate([yi[:c] + yi[c:] for yi in y], axis=1))
        for j, i in enumerate(idx):
            sl = slice(j * LANES, (j + 1) * LANES)
            ends = jnp.concatenate([stack(beta_e[:, sl]), stack(kappa_e[:, sl])], axis=0)
            state_ref[j] = s0[j] * w_end[:, sl] + _dot_x3(jnp.concatenate([u[j], v_s[i]], axis=0), ends, TN)
    y = jnp.concatenate(ys, axis=0)

    mean = head_sum(y) * (1.0 / HEAD_DIM)
    d = y - mean
    var = head_sum(d * d) * (1.0 / HEAD_DIM)
    y = d * lax.rsqrt(var + GN_EPS) * lnw_ref[...] + lnb_ref[...]
    y = y + head_sum(r * k2 * rk_ref[...]) * v
    o_ref[...] = y * gate


def _rwkv(hb, mu, w0, w2p, a0, a2p, g2, k_k, k_a, r_k, ln_w, ln_b, *, tb=2 * RWKV_CHUNK):
    b, s, cols = hb.shape
    dim = w0.shape[-1]
    c = tb
    row = lambda x: x.reshape(1, -1)
    vec = lambda n: pl.BlockSpec((1, n), lambda bi, t: (0, 0))
    mat = lambda m: pl.BlockSpec(m.shape, lambda bi, t: (0, 0))
    return pl.pallas_call(
        _rwkv_kernel,
        out_shape=jax.ShapeDtypeStruct((b, s, dim), F32),
        grid=(b, s // c),
        in_specs=[
            pl.BlockSpec((None, c, cols), lambda bi, t: (bi, t, 0)),
            vec(cols), vec(dim), mat(w2p), vec(dim), mat(a2p), mat(g2),
            vec(dim), vec(dim), vec(dim), vec(dim), vec(dim),
        ],
        out_specs=pl.BlockSpec((None, c, dim), lambda bi, t: (bi, t, 0)),
        scratch_shapes=[pltpu.VMEM((dim // LANES, LANES, LANES), F32), pltpu.VMEM((1, cols), F32)],
        compiler_params=_params("parallel", "arbitrary"),
        name="rwkv7",
    )(hb, row(mu), row(w0), w2p, row(a0), a2p, g2, row(k_k), row(k_a), row(r_k), row(ln_w), row(ln_b))


FOX_AUG = 3
LOG2E = float(np.log2(np.e))


def _fox_prep_kernel(q_ref, k_ref, v_ref, fz_ref, bf_ref, sel_ref, qa_ref, ka_ref, va_ref, carry_ref):
    tc = q_ref.shape[0]
    nheads = qa_ref.shape[-1] // LANES
    scale = HEAD_DIM ** -0.5 * LOG2E

    @pl.when(pl.program_id(1) == 0)
    def _():
        carry_ref[...] = jnp.zeros_like(carry_ref)

    z = fz_ref[...] + bf_ref[...]
    logf = jnp.minimum(z, 0.0) - jnp.log(1.0 + jnp.exp(-jnp.abs(z)))
    lane = lax.broadcasted_iota(jnp.int32, z.shape, 1)
    logf = jnp.where(lane < nheads, logf, 0.0)
    ti = lax.broadcasted_iota(jnp.int32, (tc, tc), 0)
    si = lax.broadcasted_iota(jnp.int32, (tc, tc), 1)
    cg = _sel_dot((si <= ti).astype(BF16), logf) + carry_ref[...]
    carry_ref[...] = cg[tc - 1:tc, :]

    hi, mid, low = _split3(cg * LOG2E)
    q_c = _dot(hi, sel_ref[0]) + _dot(mid, sel_ref[1]) + _dot(low, sel_ref[2])
    k_c = -(_dot(hi, sel_ref[3]) + _dot(mid, sel_ref[4]) + _dot(low, sel_ref[5]))

    lane = lax.broadcasted_iota(jnp.int32, (tc, LANES), 1)
    lo = lane < HEAD_DIM
    q_one = ((lane >= HEAD_DIM + FOX_AUG) & (lane < HEAD_DIM + 2 * FOX_AUG)).astype(F32)
    k_one = ((lane >= HEAD_DIM) & (lane < HEAD_DIM + FOX_AUG)).astype(F32)
    v_one = (lane == HEAD_DIM).astype(F32)
    sources = ((q_ref, q_c, q_one, qa_ref), (k_ref, k_c, k_one, ka_ref), (v_ref, None, v_one, va_ref))
    for j in range(nheads // 2):
        sl = slice(j * LANES, (j + 1) * LANES)
        for x_ref, c_aug, one, o_ref in sources:
            x = x_ref[:, sl]
            if x_ref is q_ref:
                x = x * scale
            xr = pltpu.roll(x, HEAD_DIM, 1)
            for e, xe in ((0, x), (1, xr)):
                hs = slice((2 * j + e) * LANES, (2 * j + e + 1) * LANES)
                fill = one if c_aug is None else c_aug[:, hs] + one
                o_ref[:, hs] = jnp.where(lo, xe, fill).astype(BF16)


def _fox_select_matrices(nheads):
    sel = np.zeros((2 * FOX_AUG, LANES, nheads * LANES), np.float32)
    for h in range(nheads):
        for i in range(FOX_AUG):
            sel[i, h, h * LANES + HEAD_DIM + i] = 1.0
            sel[FOX_AUG + i, h, h * LANES + HEAD_DIM + FOX_AUG + i] = 1.0
    return jnp.asarray(sel, BF16)


def _fox_prep(proj, b_f, nheads, *, tc=256):
    b, s, _ = proj.shape
    dim = nheads * HEAD_DIM
    ncol = dim // LANES
    bf = jnp.zeros((1, LANES), F32).at[0, :nheads].set(b_f)
    sel = _fox_select_matrices(nheads)
    wide = nheads * LANES
    return pl.pallas_call(
        _fox_prep_kernel,
        out_shape=(jax.ShapeDtypeStruct((b, s, wide), BF16),) * 3,
        grid=(b, s // tc),
        in_specs=[
            pl.BlockSpec((None, tc, dim), lambda bi, t: (bi, t, 0)),
            pl.BlockSpec((None, tc, dim), lambda bi, t: (bi, t, 1)),
            pl.BlockSpec((None, tc, dim), lambda bi, t: (bi, t, 2)),
            pl.BlockSpec((None, tc, LANES), lambda bi, t: (bi, t, 3 * ncol)),
            pl.BlockSpec((1, LANES), lambda bi, t: (0, 0)),
            pl.BlockSpec(sel.shape, lambda bi, t: (0, 0, 0)),
        ],
        out_specs=(pl.BlockSpec((None, tc, wide), lambda bi, t: (bi, t, 0)),) * 3,
        scratch_shapes=[pltpu.VMEM((1, LANES), F32)],
        compiler_params=_params("parallel", "arbitrary"),
        name="fox_prep",
    )(proj, proj, proj, proj, bf, sel)


def _fox_attn_kernel(q_ref, k_ref, v_ref, o_ref, m_ref, acc_ref):
    tq = q_ref.shape[0]
    qi = pl.program_id(2)
    m_ref[...] = jnp.full_like(m_ref, -jnp.inf)
    acc_ref[...] = jnp.zeros_like(acc_ref)
    rows = lax.broadcasted_iota(jnp.int32, (tq, tq), 0)
    cols = lax.broadcasted_iota(jnp.int32, (tq, tq), 1)

    def block(kb, diagonal):
        start = pl.multiple_of(kb * tq, tq)
        for e in range(2):
            hs = slice(e * LANES, (e + 1) * LANES)
            s = _dot(q_ref[:, hs], k_ref[pl.ds(start, tq), hs], NT)
            if diagonal:
                s = jnp.where(cols <= rows, s, -jnp.inf)
            chunks = [s[:, i * LANES:(i + 1) * LANES] for i in range(tq // LANES)]
            m_prev = m_ref[e]
            blk_max = functools.reduce(jnp.maximum, chunks)
            m_new = jnp.maximum(m_prev, jnp.max(blk_max, axis=-1, keepdims=True))
            p = jnp.concatenate([jnp.exp2(ch - m_new) for ch in chunks], axis=1).astype(BF16)
            acc_ref[e] = jnp.exp2(m_prev - m_new) * acc_ref[e] + _dot(p, v_ref[pl.ds(start, tq), hs])
            m_ref[e] = m_new

    def body(kb, carry):
        block(kb, False)
        return carry

    lax.fori_loop(0, qi, body, 0)
    block(qi, True)
    outs = []
    for e in range(2):
        acc = acc_ref[e]
        outs.append(acc / acc[:, HEAD_DIM:HEAD_DIM + 1])
    lo = lax.broadcasted_iota(jnp.int32, (tq, LANES), 1) < HEAD_DIM
    o_ref[...] = jnp.where(lo, outs[0], pltpu.roll(outs[1], HEAD_DIM, 1))


def _fox_attn(q_aug, k_aug, v_aug, *, tq=512):
    b, s, wide = q_aug.shape
    npair = wide // (2 * LANES)
    return pl.pallas_call(
        _fox_attn_kernel,
        out_shape=jax.ShapeDtypeStruct((b, s, npair * LANES), F32),
        grid=(b, npair, s // tq),
        in_specs=[
            pl.BlockSpec((None, tq, 2 * LANES), lambda bi, j, qi: (bi, qi, j)),
            pl.BlockSpec((None, s, 2 * LANES), lambda bi, j, qi: (bi, 0, j)),
            pl.BlockSpec((None, s, 2 * LANES), lambda bi, j, qi: (bi, 0, j)),
        ],
        out_specs=pl.BlockSpec((None, tq, LANES), lambda bi, j, qi: (bi, qi, j)),
        scratch_shapes=[pltpu.VMEM((2, tq, LANES), F32), pltpu.VMEM((2, tq, LANES), F32)],
        compiler_params=_params("parallel", "parallel", "arbitrary"),
        name="fox_attn",
    )(q_aug, k_aug, v_aug)


def kernel(x, p, ffn1_norm, ffn1_w_gu, ffn1_w_down, mix_norm, ffn2_norm, ffn2_w_gu, ffn2_w_down, ple_norm, ple_w_gate, ple_w_proj, even_w_in, even_w_out, swa_sinks, rwkv_mu, rwkv_w0, rwkv_w2, rwkv_a0, rwkv_a2, rwkv_g2, rwkv_k_k, rwkv_k_a, rwkv_r_k, rwkv_ln_w, rwkv_ln_b, fox_w_in, fox_b_f, fox_w_out, final_norm):
    b, s, d = x.shape
    depth = p.shape[0]
    t = b * s
    bf = lambda w: w.astype(BF16)
    swa_q = SWA_HEADS * HEAD_DIM
    swa_cols = swa_q + 2 * (SWA_HEADS // SWA_GROUP) * HEAD_DIM
    rwkv_dim = rwkv_w0.shape[-1]
    lora = rwkv_w2.shape[1]
    fox_heads = fox_b_f.shape[-1]
    fox_dim = fox_heads * HEAD_DIM

    x = x.reshape(t, d)
    for i in range(depth):
        j = i // 2
        x = _ffn(x, ffn1_norm[i], bf(ffn1_w_gu[i]), bf(ffn1_w_down[i]))
        if i % 2 == 0:
            w_in = even_w_in[j]
            qkv = _rms_matmul(x, mix_norm[i], bf(w_in[:, :swa_cols]))
            hb = _rms_matmul(x, mix_norm[i], bf(w_in[:, swa_cols:]))
            ya = _swa(qkv.reshape(b, s, swa_cols), swa_sinks[j])
            zeros = jnp.zeros((lora, rwkv_dim), F32)
            w2p = jnp.concatenate([rwkv_w2[j], zeros], axis=0)
            a2p = jnp.concatenate([zeros, rwkv_a2[j]], axis=0)
            yb = _rwkv(hb.reshape(b, s, -1), rwkv_mu[j], rwkv_w0[j], w2p, rwkv_a0[j], a2p,
                       bf(rwkv_g2[j]), rwkv_k_k[j], rwkv_k_a[j], rwkv_r_k[j].reshape(-1),
                       rwkv_ln_w[j], rwkv_ln_b[j])
            w_out = bf(even_w_out[j])
            x = _proj_res([ya.reshape(t, swa_q), yb.reshape(t, rwkv_dim)],
                          [w_out[:swa_q], w_out[swa_q:]], x)
        else:
            w_in = jnp.pad(fox_w_in[j], ((0, 0), (0, LANES - fox_heads)))
            proj = _rms_matmul(x, mix_norm[i], bf(w_in), tn=640)
            q_aug, k_aug, v = _fox_prep(proj.reshape(b, s, -1), fox_b_f[j], fox_heads)
            yc = _fox_attn(q_aug, k_aug, v)
            x = _proj_res([yc.reshape(t, fox_dim)], [bf(fox_w_out[j])], x)
        x = _ffn(x, ffn2_norm[i], bf(ffn2_w_gu[i]), bf(ffn2_w_down[i]))
        x = _ple(x, ple_norm[i], bf(ple_w_gate[i]), p[i].reshape(t, -1), bf(ple_w_proj[i]),
                 final_norm, final=(i == depth - 1))
    return x.reshape(b, s, d)
```

```python
import functools

import jax
import jax.numpy as jnp
import numpy as np
from jax import lax
from jax.experimental import pallas as pl
from jax.experimental.pallas import tpu as pltpu

F32 = jnp.float32
BF16 = jnp.bfloat16

LANES = 128
HEAD_DIM = 64
SWA_HEADS = 8
SWA_GROUP = 4
SWA_BLOCK = 128
RWKV_CHUNK = 64
NORM_EPS = 1e-6
GN_EPS = 64e-5
L2_EPS = 1e-12
VMEM_LIMIT = 56 * 1024 * 1024

NN = (((1,), (0,)), ((), ()))
NT = (((1,), (1,)), ((), ()))
TN = (((0,), (0,)), ((), ()))


def _dot(a, b, dims=NN):
    return lax.dot_general(a, b, dims, preferred_element_type=F32)


def _dot_bf(a, b, dims=NN):
    return _dot(a.astype(BF16), b.astype(BF16), dims)


def _hi_lo(x):
    hi = x.astype(BF16)
    lo = (x - hi.astype(F32)).astype(BF16)
    return hi, lo


def _dot_x3(a, b, dims=NN):
    ah, al = _hi_lo(a)
    bh, bl = _hi_lo(b)
    return _dot(ah, bh, dims) + (_dot(ah, bl, dims) + _dot(al, bh, dims))


def _split3(x):
    hi = x.astype(BF16)
    r1 = x - hi.astype(F32)
    mid = r1.astype(BF16)
    lo = (r1 - mid.astype(F32)).astype(BF16)
    return hi, mid, lo


def _sel_dot(sel, x):
    hi, mid, lo = _split3(x)
    return _dot(sel, hi) + (_dot(sel, mid) + _dot(sel, lo))


def _dot_sel(x, sel):
    hi, mid, lo = _split3(x)
    return _dot(hi, sel) + (_dot(mid, sel) + _dot(lo, sel))


def _rms(x, g):
    ms = jnp.mean(x * x, axis=-1, keepdims=True)
    return x * lax.rsqrt(ms + NORM_EPS) * g


def _params(*sem):
    return pltpu.CompilerParams(dimension_semantics=sem, vmem_limit_bytes=VMEM_LIMIT)


def _ffn_kernel(x_ref, g_ref, wg_ref, wu_ref, wd_ref, o_ref):
    x = x_ref[...]
    hn = _rms(x, g_ref[...]).astype(BF16)
    gate = _dot(hn, wg_ref[...])
    up = _dot(hn, wu_ref[...])
    act = (gate * jax.nn.sigmoid(gate) * up).astype(BF16)
    o_ref[...] = x + 0.5 * _dot(act, wd_ref[...])


def _ffn(x, g, w_gu, w_down, *, tm=512):
    t, d = x.shape
    dff = w_down.shape[0]
    resident = pl.Buffered(1)
    return pl.pallas_call(
        _ffn_kernel,
        out_shape=jax.ShapeDtypeStruct((t, d), F32),
        grid=(t // tm,),
        in_specs=[
            pl.BlockSpec((tm, d), lambda i: (i, 0)),
            pl.BlockSpec((1, d), lambda i: (0, 0)),
            pl.BlockSpec((d, dff), lambda i: (0, 0), pipeline_mode=resident),
            pl.BlockSpec((d, dff), lambda i: (0, 1), pipeline_mode=resident),
            pl.BlockSpec((dff, d), lambda i: (0, 0), pipeline_mode=resident),
        ],
        out_specs=pl.BlockSpec((tm, d), lambda i: (i, 0)),
        compiler_params=_params("parallel"),
        name="ffn",
    )(x, g.reshape(1, d), w_gu, w_gu, w_down)


def _rms_matmul_kernel(x_ref, g_ref, w_ref, o_ref, hn_ref):
    @pl.when(pl.program_id(1) == 0)
    def _():
        hn_ref[...] = _rms(x_ref[...], g_ref[...]).astype(BF16)

    o_ref[...] = _dot(hn_ref[...], w_ref[...]).astype(o_ref.dtype)


def _rms_matmul(x, g, w, *, tm=512, tn=None, out_dtype=F32):
    t, d = x.shape
    n = w.shape[1]
    tn = n if tn is None else tn
    return pl.pallas_call(
        _rms_matmul_kernel,
        out_shape=jax.ShapeDtypeStruct((t, n), out_dtype),
        grid=(t // tm, n // tn),
        in_specs=[
            pl.BlockSpec((tm, d), lambda i, j: (i, 0)),
            pl.BlockSpec((1, d), lambda i, j: (0, 0)),
            pl.BlockSpec((d, tn), lambda i, j: (0, j)),
        ],
        out_specs=pl.BlockSpec((tm, tn), lambda i, j: (i, j)),
        scratch_shapes=[pltpu.VMEM((tm, d), BF16)],
        compiler_params=_params("parallel", "arbitrary"),
        name="rms_matmul",
    )(x, g.reshape(1, d), w)


def _proj_res_kernel(n_in, *refs):
    a_refs = refs[:n_in]
    w_refs = refs[n_in:2 * n_in]
    x_ref, o_ref = refs[2 * n_in], refs[2 * n_in + 1]
    acc = x_ref[...]
    for a_ref, w_ref in zip(a_refs, w_refs):
        acc = acc + _dot(a_ref[...].astype(BF16), w_ref[...])
    o_ref[...] = acc


def _proj_res(a_list, w_list, x, *, tm=512):
    t, d = x.shape
    n_in = len(a_list)
    in_specs = [pl.BlockSpec((tm, a.shape[1]), lambda i: (i, 0)) for a in a_list]
    in_specs += [pl.BlockSpec(w.shape, lambda i: (0, 0)) for w in w_list]
    in_specs += [pl.BlockSpec((tm, d), lambda i: (i, 0))]
    return pl.pallas_call(
        functools.partial(_proj_res_kernel, n_in),
        out_shape=jax.ShapeDtypeStruct((t, d), F32),
        grid=(t // tm,),
        in_specs=in_specs,
        out_specs=pl.BlockSpec((tm, d), lambda i: (i, 0)),
        compiler_params=_params("parallel"),
        name="proj_res",
    )(*a_list, *w_list, x)


def _ple_kernel(final, x_ref, g_ref, wg_ref, p_ref, wp_ref, fn_ref, o_ref):
    x = x_ref[...]
    hn = _rms(x, g_ref[...]).astype(BF16)
    gate = jax.nn.sigmoid(_dot(hn, wg_ref[...]))
    y = x + gate * _dot(p_ref[...].astype(BF16), wp_ref[...])
    if final:
        y = _rms(y, fn_ref[...])
    o_ref[...] = y


def _ple(x, g, w_gate, p, w_proj, final_g, *, final, tm=512):
    t, d = x.shape
    pd = p.shape[1]
    return pl.pallas_call(
        functools.partial(_ple_kernel, final),
        out_shape=jax.ShapeDtypeStruct((t, d), F32),
        grid=(t // tm,),
        in_specs=[
            pl.BlockSpec((tm, d), lambda i: (i, 0)),
            pl.BlockSpec((1, d), lambda i: (0, 0)),
            pl.BlockSpec((d, d), lambda i: (0, 0)),
            pl.BlockSpec((tm, pd), lambda i: (i, 0)),
            pl.BlockSpec((pd, d), lambda i: (0, 0)),
            pl.BlockSpec((1, d), lambda i: (0, 0)),
        ],
        out_specs=pl.BlockSpec((tm, d), lambda i: (i, 0)),
        compiler_params=_params("parallel"),
        name="ple",
    )(x, g.reshape(1, d), w_gate, p, w_proj, final_g.reshape(1, d))


def _swa_kernel(sink_ref, q_ref, kp_ref, kc_ref, vp_ref, vc_ref, o_ref):
    n = pl.program_id(1)
    blk = SWA_BLOCK
    scale = HEAD_DIM ** -0.5
    k = jnp.concatenate([kp_ref[...], kc_ref[...]], axis=0)
    v = jnp.concatenate([vp_ref[...], vc_ref[...]], axis=0)
    kr = pltpu.roll(k, HEAD_DIM, 1)
    vr = pltpu.roll(v, HEAD_DIM, 1)
    lo_kv = lax.broadcasted_iota(jnp.int32, k.shape, 1) < HEAD_DIM
    kdup = [jnp.where(lo_kv, k, kr).astype(BF16), jnp.where(lo_kv, kr, k).astype(BF16)]
    vdup = [jnp.where(lo_kv, v, vr).astype(BF16), jnp.where(lo_kv, vr, v).astype(BF16)]

    qi = lax.broadcasted_iota(jnp.int32, (blk, 2 * blk), 0)
    ki = lax.broadcasted_iota(jnp.int32, (blk, 2 * blk), 1)
    dist = qi + blk - ki
    valid = (dist >= 0) & (dist < blk) & ((n > 0) | (ki >= blk))
    distf = dist.astype(F32)
    lo_q = lax.broadcasted_iota(jnp.int32, (blk, LANES), 1) < HEAD_DIM

    for j in range(SWA_HEADS // 2):
        g = (2 * j) // SWA_GROUP
        q2 = q_ref[:, j * LANES:(j + 1) * LANES] * scale
        outs = []
        for e in range(2):
            h = 2 * j + e
            qm = jnp.where(lo_q if e == 0 else ~lo_q, q2, 0.0).astype(BF16)
            s = _dot(qm, kdup[g], NT)
            slope = 2.0 ** (-8.0 * (h + 1) / SWA_HEADS)
            s = jnp.where(valid, s - slope * distf, -jnp.inf)
            sink = sink_ref[h]
            m = jnp.maximum(jnp.max(s, axis=-1, keepdims=True), sink)
            p = jnp.exp(s - m)
            denom = jnp.sum(p, axis=-1, keepdims=True) + jnp.exp(sink - m)
            p = p / denom
            outs.append(_dot(p.astype(BF16), vdup[g]))
        o_ref[:, j * LANES:(j + 1) * LANES] = jnp.where(lo_q, outs[0], outs[1])


def _swa(qkv, sinks):
    b, s, _ = qkv.shape
    blk = SWA_BLOCK
    nq = SWA_HEADS * HEAD_DIM
    kcol = nq // LANES
    vcol = kcol + 1
    prev = lambda bi, n: jnp.maximum(n - 1, 0)
    return pl.pallas_call(
        _swa_kernel,
        out_shape=jax.ShapeDtypeStruct((b, s, nq), F32),
        grid=(b, s // blk),
        in_specs=[
            pl.BlockSpec(memory_space=pltpu.SMEM),
            pl.BlockSpec((None, blk, nq), lambda bi, n: (bi, n, 0)),
            pl.BlockSpec((None, blk, LANES), lambda bi, n: (bi, prev(bi, n), kcol)),
            pl.BlockSpec((None, blk, LANES), lambda bi, n: (bi, n, kcol)),
            pl.BlockSpec((None, blk, LANES), lambda bi, n: (bi, prev(bi, n), vcol)),
            pl.BlockSpec((None, blk, LANES), lambda bi, n: (bi, n, vcol)),
        ],
        out_specs=pl.BlockSpec((None, blk, nq), lambda bi, n: (bi, n, 0)),
        compiler_params=_params("parallel", "arbitrary"),
        name="swa",
    )(sinks, qkv, qkv, qkv, qkv, qkv)


def _tri_inverse_minus_eye(lows, ri, ci):
    same = lambda w: (ri ^ ci) < w
    base = 8
    x = [jnp.where(same(base), -low, 0.0) for low in lows]
    p2 = [_dot_bf(xi, xi) for xi in x]
    e = [xi + pi + _dot_bf(xi, pi) for xi, pi in zip(x, p2)]
    p4 = [_dot_bf(pi, pi) for pi in p2]
    e = [ei + pi + _dot_bf(ei, pi) for ei, pi in zip(e, p4)]
    w = base * 2
    while w <= RWKV_CHUNK:
        off = [jnp.where(same(w) & ~same(w // 2), low, 0.0) for low in lows]
        wm = [oi + _dot_bf(ei, oi) for ei, oi in zip(e, off)]
        e = [ei - wi - _dot_bf(wi, ei) for ei, wi in zip(e, wm)]
        w *= 2
    return e


def _rwkv_kernel(h_ref, mu_ref, w0_ref, w2_ref, a0_ref, a2_ref, g2_ref, kk_ref, ka_ref,
                 rk_ref, lnw_ref, lnb_ref, o_ref, state_ref, last_ref):
    c = RWKV_CHUNK
    tb = h_ref.shape[0]
    nchunk = tb // c
    dim = o_ref.shape[-1]
    npair = dim // LANES

    @pl.when(pl.program_id(1) == 0)
    def _():
        state_ref[...] = jnp.zeros_like(state_ref)
        last_ref[...] = jnp.zeros_like(last_ref)

    h = h_ref[...]
    row = lax.broadcasted_iota(jnp.int32, h.shape, 0)
    shifted = jnp.where(row == 0, last_ref[...], pltpu.roll(h, 1, 0))
    last_ref[...] = h[tb - 1:tb, :]
    hs = h + (shifted - h) * mu_ref[...]
    r = hs[:, 0:dim]
    k = hs[:, dim:2 * dim]
    v = hs[:, 2 * dim:3 * dim]
    xwa = hs[:, 3 * dim:3 * dim + LANES]
    xg = hs[:, 3 * dim + LANES:3 * dim + 2 * LANES]

    wl = w0_ref[...] + _dot_x3(jnp.tanh(xwa), w2_ref[...])
    logw = -jax.nn.sigmoid(wl) * float(np.exp(-0.5))
    a = jax.nn.sigmoid(a0_ref[...] + _dot_x3(xwa, a2_ref[...]))
    gate = _dot_bf(jax.nn.sigmoid(xg), g2_ref[...])

    ri = lax.broadcasted_iota(jnp.int32, (LANES, LANES), 0)
    ci = lax.broadcasted_iota(jnp.int32, (LANES, LANES), 1)
    ones_bd = ((ri ^ ci) < HEAD_DIM).astype(BF16)
    ones_bd2 = jnp.concatenate([ones_bd, ones_bd], axis=0)

    def head_sum(x):
        cols = []
        for j in range(npair):
            hi, lo = _hi_lo(x[:, j * LANES:(j + 1) * LANES])
            cols.append(_dot(jnp.concatenate([hi, lo], axis=1), ones_bd2))
        return jnp.concatenate(cols, axis=1)

    kk = k * kk_ref[...]
    kk = kk / jnp.maximum(jnp.sqrt(head_sum(kk * kk)), L2_EPS)
    k2 = k * (1.0 + (a - 1.0) * ka_ref[...])
    bvec = kk * a

    ti = lax.broadcasted_iota(jnp.int32, (tb, tb), 0)
    si = lax.broadcasted_iota(jnp.int32, (tb, tb), 1)
    tri = (((ti ^ si) < c) & (si <= ti)).astype(BF16)
    cum = _dot(jnp.concatenate([tri] * 3, axis=1), jnp.concatenate(_split3(logw), axis=0))
    e_neg = jnp.exp(-cum)
    alpha = kk * jnp.exp(cum - logw)
    beta = bvec * e_neg
    kappa = k2 * e_neg
    rho = r * jnp.exp(cum)

    same_head = (ri ^ ci) < c
    strict = same_head & (ci < ri)
    incl = same_head & (ci <= ri)
    lo = lax.broadcasted_iota(jnp.int32, (c, LANES), 1) < HEAD_DIM

    def stack(x2):
        return jnp.concatenate([jnp.where(lo, x2, 0.0), jnp.where(lo, 0.0, x2)], axis=0)

    units = [(n, j) for n in range(nchunk) for j in range(npair)]
    blk = lambda x, n, j: x[n * c:(n + 1) * c, j * LANES:(j + 1) * LANES]
    a_s = [stack(blk(alpha, n, j)) for n, j in units]
    rho_s = [stack(blk(rho, n, j)) for n, j in units]
    v_s = [stack(blk(v, n, j)) for n, j in units]
    bk = [jnp.concatenate([blk(beta, n, j)] * 2 + [blk(kappa, n, j)] * 2, axis=0) for n, j in units]
    sc = [_dot_bf(jnp.concatenate([ai, ri_], axis=0), bi, NT) for ai, ri_, bi in zip(a_s, rho_s, bk)]
    l_ab = [jnp.where(strict, s[:LANES, :LANES], 0.0) for s in sc]
    l_ak = [jnp.where(strict, s[:LANES, LANES:], 0.0) for s in sc]
    r_bk = [jnp.concatenate([jnp.where(incl, s[LANES:, :LANES], 0.0),
                             jnp.where(incl, s[LANES:, LANES:], 0.0)], axis=1) for s in sc]
    e_inv = _tri_inverse_minus_eye(l_ab, ri, ci)
    lkv = [_dot_bf(li, vi) for li, vi in zip(l_ak, v_s)]

    ys = []
    for n in range(nchunk):
        cum_n = cum[n * c:(n + 1) * c, :]
        cum_end = cum_n[c - 1:c, :]
        to_end = jnp.exp(cum_end - cum_n)
        beta_e = bvec[n * c:(n + 1) * c, :] * to_end
        kappa_e = k2[n * c:(n + 1) * c, :] * to_end
        w_end = jnp.exp(cum_end)
        idx = [n * npair + j for j in range(npair)]
        s0 = [state_ref[j] for j in range(npair)]
        rhs = [_dot_x3(a_s[i], s0[j], NT) + lkv[i] for j, i in enumerate(idx)]
        u = [-(x + _dot_bf(e_inv[i], x)) for x, i in zip(rhs, idx)]
        y = [_dot_x3(rho_s[i], s0[j], NT) + _dot_bf(r_bk[i], jnp.concatenate([u[j], v_s[i]], axis=0))
             for j, i in enumerate(idx)]
        ys.append(jnp.concatenate([yi[:c] + yi[c:] for yi in y], axis=1))
        for j, i in enumerate(idx):
            sl = slice(j * LANES, (j + 1) * LANES)
            ends = jnp.concatenate([stack(beta_e[:, sl]), stack(kappa_e[:, sl])], axis=0)
            state_ref[j] = s0[j] * w_end[:, sl] + _dot_x3(jnp.concatenate([u[j], v_s[i]], axis=0), ends, TN)
    y = jnp.concatenate(ys, axis=0)

    mean = head_sum(y) * (1.0 / HEAD_DIM)
    d = y - mean
    var = head_sum(d * d) * (1.0 / HEAD_DIM)
    y = d * lax.rsqrt(var + GN_EPS) * lnw_ref[...] + lnb_ref[...]
    y = y + head_sum(r * k2 * rk_ref[...]) * v
    o_ref[...] = y * gate


def _rwkv(hb, mu, w0, w2p, a0, a2p, g2, k_k, k_a, r_k, ln_w, ln_b, *, tb=2 * RWKV_CHUNK):
    b, s, cols = hb.shape
    dim = w0.shape[-1]
    c = tb
    row = lambda x: x.reshape(1, -1)
    vec = lambda n: pl.BlockSpec((1, n), lambda bi, t: (0, 0))
    mat = lambda m: pl.BlockSpec(m.shape, lambda bi, t: (0, 0))
    return pl.pallas_call(
        _rwkv_kernel,
        out_shape=jax.ShapeDtypeStruct((b, s, dim), F32),
        grid=(b, s // c),
        in_specs=[
            pl.BlockSpec((None, c, cols), lambda bi, t: (bi, t, 0)),
            vec(cols), vec(dim), mat(w2p), vec(dim), mat(a2p), mat(g2),
            vec(dim), vec(dim), vec(dim), vec(dim), vec(dim),
        ],
        out_specs=pl.BlockSpec((None, c, dim), lambda bi, t: (bi, t, 0)),
        scratch_shapes=[pltpu.VMEM((dim // LANES, LANES, LANES), F32), pltpu.VMEM((1, cols), F32)],
        compiler_params=_params("parallel", "arbitrary"),
        name="rwkv7",
    )(hb, row(mu), row(w0), w2p, row(a0), a2p, g2, row(k_k), row(k_a), row(r_k), row(ln_w), row(ln_b))


FOX_AUG = 3
FOX_BLOCK = 512
LOG2E = float(np.log2(np.e))


def _fox_in_kernel(x_ref, g_ref, w_ref, bf_ref, sel_ref, qa_ref, ka_ref, va_ref, carry_ref):
    tc = x_ref.shape[0]
    nheads = qa_ref.shape[-1] // LANES
    dim = nheads * HEAD_DIM
    scale = HEAD_DIM ** -0.5 * LOG2E

    @pl.when(pl.program_id(1) == 0)
    def _():
        carry_ref[...] = jnp.zeros_like(carry_ref)

    proj = _dot(_rms(x_ref[...], g_ref[...]).astype(BF16), w_ref[...])
    z = proj[:, 3 * dim:] + bf_ref[...]
    logf = jnp.minimum(z, 0.0) - jnp.log(1.0 + jnp.exp(-jnp.abs(z)))
    lane = lax.broadcasted_iota(jnp.int32, z.shape, 1)
    logf = jnp.where(lane < nheads, logf, 0.0)
    ti = lax.broadcasted_iota(jnp.int32, (tc, tc), 0)
    si = lax.broadcasted_iota(jnp.int32, (tc, tc), 1)
    cg = _sel_dot((si <= ti).astype(BF16), logf) + carry_ref[...]
    carry_ref[...] = cg[tc - 1:tc, :]

    hi, mid, low = _split3(cg * LOG2E)
    pieces = (hi.astype(F32) + pltpu.roll(mid.astype(F32), nheads, 1)
              + pltpu.roll(low.astype(F32), 2 * nheads, 1)).astype(BF16)
    q_c = _dot(pieces, sel_ref[0])
    k_c = -_dot(pieces, sel_ref[1])

    lane = lax.broadcasted_iota(jnp.int32, (tc, LANES), 1)
    lo = lane < HEAD_DIM
    q_one = ((lane >= HEAD_DIM + FOX_AUG) & (lane < HEAD_DIM + 2 * FOX_AUG)).astype(F32)
    k_one = ((lane >= HEAD_DIM) & (lane < HEAD_DIM + FOX_AUG)).astype(F32)
    v_one = (lane == HEAD_DIM).astype(F32)
    sources = ((0, q_c, q_one, qa_ref), (1, k_c, k_one, ka_ref), (2, None, v_one, va_ref))
    for j in range(nheads // 2):
        for part, c_aug, one, o_ref in sources:
            x = proj[:, part * dim + j * LANES:part * dim + (j + 1) * LANES]
            if part == 0:
                x = x * scale
            xr = pltpu.roll(x, HEAD_DIM, 1)
            for e, xe in ((0, x), (1, xr)):
                hs = slice((2 * j + e) * LANES, (2 * j + e + 1) * LANES)
                if c_aug is None:
                    o_ref[2 * j + e, 0] = jnp.where(lo, xe, one).T.astype(BF16)
                else:
                    o_ref[:, hs] = jnp.where(lo, xe, c_aug[:, hs] + one).astype(BF16)


def _fox_select_matrices(nheads):
    sel = np.zeros((2, LANES, nheads * LANES), np.float32)
    for h in range(nheads):
        for i in range(FOX_AUG):
            sel[0, i * nheads + h, h * LANES + HEAD_DIM + i] = 1.0
            sel[1, i * nheads + h, h * LANES + HEAD_DIM + FOX_AUG + i] = 1.0
    return jnp.asarray(sel, BF16)


def _fox_in(x, g, w_in, b_f, *, batch, tc):
    t, d = x.shape
    s = t // batch
    nheads = b_f.shape[-1]
    assert FOX_AUG * nheads <= LANES
    w = jnp.pad(w_in, ((0, 0), (0, LANES - nheads))).astype(BF16)
    bf = jnp.zeros((1, LANES), F32).at[0, :nheads].set(b_f)
    sel = _fox_select_matrices(nheads)
    wide = nheads * LANES
    nt = s // tc
    resident = pl.Buffered(1)
    return pl.pallas_call(
        _fox_in_kernel,
        out_shape=(jax.ShapeDtypeStruct((batch, s, wide), BF16),
                   jax.ShapeDtypeStruct((batch, s, wide), BF16),
                   jax.ShapeDtypeStruct((batch, nheads, nt, LANES, tc), BF16)),
        grid=(batch, nt),
        in_specs=[
            pl.BlockSpec((tc, d), lambda bi, ti: (bi * nt + ti, 0)),
            pl.BlockSpec((1, d), lambda bi, ti: (0, 0)),
            pl.BlockSpec(w.shape, lambda bi, ti: (0, 0), pipeline_mode=resident),
            pl.BlockSpec((1, LANES), lambda bi, ti: (0, 0)),
            pl.BlockSpec(sel.shape, lambda bi, ti: (0, 0, 0), pipeline_mode=resident),
        ],
        out_specs=(pl.BlockSpec((None, tc, wide), lambda bi, ti: (bi, ti, 0)),
                   pl.BlockSpec((None, tc, wide), lambda bi, ti: (bi, ti, 0)),
                   pl.BlockSpec((None, nheads, 1, LANES, tc), lambda bi, ti: (bi, 0, ti, 0, 0))),
        scratch_shapes=[pltpu.VMEM((1, LANES), F32)],
        compiler_params=_params("parallel", "arbitrary"),
        name="fox_in",
    )(x, g.reshape(1, d), w, bf, sel)


def _fox_attn_kernel(q_ref, k_ref, v_ref, o_ref, m_ref, acc_ref):
    tq = q_ref.shape[0]
    qi = pl.program_id(2)
    m_ref[...] = jnp.full_like(m_ref, -jnp.inf)
    acc_ref[...] = jnp.zeros_like(acc_ref)
    keys = lax.broadcasted_iota(jnp.int32, (tq, tq), 0)
    queries = lax.broadcasted_iota(jnp.int32, (tq, tq), 1)

    def block(kb, diagonal):
        start = pl.multiple_of(kb * tq, tq)
        logits = [_dot(k_ref[pl.ds(start, tq), e * LANES:(e + 1) * LANES],
                       q_ref[:, e * LANES:(e + 1) * LANES], NT) for e in range(2)]
        for e, s in enumerate(logits):
            if diagonal:
                s = jnp.where(keys <= queries, s, -jnp.inf)
            m_prev = m_ref[e]
            m_new = jnp.maximum(m_prev, jnp.max(s, axis=0, keepdims=True))
            p = jnp.exp2(s - m_new).astype(BF16)
            acc_ref[e] = jnp.exp2(m_prev - m_new) * acc_ref[e] + _dot(v_ref[e, kb], p)
            m_ref[e] = m_new

    def body(kb, carry):
        block(kb, False)
        return carry

    lax.fori_loop(0, qi, body, 0)
    block(qi, True)
    outs = []
    for e in range(2):
        acc = acc_ref[e]
        outs.append(acc[:HEAD_DIM] / acc[HEAD_DIM:HEAD_DIM + 1])
    o_ref[...] = jnp.concatenate(outs, axis=0).T


def _fox_attn(q_aug, k_aug, v_aug, *, tq):
    b, s, wide = q_aug.shape
    npair = wide // (2 * LANES)
    return pl.pallas_call(
        _fox_attn_kernel,
        out_shape=jax.ShapeDtypeStruct((b, s, npair * LANES), F32),
        grid=(b, npair, s // tq),
        in_specs=[
            pl.BlockSpec((None, tq, 2 * LANES), lambda bi, j, qi: (bi, qi, j)),
            pl.BlockSpec((None, s, 2 * LANES), lambda bi, j, qi: (bi, 0, j)),
            pl.BlockSpec((None, 2, s // tq, LANES, tq), lambda bi, j, qi: (bi, j, 0, 0, 0)),
        ],
        out_specs=pl.BlockSpec((None, tq, LANES), lambda bi, j, qi: (bi, qi, j)),
        scratch_shapes=[pltpu.VMEM((2, 1, tq), F32), pltpu.VMEM((2, LANES, tq), F32)],
        compiler_params=_params("parallel", "parallel", "arbitrary"),
        name="fox_attn",
    )(q_aug, k_aug, v_aug)


def kernel(x, p, ffn1_norm, ffn1_w_gu, ffn1_w_down, mix_norm, ffn2_norm, ffn2_w_gu, ffn2_w_down, ple_norm, ple_w_gate, ple_w_proj, even_w_in, even_w_out, swa_sinks, rwkv_mu, rwkv_w0, rwkv_w2, rwkv_a0, rwkv_a2, rwkv_g2, rwkv_k_k, rwkv_k_a, rwkv_r_k, rwkv_ln_w, rwkv_ln_b, fox_w_in, fox_b_f, fox_w_out, final_norm):
    b, s, d = x.shape
    depth = p.shape[0]
    t = b * s
    bf = lambda w: w.astype(BF16)
    swa_q = SWA_HEADS * HEAD_DIM
    swa_cols = swa_q + 2 * (SWA_HEADS // SWA_GROUP) * HEAD_DIM
    rwkv_dim = rwkv_w0.shape[-1]
    lora = rwkv_w2.shape[1]
    fox_heads = fox_b_f.shape[-1]
    fox_dim = fox_heads * HEAD_DIM

    x = x.reshape(t, d)
    for i in range(depth):
        j = i // 2
        x = _ffn(x, ffn1_norm[i], bf(ffn1_w_gu[i]), bf(ffn1_w_down[i]))
        if i % 2 == 0:
            w_in = even_w_in[j]
            qkv = _rms_matmul(x, mix_norm[i], bf(w_in[:, :swa_cols]))
            hb = _rms_matmul(x, mix_norm[i], bf(w_in[:, swa_cols:]))
            ya = _swa(qkv.reshape(b, s, swa_cols), swa_sinks[j])
            zeros = jnp.zeros((lora, rwkv_dim), F32)
            w2p = jnp.concatenate([rwkv_w2[j], zeros], axis=0)
            a2p = jnp.concatenate([zeros, rwkv_a2[j]], axis=0)
            yb = _rwkv(hb.reshape(b, s, -1), rwkv_mu[j], rwkv_w0[j], w2p, rwkv_a0[j], a2p,
                       bf(rwkv_g2[j]), rwkv_k_k[j], rwkv_k_a[j], rwkv_r_k[j].reshape(-1),
                       rwkv_ln_w[j], rwkv_ln_b[j])
            w_out = bf(even_w_out[j])
            x = _proj_res([ya.reshape(t, swa_q), yb.reshape(t, rwkv_dim)],
                          [w_out[:swa_q], w_out[swa_q:]], x)
        else:
            q_aug, k_aug, v_aug = _fox_in(x, mix_norm[i], fox_w_in[j], fox_b_f[j], batch=b, tc=FOX_BLOCK)
            yc = _fox_attn(q_aug, k_aug, v_aug, tq=FOX_BLOCK)
            x = _proj_res([yc.reshape(t, fox_dim)], [bf(fox_w_out[j])], x)
        x = _ffn(x, ffn2_norm[i], bf(ffn2_w_gu[i]), bf(ffn2_w_down[i]))
        x = _ple(x, ple_norm[i], bf(ple_w_gate[i]), p[i].reshape(t, -1), bf(ple_w_proj[i]),
                 final_norm, final=(i == depth - 1))
    return x.reshape(b, s, d)
```

```python
import functools

import jax
import jax.numpy as jnp
import numpy as np
from jax import lax
from jax.experimental import pallas as pl
from jax.experimental.pallas import tpu as pltpu

F32 = jnp.float32
BF16 = jnp.bfloat16

LANES = 128
HEAD_DIM = 64
SWA_HEADS = 8
SWA_GROUP = 4
SWA_BLOCK = 128
RWKV_CHUNK = 64
NORM_EPS = 1e-6
GN_EPS = 64e-5
L2_EPS = 1e-12
VMEM_LIMIT = 56 * 1024 * 1024

NN = (((1,), (0,)), ((), ()))
NT = (((1,), (1,)), ((), ()))
TN = (((0,), (0,)), ((), ()))


def _dot(a, b, dims=NN):
    return lax.dot_general(a, b, dims, preferred_element_type=F32)


def _dot_bf(a, b, dims=NN):
    return _dot(a.astype(BF16), b.astype(BF16), dims)


def _hi_lo(x):
    hi = x.astype(BF16)
    lo = (x - hi.astype(F32)).astype(BF16)
    return hi, lo


def _dot_x3(a, b, dims=NN):
    ah, al = _hi_lo(a)
    bh, bl = _hi_lo(b)
    return _dot(ah, bh, dims) + (_dot(ah, bl, dims) + _dot(al, bh, dims))


def _split3(x):
    hi = x.astype(BF16)
    r1 = x - hi.astype(F32)
    mid = r1.astype(BF16)
    lo = (r1 - mid.astype(F32)).astype(BF16)
    return hi, mid, lo


def _sel_dot(sel, x):
    hi, mid, lo = _split3(x)
    return _dot(sel, hi) + (_dot(sel, mid) + _dot(sel, lo))


def _dot_sel(x, sel):
    hi, mid, lo = _split3(x)
    return _dot(hi, sel) + (_dot(mid, sel) + _dot(lo, sel))


def _rms(x, g):
    ms = jnp.mean(x * x, axis=-1, keepdims=True)
    return x * lax.rsqrt(ms + NORM_EPS) * g


def _params(*sem):
    return pltpu.CompilerParams(dimension_semantics=sem, vmem_limit_bytes=VMEM_LIMIT)


def _ffn_kernel(x_ref, g_ref, wg_ref, wu_ref, wd_ref, o_ref):
    x = x_ref[...]
    hn = _rms(x, g_ref[...]).astype(BF16)
    gate = _dot(hn, wg_ref[...])
    up = _dot(hn, wu_ref[...])
    act = (gate * jax.nn.sigmoid(gate) * up).astype(BF16)
    o_ref[...] = x + 0.5 * _dot(act, wd_ref[...])


def _ffn(x, g, w_gu, w_down, *, tm=512):
    t, d = x.shape
    dff = w_down.shape[0]
    resident = pl.Buffered(1)
    return pl.pallas_call(
        _ffn_kernel,
        out_shape=jax.ShapeDtypeStruct((t, d), F32),
        grid=(t // tm,),
        in_specs=[
            pl.BlockSpec((tm, d), lambda i: (i, 0)),
            pl.BlockSpec((1, d), lambda i: (0, 0)),
            pl.BlockSpec((d, dff), lambda i: (0, 0), pipeline_mode=resident),
            pl.BlockSpec((d, dff), lambda i: (0, 1), pipeline_mode=resident),
            pl.BlockSpec((dff, d), lambda i: (0, 0), pipeline_mode=resident),
        ],
        out_specs=pl.BlockSpec((tm, d), lambda i: (i, 0)),
        compiler_params=_params("parallel"),
        name="ffn",
    )(x, g.reshape(1, d), w_gu, w_gu, w_down)


def _rms_matmul_kernel(x_ref, g_ref, w_ref, o_ref, hn_ref):
    @pl.when(pl.program_id(1) == 0)
    def _():
        hn_ref[...] = _rms(x_ref[...], g_ref[...]).astype(BF16)

    o_ref[...] = _dot(hn_ref[...], w_ref[...]).astype(o_ref.dtype)


def _rms_matmul(x, g, w, *, tm=512, tn=None, out_dtype=F32):
    t, d = x.shape
    n = w.shape[1]
    tn = n if tn is None else tn
    return pl.pallas_call(
        _rms_matmul_kernel,
        out_shape=jax.ShapeDtypeStruct((t, n), out_dtype),
        grid=(t // tm, n // tn),
        in_specs=[
            pl.BlockSpec((tm, d), lambda i, j: (i, 0)),
            pl.BlockSpec((1, d), lambda i, j: (0, 0)),
            pl.BlockSpec((d, tn), lambda i, j: (0, j)),
        ],
        out_specs=pl.BlockSpec((tm, tn), lambda i, j: (i, j)),
        scratch_shapes=[pltpu.VMEM((tm, d), BF16)],
        compiler_params=_params("parallel", "arbitrary"),
        name="rms_matmul",
    )(x, g.reshape(1, d), w)


def _proj_res_kernel(n_in, *refs):
    a_refs = refs[:n_in]
    w_refs = refs[n_in:2 * n_in]
    x_ref, o_ref = refs[2 * n_in], refs[2 * n_in + 1]
    acc = x_ref[...]
    for a_ref, w_ref in zip(a_refs, w_refs):
        acc = acc + _dot(a_ref[...].astype(BF16), w_ref[...])
    o_ref[...] = acc


def _proj_res(a_list, w_list, x, *, tm=512):
    t, d = x.shape
    n_in = len(a_list)
    in_specs = [pl.BlockSpec((tm, a.shape[1]), lambda i: (i, 0)) for a in a_list]
    in_specs += [pl.BlockSpec(w.shape, lambda i: (0, 0)) for w in w_list]
    in_specs += [pl.BlockSpec((tm, d), lambda i: (i, 0))]
    return pl.pallas_call(
        functools.partial(_proj_res_kernel, n_in),
        out_shape=jax.ShapeDtypeStruct((t, d), F32),
        grid=(t // tm,),
        in_specs=in_specs,
        out_specs=pl.BlockSpec((tm, d), lambda i: (i, 0)),
        compiler_params=_params("parallel"),
        name="proj_res",
    )(*a_list, *w_list, x)


def _ple_kernel(final, x_ref, g_ref, wg_ref, p_ref, wp_ref, fn_ref, o_ref):
    x = x_ref[...]
    hn = _rms(x, g_ref[...]).astype(BF16)
    gate = jax.nn.sigmoid(_dot(hn, wg_ref[...]))
    y = x + gate * _dot(p_ref[...].astype(BF16), wp_ref[...])
    if final:
        y = _rms(y, fn_ref[...])
    o_ref[...] = y


def _ple(x, g, w_gate, p, w_proj, final_g, *, final, tm=512):
    t, d = x.shape
    pd = p.shape[1]
    return pl.pallas_call(
        functools.partial(_ple_kernel, final),
        out_shape=jax.ShapeDtypeStruct((t, d), F32),
        grid=(t // tm,),
        in_specs=[
            pl.BlockSpec((tm, d), lambda i: (i, 0)),
            pl.BlockSpec((1, d), lambda i: (0, 0)),
            pl.BlockSpec((d, d), lambda i: (0, 0)),
            pl.BlockSpec((tm, pd), lambda i: (i, 0)),
            pl.BlockSpec((pd, d), lambda i: (0, 0)),
            pl.BlockSpec((1, d), lambda i: (0, 0)),
        ],
        out_specs=pl.BlockSpec((tm, d), lambda i: (i, 0)),
        compiler_params=_params("parallel"),
        name="ple",
    )(x, g.reshape(1, d), w_gate, p, w_proj, final_g.reshape(1, d))


def _swa_kernel(sink_ref, q_ref, kp_ref, kc_ref, vp_ref, vc_ref, o_ref):
    n = pl.program_id(1)
    blk = SWA_BLOCK
    scale = HEAD_DIM ** -0.5
    k = jnp.concatenate([kp_ref[...], kc_ref[...]], axis=0)
    v = jnp.concatenate([vp_ref[...], vc_ref[...]], axis=0)
    kr = pltpu.roll(k, HEAD_DIM, 1)
    vr = pltpu.roll(v, HEAD_DIM, 1)
    lo_kv = lax.broadcasted_iota(jnp.int32, k.shape, 1) < HEAD_DIM
    kdup = [jnp.where(lo_kv, k, kr).astype(BF16), jnp.where(lo_kv, kr, k).astype(BF16)]
    vdup = [jnp.where(lo_kv, v, vr).astype(BF16), jnp.where(lo_kv, vr, v).astype(BF16)]

    qi = lax.broadcasted_iota(jnp.int32, (blk, 2 * blk), 0)
    ki = lax.broadcasted_iota(jnp.int32, (blk, 2 * blk), 1)
    dist = qi + blk - ki
    valid = (dist >= 0) & (dist < blk) & ((n > 0) | (ki >= blk))
    distf = dist.astype(F32)
    lo_q = lax.broadcasted_iota(jnp.int32, (blk, LANES), 1) < HEAD_DIM

    for j in range(SWA_HEADS // 2):
        g = (2 * j) // SWA_GROUP
        q2 = q_ref[:, j * LANES:(j + 1) * LANES] * scale
        outs = []
        for e in range(2):
            h = 2 * j + e
            qm = jnp.where(lo_q if e == 0 else ~lo_q, q2, 0.0).astype(BF16)
            s = _dot(qm, kdup[g], NT)
            slope = 2.0 ** (-8.0 * (h + 1) / SWA_HEADS)
            s = jnp.where(valid, s - slope * distf, -jnp.inf)
            sink = sink_ref[h]
            m = jnp.maximum(jnp.max(s, axis=-1, keepdims=True), sink)
            p = jnp.exp(s - m)
            denom = jnp.sum(p, axis=-1, keepdims=True) + jnp.exp(sink - m)
            p = p / denom
            outs.append(_dot(p.astype(BF16), vdup[g]))
        o_ref[:, j * LANES:(j + 1) * LANES] = jnp.where(lo_q, outs[0], outs[1])


def _swa(qkv, sinks):
    b, s, _ = qkv.shape
    blk = SWA_BLOCK
    nq = SWA_HEADS * HEAD_DIM
    kcol = nq // LANES
    vcol = kcol + 1
    prev = lambda bi, n: jnp.maximum(n - 1, 0)
    return pl.pallas_call(
        _swa_kernel,
        out_shape=jax.ShapeDtypeStruct((b, s, nq), F32),
        grid=(b, s // blk),
        in_specs=[
            pl.BlockSpec(memory_space=pltpu.SMEM),
            pl.BlockSpec((None, blk, nq), lambda bi, n: (bi, n, 0)),
            pl.BlockSpec((None, blk, LANES), lambda bi, n: (bi, prev(bi, n), kcol)),
            pl.BlockSpec((None, blk, LANES), lambda bi, n: (bi, n, kcol)),
            pl.BlockSpec((None, blk, LANES), lambda bi, n: (bi, prev(bi, n), vcol)),
            pl.BlockSpec((None, blk, LANES), lambda bi, n: (bi, n, vcol)),
        ],
        out_specs=pl.BlockSpec((None, blk, nq), lambda bi, n: (bi, n, 0)),
        compiler_params=_params("parallel", "arbitrary"),
        name="swa",
    )(sinks, qkv, qkv, qkv, qkv, qkv)


def _tri_inverse_minus_eye(lows, ri, ci):
    same = lambda w: (ri ^ ci) < w
    base = 8
    x = [jnp.where(same(base), -low, 0.0) for low in lows]
    p2 = [_dot_bf(xi, xi) for xi in x]
    e = [xi + pi + _dot_bf(xi, pi) for xi, pi in zip(x, p2)]
    p4 = [_dot_bf(pi, pi) for pi in p2]
    e = [ei + pi + _dot_bf(ei, pi) for ei, pi in zip(e, p4)]
    w = base * 2
    while w <= RWKV_CHUNK:
        off = [jnp.where(same(w) & ~same(w // 2), low, 0.0) for low in lows]
        wm = [oi + _dot_bf(ei, oi) for ei, oi in zip(e, off)]
        e = [ei - wi - _dot_bf(wi, ei) for ei, wi in zip(e, wm)]
        w *= 2
    return e


def _rwkv_kernel(h_ref, mu_ref, w0_ref, w2_ref, a0_ref, a2_ref, g2_ref, kk_ref, ka_ref,
                 rk_ref, lnw_ref, lnb_ref, o_ref, state_ref, last_ref):
    c = RWKV_CHUNK
    tb = h_ref.shape[0]
    nchunk = tb // c
    dim = o_ref.shape[-1]
    npair = dim // LANES

    @pl.when(pl.program_id(1) == 0)
    def _():
        state_ref[...] = jnp.zeros_like(state_ref)
        last_ref[...] = jnp.zeros_like(last_ref)

    h = h_ref[...]
    row = lax.broadcasted_iota(jnp.int32, h.shape, 0)
    shifted = jnp.where(row == 0, last_ref[...], pltpu.roll(h, 1, 0))
    last_ref[...] = h[tb - 1:tb, :]
    hs = h + (shifted - h) * mu_ref[...]
    r = hs[:, 0:dim]
    k = hs[:, dim:2 * dim]
    v = hs[:, 2 * dim:3 * dim]
    xwa = hs[:, 3 * dim:3 * dim + LANES]
    xg = hs[:, 3 * dim + LANES:3 * dim + 2 * LANES]

    wl = w0_ref[...] + _dot_x3(jnp.tanh(xwa), w2_ref[...])
    logw = -jax.nn.sigmoid(wl) * float(np.exp(-0.5))
    a = jax.nn.sigmoid(a0_ref[...] + _dot_x3(xwa, a2_ref[...]))
    gate = _dot_bf(jax.nn.sigmoid(xg), g2_ref[...])

    ri = lax.broadcasted_iota(jnp.int32, (LANES, LANES), 0)
    ci = lax.broadcasted_iota(jnp.int32, (LANES, LANES), 1)
    ones_bd = ((ri ^ ci) < HEAD_DIM).astype(BF16)
    ones_bd2 = jnp.concatenate([ones_bd, ones_bd], axis=0)

    def head_sum(x):
        cols = []
        for j in range(npair):
            hi, lo = _hi_lo(x[:, j * LANES:(j + 1) * LANES])
            cols.append(_dot(jnp.concatenate([hi, lo], axis=1), ones_bd2))
        return jnp.concatenate(cols, axis=1)

    kk = k * kk_ref[...]
    kk = kk / jnp.maximum(jnp.sqrt(head_sum(kk * kk)), L2_EPS)
    k2 = k * (1.0 + (a - 1.0) * ka_ref[...])
    bvec = kk * a

    ti = lax.broadcasted_iota(jnp.int32, (tb, tb), 0)
    si = lax.broadcasted_iota(jnp.int32, (tb, tb), 1)
    tri = (((ti ^ si) < c) & (si <= ti)).astype(BF16)
    cum = _dot(jnp.concatenate([tri] * 3, axis=1), jnp.concatenate(_split3(logw), axis=0))
    e_neg = jnp.exp(-cum)
    alpha = kk * jnp.exp(cum - logw)
    beta = bvec * e_neg
    kappa = k2 * e_neg
    rho = r * jnp.exp(cum)

    same_head = (ri ^ ci) < c
    strict = same_head & (ci < ri)
    incl = same_head & (ci <= ri)
    lo = lax.broadcasted_iota(jnp.int32, (c, LANES), 1) < HEAD_DIM

    def stack(x2):
        return jnp.concatenate([jnp.where(lo, x2, 0.0), jnp.where(lo, 0.0, x2)], axis=0)

    units = [(n, j) for n in range(nchunk) for j in range(npair)]
    blk = lambda x, n, j: x[n * c:(n + 1) * c, j * LANES:(j + 1) * LANES]
    a_s = [stack(blk(alpha, n, j)) for n, j in units]
    rho_s = [stack(blk(rho, n, j)) for n, j in units]
    v_s = [stack(blk(v, n, j)) for n, j in units]
    bk = [jnp.concatenate([blk(beta, n, j)] * 2 + [blk(kappa, n, j)] * 2, axis=0) for n, j in units]
    sc = [_dot_bf(jnp.concatenate([ai, ri_], axis=0), bi, NT) for ai, ri_, bi in zip(a_s, rho_s, bk)]
    l_ab = [jnp.where(strict, s[:LANES, :LANES], 0.0) for s in sc]
    l_ak = [jnp.where(strict, s[:LANES, LANES:], 0.0) for s in sc]
    r_bk = [jnp.concatenate([jnp.where(incl, s[LANES:, :LANES], 0.0),
                             jnp.where(incl, s[LANES:, LANES:], 0.0)], axis=1) for s in sc]
    e_inv = _tri_inverse_minus_eye(l_ab, ri, ci)
    lkv = [_dot_bf(li, vi) for li, vi in zip(l_ak, v_s)]

    ys = []
    for n in range(nchunk):
        cum_n = cum[n * c:(n + 1) * c, :]
        cum_end = cum_n[c - 1:c, :]
        to_end = jnp.exp(cum_end - cum_n)
        beta_e = bvec[n * c:(n + 1) * c, :] * to_end
        kappa_e = k2[n * c:(n + 1) * c, :] * to_end
        w_end = jnp.exp(cum_end)
        idx = [n * npair + j for j in range(npair)]
        s0 = [state_ref[j] for j in range(npair)]
        rhs = [_dot_x3(a_s[i], s0[j], NT) + lkv[i] for j, i in enumerate(idx)]
        u = [-(x + _dot_bf(e_inv[i], x)) for x, i in zip(rhs, idx)]
        y = [_dot_x3(rho_s[i], s0[j], NT) + _dot_bf(r_bk[i], jnp.concatenate([u[j], v_s[i]], axis=0))
             for j, i in enumerate(idx)]
        ys.append(jnp.concatenate([yi[:c] + yi[c:] for yi in y], axis=1))
        for j, i in enumerate(idx):
            sl = slice(j * LANES, (j + 1) * LANES)
            ends = jnp.concatenate([stack(beta_e[:, sl]), stack(kappa_e[:, sl])], axis=0)
            state_ref[j] = s0[j] * w_end[:, sl] + _dot_x3(jnp.concatenate([u[j], v_s[i]], axis=0), ends, TN)
    y = jnp.concatenate(ys, axis=0)

    mean = head_sum(y) * (1.0 / HEAD_DIM)
    d = y - mean
    var = head_sum(d * d) * (1.0 / HEAD_DIM)
    y = d * lax.rsqrt(var + GN_EPS) * lnw_ref[...] + lnb_ref[...]
    y = y + head_sum(r * k2 * rk_ref[...]) * v
    o_ref[...] = y * gate


def _rwkv(hb, mu, w0, w2p, a0, a2p, g2, k_k, k_a, r_k, ln_w, ln_b, *, tb=2 * RWKV_CHUNK):
    b, s, cols = hb.shape
    dim = w0.shape[-1]
    c = tb
    row = lambda x: x.reshape(1, -1)
    vec = lambda n: pl.BlockSpec((1, n), lambda bi, t: (0, 0))
    mat = lambda m: pl.BlockSpec(m.shape, lambda bi, t: (0, 0))
    return pl.pallas_call(
        _rwkv_kernel,
        out_shape=jax.ShapeDtypeStruct((b, s, dim), F32),
        grid=(b, s // c),
        in_specs=[
            pl.BlockSpec((None, c, cols), lambda bi, t: (bi, t, 0)),
            vec(cols), vec(dim), mat(w2p), vec(dim), mat(a2p), mat(g2),
            vec(dim), vec(dim), vec(dim), vec(dim), vec(dim),
        ],
        out_specs=pl.BlockSpec((None, c, dim), lambda bi, t: (bi, t, 0)),
        scratch_shapes=[pltpu.VMEM((dim // LANES, LANES, LANES), F32), pltpu.VMEM((1, cols), F32)],
        compiler_params=_params("parallel", "arbitrary"),
        name="rwkv7",
    )(hb, row(mu), row(w0), w2p, row(a0), a2p, g2, row(k_k), row(k_a), row(r_k), row(ln_w), row(ln_b))


FOX_AUG = 3
FOX_BLOCK = 512
LOG2E = float(np.log2(np.e))


def _fox_in_kernel(x_ref, g_ref, w_ref, bf_ref, sel_ref, qa_ref, ka_ref, va_ref, carry_ref):
    tc = x_ref.shape[0]
    nheads = qa_ref.shape[-1] // LANES
    dim = nheads * HEAD_DIM
    scale = HEAD_DIM ** -0.5 * LOG2E

    @pl.when(pl.program_id(1) == 0)
    def _():
        carry_ref[...] = jnp.zeros_like(carry_ref)

    proj = _dot(_rms(x_ref[...], g_ref[...]).astype(BF16), w_ref[...])
    z = proj[:, 3 * dim:] + bf_ref[...]
    logf = jnp.minimum(z, 0.0) - jnp.log(1.0 + jnp.exp(-jnp.abs(z)))
    lane = lax.broadcasted_iota(jnp.int32, z.shape, 1)
    logf = jnp.where(lane < nheads, logf, 0.0)
    ti = lax.broadcasted_iota(jnp.int32, (tc, tc), 0)
    si = lax.broadcasted_iota(jnp.int32, (tc, tc), 1)
    cg = _sel_dot((si <= ti).astype(BF16), logf) + carry_ref[...]
    carry_ref[...] = cg[tc - 1:tc, :]

    hi, mid, low = _split3(cg * LOG2E)
    pieces = (hi.astype(F32) + pltpu.roll(mid.astype(F32), nheads, 1)
              + pltpu.roll(low.astype(F32), 2 * nheads, 1)).astype(BF16)
    q_c = _dot(pieces, sel_ref[0])
    k_c = -_dot(pieces, sel_ref[1])

    lane = lax.broadcasted_iota(jnp.int32, (tc, LANES), 1)
    lo = lane < HEAD_DIM
    q_one = ((lane >= HEAD_DIM + FOX_AUG) & (lane < HEAD_DIM + 2 * FOX_AUG)).astype(F32)
    k_one = ((lane >= HEAD_DIM) & (lane < HEAD_DIM + FOX_AUG)).astype(F32)
    v_one = (lane == HEAD_DIM).astype(F32)
    sources = ((0, q_c, q_one, qa_ref), (1, k_c, k_one, ka_ref), (2, None, v_one, va_ref))
    for j in range(nheads // 2):
        for part, c_aug, one, o_ref in sources:
            x = proj[:, part * dim + j * LANES:part * dim + (j + 1) * LANES]
            if part == 0:
                x = x * scale
            xr = pltpu.roll(x, HEAD_DIM, 1)
            for e, xe in ((0, x), (1, xr)):
                hs = slice((2 * j + e) * LANES, (2 * j + e + 1) * LANES)
                if c_aug is None:
                    o_ref[2 * j + e, 0] = jnp.where(lo, xe, one).T.astype(BF16)
                else:
                    o_ref[:, hs] = jnp.where(lo, xe, c_aug[:, hs] + one).astype(BF16)


def _fox_select_matrices(nheads):
    sel = np.zeros((2, LANES, nheads * LANES), np.float32)
    for h in range(nheads):
        for i in range(FOX_AUG):
            sel[0, i * nheads + h, h * LANES + HEAD_DIM + i] = 1.0
            sel[1, i * nheads + h, h * LANES + HEAD_DIM + FOX_AUG + i] = 1.0
    return jnp.asarray(sel, BF16)


def _fox_in(x, g, w_in, b_f, *, batch, tc):
    t, d = x.shape
    s = t // batch
    nheads = b_f.shape[-1]
    assert FOX_AUG * nheads <= LANES
    w = jnp.pad(w_in, ((0, 0), (0, LANES - nheads))).astype(BF16)
    bf = jnp.zeros((1, LANES), F32).at[0, :nheads].set(b_f)
    sel = _fox_select_matrices(nheads)
    wide = nheads * LANES
    nt = s // tc
    resident = pl.Buffered(1)
    return pl.pallas_call(
        _fox_in_kernel,
        out_shape=(jax.ShapeDtypeStruct((batch, s, wide), BF16),
                   jax.ShapeDtypeStruct((batch, s, wide), BF16),
                   jax.ShapeDtypeStruct((batch, nheads, nt, LANES, tc), BF16)),
        grid=(batch, nt),
        in_specs=[
            pl.BlockSpec((tc, d), lambda bi, ti: (bi * nt + ti, 0)),
            pl.BlockSpec((1, d), lambda bi, ti: (0, 0)),
            pl.BlockSpec(w.shape, lambda bi, ti: (0, 0), pipeline_mode=resident),
            pl.BlockSpec((1, LANES), lambda bi, ti: (0, 0)),
            pl.BlockSpec(sel.shape, lambda bi, ti: (0, 0, 0), pipeline_mode=resident),
        ],
        out_specs=(pl.BlockSpec((None, tc, wide), lambda bi, ti: (bi, ti, 0)),
                   pl.BlockSpec((None, tc, wide), lambda bi, ti: (bi, ti, 0)),
                   pl.BlockSpec((None, nheads, 1, LANES, tc), lambda bi, ti: (bi, 0, ti, 0, 0))),
        scratch_shapes=[pltpu.VMEM((1, LANES), F32)],
        compiler_params=_params("parallel", "arbitrary"),
        name="fox_in",
    )(x, g.reshape(1, d), w, bf, sel)


def _fox_attn_kernel(q_ref, k_ref, v_ref, o_ref, m_ref, acc_ref):
    tq = q_ref.shape[0]
    nh = q_ref.shape[1] // LANES
    qi = pl.program_id(2)
    m_ref[...] = jnp.full_like(m_ref, -jnp.inf)
    acc_ref[...] = jnp.zeros_like(acc_ref)
    keys = lax.broadcasted_iota(jnp.int32, (tq, tq), 0)
    queries = lax.broadcasted_iota(jnp.int32, (tq, tq), 1)

    def block(kb, diagonal):
        start = pl.multiple_of(kb * tq, tq)
        logits = [_dot(k_ref[pl.ds(start, tq), e * LANES:(e + 1) * LANES],
                       q_ref[:, e * LANES:(e + 1) * LANES], NT) for e in range(nh)]
        for e, s in enumerate(logits):
            if diagonal:
                s = jnp.where(keys <= queries, s, -jnp.inf)
            m_prev = m_ref[e]
            m_new = jnp.maximum(m_prev, jnp.max(s, axis=0, keepdims=True))
            p = jnp.exp2(s - m_new).astype(BF16)
            acc_ref[e] = jnp.exp2(m_prev - m_new) * acc_ref[e] + _dot(v_ref[e, kb], p)
            m_ref[e] = m_new

    def body(kb, carry):
        block(kb, False)
        return carry

    lax.fori_loop(0, qi, body, 0)
    block(qi, True)
    for pair in range(nh // 2):
        outs = []
        for e in (2 * pair, 2 * pair + 1):
            acc = acc_ref[e]
            outs.append(acc[:HEAD_DIM] / acc[HEAD_DIM:HEAD_DIM + 1])
        o_ref[:, pair * LANES:(pair + 1) * LANES] = jnp.concatenate(outs, axis=0).T


def _fox_attn(q_aug, k_aug, v_aug, *, tq, nh=4):
    b, s, wide = q_aug.shape
    nheads = wide // LANES
    return pl.pallas_call(
        _fox_attn_kernel,
        out_shape=jax.ShapeDtypeStruct((b, s, nheads * HEAD_DIM), F32),
        grid=(b, nheads // nh, s // tq),
        in_specs=[
            pl.BlockSpec((None, tq, nh * LANES), lambda bi, g, qi: (bi, qi, g)),
            pl.BlockSpec((None, s, nh * LANES), lambda bi, g, qi: (bi, 0, g)),
            pl.BlockSpec((None, nh, s // tq, LANES, tq), lambda bi, g, qi: (bi, g, 0, 0, 0)),
        ],
        out_specs=pl.BlockSpec((None, tq, nh * HEAD_DIM), lambda bi, g, qi: (bi, qi, g)),
        scratch_shapes=[pltpu.VMEM((nh, 1, tq), F32), pltpu.VMEM((nh, LANES, tq), F32)],
        compiler_params=_params("parallel", "parallel", "arbitrary"),
        name="fox_attn",
    )(q_aug, k_aug, v_aug)


def kernel(x, p, ffn1_norm, ffn1_w_gu, ffn1_w_down, mix_norm, ffn2_norm, ffn2_w_gu, ffn2_w_down, ple_norm, ple_w_gate, ple_w_proj, even_w_in, even_w_out, swa_sinks, rwkv_mu, rwkv_w0, rwkv_w2, rwkv_a0, rwkv_a2, rwkv_g2, rwkv_k_k, rwkv_k_a, rwkv_r_k, rwkv_ln_w, rwkv_ln_b, fox_w_in, fox_b_f, fox_w_out, final_norm):
    b, s, d = x.shape
    depth = p.shape[0]
    t = b * s
    bf = lambda w: w.astype(BF16)
    swa_q = SWA_HEADS * HEAD_DIM
    swa_cols = swa_q + 2 * (SWA_HEADS // SWA_GROUP) * HEAD_DIM
    rwkv_dim = rwkv_w0.shape[-1]
    lora = rwkv_w2.shape[1]
    fox_heads = fox_b_f.shape[-1]
    fox_dim = fox_heads * HEAD_DIM

    x = x.reshape(t, d)
    for i in range(depth):
        j = i // 2
        x = _ffn(x, ffn1_norm[i], bf(ffn1_w_gu[i]), bf(ffn1_w_down[i]))
        if i % 2 == 0:
            w_in = even_w_in[j]
            qkv = _rms_matmul(x, mix_norm[i], bf(w_in[:, :swa_cols]))
            hb = _rms_matmul(x, mix_norm[i], bf(w_in[:, swa_cols:]))
            ya = _swa(qkv.reshape(b, s, swa_cols), swa_sinks[j])
            zeros = jnp.zeros((lora, rwkv_dim), F32)
            w2p = jnp.concatenate([rwkv_w2[j], zeros], axis=0)
            a2p = jnp.concatenate([zeros, rwkv_a2[j]], axis=0)
            yb = _rwkv(hb.reshape(b, s, -1), rwkv_mu[j], rwkv_w0[j], w2p, rwkv_a0[j], a2p,
                       bf(rwkv_g2[j]), rwkv_k_k[j], rwkv_k_a[j], rwkv_r_k[j].reshape(-1),
                       rwkv_ln_w[j], rwkv_ln_b[j])
            w_out = bf(even_w_out[j])
            x = _proj_res([ya.reshape(t, swa_q), yb.reshape(t, rwkv_dim)],
                          [w_out[:swa_q], w_out[swa_q:]], x)
        else:
            q_aug, k_aug, v_aug = _fox_in(x, mix_norm[i], fox_w_in[j], fox_b_f[j], batch=b, tc=FOX_BLOCK)
            yc = _fox_attn(q_aug, k_aug, v_aug, tq=FOX_BLOCK)
            x = _proj_res([yc.reshape(t, fox_dim)], [bf(fox_w_out[j])], x)
        x = _ffn(x, ffn2_norm[i], bf(ffn2_w_gu[i]), bf(ffn2_w_down[i]))
        x = _ple(x, ple_norm[i], bf(ple_w_gate[i]), p[i].reshape(t, -1), bf(ple_w_proj[i]),
                 final_norm, final=(i == depth - 1))
    return x.reshape(b, s, d)
```

```python
import functools

import jax
import jax.numpy as jnp
import numpy as np
from jax import lax
from jax.experimental import pallas as pl
from jax.experimental.pallas import tpu as pltpu

F32 = jnp.float32
BF16 = jnp.bfloat16

LANES = 128
HEAD_DIM = 64
SWA_HEADS = 8
SWA_GROUP = 4
SWA_BLOCK = 128
RWKV_CHUNK = 64
NORM_EPS = 1e-6
GN_EPS = 64e-5
L2_EPS = 1e-12
VMEM_LIMIT = 56 * 1024 * 1024

NN = (((1,), (0,)), ((), ()))
NT = (((1,), (1,)), ((), ()))
TN = (((0,), (0,)), ((), ()))


def _dot(a, b, dims=NN):
    return lax.dot_general(a, b, dims, preferred_element_type=F32)


def _dot_bf(a, b, dims=NN):
    return _dot(a.astype(BF16), b.astype(BF16), dims)


def _hi_lo(x):
    hi = x.astype(BF16)
    lo = (x - hi.astype(F32)).astype(BF16)
    return hi, lo


def _dot_x3(a, b, dims=NN):
    ah, al = _hi_lo(a)
    bh, bl = _hi_lo(b)
    return _dot(ah, bh, dims) + (_dot(ah, bl, dims) + _dot(al, bh, dims))


def _split3(x):
    hi = x.astype(BF16)
    r1 = x - hi.astype(F32)
    mid = r1.astype(BF16)
    lo = (r1 - mid.astype(F32)).astype(BF16)
    return hi, mid, lo


def _sel_dot(sel, x):
    hi, mid, lo = _split3(x)
    return _dot(sel, hi) + (_dot(sel, mid) + _dot(sel, lo))


def _dot_sel(x, sel):
    hi, mid, lo = _split3(x)
    return _dot(hi, sel) + (_dot(mid, sel) + _dot(lo, sel))


def _rms(x, g):
    ms = jnp.mean(x * x, axis=-1, keepdims=True)
    return x * lax.rsqrt(ms + NORM_EPS) * g


def _params(*sem):
    return pltpu.CompilerParams(dimension_semantics=sem, vmem_limit_bytes=VMEM_LIMIT)


def _ffn_kernel(x_ref, g_ref, wg_ref, wu_ref, wd_ref, o_ref):
    x = x_ref[...]
    hn = _rms(x, g_ref[...]).astype(BF16)
    gate = _dot(hn, wg_ref[...])
    up = _dot(hn, wu_ref[...])
    act = (gate * jax.nn.sigmoid(gate) * up).astype(BF16)
    o_ref[...] = x + 0.5 * _dot(act, wd_ref[...])


def _ffn(x, g, w_gu, w_down, *, tm=512):
    t, d = x.shape
    dff = w_down.shape[0]
    resident = pl.Buffered(1)
    return pl.pallas_call(
        _ffn_kernel,
        out_shape=jax.ShapeDtypeStruct((t, d), F32),
        grid=(t // tm,),
        in_specs=[
            pl.BlockSpec((tm, d), lambda i: (i, 0)),
            pl.BlockSpec((1, d), lambda i: (0, 0)),
            pl.BlockSpec((d, dff), lambda i: (0, 0), pipeline_mode=resident),
            pl.BlockSpec((d, dff), lambda i: (0, 1), pipeline_mode=resident),
            pl.BlockSpec((dff, d), lambda i: (0, 0), pipeline_mode=resident),
        ],
        out_specs=pl.BlockSpec((tm, d), lambda i: (i, 0)),
        compiler_params=_params("parallel"),
        name="ffn",
    )(x, g.reshape(1, d), w_gu, w_gu, w_down)


def _rms_matmul_kernel(x_ref, g_ref, w_ref, o_ref, hn_ref):
    @pl.when(pl.program_id(1) == 0)
    def _():
        hn_ref[...] = _rms(x_ref[...], g_ref[...]).astype(BF16)

    o_ref[...] = _dot(hn_ref[...], w_ref[...]).astype(o_ref.dtype)


def _rms_matmul(x, g, w, *, tm=512, tn=None, out_dtype=F32):
    t, d = x.shape
    n = w.shape[1]
    tn = n if tn is None else tn
    return pl.pallas_call(
        _rms_matmul_kernel,
        out_shape=jax.ShapeDtypeStruct((t, n), out_dtype),
        grid=(t // tm, n // tn),
        in_specs=[
            pl.BlockSpec((tm, d), lambda i, j: (i, 0)),
            pl.BlockSpec((1, d), lambda i, j: (0, 0)),
            pl.BlockSpec((d, tn), lambda i, j: (0, j)),
        ],
        out_specs=pl.BlockSpec((tm, tn), lambda i, j: (i, j)),
        scratch_shapes=[pltpu.VMEM((tm, d), BF16)],
        compiler_params=_params("parallel", "arbitrary"),
        name="rms_matmul",
    )(x, g.reshape(1, d), w)


def _proj_res_kernel(n_in, *refs):
    a_refs = refs[:n_in]
    w_refs = refs[n_in:2 * n_in]
    x_ref, o_ref = refs[2 * n_in], refs[2 * n_in + 1]
    acc = x_ref[...]
    for a_ref, w_ref in zip(a_refs, w_refs):
        acc = acc + _dot(a_ref[...].astype(BF16), w_ref[...])
    o_ref[...] = acc


def _proj_res(a_list, w_list, x, *, tm=512):
    t, d = x.shape
    n_in = len(a_list)
    in_specs = [pl.BlockSpec((tm, a.shape[1]), lambda i: (i, 0)) for a in a_list]
    in_specs += [pl.BlockSpec(w.shape, lambda i: (0, 0)) for w in w_list]
    in_specs += [pl.BlockSpec((tm, d), lambda i: (i, 0))]
    return pl.pallas_call(
        functools.partial(_proj_res_kernel, n_in),
        out_shape=jax.ShapeDtypeStruct((t, d), F32),
        grid=(t // tm,),
        in_specs=in_specs,
        out_specs=pl.BlockSpec((tm, d), lambda i: (i, 0)),
        compiler_params=_params("parallel"),
        name="proj_res",
    )(*a_list, *w_list, x)


def _ple_kernel(final, x_ref, g_ref, wg_ref, p_ref, wp_ref, fn_ref, o_ref):
    x = x_ref[...]
    hn = _rms(x, g_ref[...]).astype(BF16)
    gate = jax.nn.sigmoid(_dot(hn, wg_ref[...]))
    y = x + gate * _dot(p_ref[...].astype(BF16), wp_ref[...])
    if final:
        y = _rms(y, fn_ref[...])
    o_ref[...] = y


def _ple(x, g, w_gate, p, w_proj, final_g, *, final, tm=512):
    t, d = x.shape
    pd = p.shape[1]
    return pl.pallas_call(
        functools.partial(_ple_kernel, final),
        out_shape=jax.ShapeDtypeStruct((t, d), F32),
        grid=(t // tm,),
        in_specs=[
            pl.BlockSpec((tm, d), lambda i: (i, 0)),
            pl.BlockSpec((1, d), lambda i: (0, 0)),
            pl.BlockSpec((d, d), lambda i: (0, 0)),
            pl.BlockSpec((tm, pd), lambda i: (i, 0)),
            pl.BlockSpec((pd, d), lambda i: (0, 0)),
            pl.BlockSpec((1, d), lambda i: (0, 0)),
        ],
        out_specs=pl.BlockSpec((tm, d), lambda i: (i, 0)),
        compiler_params=_params("parallel"),
        name="ple",
    )(x, g.reshape(1, d), w_gate, p, w_proj, final_g.reshape(1, d))


def _swa_kernel(sink_ref, q_ref, kp_ref, kc_ref, vp_ref, vc_ref, o_ref):
    n = pl.program_id(1)
    blk = SWA_BLOCK
    scale = HEAD_DIM ** -0.5
    k = jnp.concatenate([kp_ref[...], kc_ref[...]], axis=0)
    v = jnp.concatenate([vp_ref[...], vc_ref[...]], axis=0)
    kr = pltpu.roll(k, HEAD_DIM, 1)
    vr = pltpu.roll(v, HEAD_DIM, 1)
    lo_kv = lax.broadcasted_iota(jnp.int32, k.shape, 1) < HEAD_DIM
    kdup = [jnp.where(lo_kv, k, kr).astype(BF16), jnp.where(lo_kv, kr, k).astype(BF16)]
    vdup = [jnp.where(lo_kv, v, vr).astype(BF16), jnp.where(lo_kv, vr, v).astype(BF16)]

    qi = lax.broadcasted_iota(jnp.int32, (blk, 2 * blk), 0)
    ki = lax.broadcasted_iota(jnp.int32, (blk, 2 * blk), 1)
    dist = qi + blk - ki
    valid = (dist >= 0) & (dist < blk) & ((n > 0) | (ki >= blk))
    distf = dist.astype(F32)
    lo_q = lax.broadcasted_iota(jnp.int32, (blk, LANES), 1) < HEAD_DIM

    for j in range(SWA_HEADS // 2):
        g = (2 * j) // SWA_GROUP
        q2 = q_ref[:, j * LANES:(j + 1) * LANES] * scale
        outs = []
        for e in range(2):
            h = 2 * j + e
            qm = jnp.where(lo_q if e == 0 else ~lo_q, q2, 0.0).astype(BF16)
            s = _dot(qm, kdup[g], NT)
            slope = 2.0 ** (-8.0 * (h + 1) / SWA_HEADS)
            s = jnp.where(valid, s - slope * distf, -jnp.inf)
            sink = sink_ref[h]
            m = jnp.maximum(jnp.max(s, axis=-1, keepdims=True), sink)
            p = jnp.exp(s - m)
            denom = jnp.sum(p, axis=-1, keepdims=True) + jnp.exp(sink - m)
            p = p / denom
            outs.append(_dot(p.astype(BF16), vdup[g]))
        o_ref[:, j * LANES:(j + 1) * LANES] = jnp.where(lo_q, outs[0], outs[1])


def _swa(qkv, sinks):
    b, s, _ = qkv.shape
    blk = SWA_BLOCK
    nq = SWA_HEADS * HEAD_DIM
    kcol = nq // LANES
    vcol = kcol + 1
    prev = lambda bi, n: jnp.maximum(n - 1, 0)
    return pl.pallas_call(
        _swa_kernel,
        out_shape=jax.ShapeDtypeStruct((b, s, nq), F32),
        grid=(b, s // blk),
        in_specs=[
            pl.BlockSpec(memory_space=pltpu.SMEM),
            pl.BlockSpec((None, blk, nq), lambda bi, n: (bi, n, 0)),
            pl.BlockSpec((None, blk, LANES), lambda bi, n: (bi, prev(bi, n), kcol)),
            pl.BlockSpec((None, blk, LANES), lambda bi, n: (bi, n, kcol)),
            pl.BlockSpec((None, blk, LANES), lambda bi, n: (bi, prev(bi, n), vcol)),
            pl.BlockSpec((None, blk, LANES), lambda bi, n: (bi, n, vcol)),
        ],
        out_specs=pl.BlockSpec((None, blk, nq), lambda bi, n: (bi, n, 0)),
        compiler_params=_params("parallel", "arbitrary"),
        name="swa",
    )(sinks, qkv, qkv, qkv, qkv, qkv)


def _tri_inverse_minus_eye(lows, ri, ci):
    same = lambda w: (ri ^ ci) < w
    base = 8
    x = [jnp.where(same(base), -low, 0.0) for low in lows]
    p2 = [_dot_bf(xi, xi) for xi in x]
    e = [xi + pi + _dot_bf(xi, pi) for xi, pi in zip(x, p2)]
    p4 = [_dot_bf(pi, pi) for pi in p2]
    e = [ei + pi + _dot_bf(ei, pi) for ei, pi in zip(e, p4)]
    w = base * 2
    while w <= RWKV_CHUNK:
        off = [jnp.where(same(w) & ~same(w // 2), low, 0.0) for low in lows]
        wm = [oi + _dot_bf(ei, oi) for ei, oi in zip(e, off)]
        e = [ei - wi - _dot_bf(wi, ei) for ei, wi in zip(e, wm)]
        w *= 2
    return e


def _rwkv_kernel(h_ref, mu_ref, w0_ref, w2_ref, a0_ref, a2_ref, g2_ref, kk_ref, ka_ref,
                 rk_ref, lnw_ref, lnb_ref, o_ref, state_ref, last_ref):
    c = RWKV_CHUNK
    tb = h_ref.shape[0]
    nchunk = tb // c
    dim = o_ref.shape[-1]
    npair = dim // LANES

    @pl.when(pl.program_id(1) == 0)
    def _():
        state_ref[...] = jnp.zeros_like(state_ref)
        last_ref[...] = jnp.zeros_like(last_ref)

    h = h_ref[...]
    row = lax.broadcasted_iota(jnp.int32, h.shape, 0)
    shifted = jnp.where(row == 0, last_ref[...], pltpu.roll(h, 1, 0))
    last_ref[...] = h[tb - 1:tb, :]
    hs = h + (shifted - h) * mu_ref[...]
    r = hs[:, 0:dim]
    k = hs[:, dim:2 * dim]
    v = hs[:, 2 * dim:3 * dim]
    xwa = hs[:, 3 * dim:3 * dim + LANES]
    xg = hs[:, 3 * dim + LANES:3 * dim + 2 * LANES]

    wl = w0_ref[...] + _dot_x3(jnp.tanh(xwa), w2_ref[...])
    logw = -jax.nn.sigmoid(wl) * float(np.exp(-0.5))
    a = jax.nn.sigmoid(a0_ref[...] + _dot_x3(xwa, a2_ref[...]))
    gate = _dot_bf(jax.nn.sigmoid(xg), g2_ref[...])

    ri = lax.broadcasted_iota(jnp.int32, (LANES, LANES), 0)
    ci = lax.broadcasted_iota(jnp.int32, (LANES, LANES), 1)
    ones_bd = ((ri ^ ci) < HEAD_DIM).astype(BF16)
    ones_bd2 = jnp.concatenate([ones_bd, ones_bd], axis=0)

    def head_sum(x):
        cols = []
        for j in range(npair):
            hi, lo = _hi_lo(x[:, j * LANES:(j + 1) * LANES])
            cols.append(_dot(jnp.concatenate([hi, lo], axis=1), ones_bd2))
        return jnp.concatenate(cols, axis=1)

    kk = k * kk_ref[...]
    kk = kk / jnp.maximum(jnp.sqrt(head_sum(kk * kk)), L2_EPS)
    k2 = k * (1.0 + (a - 1.0) * ka_ref[...])
    bvec = kk * a

    ti = lax.broadcasted_iota(jnp.int32, (tb, tb), 0)
    si = lax.broadcasted_iota(jnp.int32, (tb, tb), 1)
    tri = (((ti ^ si) < c) & (si <= ti)).astype(BF16)
    cum = _dot(jnp.concatenate([tri] * 3, axis=1), jnp.concatenate(_split3(logw), axis=0))
    e_neg = jnp.exp(-cum)
    alpha = kk * jnp.exp(cum - logw)
    beta = bvec * e_neg
    kappa = k2 * e_neg
    rho = r * jnp.exp(cum)
    cum_end = jnp.concatenate(
        [jnp.broadcast_to(cum[(n + 1) * c - 1:(n + 1) * c, :], (c, dim)) for n in range(nchunk)], axis=0)
    to_end = jnp.exp(cum_end - cum)
    beta_e = bvec * to_end
    kappa_e = k2 * to_end
    w_end = jnp.exp(cum_end)

    same_head = (ri ^ ci) < c
    strict = same_head & (ci < ri)
    incl = same_head & (ci <= ri)
    eye = ri == ci
    lo = lax.broadcasted_iota(jnp.int32, (c, LANES), 1) < HEAD_DIM

    def stack(x2):
        return jnp.concatenate([jnp.where(lo, x2, 0.0), jnp.where(lo, 0.0, x2)], axis=0)

    units = [(n, j) for n in range(nchunk) for j in range(npair)]
    blk = lambda x, n, j: x[n * c:(n + 1) * c, j * LANES:(j + 1) * LANES]
    a_s = [stack(blk(alpha, n, j)) for n, j in units]
    rho_s = [stack(blk(rho, n, j)) for n, j in units]
    v_s = [stack(blk(v, n, j)) for n, j in units]
    ends = [jnp.concatenate([stack(blk(beta_e, n, j)), stack(blk(kappa_e, n, j))], axis=0) for n, j in units]
    bk = [jnp.concatenate([blk(beta, n, j)] * 2 + [blk(kappa, n, j)] * 2, axis=0) for n, j in units]
    sc = [_dot_bf(jnp.concatenate([ai, ri_], axis=0), bi, NT) for ai, ri_, bi in zip(a_s, rho_s, bk)]
    l_ab = [jnp.where(strict, s[:LANES, :LANES], 0.0) for s in sc]
    l_ak = [jnp.where(strict, s[:LANES, LANES:], 0.0) for s in sc]
    r_b = [jnp.where(incl, s[LANES:, :LANES], 0.0) for s in sc]
    r_k = [jnp.where(incl, s[LANES:, LANES:], 0.0) for s in sc]
    e_inv = _tri_inverse_minus_eye(l_ab, ri, ci)
    lkv = [_dot_bf(li, vi) for li, vi in zip(l_ak, v_s)]
    p_m = [-(ai + _dot_bf(ei, ai)) for ei, ai in zip(e_inv, a_s)]
    q_m = [-(xi + _dot_bf(ei, xi)) for ei, xi in zip(e_inv, lkv)]
    m_m = [jnp.where(eye, blk(w_end, n, j)[:1, :], 0.0) + _dot_bf(pi, ei[:LANES], TN)
           for (n, j), pi, ei in zip(units, p_m, ends)]
    n_m = [_dot_bf(jnp.concatenate([qi, vi], axis=0), ei, TN) for qi, vi, ei in zip(q_m, v_s, ends)]
    g_m = [ri_ + _dot_bf(rb, pi) for ri_, rb, pi in zip(rho_s, r_b, p_m)]
    h_m = [_dot_bf(jnp.concatenate([rb, rk], axis=1), jnp.concatenate([qi, vi], axis=0))
           for rb, rk, qi, vi in zip(r_b, r_k, q_m, v_s)]

    state = [state_ref[j] for j in range(npair)]
    ys = []
    for n in range(nchunk):
        idx = [n * npair + j for j in range(npair)]
        nxt = [_dot_bf(state[j], m_m[i]) + n_m[i] for j, i in enumerate(idx)]
        y = [_dot_bf(g_m[i], state[j], NT) + h_m[i] for j, i in enumerate(idx)]
        ys.append(jnp.concatenate([yi[:c] + yi[c:] for yi in y], axis=1))
        state = nxt
    for j in range(npair):
        state_ref[j] = state[j]
    y = jnp.concatenate(ys, axis=0)

    mean = head_sum(y) * (1.0 / HEAD_DIM)
    d = y - mean
    var = head_sum(d * d) * (1.0 / HEAD_DIM)
    y = d * lax.rsqrt(var + GN_EPS) * lnw_ref[...] + lnb_ref[...]
    y = y + head_sum(r * k2 * rk_ref[...]) * v
    o_ref[...] = y * gate


def _rwkv(hb, mu, w0, w2p, a0, a2p, g2, k_k, k_a, r_k, ln_w, ln_b, *, tb=4 * RWKV_CHUNK):
    b, s, cols = hb.shape
    dim = w0.shape[-1]
    c = tb
    row = lambda x: x.reshape(1, -1)
    vec = lambda n: pl.BlockSpec((1, n), lambda bi, t: (0, 0))
    mat = lambda m: pl.BlockSpec(m.shape, lambda bi, t: (0, 0))
    return pl.pallas_call(
        _rwkv_kernel,
        out_shape=jax.ShapeDtypeStruct((b, s, dim), F32),
        grid=(b, s // c),
        in_specs=[
            pl.BlockSpec((None, c, cols), lambda bi, t: (bi, t, 0)),
            vec(cols), vec(dim), mat(w2p), vec(dim), mat(a2p), mat(g2),
            vec(dim), vec(dim), vec(dim), vec(dim), vec(dim),
        ],
        out_specs=pl.BlockSpec((None, c, dim), lambda bi, t: (bi, t, 0)),
        scratch_shapes=[pltpu.VMEM((dim // LANES, LANES, LANES), F32), pltpu.VMEM((1, cols), F32)],
        compiler_params=_params("parallel", "arbitrary"),
        name="rwkv7",
    )(hb, row(mu), row(w0), w2p, row(a0), a2p, g2, row(k_k), row(k_a), row(r_k), row(ln_w), row(ln_b))


FOX_AUG = 3
FOX_BLOCK = 512
LOG2E = float(np.log2(np.e))


def _fox_in_kernel(x_ref, g_ref, w_ref, bf_ref, sel_ref, qa_ref, ka_ref, va_ref, carry_ref):
    tc = x_ref.shape[0]
    nheads = qa_ref.shape[-1] // LANES
    dim = nheads * HEAD_DIM
    scale = HEAD_DIM ** -0.5 * LOG2E

    @pl.when(pl.program_id(1) == 0)
    def _():
        carry_ref[...] = jnp.zeros_like(carry_ref)

    proj = _dot(_rms(x_ref[...], g_ref[...]).astype(BF16), w_ref[...])
    z = proj[:, 3 * dim:] + bf_ref[...]
    logf = jnp.minimum(z, 0.0) - jnp.log(1.0 + jnp.exp(-jnp.abs(z)))
    lane = lax.broadcasted_iota(jnp.int32, z.shape, 1)
    logf = jnp.where(lane < nheads, logf, 0.0)
    ti = lax.broadcasted_iota(jnp.int32, (tc, tc), 0)
    si = lax.broadcasted_iota(jnp.int32, (tc, tc), 1)
    cg = _sel_dot((si <= ti).astype(BF16), logf) + carry_ref[...]
    carry_ref[...] = cg[tc - 1:tc, :]

    hi, mid, low = _split3(cg * LOG2E)
    pieces = (hi.astype(F32) + pltpu.roll(mid.astype(F32), nheads, 1)
              + pltpu.roll(low.astype(F32), 2 * nheads, 1)).astype(BF16)
    q_c = _dot(pieces, sel_ref[0])
    k_c = -_dot(pieces, sel_ref[1])

    lane = lax.broadcasted_iota(jnp.int32, (tc, LANES), 1)
    lo = lane < HEAD_DIM
    q_one = ((lane >= HEAD_DIM + FOX_AUG) & (lane < HEAD_DIM + 2 * FOX_AUG)).astype(F32)
    k_one = ((lane >= HEAD_DIM) & (lane < HEAD_DIM + FOX_AUG)).astype(F32)
    v_one = (lane == HEAD_DIM).astype(F32)
    sources = ((0, q_c, q_one, qa_ref), (1, k_c, k_one, ka_ref), (2, None, v_one, va_ref))
    for j in range(nheads // 2):
        for part, c_aug, one, o_ref in sources:
            x = proj[:, part * dim + j * LANES:part * dim + (j + 1) * LANES]
            if part == 0:
                x = x * scale
            xr = pltpu.roll(x, HEAD_DIM, 1)
            for e, xe in ((0, x), (1, xr)):
                hs = slice((2 * j + e) * LANES, (2 * j + e + 1) * LANES)
                if c_aug is None:
                    o_ref[2 * j + e, 0] = jnp.where(lo, xe, one).T.astype(BF16)
                else:
                    o_ref[:, hs] = jnp.where(lo, xe, c_aug[:, hs] + one).astype(BF16)


def _fox_select_matrices(nheads):
    sel = np.zeros((2, LANES, nheads * LANES), np.float32)
    for h in range(nheads):
        for i in range(FOX_AUG):
            sel[0, i * nheads + h, h * LANES + HEAD_DIM + i] = 1.0
            sel[1, i * nheads + h, h * LANES + HEAD_DIM + FOX_AUG + i] = 1.0
    return jnp.asarray(sel, BF16)


def _fox_in(x, g, w_in, b_f, *, batch, tc):
    t, d = x.shape
    s = t // batch
    nheads = b_f.shape[-1]
    assert FOX_AUG * nheads <= LANES
    w = jnp.pad(w_in, ((0, 0), (0, LANES - nheads))).astype(BF16)
    bf = jnp.zeros((1, LANES), F32).at[0, :nheads].set(b_f)
    sel = _fox_select_matrices(nheads)
    wide = nheads * LANES
    nt = s // tc
    resident = pl.Buffered(1)
    return pl.pallas_call(
        _fox_in_kernel,
        out_shape=(jax.ShapeDtypeStruct((batch, s, wide), BF16),
                   jax.ShapeDtypeStruct((batch, s, wide), BF16),
                   jax.ShapeDtypeStruct((batch, nheads, nt, LANES, tc), BF16)),
        grid=(batch, nt),
        in_specs=[
            pl.BlockSpec((tc, d), lambda bi, ti: (bi * nt + ti, 0)),
            pl.BlockSpec((1, d), lambda bi, ti: (0, 0)),
            pl.BlockSpec(w.shape, lambda bi, ti: (0, 0), pipeline_mode=resident),
            pl.BlockSpec((1, LANES), lambda bi, ti: (0, 0)),
            pl.BlockSpec(sel.shape, lambda bi, ti: (0, 0, 0), pipeline_mode=resident),
        ],
        out_specs=(pl.BlockSpec((None, tc, wide), lambda bi, ti: (bi, ti, 0)),
                   pl.BlockSpec((None, tc, wide), lambda bi, ti: (bi, ti, 0)),
                   pl.BlockSpec((None, nheads, 1, LANES, tc), lambda bi, ti: (bi, 0, ti, 0, 0))),
        scratch_shapes=[pltpu.VMEM((1, LANES), F32)],
        compiler_params=_params("parallel", "arbitrary"),
        name="fox_in",
    )(x, g.reshape(1, d), w, bf, sel)


def _fox_attn_kernel(q_ref, k_ref, v_ref, o_ref, m_ref, acc_ref):
    tq = q_ref.shape[0]
    nh = q_ref.shape[1] // LANES
    qi = pl.program_id(2)
    m_ref[...] = jnp.full_like(m_ref, -jnp.inf)
    acc_ref[...] = jnp.zeros_like(acc_ref)
    keys = lax.broadcasted_iota(jnp.int32, (tq, tq), 0)
    queries = lax.broadcasted_iota(jnp.int32, (tq, tq), 1)

    def block(kb, diagonal):
        start = pl.multiple_of(kb * tq, tq)
        logits = [_dot(k_ref[pl.ds(start, tq), e * LANES:(e + 1) * LANES],
                       q_ref[:, e * LANES:(e + 1) * LANES], NT) for e in range(nh)]
        for e, s in enumerate(logits):
            if diagonal:
                s = jnp.where(keys <= queries, s, -jnp.inf)
            m_prev = m_ref[e]
            m_new = jnp.maximum(m_prev, jnp.max(s, axis=0, keepdims=True))
            p = jnp.exp2(s - m_new).astype(BF16)
            acc_ref[e] = jnp.exp2(m_prev - m_new) * acc_ref[e] + _dot(v_ref[e, kb], p)
            m_ref[e] = m_new

    def body(kb, carry):
        block(kb, False)
        return carry

    lax.fori_loop(0, qi, body, 0)
    block(qi, True)
    for pair in range(nh // 2):
        outs = []
        for e in (2 * pair, 2 * pair + 1):
            acc = acc_ref[e]
            outs.append(acc[:HEAD_DIM] / acc[HEAD_DIM:HEAD_DIM + 1])
        o_ref[:, pair * LANES:(pair + 1) * LANES] = jnp.concatenate(outs, axis=0).T


def _fox_attn(q_aug, k_aug, v_aug, *, tq, nh=4):
    b, s, wide = q_aug.shape
    nheads = wide // LANES
    return pl.pallas_call(
        _fox_attn_kernel,
        out_shape=jax.ShapeDtypeStruct((b, s, nheads * HEAD_DIM), F32),
        grid=(b, nheads // nh, s // tq),
        in_specs=[
            pl.BlockSpec((None, tq, nh * LANES), lambda bi, g, qi: (bi, qi, g)),
            pl.BlockSpec((None, s, nh * LANES), lambda bi, g, qi: (bi, 0, g)),
            pl.BlockSpec((None, nh, s // tq, LANES, tq), lambda bi, g, qi: (bi, g, 0, 0, 0)),
        ],
        out_specs=pl.BlockSpec((None, tq, nh * HEAD_DIM), lambda bi, g, qi: (bi, qi, g)),
        scratch_shapes=[pltpu.VMEM((nh, 1, tq), F32), pltpu.VMEM((nh, LANES, tq), F32)],
        compiler_params=_params("parallel", "parallel", "arbitrary"),
        name="fox_attn",
    )(q_aug, k_aug, v_aug)


def kernel(x, p, ffn1_norm, ffn1_w_gu, ffn1_w_down, mix_norm, ffn2_norm, ffn2_w_gu, ffn2_w_down, ple_norm, ple_w_gate, ple_w_proj, even_w_in, even_w_out, swa_sinks, rwkv_mu, rwkv_w0, rwkv_w2, rwkv_a0, rwkv_a2, rwkv_g2, rwkv_k_k, rwkv_k_a, rwkv_r_k, rwkv_ln_w, rwkv_ln_b, fox_w_in, fox_b_f, fox_w_out, final_norm):
    b, s, d = x.shape
    depth = p.shape[0]
    t = b * s
    bf = lambda w: w.astype(BF16)
    swa_q = SWA_HEADS * HEAD_DIM
    swa_cols = swa_q + 2 * (SWA_HEADS // SWA_GROUP) * HEAD_DIM
    rwkv_dim = rwkv_w0.shape[-1]
    lora = rwkv_w2.shape[1]
    fox_heads = fox_b_f.shape[-1]
    fox_dim = fox_heads * HEAD_DIM

    x = x.reshape(t, d)
    for i in range(depth):
        j = i // 2
        x = _ffn(x, ffn1_norm[i], bf(ffn1_w_gu[i]), bf(ffn1_w_down[i]))
        if i % 2 == 0:
            w_in = even_w_in[j]
            qkv = _rms_matmul(x, mix_norm[i], bf(w_in[:, :swa_cols]))
            hb = _rms_matmul(x, mix_norm[i], bf(w_in[:, swa_cols:]))
            ya = _swa(qkv.reshape(b, s, swa_cols), swa_sinks[j])
            zeros = jnp.zeros((lora, rwkv_dim), F32)
            w2p = jnp.concatenate([rwkv_w2[j], zeros], axis=0)
            a2p = jnp.concatenate([zeros, rwkv_a2[j]], axis=0)
            yb = _rwkv(hb.reshape(b, s, -1), rwkv_mu[j], rwkv_w0[j], w2p, rwkv_a0[j], a2p,
                       bf(rwkv_g2[j]), rwkv_k_k[j], rwkv_k_a[j], rwkv_r_k[j].reshape(-1),
                       rwkv_ln_w[j], rwkv_ln_b[j])
            w_out = bf(even_w_out[j])
            x = _proj_res([ya.reshape(t, swa_q), yb.reshape(t, rwkv_dim)],
                          [w_out[:swa_q], w_out[swa_q:]], x)
        else:
            q_aug, k_aug, v_aug = _fox_in(x, mix_norm[i], fox_w_in[j], fox_b_f[j], batch=b, tc=FOX_BLOCK)
            yc = _fox_attn(q_aug, k_aug, v_aug, tq=FOX_BLOCK)
            x = _proj_res([yc.reshape(t, fox_dim)], [bf(fox_w_out[j])], x)
        x = _ffn(x, ffn2_norm[i], bf(ffn2_w_gu[i]), bf(ffn2_w_down[i]))
        x = _ple(x, ple_norm[i], bf(ple_w_gate[i]), p[i].reshape(t, -1), bf(ple_w_proj[i]),
                 final_norm, final=(i == depth - 1))
    return x.reshape(b, s, d)
```

```python
import functools

import jax
import jax.numpy as jnp
import numpy as np
from jax import lax
from jax.experimental import pallas as pl
from jax.experimental.pallas import tpu as pltpu

F32 = jnp.float32
BF16 = jnp.bfloat16

LANES = 128
HEAD_DIM = 64
SWA_HEADS = 8
SWA_GROUP = 4
SWA_BLOCK = 128
RWKV_CHUNK = 64
NORM_EPS = 1e-6
GN_EPS = 64e-5
L2_EPS = 1e-12
LOG2E = float(np.log2(np.e))
VMEM_LIMIT = 56 * 1024 * 1024

NN = (((1,), (0,)), ((), ()))
NT = (((1,), (1,)), ((), ()))
TN = (((0,), (0,)), ((), ()))


def _dot(a, b, dims=NN):
    return lax.dot_general(a, b, dims, preferred_element_type=F32)


def _dot_bf(a, b, dims=NN):
    return _dot(a.astype(BF16), b.astype(BF16), dims)


def _hi_lo(x):
    hi = x.astype(BF16)
    lo = (x - hi.astype(F32)).astype(BF16)
    return hi, lo


def _dot_x3(a, b, dims=NN):
    ah, al = _hi_lo(a)
    bh, bl = _hi_lo(b)
    return _dot(ah, bh, dims) + (_dot(ah, bl, dims) + _dot(al, bh, dims))


def _split3(x):
    hi = x.astype(BF16)
    r1 = x - hi.astype(F32)
    mid = r1.astype(BF16)
    lo = (r1 - mid.astype(F32)).astype(BF16)
    return hi, mid, lo


def _sel_dot(sel, x):
    hi, mid, lo = _split3(x)
    return _dot(sel, hi) + (_dot(sel, mid) + _dot(sel, lo))


def _dot_sel(x, sel):
    hi, mid, lo = _split3(x)
    return _dot(hi, sel) + (_dot(mid, sel) + _dot(lo, sel))


def _rms(x, g):
    ms = jnp.mean(x * x, axis=-1, keepdims=True)
    return x * lax.rsqrt(ms + NORM_EPS) * g


def _params(*sem):
    return pltpu.CompilerParams(dimension_semantics=sem, vmem_limit_bytes=VMEM_LIMIT)


def _ffn_kernel(x_ref, g_ref, wg_ref, wu_ref, wd_ref, o_ref):
    x = x_ref[...]
    hn = _rms(x, g_ref[...]).astype(BF16)
    gate = _dot(hn, wg_ref[...])
    up = _dot(hn, wu_ref[...])
    act = (gate * jax.nn.sigmoid(gate) * up).astype(BF16)
    o_ref[...] = x + 0.5 * _dot(act, wd_ref[...])


def _ffn(x, g, w_gu, w_down, *, tm=512):
    t, d = x.shape
    dff = w_down.shape[0]
    resident = pl.Buffered(1)
    return pl.pallas_call(
        _ffn_kernel,
        out_shape=jax.ShapeDtypeStruct((t, d), F32),
        grid=(t // tm,),
        in_specs=[
            pl.BlockSpec((tm, d), lambda i: (i, 0)),
            pl.BlockSpec((1, d), lambda i: (0, 0)),
            pl.BlockSpec((d, dff), lambda i: (0, 0), pipeline_mode=resident),
            pl.BlockSpec((d, dff), lambda i: (0, 1), pipeline_mode=resident),
            pl.BlockSpec((dff, d), lambda i: (0, 0), pipeline_mode=resident),
        ],
        out_specs=pl.BlockSpec((tm, d), lambda i: (i, 0)),
        compiler_params=_params("parallel"),
        name="ffn",
    )(x, g.reshape(1, d), w_gu, w_gu, w_down)


def _rms_matmul_kernel(x_ref, g_ref, w_ref, o_ref, hn_ref):
    @pl.when(pl.program_id(1) == 0)
    def _():
        hn_ref[...] = _rms(x_ref[...], g_ref[...]).astype(BF16)

    o_ref[...] = _dot(hn_ref[...], w_ref[...]).astype(o_ref.dtype)


def _rms_matmul(x, g, w, *, tm=512, tn=None, out_dtype=F32):
    t, d = x.shape
    n = w.shape[1]
    tn = n if tn is None else tn
    return pl.pallas_call(
        _rms_matmul_kernel,
        out_shape=jax.ShapeDtypeStruct((t, n), out_dtype),
        grid=(t // tm, n // tn),
        in_specs=[
            pl.BlockSpec((tm, d), lambda i, j: (i, 0)),
            pl.BlockSpec((1, d), lambda i, j: (0, 0)),
            pl.BlockSpec((d, tn), lambda i, j: (0, j)),
        ],
        out_specs=pl.BlockSpec((tm, tn), lambda i, j: (i, j)),
        scratch_shapes=[pltpu.VMEM((tm, d), BF16)],
        compiler_params=_params("parallel", "arbitrary"),
        name="rms_matmul",
    )(x, g.reshape(1, d), w)


def _proj_res_kernel(n_in, *refs):
    a_refs = refs[:n_in]
    w_refs = refs[n_in:2 * n_in]
    x_ref, o_ref = refs[2 * n_in], refs[2 * n_in + 1]
    acc = x_ref[...]
    for a_ref, w_ref in zip(a_refs, w_refs):
        acc = acc + _dot(a_ref[...].astype(BF16), w_ref[...])
    o_ref[...] = acc


def _proj_res(a_list, w_list, x, *, tm=512):
    t, d = x.shape
    n_in = len(a_list)
    in_specs = [pl.BlockSpec((tm, a.shape[1]), lambda i: (i, 0)) for a in a_list]
    in_specs += [pl.BlockSpec(w.shape, lambda i: (0, 0)) for w in w_list]
    in_specs += [pl.BlockSpec((tm, d), lambda i: (i, 0))]
    return pl.pallas_call(
        functools.partial(_proj_res_kernel, n_in),
        out_shape=jax.ShapeDtypeStruct((t, d), F32),
        grid=(t // tm,),
        in_specs=in_specs,
        out_specs=pl.BlockSpec((tm, d), lambda i: (i, 0)),
        compiler_params=_params("parallel"),
        name="proj_res",
    )(*a_list, *w_list, x)


def _ple_kernel(final, x_ref, g_ref, wg_ref, p_ref, wp_ref, fn_ref, o_ref):
    x = x_ref[...]
    hn = _rms(x, g_ref[...]).astype(BF16)
    gate = jax.nn.sigmoid(_dot(hn, wg_ref[...]))
    y = x + gate * _dot(p_ref[...].astype(BF16), wp_ref[...])
    if final:
        y = _rms(y, fn_ref[...])
    o_ref[...] = y


def _ple(x, g, w_gate, p, w_proj, final_g, *, final, tm=512):
    t, d = x.shape
    pd = p.shape[1]
    return pl.pallas_call(
        functools.partial(_ple_kernel, final),
        out_shape=jax.ShapeDtypeStruct((t, d), F32),
        grid=(t // tm,),
        in_specs=[
            pl.BlockSpec((tm, d), lambda i: (i, 0)),
            pl.BlockSpec((1, d), lambda i: (0, 0)),
            pl.BlockSpec((d, d), lambda i: (0, 0)),
            pl.BlockSpec((tm, pd), lambda i: (i, 0)),
            pl.BlockSpec((pd, d), lambda i: (0, 0)),
            pl.BlockSpec((1, d), lambda i: (0, 0)),
        ],
        out_specs=pl.BlockSpec((tm, d), lambda i: (i, 0)),
        compiler_params=_params("parallel"),
        name="ple",
    )(x, g.reshape(1, d), w_gate, p, w_proj, final_g.reshape(1, d))


def _swa_kernel(sink_ref, q_ref, kp_ref, kc_ref, vp_ref, vc_ref, o_ref):
    n = pl.program_id(1)
    blk = SWA_BLOCK
    scale = HEAD_DIM ** -0.5
    k = jnp.concatenate([kp_ref[...], kc_ref[...]], axis=0)
    v = jnp.concatenate([vp_ref[...], vc_ref[...]], axis=0)
    kr = pltpu.roll(k, HEAD_DIM, 1)
    vr = pltpu.roll(v, HEAD_DIM, 1)
    lo_kv = lax.broadcasted_iota(jnp.int32, k.shape, 1) < HEAD_DIM
    kdup = [jnp.where(lo_kv, k, kr).astype(BF16), jnp.where(lo_kv, kr, k).astype(BF16)]
    vdup = [jnp.where(lo_kv, v, vr).astype(BF16), jnp.where(lo_kv, vr, v).astype(BF16)]

    qi = lax.broadcasted_iota(jnp.int32, (blk, 2 * blk), 0)
    ki = lax.broadcasted_iota(jnp.int32, (blk, 2 * blk), 1)
    dist = qi + blk - ki
    valid = (dist >= 0) & (dist < blk) & ((n > 0) | (ki >= blk))
    distf = dist.astype(F32)
    lo_q = lax.broadcasted_iota(jnp.int32, (blk, LANES), 1) < HEAD_DIM

    heads = range(SWA_HEADS)
    qm = []
    for h in heads:
        j, e = divmod(h, 2)
        q2 = q_ref[:, j * LANES:(j + 1) * LANES] * (scale * LOG2E)
        qm.append(jnp.where(lo_q if e == 0 else ~lo_q, q2, 0.0).astype(BF16))
    logits = [_dot(qm[h], kdup[h // SWA_GROUP], NT) for h in heads]
    ps, inv_denoms = [], []
    for h in heads:
        slope = 2.0 ** (-8.0 * (h + 1) / SWA_HEADS) * LOG2E
        s = jnp.where(valid, logits[h] - slope * distf, -jnp.inf)
        sink = sink_ref[h] * LOG2E
        m = jnp.maximum(jnp.max(s, axis=-1, keepdims=True), sink)
        p = jnp.exp2(s - m)
        inv_denoms.append(1.0 / (jnp.sum(p, axis=-1, keepdims=True) + jnp.exp2(sink - m)))
        ps.append(p.astype(BF16))
    outs = [_dot(ps[h], vdup[h // SWA_GROUP]) * inv_denoms[h] for h in heads]
    for j in range(SWA_HEADS // 2):
        o_ref[:, j * LANES:(j + 1) * LANES] = jnp.where(lo_q, outs[2 * j], outs[2 * j + 1])


def _swa(qkv, sinks):
    b, s, _ = qkv.shape
    blk = SWA_BLOCK
    nq = SWA_HEADS * HEAD_DIM
    kcol = nq // LANES
    vcol = kcol + 1
    prev = lambda bi, n: jnp.maximum(n - 1, 0)
    return pl.pallas_call(
        _swa_kernel,
        out_shape=jax.ShapeDtypeStruct((b, s, nq), F32),
        grid=(b, s // blk),
        in_specs=[
            pl.BlockSpec(memory_space=pltpu.SMEM),
            pl.BlockSpec((None, blk, nq), lambda bi, n: (bi, n, 0)),
            pl.BlockSpec((None, blk, LANES), lambda bi, n: (bi, prev(bi, n), kcol)),
            pl.BlockSpec((None, blk, LANES), lambda bi, n: (bi, n, kcol)),
            pl.BlockSpec((None, blk, LANES), lambda bi, n: (bi, prev(bi, n), vcol)),
            pl.BlockSpec((None, blk, LANES), lambda bi, n: (bi, n, vcol)),
        ],
        out_specs=pl.BlockSpec((None, blk, nq), lambda bi, n: (bi, n, 0)),
        compiler_params=_params("parallel", "arbitrary"),
        name="swa",
    )(sinks, qkv, qkv, qkv, qkv, qkv)


def _tri_inverse_minus_eye(lows, ri, ci):
    same = lambda w: (ri ^ ci) < w
    base = 8
    x = [jnp.where(same(base), -low, 0.0) for low in lows]
    p2 = [_dot_bf(xi, xi) for xi in x]
    e = [xi + pi + _dot_bf(xi, pi) for xi, pi in zip(x, p2)]
    p4 = [_dot_bf(pi, pi) for pi in p2]
    e = [ei + pi + _dot_bf(ei, pi) for ei, pi in zip(e, p4)]
    w = base * 2
    while w <= RWKV_CHUNK:
        off = [jnp.where(same(w) & ~same(w // 2), low, 0.0) for low in lows]
        wm = [oi + _dot_bf(ei, oi) for ei, oi in zip(e, off)]
        e = [ei - wi - _dot_bf(wi, ei) for ei, wi in zip(e, wm)]
        w *= 2
    return e


def _rwkv_kernel(h_ref, mu_ref, w0_ref, w2_ref, a0_ref, a2_ref, g2_ref, kk_ref, ka_ref,
                 rk_ref, lnw_ref, lnb_ref, o_ref, state_ref, last_ref):
    c = RWKV_CHUNK
    tb = h_ref.shape[0]
    nchunk = tb // c
    dim = o_ref.shape[-1]
    npair = dim // LANES

    @pl.when(pl.program_id(1) == 0)
    def _():
        state_ref[...] = jnp.zeros_like(state_ref)
        last_ref[...] = jnp.zeros_like(last_ref)

    h = h_ref[...]
    row = lax.broadcasted_iota(jnp.int32, h.shape, 0)
    shifted = jnp.where(row == 0, last_ref[...], pltpu.roll(h, 1, 0))
    last_ref[...] = h[tb - 1:tb, :]
    hs = h + (shifted - h) * mu_ref[...]
    r = hs[:, 0:dim]
    k = hs[:, dim:2 * dim]
    v = hs[:, 2 * dim:3 * dim]
    xwa = hs[:, 3 * dim:3 * dim + LANES]
    xg = hs[:, 3 * dim + LANES:3 * dim + 2 * LANES]

    wl = w0_ref[...] + _dot_x3(jnp.tanh(xwa), w2_ref[...])
    logw = -jax.nn.sigmoid(wl) * float(np.exp(-0.5))
    a = jax.nn.sigmoid(a0_ref[...] + _dot_x3(xwa, a2_ref[...]))
    gate = _dot_bf(jax.nn.sigmoid(xg), g2_ref[...])

    ri = lax.broadcasted_iota(jnp.int32, (LANES, LANES), 0)
    ci = lax.broadcasted_iota(jnp.int32, (LANES, LANES), 1)
    ones_bd = ((ri ^ ci) < HEAD_DIM).astype(BF16)
    ones_bd2 = jnp.concatenate([ones_bd, ones_bd], axis=0)

    def head_sum(x):
        cols = []
        for j in range(npair):
            hi, lo = _hi_lo(x[:, j * LANES:(j + 1) * LANES])
            cols.append(_dot(jnp.concatenate([hi, lo], axis=1), ones_bd2))
        return jnp.concatenate(cols, axis=1)

    kk = k * kk_ref[...]
    kk = kk / jnp.maximum(jnp.sqrt(head_sum(kk * kk)), L2_EPS)
    k2 = k * (1.0 + (a - 1.0) * ka_ref[...])
    bvec = kk * a

    ti = lax.broadcasted_iota(jnp.int32, (tb, tb), 0)
    si = lax.broadcasted_iota(jnp.int32, (tb, tb), 1)
    tri = (((ti ^ si) < c) & (si <= ti)).astype(BF16)
    cum = _dot(jnp.concatenate([tri] * 3, axis=1), jnp.concatenate(_split3(logw), axis=0))
    e_neg = jnp.exp(-cum)
    alpha = kk * jnp.exp(cum - logw)
    beta = bvec * e_neg
    kappa = k2 * e_neg
    rho = r * jnp.exp(cum)
    cum_end = jnp.concatenate(
        [jnp.broadcast_to(cum[(n + 1) * c - 1:(n + 1) * c, :], (c, dim)) for n in range(nchunk)], axis=0)
    to_end = jnp.exp(cum_end - cum)
    beta_e = bvec * to_end
    kappa_e = k2 * to_end
    w_end = jnp.exp(cum_end)

    same_head = (ri ^ ci) < c
    strict = same_head & (ci < ri)
    incl = same_head & (ci <= ri)
    eye = ri == ci
    lo = lax.broadcasted_iota(jnp.int32, (c, LANES), 1) < HEAD_DIM

    def stack(x2):
        return jnp.concatenate([jnp.where(lo, x2, 0.0), jnp.where(lo, 0.0, x2)], axis=0)

    units = [(n, j) for n in range(nchunk) for j in range(npair)]
    blk = lambda x, n, j: x[n * c:(n + 1) * c, j * LANES:(j + 1) * LANES]
    a_s = [stack(blk(alpha, n, j)) for n, j in units]
    rho_s = [stack(blk(rho, n, j)) for n, j in units]
    v_s = [stack(blk(v, n, j)) for n, j in units]
    ends = [jnp.concatenate([stack(blk(beta_e, n, j)), stack(blk(kappa_e, n, j))], axis=0) for n, j in units]
    bk = [jnp.concatenate([blk(beta, n, j)] * 2 + [blk(kappa, n, j)] * 2, axis=0) for n, j in units]
    sc = [_dot_bf(jnp.concatenate([ai, ri_], axis=0), bi, NT) for ai, ri_, bi in zip(a_s, rho_s, bk)]
    l_ab = [jnp.where(strict, s[:LANES, :LANES], 0.0) for s in sc]
    l_ak = [jnp.where(strict, s[:LANES, LANES:], 0.0) for s in sc]
    r_b = [jnp.where(incl, s[LANES:, :LANES], 0.0) for s in sc]
    r_k = [jnp.where(incl, s[LANES:, LANES:], 0.0) for s in sc]
    e_inv = _tri_inverse_minus_eye(l_ab, ri, ci)
    lkv = [_dot_bf(li, vi) for li, vi in zip(l_ak, v_s)]
    p_m = [-(ai + _dot_bf(ei, ai)) for ei, ai in zip(e_inv, a_s)]
    q_m = [-(xi + _dot_bf(ei, xi)) for ei, xi in zip(e_inv, lkv)]
    m_m = [jnp.where(eye, blk(w_end, n, j)[:1, :], 0.0) + _dot_bf(pi, ei[:LANES], TN)
           for (n, j), pi, ei in zip(units, p_m, ends)]
    n_m = [_dot_bf(jnp.concatenate([qi, vi], axis=0), ei, TN) for qi, vi, ei in zip(q_m, v_s, ends)]
    g_m = [ri_ + _dot_bf(rb, pi) for ri_, rb, pi in zip(rho_s, r_b, p_m)]
    h_m = [_dot_bf(jnp.concatenate([rb, rk], axis=1), jnp.concatenate([qi, vi], axis=0))
           for rb, rk, qi, vi in zip(r_b, r_k, q_m, v_s)]

    state = [state_ref[j] for j in range(npair)]
    ys = []
    for n in range(nchunk):
        idx = [n * npair + j for j in range(npair)]
        nxt = [_dot_bf(state[j], m_m[i]) + n_m[i] for j, i in enumerate(idx)]
        y = [_dot_bf(g_m[i], state[j], NT) + h_m[i] for j, i in enumerate(idx)]
        ys.append(jnp.concatenate([yi[:c] + yi[c:] for yi in y], axis=1))
        state = nxt
    for j in range(npair):
        state_ref[j] = state[j]
    y = jnp.concatenate(ys, axis=0)

    mean = head_sum(y) * (1.0 / HEAD_DIM)
    d = y - mean
    var = head_sum(d * d) * (1.0 / HEAD_DIM)
    y = d * lax.rsqrt(var + GN_EPS) * lnw_ref[...] + lnb_ref[...]
    y = y + head_sum(r * k2 * rk_ref[...]) * v
    o_ref[...] = y * gate


def _rwkv(hb, mu, w0, w2p, a0, a2p, g2, k_k, k_a, r_k, ln_w, ln_b, *, tb=4 * RWKV_CHUNK):
    b, s, cols = hb.shape
    dim = w0.shape[-1]
    c = tb
    row = lambda x: x.reshape(1, -1)
    vec = lambda n: pl.BlockSpec((1, n), lambda bi, t: (0, 0))
    mat = lambda m: pl.BlockSpec(m.shape, lambda bi, t: (0, 0))
    return pl.pallas_call(
        _rwkv_kernel,
        out_shape=jax.ShapeDtypeStruct((b, s, dim), F32),
        grid=(b, s // c),
        in_specs=[
            pl.BlockSpec((None, c, cols), lambda bi, t: (bi, t, 0)),
            vec(cols), vec(dim), mat(w2p), vec(dim), mat(a2p), mat(g2),
            vec(dim), vec(dim), vec(dim), vec(dim), vec(dim),
        ],
        out_specs=pl.BlockSpec((None, c, dim), lambda bi, t: (bi, t, 0)),
        scratch_shapes=[pltpu.VMEM((dim // LANES, LANES, LANES), F32), pltpu.VMEM((1, cols), F32)],
        compiler_params=_params("parallel", "arbitrary"),
        name="rwkv7",
    )(hb, row(mu), row(w0), w2p, row(a0), a2p, g2, row(k_k), row(k_a), row(r_k), row(ln_w), row(ln_b))


FOX_AUG = 3
FOX_BLOCK = 512


def _fox_in_kernel(x_ref, g_ref, w_ref, bf_ref, sel_ref, qa_ref, ka_ref, va_ref, carry_ref):
    tc = x_ref.shape[0]
    nheads = qa_ref.shape[-1] // LANES
    dim = nheads * HEAD_DIM
    scale = HEAD_DIM ** -0.5 * LOG2E

    @pl.when(pl.program_id(1) == 0)
    def _():
        carry_ref[...] = jnp.zeros_like(carry_ref)

    proj = _dot(_rms(x_ref[...], g_ref[...]).astype(BF16), w_ref[...])
    z = proj[:, 3 * dim:] + bf_ref[...]
    logf = jnp.minimum(z, 0.0) - jnp.log(1.0 + jnp.exp(-jnp.abs(z)))
    lane = lax.broadcasted_iota(jnp.int32, z.shape, 1)
    logf = jnp.where(lane < nheads, logf, 0.0)
    ti = lax.broadcasted_iota(jnp.int32, (tc, tc), 0)
    si = lax.broadcasted_iota(jnp.int32, (tc, tc), 1)
    cg = _sel_dot((si <= ti).astype(BF16), logf) + carry_ref[...]
    carry_ref[...] = cg[tc - 1:tc, :]

    hi, mid, low = _split3(cg * LOG2E)
    pieces = (hi.astype(F32) + pltpu.roll(mid.astype(F32), nheads, 1)
              + pltpu.roll(low.astype(F32), 2 * nheads, 1)).astype(BF16)
    q_c = _dot(pieces, sel_ref[0])
    k_c = -_dot(pieces, sel_ref[1])

    lane = lax.broadcasted_iota(jnp.int32, (tc, LANES), 1)
    lo = lane < HEAD_DIM
    q_one = ((lane >= HEAD_DIM + FOX_AUG) & (lane < HEAD_DIM + 2 * FOX_AUG)).astype(F32)
    k_one = ((lane >= HEAD_DIM) & (lane < HEAD_DIM + FOX_AUG)).astype(F32)
    v_one = (lane == HEAD_DIM).astype(F32)
    sources = ((0, q_c, q_one, qa_ref), (1, k_c, k_one, ka_ref), (2, None, v_one, va_ref))
    for j in range(nheads // 2):
        for part, c_aug, one, o_ref in sources:
            x = proj[:, part * dim + j * LANES:part * dim + (j + 1) * LANES]
            if part == 0:
                x = x * scale
            xr = pltpu.roll(x, HEAD_DIM, 1)
            for e, xe in ((0, x), (1, xr)):
                hs = slice((2 * j + e) * LANES, (2 * j + e + 1) * LANES)
                if c_aug is None:
                    o_ref[2 * j + e, 0] = jnp.where(lo, xe, one).T.astype(BF16)
                else:
                    o_ref[:, hs] = jnp.where(lo, xe, c_aug[:, hs] + one).astype(BF16)


def _fox_select_matrices(nheads):
    sel = np.zeros((2, LANES, nheads * LANES), np.float32)
    for h in range(nheads):
        for i in range(FOX_AUG):
            sel[0, i * nheads + h, h * LANES + HEAD_DIM + i] = 1.0
            sel[1, i * nheads + h, h * LANES + HEAD_DIM + FOX_AUG + i] = 1.0
    return jnp.asarray(sel, BF16)


def _fox_in(x, g, w_in, b_f, *, batch, tc):
    t, d = x.shape
    s = t // batch
    nheads = b_f.shape[-1]
    assert FOX_AUG * nheads <= LANES
    w = jnp.pad(w_in, ((0, 0), (0, LANES - nheads))).astype(BF16)
    bf = jnp.zeros((1, LANES), F32).at[0, :nheads].set(b_f)
    sel = _fox_select_matrices(nheads)
    wide = nheads * LANES
    nt = s // tc
    resident = pl.Buffered(1)
    return pl.pallas_call(
        _fox_in_kernel,
        out_shape=(jax.ShapeDtypeStruct((batch, s, wide), BF16),
                   jax.ShapeDtypeStruct((batch, s, wide), BF16),
                   jax.ShapeDtypeStruct((batch, nheads, nt, LANES, tc), BF16)),
        grid=(batch, nt),
        in_specs=[
            pl.BlockSpec((tc, d), lambda bi, ti: (bi * nt + ti, 0)),
            pl.BlockSpec((1, d), lambda bi, ti: (0, 0)),
            pl.BlockSpec(w.shape, lambda bi, ti: (0, 0), pipeline_mode=resident),
            pl.BlockSpec((1, LANES), lambda bi, ti: (0, 0)),
            pl.BlockSpec(sel.shape, lambda bi, ti: (0, 0, 0), pipeline_mode=resident),
        ],
        out_specs=(pl.BlockSpec((None, tc, wide), lambda bi, ti: (bi, ti, 0)),
                   pl.BlockSpec((None, tc, wide), lambda bi, ti: (bi, ti, 0)),
                   pl.BlockSpec((None, nheads, 1, LANES, tc), lambda bi, ti: (bi, 0, ti, 0, 0))),
        scratch_shapes=[pltpu.VMEM((1, LANES), F32)],
        compiler_params=_params("parallel", "arbitrary"),
        name="fox_in",
    )(x, g.reshape(1, d), w, bf, sel)


def _fox_attn_kernel(q_ref, k_ref, v_ref, o_ref, m_ref, acc_ref):
    tq = q_ref.shape[0]
    nh = q_ref.shape[1] // LANES
    qi = pl.program_id(2)
    m_ref[...] = jnp.full_like(m_ref, -jnp.inf)
    acc_ref[...] = jnp.zeros_like(acc_ref)
    keys = lax.broadcasted_iota(jnp.int32, (tq, tq), 0)
    queries = lax.broadcasted_iota(jnp.int32, (tq, tq), 1)

    def block(kb, diagonal):
        start = pl.multiple_of(kb * tq, tq)
        logits = [_dot(k_ref[pl.ds(start, tq), e * LANES:(e + 1) * LANES],
                       q_ref[:, e * LANES:(e + 1) * LANES], NT) for e in range(nh)]
        for e, s in enumerate(logits):
            if diagonal:
                s = jnp.where(keys <= queries, s, -jnp.inf)
            m_prev = m_ref[e]
            m_new = jnp.maximum(m_prev, jnp.max(s, axis=0, keepdims=True))
            p = jnp.exp2(s - m_new).astype(BF16)
            acc_ref[e] = jnp.exp2(m_prev - m_new) * acc_ref[e] + _dot(v_ref[e, kb], p)
            m_ref[e] = m_new

    def body(kb, carry):
        block(kb, False)
        return carry

    lax.fori_loop(0, qi, body, 0)
    block(qi, True)
    for pair in range(nh // 2):
        outs = []
        for e in (2 * pair, 2 * pair + 1):
            acc = acc_ref[e]
            outs.append(acc[:HEAD_DIM] / acc[HEAD_DIM:HEAD_DIM + 1])
        o_ref[:, pair * LANES:(pair + 1) * LANES] = jnp.concatenate(outs, axis=0).T


def _fox_attn(q_aug, k_aug, v_aug, *, tq, nh=4):
    b, s, wide = q_aug.shape
    nheads = wide // LANES
    return pl.pallas_call(
        _fox_attn_kernel,
        out_shape=jax.ShapeDtypeStruct((b, s, nheads * HEAD_DIM), F32),
        grid=(b, nheads // nh, s // tq),
        in_specs=[
            pl.BlockSpec((None, tq, nh * LANES), lambda bi, g, qi: (bi, qi, g)),
            pl.BlockSpec((None, s, nh * LANES), lambda bi, g, qi: (bi, 0, g)),
            pl.BlockSpec((None, nh, s // tq, LANES, tq), lambda bi, g, qi: (bi, g, 0, 0, 0)),
        ],
        out_specs=pl.BlockSpec((None, tq, nh * HEAD_DIM), lambda bi, g, qi: (bi, qi, g)),
        scratch_shapes=[pltpu.VMEM((nh, 1, tq), F32), pltpu.VMEM((nh, LANES, tq), F32)],
        compiler_params=_params("parallel", "parallel", "arbitrary"),
        name="fox_attn",
    )(q_aug, k_aug, v_aug)


def kernel(x, p, ffn1_norm, ffn1_w_gu, ffn1_w_down, mix_norm, ffn2_norm, ffn2_w_gu, ffn2_w_down, ple_norm, ple_w_gate, ple_w_proj, even_w_in, even_w_out, swa_sinks, rwkv_mu, rwkv_w0, rwkv_w2, rwkv_a0, rwkv_a2, rwkv_g2, rwkv_k_k, rwkv_k_a, rwkv_r_k, rwkv_ln_w, rwkv_ln_b, fox_w_in, fox_b_f, fox_w_out, final_norm):
    b, s, d = x.shape
    depth = p.shape[0]
    t = b * s
    bf = lambda w: w.astype(BF16)
    swa_q = SWA_HEADS * HEAD_DIM
    swa_cols = swa_q + 2 * (SWA_HEADS // SWA_GROUP) * HEAD_DIM
    rwkv_dim = rwkv_w0.shape[-1]
    lora = rwkv_w2.shape[1]
    fox_heads = fox_b_f.shape[-1]
    fox_dim = fox_heads * HEAD_DIM

    x = x.reshape(t, d)
    for i in range(depth):
        j = i // 2
        x = _ffn(x, ffn1_norm[i], bf(ffn1_w_gu[i]), bf(ffn1_w_down[i]))
        if i % 2 == 0:
            w_in = even_w_in[j]
            qkv = _rms_matmul(x, mix_norm[i], bf(w_in[:, :swa_cols]))
            hb = _rms_matmul(x, mix_norm[i], bf(w_in[:, swa_cols:]))
            ya = _swa(qkv.reshape(b, s, swa_cols), swa_sinks[j])
            zeros = jnp.zeros((lora, rwkv_dim), F32)
            w2p = jnp.concatenate([rwkv_w2[j], zeros], axis=0)
            a2p = jnp.concatenate([zeros, rwkv_a2[j]], axis=0)
            yb = _rwkv(hb.reshape(b, s, -1), rwkv_mu[j], rwkv_w0[j], w2p, rwkv_a0[j], a2p,
                       bf(rwkv_g2[j]), rwkv_k_k[j], rwkv_k_a[j], rwkv_r_k[j].reshape(-1),
                       rwkv_ln_w[j], rwkv_ln_b[j])
            w_out = bf(even_w_out[j])
            x = _proj_res([ya.reshape(t, swa_q), yb.reshape(t, rwkv_dim)],
                          [w_out[:swa_q], w_out[swa_q:]], x)
        else:
            q_aug, k_aug, v_aug = _fox_in(x, mix_norm[i], fox_w_in[j], fox_b_f[j], batch=b, tc=FOX_BLOCK)
            yc = _fox_attn(q_aug, k_aug, v_aug, tq=FOX_BLOCK)
            x = _proj_res([yc.reshape(t, fox_dim)], [bf(fox_w_out[j])], x)
        x = _ffn(x, ffn2_norm[i], bf(ffn2_w_gu[i]), bf(ffn2_w_down[i]))
        x = _ple(x, ple_norm[i], bf(ple_w_gate[i]), p[i].reshape(t, -1), bf(ple_w_proj[i]),
                 final_norm, final=(i == depth - 1))
    return x.reshape(b, s, d)
```

```python
import functools

import jax
import jax.numpy as jnp
import numpy as np
from jax import lax
from jax.experimental import pallas as pl
from jax.experimental.pallas import tpu as pltpu

F32 = jnp.float32
BF16 = jnp.bfloat16

LANES = 128
HEAD_DIM = 64
SWA_HEADS = 8
SWA_GROUP = 4
SWA_BLOCK = 128
RWKV_CHUNK = 64
NORM_EPS = 1e-6
GN_EPS = 64e-5
L2_EPS = 1e-12
LOG2E = float(np.log2(np.e))
VMEM_LIMIT = 56 * 1024 * 1024

NN = (((1,), (0,)), ((), ()))
NT = (((1,), (1,)), ((), ()))
TN = (((0,), (0,)), ((), ()))


def _dot(a, b, dims=NN):
    return lax.dot_general(a, b, dims, preferred_element_type=F32)


def _dot_bf(a, b, dims=NN):
    return _dot(a.astype(BF16), b.astype(BF16), dims)


def _hi_lo(x):
    hi = x.astype(BF16)
    lo = (x - hi.astype(F32)).astype(BF16)
    return hi, lo


def _dot_x3(a, b, dims=NN):
    ah, al = _hi_lo(a)
    bh, bl = _hi_lo(b)
    return _dot(ah, bh, dims) + (_dot(ah, bl, dims) + _dot(al, bh, dims))


def _split3(x):
    hi = x.astype(BF16)
    r1 = x - hi.astype(F32)
    mid = r1.astype(BF16)
    lo = (r1 - mid.astype(F32)).astype(BF16)
    return hi, mid, lo


def _sel_dot(sel, x):
    hi, mid, lo = _split3(x)
    return _dot(sel, hi) + (_dot(sel, mid) + _dot(sel, lo))


def _dot_sel(x, sel):
    hi, mid, lo = _split3(x)
    return _dot(hi, sel) + (_dot(mid, sel) + _dot(lo, sel))


def _rms(x, g):
    ms = jnp.mean(x * x, axis=-1, keepdims=True)
    return x * lax.rsqrt(ms + NORM_EPS) * g


def _params(*sem):
    return pltpu.CompilerParams(dimension_semantics=sem, vmem_limit_bytes=VMEM_LIMIT)


def _swiglu_half_step(x, g, wg, wu, wd):
    hn = _rms(x, g).astype(BF16)
    gate = _dot(hn, wg)
    up = _dot(hn, wu)
    act = (gate * jax.nn.sigmoid(gate) * up).astype(BF16)
    return x + 0.5 * _dot(act, wd)


def _ffn_kernel(n_proj, x_ref, g_ref, wg_ref, wu_ref, wd_ref, *refs):
    x = _swiglu_half_step(x_ref[...], g_ref[...], wg_ref[...], wu_ref[...], wd_ref[...])
    if n_proj:
        gm_ref, w_ref = refs[:2]
        o_ref, proj_refs = refs[2], refs[3:]
        proj = _dot(_rms(x, gm_ref[...]).astype(BF16), w_ref[...])
        col = 0
        for p_ref in proj_refs:
            p_ref[...] = proj[:, col:col + p_ref.shape[1]]
            col += p_ref.shape[1]
    else:
        o_ref = refs[0]
    o_ref[...] = x


def _ffn(x, g, w_gu, w_down, proj=None, *, tm=512):
    t, d = x.shape
    dff = w_down.shape[0]
    resident = pl.Buffered(1)
    row = lambda n: pl.BlockSpec((tm, n), lambda i: (i, 0))
    const = lambda shape: pl.BlockSpec(shape, lambda i: (0, 0), pipeline_mode=resident)
    in_specs = [row(d), const((1, d)),
                const((d, dff)),
                pl.BlockSpec((d, dff), lambda i: (0, 1), pipeline_mode=resident),
                const((dff, d))]
    args = [x, g.reshape(1, d), w_gu, w_gu, w_down]
    out_shape = [jax.ShapeDtypeStruct((t, d), F32)]
    out_specs = [row(d)]
    widths = ()
    if proj is not None:
        gm, w, widths = proj
        assert sum(widths) == w.shape[1]
        in_specs += [const((1, d)), const(w.shape)]
        args += [gm.reshape(1, d), w]
        out_shape += [jax.ShapeDtypeStruct((t, n), F32) for n in widths]
        out_specs += [row(n) for n in widths]
    return pl.pallas_call(
        functools.partial(_ffn_kernel, len(widths)),
        out_shape=out_shape,
        grid=(t // tm,),
        in_specs=in_specs,
        out_specs=out_specs,
        compiler_params=_params("parallel"),
        name="ffn",
    )(*args)


def _post_mix_kernel(n_in, final, *refs):
    a_refs, w_refs = refs[:n_in], refs[n_in:2 * n_in]
    (x_ref, g2_ref, wg_ref, wu_ref, wd_ref, gp_ref, wpg_ref, p_ref, wpp_ref, fn_ref, o_ref) = refs[2 * n_in:]
    x = x_ref[...]
    for a_ref, w_ref in zip(a_refs, w_refs):
        x = x + _dot(a_ref[...], w_ref[...])
    x = _swiglu_half_step(x, g2_ref[...], wg_ref[...], wu_ref[...], wd_ref[...])
    gate = jax.nn.sigmoid(_dot(_rms(x, gp_ref[...]).astype(BF16), wpg_ref[...]))
    x = x + gate * _dot(p_ref[...].astype(BF16), wpp_ref[...])
    if final:
        x = _rms(x, fn_ref[...])
    o_ref[...] = x


def _post_mix(a_list, w_out_list, x, g2, w_gu, w_down, gp, w_gate, p, w_proj, final_g, *, final, tm=512):
    t, d = x.shape
    dff = w_down.shape[0]
    resident = pl.Buffered(1)
    row = lambda n: pl.BlockSpec((tm, n), lambda i: (i, 0))
    const = lambda shape: pl.BlockSpec(shape, lambda i: (0, 0), pipeline_mode=resident)
    in_specs = [row(a.shape[1]) for a in a_list] + [const(w.shape) for w in w_out_list]
    in_specs += [row(d), const((1, d)),
                 const((d, dff)),
                 pl.BlockSpec((d, dff), lambda i: (0, 1), pipeline_mode=resident),
                 const((dff, d)),
                 const((1, d)), const((d, d)), row(p.shape[1]), const(w_proj.shape), const((1, d))]
    return pl.pallas_call(
        functools.partial(_post_mix_kernel, len(a_list), final),
        out_shape=jax.ShapeDtypeStruct((t, d), F32),
        grid=(t // tm,),
        in_specs=in_specs,
        out_specs=row(d),
        compiler_params=_params("parallel"),
        name="post_mix",
    )(*a_list, *w_out_list, x, g2.reshape(1, d), w_gu, w_gu, w_down, gp.reshape(1, d), w_gate, p, w_proj,
      final_g.reshape(1, d))


def _swa_kernel(sink_ref, q_ref, kp_ref, kc_ref, vp_ref, vc_ref, o_ref):
    n = pl.program_id(1)
    blk = SWA_BLOCK
    scale = HEAD_DIM ** -0.5
    k = jnp.concatenate([kp_ref[...], kc_ref[...]], axis=0)
    v = jnp.concatenate([vp_ref[...], vc_ref[...]], axis=0)
    kr = pltpu.roll(k, HEAD_DIM, 1)
    vr = pltpu.roll(v, HEAD_DIM, 1)
    lo_kv = lax.broadcasted_iota(jnp.int32, k.shape, 1) < HEAD_DIM
    kdup = [jnp.where(lo_kv, k, kr).astype(BF16), jnp.where(lo_kv, kr, k).astype(BF16)]
    vdup = [jnp.where(lo_kv, v, vr).astype(BF16), jnp.where(lo_kv, vr, v).astype(BF16)]

    qi = lax.broadcasted_iota(jnp.int32, (blk, 2 * blk), 0)
    ki = lax.broadcasted_iota(jnp.int32, (blk, 2 * blk), 1)
    dist = qi + blk - ki
    valid = (dist >= 0) & (dist < blk) & ((n > 0) | (ki >= blk))
    distf = dist.astype(F32)
    lo_q = lax.broadcasted_iota(jnp.int32, (blk, LANES), 1) < HEAD_DIM

    heads = range(SWA_HEADS)
    qm = []
    for h in heads:
        j, e = divmod(h, 2)
        q2 = q_ref[:, j * LANES:(j + 1) * LANES] * (scale * LOG2E)
        qm.append(jnp.where(lo_q if e == 0 else ~lo_q, q2, 0.0).astype(BF16))
    logits = [_dot(qm[h], kdup[h // SWA_GROUP], NT) for h in heads]
    ps, inv_denoms = [], []
    for h in heads:
        slope = 2.0 ** (-8.0 * (h + 1) / SWA_HEADS) * LOG2E
        s = jnp.where(valid, logits[h] - slope * distf, -jnp.inf)
        sink = sink_ref[h] * LOG2E
        m = jnp.maximum(jnp.max(s, axis=-1, keepdims=True), sink)
        p = jnp.exp2(s - m)
        inv_denoms.append(1.0 / (jnp.sum(p, axis=-1, keepdims=True) + jnp.exp2(sink - m)))
        ps.append(p.astype(BF16))
    outs = [_dot(ps[h], vdup[h // SWA_GROUP]) * inv_denoms[h] for h in heads]
    for j in range(SWA_HEADS // 2):
        o_ref[:, j * LANES:(j + 1) * LANES] = jnp.where(lo_q, outs[2 * j], outs[2 * j + 1]).astype(o_ref.dtype)


def _swa(qkv, sinks):
    b, s, _ = qkv.shape
    blk = SWA_BLOCK
    nq = SWA_HEADS * HEAD_DIM
    kcol = nq // LANES
    vcol = kcol + 1
    prev = lambda bi, n: jnp.maximum(n - 1, 0)
    return pl.pallas_call(
        _swa_kernel,
        out_shape=jax.ShapeDtypeStruct((b, s, nq), BF16),
        grid=(b, s // blk),
        in_specs=[
            pl.BlockSpec(memory_space=pltpu.SMEM),
            pl.BlockSpec((None, blk, nq), lambda bi, n: (bi, n, 0)),
            pl.BlockSpec((None, blk, LANES), lambda bi, n: (bi, prev(bi, n), kcol)),
            pl.BlockSpec((None, blk, LANES), lambda bi, n: (bi, n, kcol)),
            pl.BlockSpec((None, blk, LANES), lambda bi, n: (bi, prev(bi, n), vcol)),
            pl.BlockSpec((None, blk, LANES), lambda bi, n: (bi, n, vcol)),
        ],
        out_specs=pl.BlockSpec((None, blk, nq), lambda bi, n: (bi, n, 0)),
        compiler_params=_params("parallel", "arbitrary"),
        name="swa",
    )(sinks, qkv, qkv, qkv, qkv, qkv)


def _tri_inverse_minus_eye(lows, ri, ci):
    same = lambda w: (ri ^ ci) < w
    base = 8
    x = [jnp.where(same(base), -low, 0.0) for low in lows]
    p2 = [_dot_bf(xi, xi) for xi in x]
    e = [xi + pi + _dot_bf(xi, pi) for xi, pi in zip(x, p2)]
    p4 = [_dot_bf(pi, pi) for pi in p2]
    e = [ei + pi + _dot_bf(ei, pi) for ei, pi in zip(e, p4)]
    w = base * 2
    while w <= RWKV_CHUNK:
        off = [jnp.where(same(w) & ~same(w // 2), low, 0.0) for low in lows]
        wm = [oi + _dot_bf(ei, oi) for ei, oi in zip(e, off)]
        e = [ei - wi - _dot_bf(wi, ei) for ei, wi in zip(e, wm)]
        w *= 2
    return e


def _rwkv_kernel(h_ref, mu_ref, w0_ref, w2_ref, a0_ref, a2_ref, g2_ref, kk_ref, ka_ref,
                 rk_ref, lnw_ref, lnb_ref, o_ref, state_ref, last_ref):
    c = RWKV_CHUNK
    tb = h_ref.shape[0]
    nchunk = tb // c
    dim = o_ref.shape[-1]
    npair = dim // LANES

    @pl.when(pl.program_id(1) == 0)
    def _():
        state_ref[...] = jnp.zeros_like(state_ref)
        last_ref[...] = jnp.zeros_like(last_ref)

    h = h_ref[...]
    row = lax.broadcasted_iota(jnp.int32, h.shape, 0)
    shifted = jnp.where(row == 0, last_ref[...], pltpu.roll(h, 1, 0))
    last_ref[...] = h[tb - 1:tb, :]
    hs = h + (shifted - h) * mu_ref[...]
    r = hs[:, 0:dim]
    k = hs[:, dim:2 * dim]
    v = hs[:, 2 * dim:3 * dim]
    xwa = hs[:, 3 * dim:3 * dim + LANES]
    xg = hs[:, 3 * dim + LANES:3 * dim + 2 * LANES]

    wl = w0_ref[...] + _dot_x3(jnp.tanh(xwa), w2_ref[...])
    logw = -jax.nn.sigmoid(wl) * float(np.exp(-0.5))
    a = jax.nn.sigmoid(a0_ref[...] + _dot_x3(xwa, a2_ref[...]))
    gate = _dot_bf(jax.nn.sigmoid(xg), g2_ref[...])

    ri = lax.broadcasted_iota(jnp.int32, (LANES, LANES), 0)
    ci = lax.broadcasted_iota(jnp.int32, (LANES, LANES), 1)
    ones_bd = ((ri ^ ci) < HEAD_DIM).astype(BF16)
    ones_bd2 = jnp.concatenate([ones_bd, ones_bd], axis=0)

    def head_sum(x):
        cols = []
        for j in range(npair):
            hi, lo = _hi_lo(x[:, j * LANES:(j + 1) * LANES])
            cols.append(_dot(jnp.concatenate([hi, lo], axis=1), ones_bd2))
        return jnp.concatenate(cols, axis=1)

    kk = k * kk_ref[...]
    kk = kk / jnp.maximum(jnp.sqrt(head_sum(kk * kk)), L2_EPS)
    k2 = k * (1.0 + (a - 1.0) * ka_ref[...])
    bvec = kk * a

    ti = lax.broadcasted_iota(jnp.int32, (tb, tb), 0)
    si = lax.broadcasted_iota(jnp.int32, (tb, tb), 1)
    tri = (((ti ^ si) < c) & (si <= ti)).astype(BF16)
    cum = _dot(jnp.concatenate([tri] * 3, axis=1), jnp.concatenate(_split3(logw), axis=0))
    e_neg = jnp.exp(-cum)
    alpha = kk * jnp.exp(cum - logw)
    beta = bvec * e_neg
    kappa = k2 * e_neg
    rho = r * jnp.exp(cum)
    cum_end = jnp.concatenate(
        [jnp.broadcast_to(cum[(n + 1) * c - 1:(n + 1) * c, :], (c, dim)) for n in range(nchunk)], axis=0)
    to_end = jnp.exp(cum_end - cum)
    beta_e = bvec * to_end
    kappa_e = k2 * to_end
    w_end = jnp.exp(cum_end)

    same_head = (ri ^ ci) < c
    strict = same_head & (ci < ri)
    incl = same_head & (ci <= ri)
    eye = ri == ci
    lo = lax.broadcasted_iota(jnp.int32, (c, LANES), 1) < HEAD_DIM

    def stack(x2):
        return jnp.concatenate([jnp.where(lo, x2, 0.0), jnp.where(lo, 0.0, x2)], axis=0)

    units = [(n, j) for n in range(nchunk) for j in range(npair)]
    blk = lambda x, n, j: x[n * c:(n + 1) * c, j * LANES:(j + 1) * LANES]
    a_s = [stack(blk(alpha, n, j)) for n, j in units]
    rho_s = [stack(blk(rho, n, j)) for n, j in units]
    v_s = [stack(blk(v, n, j)) for n, j in units]
    ends = [jnp.concatenate([stack(blk(beta_e, n, j)), stack(blk(kappa_e, n, j))], axis=0) for n, j in units]
    bk = [jnp.concatenate([blk(beta, n, j)] * 2 + [blk(kappa, n, j)] * 2, axis=0) for n, j in units]
    sc = [_dot_bf(jnp.concatenate([ai, ri_], axis=0), bi, NT) for ai, ri_, bi in zip(a_s, rho_s, bk)]
    l_ab = [jnp.where(strict, s[:LANES, :LANES], 0.0) for s in sc]
    l_ak = [jnp.where(strict, s[:LANES, LANES:], 0.0) for s in sc]
    r_b = [jnp.where(incl, s[LANES:, :LANES], 0.0) for s in sc]
    r_k = [jnp.where(incl, s[LANES:, LANES:], 0.0) for s in sc]
    e_inv = _tri_inverse_minus_eye(l_ab, ri, ci)
    lkv = [_dot_bf(li, vi) for li, vi in zip(l_ak, v_s)]
    p_m = [-(ai + _dot_bf(ei, ai)) for ei, ai in zip(e_inv, a_s)]
    q_m = [-(xi + _dot_bf(ei, xi)) for ei, xi in zip(e_inv, lkv)]
    m_m = [jnp.where(eye, blk(w_end, n, j)[:1, :], 0.0) + _dot_bf(pi, ei[:LANES], TN)
           for (n, j), pi, ei in zip(units, p_m, ends)]
    n_m = [_dot_bf(jnp.concatenate([qi, vi], axis=0), ei, TN) for qi, vi, ei in zip(q_m, v_s, ends)]
    g_m = [ri_ + _dot_bf(rb, pi) for ri_, rb, pi in zip(rho_s, r_b, p_m)]
    h_m = [_dot_bf(jnp.concatenate([rb, rk], axis=1), jnp.concatenate([qi, vi], axis=0))
           for rb, rk, qi, vi in zip(r_b, r_k, q_m, v_s)]

    state = [state_ref[j] for j in range(npair)]
    ys = []
    for n in range(nchunk):
        idx = [n * npair + j for j in range(npair)]
        nxt = [_dot_bf(state[j], m_m[i]) + n_m[i] for j, i in enumerate(idx)]
        y = [_dot_bf(g_m[i], state[j], NT) + h_m[i] for j, i in enumerate(idx)]
        ys.append(jnp.concatenate([yi[:c] + yi[c:] for yi in y], axis=1))
        state = nxt
    for j in range(npair):
        state_ref[j] = state[j]
    y = jnp.concatenate(ys, axis=0)

    mean = head_sum(y) * (1.0 / HEAD_DIM)
    d = y - mean
    var = head_sum(d * d) * (1.0 / HEAD_DIM)
    y = d * lax.rsqrt(var + GN_EPS) * lnw_ref[...] + lnb_ref[...]
    y = y + head_sum(r * k2 * rk_ref[...]) * v
    o_ref[...] = (y * gate).astype(o_ref.dtype)


def _rwkv(hb, mu, w0, w2p, a0, a2p, g2, k_k, k_a, r_k, ln_w, ln_b, *, tb=4 * RWKV_CHUNK):
    b, s, cols = hb.shape
    dim = w0.shape[-1]
    c = tb
    row = lambda x: x.reshape(1, -1)
    vec = lambda n: pl.BlockSpec((1, n), lambda bi, t: (0, 0))
    mat = lambda m: pl.BlockSpec(m.shape, lambda bi, t: (0, 0))
    return pl.pallas_call(
        _rwkv_kernel,
        out_shape=jax.ShapeDtypeStruct((b, s, dim), BF16),
        grid=(b, s // c),
        in_specs=[
            pl.BlockSpec((None, c, cols), lambda bi, t: (bi, t, 0)),
            vec(cols), vec(dim), mat(w2p), vec(dim), mat(a2p), mat(g2),
            vec(dim), vec(dim), vec(dim), vec(dim), vec(dim),
        ],
        out_specs=pl.BlockSpec((None, c, dim), lambda bi, t: (bi, t, 0)),
        scratch_shapes=[pltpu.VMEM((dim // LANES, LANES, LANES), F32), pltpu.VMEM((1, cols), F32)],
        compiler_params=_params("parallel", "arbitrary"),
        name="rwkv7",
    )(hb, row(mu), row(w0), w2p, row(a0), a2p, g2, row(k_k), row(k_a), row(r_k), row(ln_w), row(ln_b))


FOX_AUG = 3
FOX_BLOCK = 512


def _fox_in_kernel(x_ref, g_ref, w_ref, bf_ref, sel_ref, qa_ref, ka_ref, va_ref, carry_ref):
    tc = x_ref.shape[0]
    nheads = qa_ref.shape[-1] // LANES
    dim = nheads * HEAD_DIM
    scale = HEAD_DIM ** -0.5 * LOG2E

    @pl.when(pl.program_id(1) == 0)
    def _():
        carry_ref[...] = jnp.zeros_like(carry_ref)

    proj = _dot(_rms(x_ref[...], g_ref[...]).astype(BF16), w_ref[...])
    z = proj[:, 3 * dim:] + bf_ref[...]
    logf = jnp.minimum(z, 0.0) - jnp.log(1.0 + jnp.exp(-jnp.abs(z)))
    lane = lax.broadcasted_iota(jnp.int32, z.shape, 1)
    logf = jnp.where(lane < nheads, logf, 0.0)
    ti = lax.broadcasted_iota(jnp.int32, (tc, tc), 0)
    si = lax.broadcasted_iota(jnp.int32, (tc, tc), 1)
    cg = _sel_dot((si <= ti).astype(BF16), logf) + carry_ref[...]
    carry_ref[...] = cg[tc - 1:tc, :]

    hi, mid, low = _split3(cg * LOG2E)
    pieces = (hi.astype(F32) + pltpu.roll(mid.astype(F32), nheads, 1)
              + pltpu.roll(low.astype(F32), 2 * nheads, 1)).astype(BF16)
    q_c = _dot(pieces, sel_ref[0])
    k_c = -_dot(pieces, sel_ref[1])

    lane = lax.broadcasted_iota(jnp.int32, (tc, LANES), 1)
    lo = lane < HEAD_DIM
    q_one = ((lane >= HEAD_DIM + FOX_AUG) & (lane < HEAD_DIM + 2 * FOX_AUG)).astype(F32)
    k_one = ((lane >= HEAD_DIM) & (lane < HEAD_DIM + FOX_AUG)).astype(F32)
    v_one = (lane == HEAD_DIM).astype(F32)
    sources = ((0, q_c, q_one, qa_ref), (1, k_c, k_one, ka_ref), (2, None, v_one, va_ref))
    for j in range(nheads // 2):
        for part, c_aug, one, o_ref in sources:
            x = proj[:, part * dim + j * LANES:part * dim + (j + 1) * LANES]
            if part == 0:
                x = x * scale
            xr = pltpu.roll(x, HEAD_DIM, 1)
            for e, xe in ((0, x), (1, xr)):
                hs = slice((2 * j + e) * LANES, (2 * j + e + 1) * LANES)
                if c_aug is None:
                    o_ref[2 * j + e, 0] = jnp.where(lo, xe, one).T.astype(BF16)
                else:
                    o_ref[:, hs] = jnp.where(lo, xe, c_aug[:, hs] + one).astype(BF16)


def _fox_select_matrices(nheads):
    sel = np.zeros((2, LANES, nheads * LANES), np.float32)
    for h in range(nheads):
        for i in range(FOX_AUG):
            sel[0, i * nheads + h, h * LANES + HEAD_DIM + i] = 1.0
            sel[1, i * nheads + h, h * LANES + HEAD_DIM + FOX_AUG + i] = 1.0
    return jnp.asarray(sel, BF16)


def _fox_in(x, g, w_in, b_f, *, batch, tc):
    t, d = x.shape
    s = t // batch
    nheads = b_f.shape[-1]
    assert FOX_AUG * nheads <= LANES
    w = jnp.pad(w_in, ((0, 0), (0, LANES - nheads))).astype(BF16)
    bf = jnp.zeros((1, LANES), F32).at[0, :nheads].set(b_f)
    sel = _fox_select_matrices(nheads)
    wide = nheads * LANES
    nt = s // tc
    resident = pl.Buffered(1)
    return pl.pallas_call(
        _fox_in_kernel,
        out_shape=(jax.ShapeDtypeStruct((batch, s, wide), BF16),
                   jax.ShapeDtypeStruct((batch, s, wide), BF16),
                   jax.ShapeDtypeStruct((batch, nheads, nt, LANES, tc), BF16)),
        grid=(batch, nt),
        in_specs=[
            pl.BlockSpec((tc, d), lambda bi, ti: (bi * nt + ti, 0)),
            pl.BlockSpec((1, d), lambda bi, ti: (0, 0)),
            pl.BlockSpec(w.shape, lambda bi, ti: (0, 0), pipeline_mode=resident),
            pl.BlockSpec((1, LANES), lambda bi, ti: (0, 0)),
            pl.BlockSpec(sel.shape, lambda bi, ti: (0, 0, 0), pipeline_mode=resident),
        ],
        out_specs=(pl.BlockSpec((None, tc, wide), lambda bi, ti: (bi, ti, 0)),
                   pl.BlockSpec((None, tc, wide), lambda bi, ti: (bi, ti, 0)),
                   pl.BlockSpec((None, nheads, 1, LANES, tc), lambda bi, ti: (bi, 0, ti, 0, 0))),
        scratch_shapes=[pltpu.VMEM((1, LANES), F32)],
        compiler_params=_params("parallel", "arbitrary"),
        name="fox_in",
    )(x, g.reshape(1, d), w, bf, sel)


def _fox_attn_kernel(q_ref, k_ref, v_ref, o_ref, m_ref, acc_ref):
    tq = q_ref.shape[0]
    nh = q_ref.shape[1] // LANES
    qi = pl.program_id(2)
    m_ref[...] = jnp.full_like(m_ref, -jnp.inf)
    acc_ref[...] = jnp.zeros_like(acc_ref)
    keys = lax.broadcasted_iota(jnp.int32, (tq, tq), 0)
    queries = lax.broadcasted_iota(jnp.int32, (tq, tq), 1)

    def block(kb, diagonal):
        start = pl.multiple_of(kb * tq, tq)
        logits = [_dot(k_ref[pl.ds(start, tq), e * LANES:(e + 1) * LANES],
                       q_ref[:, e * LANES:(e + 1) * LANES], NT) for e in range(nh)]
        for e, s in enumerate(logits):
            if diagonal:
                s = jnp.where(keys <= queries, s, -jnp.inf)
            m_prev = m_ref[e]
            m_new = jnp.maximum(m_prev, jnp.max(s, axis=0, keepdims=True))
            p = jnp.exp2(s - m_new).astype(BF16)
            acc_ref[e] = jnp.exp2(m_prev - m_new) * acc_ref[e] + _dot(v_ref[e, kb], p)
            m_ref[e] = m_new

    def body(kb, carry):
        block(kb, False)
        return carry

    lax.fori_loop(0, qi, body, 0)
    block(qi, True)
    for pair in range(nh // 2):
        outs = []
        for e in (2 * pair, 2 * pair + 1):
            acc = acc_ref[e]
            outs.append(acc[:HEAD_DIM] / acc[HEAD_DIM:HEAD_DIM + 1])
        o_ref[:, pair * LANES:(pair + 1) * LANES] = jnp.concatenate(outs, axis=0).T.astype(o_ref.dtype)


def _fox_attn(q_aug, k_aug, v_aug, *, tq, nh=4):
    b, s, wide = q_aug.shape
    nheads = wide // LANES
    return pl.pallas_call(
        _fox_attn_kernel,
        out_shape=jax.ShapeDtypeStruct((b, s, nheads * HEAD_DIM), BF16),
        grid=(b, nheads // nh, s // tq),
        in_specs=[
            pl.BlockSpec((None, tq, nh * LANES), lambda bi, g, qi: (bi, qi, g)),
            pl.BlockSpec((None, s, nh * LANES), lambda bi, g, qi: (bi, 0, g)),
            pl.BlockSpec((None, nh, s // tq, LANES, tq), lambda bi, g, qi: (bi, g, 0, 0, 0)),
        ],
        out_specs=pl.BlockSpec((None, tq, nh * HEAD_DIM), lambda bi, g, qi: (bi, qi, g)),
        scratch_shapes=[pltpu.VMEM((nh, 1, tq), F32), pltpu.VMEM((nh, LANES, tq), F32)],
        compiler_params=_params("parallel", "parallel", "arbitrary"),
        name="fox_attn",
    )(q_aug, k_aug, v_aug)


def kernel(x, p, ffn1_norm, ffn1_w_gu, ffn1_w_down, mix_norm, ffn2_norm, ffn2_w_gu, ffn2_w_down, ple_norm, ple_w_gate, ple_w_proj, even_w_in, even_w_out, swa_sinks, rwkv_mu, rwkv_w0, rwkv_w2, rwkv_a0, rwkv_a2, rwkv_g2, rwkv_k_k, rwkv_k_a, rwkv_r_k, rwkv_ln_w, rwkv_ln_b, fox_w_in, fox_b_f, fox_w_out, final_norm):
    b, s, d = x.shape
    depth = p.shape[0]
    t = b * s
    bf = lambda w: w.astype(BF16)
    swa_q = SWA_HEADS * HEAD_DIM
    swa_cols = swa_q + 2 * (SWA_HEADS // SWA_GROUP) * HEAD_DIM
    rwkv_dim = rwkv_w0.shape[-1]
    lora = rwkv_w2.shape[1]
    fox_heads = fox_b_f.shape[-1]
    fox_dim = fox_heads * HEAD_DIM

    x = x.reshape(t, d)
    for i in range(depth):
        j = i // 2
        ffn1 = (x, ffn1_norm[i], bf(ffn1_w_gu[i]), bf(ffn1_w_down[i]))
        if i % 2 == 0:
            x, qkv, hb = _ffn(*ffn1, proj=(mix_norm[i], bf(even_w_in[j]), (swa_cols, even_w_in.shape[-1] - swa_cols)))
            ya = _swa(qkv.reshape(b, s, swa_cols), swa_sinks[j])
            zeros = jnp.zeros((lora, rwkv_dim), F32)
            w2p = jnp.concatenate([rwkv_w2[j], zeros], axis=0)
            a2p = jnp.concatenate([zeros, rwkv_a2[j]], axis=0)
            yb = _rwkv(hb.reshape(b, s, -1), rwkv_mu[j], rwkv_w0[j], w2p, rwkv_a0[j], a2p,
                       bf(rwkv_g2[j]), rwkv_k_k[j], rwkv_k_a[j], rwkv_r_k[j].reshape(-1),
                       rwkv_ln_w[j], rwkv_ln_b[j])
            w_out = bf(even_w_out[j])
            mixed = ([ya.reshape(t, swa_q), yb.reshape(t, rwkv_dim)], [w_out[:swa_q], w_out[swa_q:]])
        else:
            (x,) = _ffn(*ffn1)
            q_aug, k_aug, v_aug = _fox_in(x, mix_norm[i], fox_w_in[j], fox_b_f[j], batch=b, tc=FOX_BLOCK)
            yc = _fox_attn(q_aug, k_aug, v_aug, tq=FOX_BLOCK)
            mixed = ([yc.reshape(t, fox_dim)], [bf(fox_w_out[j])])
        x = _post_mix(*mixed, x, ffn2_norm[i], bf(ffn2_w_gu[i]), bf(ffn2_w_down[i]), ple_norm[i],
                      bf(ple_w_gate[i]), p[i].reshape(t, -1), bf(ple_w_proj[i]), final_norm,
                      final=(i == depth - 1))
    return x.reshape(b, s, d)
```

```python
import functools

import jax
import jax.numpy as jnp
import numpy as np
from jax import lax
from jax.experimental import pallas as pl
from jax.experimental.pallas import tpu as pltpu

F32 = jnp.float32
BF16 = jnp.bfloat16

LANES = 128
HEAD_DIM = 64
SWA_HEADS = 8
SWA_GROUP = 4
SWA_BLOCK = 128
RWKV_CHUNK = 64
NORM_EPS = 1e-6
GN_EPS = 64e-5
L2_EPS = 1e-12
LOG2E = float(np.log2(np.e))
VMEM_LIMIT = 56 * 1024 * 1024

NN = (((1,), (0,)), ((), ()))
NT = (((1,), (1,)), ((), ()))
TN = (((0,), (0,)), ((), ()))


def _dot(a, b, dims=NN):
    return lax.dot_general(a, b, dims, preferred_element_type=F32)


def _dot_bf(a, b, dims=NN):
    return _dot(a.astype(BF16), b.astype(BF16), dims)


def _hi_lo(x):
    hi = x.astype(BF16)
    lo = (x - hi.astype(F32)).astype(BF16)
    return hi, lo


def _dot_x3(a, b, dims=NN):
    ah, al = _hi_lo(a)
    bh, bl = _hi_lo(b)
    return _dot(ah, bh, dims) + (_dot(ah, bl, dims) + _dot(al, bh, dims))


def _split3(x):
    hi = x.astype(BF16)
    r1 = x - hi.astype(F32)
    mid = r1.astype(BF16)
    lo = (r1 - mid.astype(F32)).astype(BF16)
    return hi, mid, lo


def _sel_dot(sel, x):
    hi, mid, lo = _split3(x)
    return _dot(sel, hi) + (_dot(sel, mid) + _dot(sel, lo))


def _dot_sel(x, sel):
    hi, mid, lo = _split3(x)
    return _dot(hi, sel) + (_dot(mid, sel) + _dot(lo, sel))


def _rms(x, g):
    ms = jnp.mean(x * x, axis=-1, keepdims=True)
    return x * lax.rsqrt(ms + NORM_EPS) * g


def _params(*sem):
    return pltpu.CompilerParams(dimension_semantics=sem, vmem_limit_bytes=VMEM_LIMIT)


def _swiglu_half_step(x, g, wg, wu, wd):
    hn = _rms(x, g).astype(BF16)
    gate = _dot(hn, wg)
    up = _dot(hn, wu)
    act = (gate * jax.nn.sigmoid(gate) * up).astype(BF16)
    return x + 0.5 * _dot(act, wd)


def _ffn_kernel(n_proj, x_ref, g_ref, wg_ref, wu_ref, wd_ref, *refs):
    x = _swiglu_half_step(x_ref[...], g_ref[...], wg_ref[...], wu_ref[...], wd_ref[...])
    if n_proj:
        gm_ref, w_ref = refs[:2]
        o_ref, proj_refs = refs[2], refs[3:]
        proj = _dot(_rms(x, gm_ref[...]).astype(BF16), w_ref[...])
        col = 0
        for p_ref in proj_refs:
            p_ref[...] = proj[:, col:col + p_ref.shape[1]]
            col += p_ref.shape[1]
    else:
        o_ref = refs[0]
    o_ref[...] = x


def _ffn(x, g, w_gu, w_down, proj=None, *, tm=512):
    t, d = x.shape
    dff = w_down.shape[0]
    resident = pl.Buffered(1)
    row = lambda n: pl.BlockSpec((tm, n), lambda i: (i, 0))
    const = lambda shape: pl.BlockSpec(shape, lambda i: (0, 0), pipeline_mode=resident)
    in_specs = [row(d), const((1, d)),
                const((d, dff)),
                pl.BlockSpec((d, dff), lambda i: (0, 1), pipeline_mode=resident),
                const((dff, d))]
    args = [x, g.reshape(1, d), w_gu, w_gu, w_down]
    out_shape = [jax.ShapeDtypeStruct((t, d), F32)]
    out_specs = [row(d)]
    widths = ()
    if proj is not None:
        gm, w, widths = proj
        assert sum(widths) == w.shape[1]
        in_specs += [const((1, d)), const(w.shape)]
        args += [gm.reshape(1, d), w]
        out_shape += [jax.ShapeDtypeStruct((t, n), F32) for n in widths]
        out_specs += [row(n) for n in widths]
    return pl.pallas_call(
        functools.partial(_ffn_kernel, len(widths)),
        out_shape=out_shape,
        grid=(t // tm,),
        in_specs=in_specs,
        out_specs=out_specs,
        compiler_params=_params("parallel"),
        name="ffn",
    )(*args)


def _post_mix_kernel(n_in, final, *refs):
    a_refs, w_refs = refs[:n_in], refs[n_in:2 * n_in]
    (x_ref, g2_ref, wg_ref, wu_ref, wd_ref, gp_ref, wpg_ref, p_ref, wpp_ref, fn_ref, o_ref) = refs[2 * n_in:]
    x = x_ref[...]
    for a_ref, w_ref in zip(a_refs, w_refs):
        x = x + _dot(a_ref[...], w_ref[...])
    x = _swiglu_half_step(x, g2_ref[...], wg_ref[...], wu_ref[...], wd_ref[...])
    gate = jax.nn.sigmoid(_dot(_rms(x, gp_ref[...]).astype(BF16), wpg_ref[...]))
    x = x + gate * _dot(p_ref[...].astype(BF16), wpp_ref[...])
    if final:
        x = _rms(x, fn_ref[...])
    o_ref[...] = x


def _post_mix(a_list, w_out_list, x, g2, w_gu, w_down, gp, w_gate, p, w_proj, final_g, *, final, tm=512):
    t, d = x.shape
    dff = w_down.shape[0]
    resident = pl.Buffered(1)
    row = lambda n: pl.BlockSpec((tm, n), lambda i: (i, 0))
    const = lambda shape: pl.BlockSpec(shape, lambda i: (0, 0), pipeline_mode=resident)
    in_specs = [row(a.shape[1]) for a in a_list] + [const(w.shape) for w in w_out_list]
    in_specs += [row(d), const((1, d)),
                 const((d, dff)),
                 pl.BlockSpec((d, dff), lambda i: (0, 1), pipeline_mode=resident),
                 const((dff, d)),
                 const((1, d)), const((d, d)), row(p.shape[1]), const(w_proj.shape), const((1, d))]
    return pl.pallas_call(
        functools.partial(_post_mix_kernel, len(a_list), final),
        out_shape=jax.ShapeDtypeStruct((t, d), F32),
        grid=(t // tm,),
        in_specs=in_specs,
        out_specs=row(d),
        compiler_params=_params("parallel"),
        name="post_mix",
    )(*a_list, *w_out_list, x, g2.reshape(1, d), w_gu, w_gu, w_down, gp.reshape(1, d), w_gate, p, w_proj,
      final_g.reshape(1, d))


def _swa_kernel(sink_ref, q_ref, kp_ref, kc_ref, vp_ref, vc_ref, o_ref):
    n = pl.program_id(1)
    blk = SWA_BLOCK
    scale = HEAD_DIM ** -0.5
    k = jnp.concatenate([kp_ref[...], kc_ref[...]], axis=0)
    v = jnp.concatenate([vp_ref[...], vc_ref[...]], axis=0)
    kr = pltpu.roll(k, HEAD_DIM, 1)
    vr = pltpu.roll(v, HEAD_DIM, 1)
    lo_kv = lax.broadcasted_iota(jnp.int32, k.shape, 1) < HEAD_DIM
    kdup = [jnp.where(lo_kv, k, kr).astype(BF16), jnp.where(lo_kv, kr, k).astype(BF16)]
    vdup = [jnp.where(lo_kv, v, vr).astype(BF16), jnp.where(lo_kv, vr, v).astype(BF16)]

    qi = lax.broadcasted_iota(jnp.int32, (blk, 2 * blk), 0)
    ki = lax.broadcasted_iota(jnp.int32, (blk, 2 * blk), 1)
    dist = qi + blk - ki
    valid = (dist >= 0) & (dist < blk) & ((n > 0) | (ki >= blk))
    distf = dist.astype(F32)
    lo_q = lax.broadcasted_iota(jnp.int32, (blk, LANES), 1) < HEAD_DIM

    heads = range(SWA_HEADS)
    qm = []
    for h in heads:
        j, e = divmod(h, 2)
        q2 = q_ref[:, j * LANES:(j + 1) * LANES] * (scale * LOG2E)
        qm.append(jnp.where(lo_q if e == 0 else ~lo_q, q2, 0.0).astype(BF16))
    logits = [_dot(qm[h], kdup[h // SWA_GROUP], NT) for h in heads]
    ps, inv_denoms = [], []
    for h in heads:
        slope = 2.0 ** (-8.0 * (h + 1) / SWA_HEADS) * LOG2E
        s = jnp.where(valid, logits[h] - slope * distf, -jnp.inf)
        sink = sink_ref[h] * LOG2E
        m = jnp.maximum(jnp.max(s, axis=-1, keepdims=True), sink)
        p = jnp.exp2(s - m)
        inv_denoms.append(1.0 / (jnp.sum(p, axis=-1, keepdims=True) + jnp.exp2(sink - m)))
        ps.append(p.astype(BF16))
    outs = [_dot(ps[h], vdup[h // SWA_GROUP]) * inv_denoms[h] for h in heads]
    for j in range(SWA_HEADS // 2):
        o_ref[:, j * LANES:(j + 1) * LANES] = jnp.where(lo_q, outs[2 * j], outs[2 * j + 1]).astype(o_ref.dtype)


def _swa(qkv, sinks):
    b, s, _ = qkv.shape
    blk = SWA_BLOCK
    nq = SWA_HEADS * HEAD_DIM
    kcol = nq // LANES
    vcol = kcol + 1
    prev = lambda bi, n: jnp.maximum(n - 1, 0)
    return pl.pallas_call(
        _swa_kernel,
        out_shape=jax.ShapeDtypeStruct((b, s, nq), BF16),
        grid=(b, s // blk),
        in_specs=[
            pl.BlockSpec(memory_space=pltpu.SMEM),
            pl.BlockSpec((None, blk, nq), lambda bi, n: (bi, n, 0)),
            pl.BlockSpec((None, blk, LANES), lambda bi, n: (bi, prev(bi, n), kcol)),
            pl.BlockSpec((None, blk, LANES), lambda bi, n: (bi, n, kcol)),
            pl.BlockSpec((None, blk, LANES), lambda bi, n: (bi, prev(bi, n), vcol)),
            pl.BlockSpec((None, blk, LANES), lambda bi, n: (bi, n, vcol)),
        ],
        out_specs=pl.BlockSpec((None, blk, nq), lambda bi, n: (bi, n, 0)),
        compiler_params=_params("parallel", "arbitrary"),
        name="swa",
    )(sinks, qkv, qkv, qkv, qkv, qkv)


def _tri_inverse_minus_eye(lows, ri, ci):
    same = lambda w: (ri ^ ci) < w
    base = 8
    x = [jnp.where(same(base), -low, 0.0) for low in lows]
    p2 = [_dot_bf(xi, xi) for xi in x]
    e = [xi + pi + _dot_bf(xi, pi) for xi, pi in zip(x, p2)]
    p4 = [_dot_bf(pi, pi) for pi in p2]
    e = [ei + pi + _dot_bf(ei, pi) for ei, pi in zip(e, p4)]
    w = base * 2
    while w <= RWKV_CHUNK:
        off = [jnp.where(same(w) & ~same(w // 2), low, 0.0) for low in lows]
        wm = [oi + _dot_bf(ei, oi) for ei, oi in zip(e, off)]
        e = [ei - wi - _dot_bf(wi, ei) for ei, wi in zip(e, wm)]
        w *= 2
    return e


def _rwkv_kernel(h_ref, mu_ref, w0_ref, w2_ref, a0_ref, a2_ref, g2_ref, kk_ref, ka_ref,
                 rk_ref, lnw_ref, lnb_ref, o_ref, state_ref, last_ref):
    c = RWKV_CHUNK
    tb = h_ref.shape[0]
    nchunk = tb // c
    dim = o_ref.shape[-1]
    npair = dim // LANES

    @pl.when(pl.program_id(1) == 0)
    def _():
        state_ref[...] = jnp.zeros_like(state_ref)
        last_ref[...] = jnp.zeros_like(last_ref)

    h = h_ref[...]
    row = lax.broadcasted_iota(jnp.int32, h.shape, 0)
    shifted = jnp.where(row == 0, last_ref[...], pltpu.roll(h, 1, 0))
    last_ref[...] = h[tb - 1:tb, :]
    hs = h + (shifted - h) * mu_ref[...]
    r = hs[:, 0:dim]
    k = hs[:, dim:2 * dim]
    v = hs[:, 2 * dim:3 * dim]
    xwa = hs[:, 3 * dim:3 * dim + LANES]
    xg = hs[:, 3 * dim + LANES:3 * dim + 2 * LANES]

    wl = w0_ref[...] + _dot_x3(jnp.tanh(xwa), w2_ref[...])
    logw = -jax.nn.sigmoid(wl) * float(np.exp(-0.5))
    a = jax.nn.sigmoid(a0_ref[...] + _dot_x3(xwa, a2_ref[...]))
    gate = _dot_bf(jax.nn.sigmoid(xg), g2_ref[...])

    ri = lax.broadcasted_iota(jnp.int32, (LANES, LANES), 0)
    ci = lax.broadcasted_iota(jnp.int32, (LANES, LANES), 1)
    ones_bd = ((ri ^ ci) < HEAD_DIM).astype(BF16)
    ones_bd2 = jnp.concatenate([ones_bd, ones_bd], axis=0)

    def head_sum(x):
        cols = []
        for j in range(npair):
            hi, lo = _hi_lo(x[:, j * LANES:(j + 1) * LANES])
            cols.append(_dot(jnp.concatenate([hi, lo], axis=1), ones_bd2))
        return jnp.concatenate(cols, axis=1)

    kk = k * kk_ref[...]
    kk = kk / jnp.maximum(jnp.sqrt(head_sum(kk * kk)), L2_EPS)
    k2 = k * (1.0 + (a - 1.0) * ka_ref[...])
    bvec = kk * a

    ti = lax.broadcasted_iota(jnp.int32, (tb, tb), 0)
    si = lax.broadcasted_iota(jnp.int32, (tb, tb), 1)
    tri = (((ti ^ si) < c) & (si <= ti)).astype(BF16)
    cum = _dot(jnp.concatenate([tri] * 3, axis=1), jnp.concatenate(_split3(logw), axis=0))
    e_neg = jnp.exp(-cum)
    alpha = kk * jnp.exp(cum - logw)
    beta = bvec * e_neg
    kappa = k2 * e_neg
    rho = r * jnp.exp(cum)
    cum_end = jnp.concatenate(
        [jnp.broadcast_to(cum[(n + 1) * c - 1:(n + 1) * c, :], (c, dim)) for n in range(nchunk)], axis=0)
    to_end = jnp.exp(cum_end - cum)
    beta_e = bvec * to_end
    kappa_e = k2 * to_end
    w_end = jnp.exp(cum_end)

    same_head = (ri ^ ci) < c
    strict = same_head & (ci < ri)
    incl = same_head & (ci <= ri)
    eye = ri == ci
    lo = lax.broadcasted_iota(jnp.int32, (c, LANES), 1) < HEAD_DIM

    def stack(x2):
        return jnp.concatenate([jnp.where(lo, x2, 0.0), jnp.where(lo, 0.0, x2)], axis=0)

    units = [(n, j) for n in range(nchunk) for j in range(npair)]
    blk = lambda x, n, j: x[n * c:(n + 1) * c, j * LANES:(j + 1) * LANES]
    a_s = [stack(blk(alpha, n, j)) for n, j in units]
    rho_s = [stack(blk(rho, n, j)) for n, j in units]
    v_s = [stack(blk(v, n, j)) for n, j in units]
    ends = [jnp.concatenate([stack(blk(beta_e, n, j)), stack(blk(kappa_e, n, j))], axis=0) for n, j in units]
    bk = [jnp.concatenate([blk(beta, n, j)] * 2 + [blk(kappa, n, j)] * 2, axis=0) for n, j in units]
    sc = [_dot_bf(jnp.concatenate([ai, ri_], axis=0), bi, NT) for ai, ri_, bi in zip(a_s, rho_s, bk)]
    l_ab = [jnp.where(strict, s[:LANES, :LANES], 0.0) for s in sc]
    l_ak = [jnp.where(strict, s[:LANES, LANES:], 0.0) for s in sc]
    r_b = [jnp.where(incl, s[LANES:, :LANES], 0.0) for s in sc]
    r_k = [jnp.where(incl, s[LANES:, LANES:], 0.0) for s in sc]
    e_inv = _tri_inverse_minus_eye(l_ab, ri, ci)
    lkv = [_dot_bf(li, vi) for li, vi in zip(l_ak, v_s)]
    p_m = [-(ai + _dot_bf(ei, ai)) for ei, ai in zip(e_inv, a_s)]
    q_m = [-(xi + _dot_bf(ei, xi)) for ei, xi in zip(e_inv, lkv)]
    m_m = [jnp.where(eye, blk(w_end, n, j)[:1, :], 0.0) + _dot_bf(pi, ei[:LANES], TN)
           for (n, j), pi, ei in zip(units, p_m, ends)]
    n_m = [_dot_bf(jnp.concatenate([qi, vi], axis=0), ei, TN) for qi, vi, ei in zip(q_m, v_s, ends)]
    g_m = [ri_ + _dot_bf(rb, pi) for ri_, rb, pi in zip(rho_s, r_b, p_m)]
    h_m = [_dot_bf(jnp.concatenate([rb, rk], axis=1), jnp.concatenate([qi, vi], axis=0))
           for rb, rk, qi, vi in zip(r_b, r_k, q_m, v_s)]

    state = [state_ref[j] for j in range(npair)]
    ys = []
    for n in range(nchunk):
        idx = [n * npair + j for j in range(npair)]
        nxt = [_dot_bf(state[j], m_m[i]) + n_m[i] for j, i in enumerate(idx)]
        y = [_dot_bf(g_m[i], state[j], NT) + h_m[i] for j, i in enumerate(idx)]
        ys.append(jnp.concatenate([yi[:c] + yi[c:] for yi in y], axis=1))
        state = nxt
    for j in range(npair):
        state_ref[j] = state[j]
    y = jnp.concatenate(ys, axis=0)

    mean = head_sum(y) * (1.0 / HEAD_DIM)
    d = y - mean
    var = head_sum(d * d) * (1.0 / HEAD_DIM)
    y = d * lax.rsqrt(var + GN_EPS) * lnw_ref[...] + lnb_ref[...]
    y = y + head_sum(r * k2 * rk_ref[...]) * v
    o_ref[...] = (y * gate).astype(o_ref.dtype)


def _rwkv(hb, mu, w0, w2p, a0, a2p, g2, k_k, k_a, r_k, ln_w, ln_b, *, tb=4 * RWKV_CHUNK):
    b, s, cols = hb.shape
    dim = w0.shape[-1]
    c = tb
    row = lambda x: x.reshape(1, -1)
    vec = lambda n: pl.BlockSpec((1, n), lambda bi, t: (0, 0))
    mat = lambda m: pl.BlockSpec(m.shape, lambda bi, t: (0, 0))
    return pl.pallas_call(
        _rwkv_kernel,
        out_shape=jax.ShapeDtypeStruct((b, s, dim), BF16),
        grid=(b, s // c),
        in_specs=[
            pl.BlockSpec((None, c, cols), lambda bi, t: (bi, t, 0)),
            vec(cols), vec(dim), mat(w2p), vec(dim), mat(a2p), mat(g2),
            vec(dim), vec(dim), vec(dim), vec(dim), vec(dim),
        ],
        out_specs=pl.BlockSpec((None, c, dim), lambda bi, t: (bi, t, 0)),
        scratch_shapes=[pltpu.VMEM((dim // LANES, LANES, LANES), F32), pltpu.VMEM((1, cols), F32)],
        compiler_params=_params("parallel", "arbitrary"),
        name="rwkv7",
    )(hb, row(mu), row(w0), w2p, row(a0), a2p, g2, row(k_k), row(k_a), row(r_k), row(ln_w), row(ln_b))


FOX_AUG = 3
FOX_BLOCK = 512


def _fox_in_kernel(x_ref, g_ref, w_ref, bf_ref, sel_ref, qa_ref, ka_ref, va_ref, carry_ref):
    tc = x_ref.shape[0]
    nheads = qa_ref.shape[-1] // LANES
    dim = nheads * HEAD_DIM
    scale = HEAD_DIM ** -0.5 * LOG2E

    @pl.when(pl.program_id(1) == 0)
    def _():
        carry_ref[...] = jnp.zeros_like(carry_ref)

    proj = _dot(_rms(x_ref[...], g_ref[...]).astype(BF16), w_ref[...])
    z = proj[:, 3 * dim:] + bf_ref[...]
    logf = jnp.minimum(z, 0.0) - jnp.log(1.0 + jnp.exp(-jnp.abs(z)))
    lane = lax.broadcasted_iota(jnp.int32, z.shape, 1)
    logf = jnp.where(lane < nheads, logf, 0.0)
    ti = lax.broadcasted_iota(jnp.int32, (tc, tc), 0)
    si = lax.broadcasted_iota(jnp.int32, (tc, tc), 1)
    cg = _sel_dot((si <= ti).astype(BF16), logf) + carry_ref[...]
    carry_ref[...] = cg[tc - 1:tc, :]

    hi, mid, low = _split3(cg * LOG2E)
    pieces = (hi.astype(F32) + pltpu.roll(mid.astype(F32), nheads, 1)
              + pltpu.roll(low.astype(F32), 2 * nheads, 1)).astype(BF16)
    q_c = _dot(pieces, sel_ref[0])
    k_c = -_dot(pieces, sel_ref[1])

    lane = lax.broadcasted_iota(jnp.int32, (tc, LANES), 1)
    lo = lane < HEAD_DIM
    q_one = ((lane >= HEAD_DIM + FOX_AUG) & (lane < HEAD_DIM + 2 * FOX_AUG)).astype(F32)
    k_one = ((lane >= HEAD_DIM) & (lane < HEAD_DIM + FOX_AUG)).astype(F32)
    v_one = (lane == HEAD_DIM).astype(F32)
    sources = ((0, q_c, q_one, qa_ref), (1, k_c, k_one, ka_ref), (2, None, v_one, va_ref))
    for j in range(nheads // 2):
        for part, c_aug, one, o_ref in sources:
            x = proj[:, part * dim + j * LANES:part * dim + (j + 1) * LANES]
            if part == 0:
                x = x * scale
            xr = pltpu.roll(x, HEAD_DIM, 1)
            for e, xe in ((0, x), (1, xr)):
                hs = slice((2 * j + e) * LANES, (2 * j + e + 1) * LANES)
                if c_aug is None:
                    o_ref[2 * j + e, 0] = jnp.where(lo, xe, one).T.astype(BF16)
                else:
                    o_ref[:, hs] = jnp.where(lo, xe, c_aug[:, hs] + one).astype(BF16)


def _fox_select_matrices(nheads):
    sel = np.zeros((2, LANES, nheads * LANES), np.float32)
    for h in range(nheads):
        for i in range(FOX_AUG):
            sel[0, i * nheads + h, h * LANES + HEAD_DIM + i] = 1.0
            sel[1, i * nheads + h, h * LANES + HEAD_DIM + FOX_AUG + i] = 1.0
    return jnp.asarray(sel, BF16)


def _fox_in(x, g, w_in, b_f, *, batch, tc):
    t, d = x.shape
    s = t // batch
    nheads = b_f.shape[-1]
    assert FOX_AUG * nheads <= LANES
    w = jnp.pad(w_in, ((0, 0), (0, LANES - nheads))).astype(BF16)
    bf = jnp.zeros((1, LANES), F32).at[0, :nheads].set(b_f)
    sel = _fox_select_matrices(nheads)
    wide = nheads * LANES
    nt = s // tc
    resident = pl.Buffered(1)
    return pl.pallas_call(
        _fox_in_kernel,
        out_shape=(jax.ShapeDtypeStruct((batch, s, wide), BF16),
                   jax.ShapeDtypeStruct((batch, s, wide), BF16),
                   jax.ShapeDtypeStruct((batch, nheads, nt, LANES, tc), BF16)),
        grid=(batch, nt),
        in_specs=[
            pl.BlockSpec((tc, d), lambda bi, ti: (bi * nt + ti, 0)),
            pl.BlockSpec((1, d), lambda bi, ti: (0, 0)),
            pl.BlockSpec(w.shape, lambda bi, ti: (0, 0), pipeline_mode=resident),
            pl.BlockSpec((1, LANES), lambda bi, ti: (0, 0)),
            pl.BlockSpec(sel.shape, lambda bi, ti: (0, 0, 0), pipeline_mode=resident),
        ],
        out_specs=(pl.BlockSpec((None, tc, wide), lambda bi, ti: (bi, ti, 0)),
                   pl.BlockSpec((None, tc, wide), lambda bi, ti: (bi, ti, 0)),
                   pl.BlockSpec((None, nheads, 1, LANES, tc), lambda bi, ti: (bi, 0, ti, 0, 0))),
        scratch_shapes=[pltpu.VMEM((1, LANES), F32)],
        compiler_params=_params("parallel", "arbitrary"),
        name="fox_in",
    )(x, g.reshape(1, d), w, bf, sel)


def _fox_attn_kernel(q_ref, k_ref, v_ref, o_ref, m_ref, acc_ref, sa_ref, sb_ref):
    tq = q_ref.shape[0]
    nh = q_ref.shape[1] // LANES
    qi = pl.program_id(2)
    m_ref[...] = jnp.full_like(m_ref, -jnp.inf)
    acc_ref[...] = jnp.zeros_like(acc_ref)
    keys = lax.broadcasted_iota(jnp.int32, (tq, tq), 0)
    queries = lax.broadcasted_iota(jnp.int32, (tq, tq), 1)

    def logits_into(dst_ref, kb):
        start = pl.multiple_of(kb * tq, tq)
        for e in range(nh):
            hs = slice(e * LANES, (e + 1) * LANES)
            dst_ref[e] = _dot(k_ref[pl.ds(start, tq), hs], q_ref[:, hs], NT)

    def consume(src_ref, kb, diagonal):
        for e in range(nh):
            s = src_ref[e]
            if diagonal:
                s = jnp.where(keys <= queries, s, -jnp.inf)
            m_prev = m_ref[e]
            m_new = jnp.maximum(m_prev, jnp.max(s, axis=0, keepdims=True))
            p = jnp.exp2(s - m_new).astype(BF16)
            acc_ref[e] = jnp.exp2(m_prev - m_new) * acc_ref[e] + _dot(v_ref[e, kb], p)
            m_ref[e] = m_new

    logits_into(sa_ref, 0)

    def two_blocks(t, carry):
        kb = 2 * t
        logits_into(sb_ref, kb + 1)
        consume(sa_ref, kb, False)
        logits_into(sa_ref, kb + 2)
        consume(sb_ref, kb + 1, False)
        return carry

    lax.fori_loop(0, qi // 2, two_blocks, 0)

    @pl.when(qi % 2 == 0)
    def _():
        consume(sa_ref, qi, True)

    @pl.when(qi % 2 == 1)
    def _():
        logits_into(sb_ref, qi)
        consume(sa_ref, qi - 1, False)
        consume(sb_ref, qi, True)

    for pair in range(nh // 2):
        outs = []
        for e in (2 * pair, 2 * pair + 1):
            acc = acc_ref[e]
            outs.append(acc[:HEAD_DIM] / acc[HEAD_DIM:HEAD_DIM + 1])
        o_ref[:, pair * LANES:(pair + 1) * LANES] = jnp.concatenate(outs, axis=0).T.astype(o_ref.dtype)


def _fox_attn(q_aug, k_aug, v_aug, *, tq, nh=4):
    b, s, wide = q_aug.shape
    nheads = wide // LANES
    return pl.pallas_call(
        _fox_attn_kernel,
        out_shape=jax.ShapeDtypeStruct((b, s, nheads * HEAD_DIM), BF16),
        grid=(b, nheads // nh, s // tq),
        in_specs=[
            pl.BlockSpec((None, tq, nh * LANES), lambda bi, g, qi: (bi, qi, g)),
            pl.BlockSpec((None, s, nh * LANES), lambda bi, g, qi: (bi, 0, g)),
            pl.BlockSpec((None, nh, s // tq, LANES, tq), lambda bi, g, qi: (bi, g, 0, 0, 0)),
        ],
        out_specs=pl.BlockSpec((None, tq, nh * HEAD_DIM), lambda bi, g, qi: (bi, qi, g)),
        scratch_shapes=[pltpu.VMEM((nh, 1, tq), F32), pltpu.VMEM((nh, LANES, tq), F32),
                        pltpu.VMEM((nh, tq, tq), F32), pltpu.VMEM((nh, tq, tq), F32)],
        compiler_params=_params("parallel", "parallel", "arbitrary"),
        name="fox_attn",
    )(q_aug, k_aug, v_aug)


def kernel(x, p, ffn1_norm, ffn1_w_gu, ffn1_w_down, mix_norm, ffn2_norm, ffn2_w_gu, ffn2_w_down, ple_norm, ple_w_gate, ple_w_proj, even_w_in, even_w_out, swa_sinks, rwkv_mu, rwkv_w0, rwkv_w2, rwkv_a0, rwkv_a2, rwkv_g2, rwkv_k_k, rwkv_k_a, rwkv_r_k, rwkv_ln_w, rwkv_ln_b, fox_w_in, fox_b_f, fox_w_out, final_norm):
    b, s, d = x.shape
    depth = p.shape[0]
    t = b * s
    bf = lambda w: w.astype(BF16)
    swa_q = SWA_HEADS * HEAD_DIM
    swa_cols = swa_q + 2 * (SWA_HEADS // SWA_GROUP) * HEAD_DIM
    rwkv_dim = rwkv_w0.shape[-1]
    lora = rwkv_w2.shape[1]
    fox_heads = fox_b_f.shape[-1]
    fox_dim = fox_heads * HEAD_DIM

    x = x.reshape(t, d)
    for i in range(depth):
        j = i // 2
        ffn1 = (x, ffn1_norm[i], bf(ffn1_w_gu[i]), bf(ffn1_w_down[i]))
        if i % 2 == 0:
            x, qkv, hb = _ffn(*ffn1, proj=(mix_norm[i], bf(even_w_in[j]), (swa_cols, even_w_in.shape[-1] - swa_cols)))
            ya = _swa(qkv.reshape(b, s, swa_cols), swa_sinks[j])
            zeros = jnp.zeros((lora, rwkv_dim), F32)
            w2p = jnp.concatenate([rwkv_w2[j], zeros], axis=0)
            a2p = jnp.concatenate([zeros, rwkv_a2[j]], axis=0)
            yb = _rwkv(hb.reshape(b, s, -1), rwkv_mu[j], rwkv_w0[j], w2p, rwkv_a0[j], a2p,
                       bf(rwkv_g2[j]), rwkv_k_k[j], rwkv_k_a[j], rwkv_r_k[j].reshape(-1),
                       rwkv_ln_w[j], rwkv_ln_b[j])
            w_out = bf(even_w_out[j])
            mixed = ([ya.reshape(t, swa_q), yb.reshape(t, rwkv_dim)], [w_out[:swa_q], w_out[swa_q:]])
        else:
            (x,) = _ffn(*ffn1)
            q_aug, k_aug, v_aug = _fox_in(x, mix_norm[i], fox_w_in[j], fox_b_f[j], batch=b, tc=FOX_BLOCK)
            yc = _fox_attn(q_aug, k_aug, v_aug, tq=FOX_BLOCK)
            mixed = ([yc.reshape(t, fox_dim)], [bf(fox_w_out[j])])
        x = _post_mix(*mixed, x, ffn2_norm[i], bf(ffn2_w_gu[i]), bf(ffn2_w_down[i]), ple_norm[i],
                      bf(ple_w_gate[i]), p[i].reshape(t, -1), bf(ple_w_proj[i]), final_norm,
                      final=(i == depth - 1))
    return x.reshape(b, s, d)
```

```python
import functools

import jax
import jax.numpy as jnp
import numpy as np
from jax import lax
from jax.experimental import pallas as pl
from jax.experimental.pallas import tpu as pltpu

F32 = jnp.float32
BF16 = jnp.bfloat16

LANES = 128
HEAD_DIM = 64
SWA_HEADS = 8
SWA_GROUP = 4
SWA_BLOCK = 128
RWKV_CHUNK = 64
NORM_EPS = 1e-6
GN_EPS = 64e-5
L2_EPS = 1e-12
LOG2E = float(np.log2(np.e))
VMEM_LIMIT = 56 * 1024 * 1024

NN = (((1,), (0,)), ((), ()))
NT = (((1,), (1,)), ((), ()))
TN = (((0,), (0,)), ((), ()))


def _dot(a, b, dims=NN):
    return lax.dot_general(a, b, dims, preferred_element_type=F32)


def _dot_bf(a, b, dims=NN):
    return _dot(a.astype(BF16), b.astype(BF16), dims)


def _hi_lo(x):
    hi = x.astype(BF16)
    lo = (x - hi.astype(F32)).astype(BF16)
    return hi, lo


def _dot_x3(a, b, dims=NN):
    ah, al = _hi_lo(a)
    bh, bl = _hi_lo(b)
    return _dot(ah, bh, dims) + (_dot(ah, bl, dims) + _dot(al, bh, dims))


def _split3(x):
    hi = x.astype(BF16)
    r1 = x - hi.astype(F32)
    mid = r1.astype(BF16)
    lo = (r1 - mid.astype(F32)).astype(BF16)
    return hi, mid, lo


def _sel_dot(sel, x):
    hi, mid, lo = _split3(x)
    return _dot(sel, hi) + (_dot(sel, mid) + _dot(sel, lo))


def _dot_sel(x, sel):
    hi, mid, lo = _split3(x)
    return _dot(hi, sel) + (_dot(mid, sel) + _dot(lo, sel))


def _rms(x, g):
    ms = jnp.mean(x * x, axis=-1, keepdims=True)
    return x * lax.rsqrt(ms + NORM_EPS) * g


def _params(*sem):
    return pltpu.CompilerParams(dimension_semantics=sem, vmem_limit_bytes=VMEM_LIMIT)


def _layer_block(layer, shape, block=(0, 0)):
    return pl.BlockSpec((None,) + tuple(shape), lambda i: (layer,) + tuple(block), pipeline_mode=pl.Buffered(1))


def _swiglu_half_step(x, g, wg, wu, wd):
    hn = _rms(x, g).astype(BF16)
    gate = _dot(hn, wg)
    up = _dot(hn, wu)
    act = (gate * jax.nn.sigmoid(gate) * up).astype(BF16)
    return x + 0.5 * _dot(act, wd)


def _ffn_kernel(n_proj, x_ref, g_ref, wg_ref, wu_ref, wd_ref, *refs):
    x = _swiglu_half_step(x_ref[...], g_ref[...], wg_ref[...], wu_ref[...], wd_ref[...])
    if n_proj:
        gm_ref, w_ref = refs[:2]
        o_ref, proj_refs = refs[2], refs[3:]
        proj = _dot(_rms(x, gm_ref[...]).astype(BF16), w_ref[...])
        col = 0
        for p_ref in proj_refs:
            p_ref[...] = proj[:, col:col + p_ref.shape[1]]
            col += p_ref.shape[1]
    else:
        o_ref = refs[0]
    o_ref[...] = x


def _ffn(x, g, w_gu, w_down, layer, proj=None, *, tm=512):
    t, d = x.shape
    dff = w_down.shape[1]
    row = lambda n: pl.BlockSpec((tm, n), lambda i: (i, 0))
    vec = pl.BlockSpec((1, d), lambda i: (0, 0))
    in_specs = [row(d), vec, _layer_block(layer, (d, dff)), _layer_block(layer, (d, dff), (0, 1)),
                _layer_block(layer, (dff, d))]
    args = [x, g.reshape(1, d), w_gu, w_gu, w_down]
    out_shape = [jax.ShapeDtypeStruct((t, d), F32)]
    out_specs = [row(d)]
    widths = ()
    if proj is not None:
        gm, w, w_layer, widths = proj
        assert sum(widths) == w.shape[2]
        in_specs += [vec, _layer_block(w_layer, w.shape[1:])]
        args += [gm.reshape(1, d), w]
        out_shape += [jax.ShapeDtypeStruct((t, n), F32) for n in widths]
        out_specs += [row(n) for n in widths]
    return pl.pallas_call(
        functools.partial(_ffn_kernel, len(widths)),
        out_shape=out_shape,
        grid=(t // tm,),
        in_specs=in_specs,
        out_specs=out_specs,
        compiler_params=_params("parallel"),
        name="ffn",
    )(*args)


def _post_mix_kernel(n_in, final, *refs):
    a_refs, w_refs = refs[:n_in], refs[n_in:2 * n_in]
    (x_ref, g2_ref, wg_ref, wu_ref, wd_ref, gp_ref, wpg_ref, p_ref, wpp_ref, fn_ref, o_ref) = refs[2 * n_in:]
    x = x_ref[...]
    for a_ref, w_ref in zip(a_refs, w_refs):
        x = x + _dot(a_ref[...], w_ref[...])
    x = _swiglu_half_step(x, g2_ref[...], wg_ref[...], wu_ref[...], wd_ref[...])
    gate = jax.nn.sigmoid(_dot(_rms(x, gp_ref[...]).astype(BF16), wpg_ref[...]))
    x = x + gate * _dot(p_ref[...].astype(BF16), wpp_ref[...])
    if final:
        x = _rms(x, fn_ref[...])
    o_ref[...] = x


def _post_mix(a_list, w_out, out_layer, x, g2, w_gu, w_down, gp, w_gate, p, w_proj, final_g, layer, *, final,
              tm=512):
    t, d = x.shape
    dff = w_down.shape[1]
    row = lambda n: pl.BlockSpec((tm, n), lambda i: (i, 0))
    vec = pl.BlockSpec((1, d), lambda i: (0, 0))
    assert all(a.shape[1] == a_list[0].shape[1] for a in a_list)
    in_specs = [row(a.shape[1]) for a in a_list]
    in_specs += [_layer_block(out_layer, (a.shape[1], d), (k, 0)) for k, a in enumerate(a_list)]
    in_specs += [row(d), vec, _layer_block(layer, (d, dff)), _layer_block(layer, (d, dff), (0, 1)),
                 _layer_block(layer, (dff, d)), vec, _layer_block(layer, (d, d)), row(p.shape[1]),
                 _layer_block(layer, w_proj.shape[1:]), vec]
    return pl.pallas_call(
        functools.partial(_post_mix_kernel, len(a_list), final),
        out_shape=jax.ShapeDtypeStruct((t, d), F32),
        grid=(t // tm,),
        in_specs=in_specs,
        out_specs=row(d),
        compiler_params=_params("parallel"),
        name="post_mix",
    )(*a_list, *([w_out] * len(a_list)), x, g2.reshape(1, d), w_gu, w_gu, w_down, gp.reshape(1, d), w_gate, p,
      w_proj, final_g.reshape(1, d))


def _swa_kernel(sink_ref, q_ref, kp_ref, kc_ref, vp_ref, vc_ref, o_ref):
    n = pl.program_id(1)
    blk = SWA_BLOCK
    scale = HEAD_DIM ** -0.5
    k = jnp.concatenate([kp_ref[...], kc_ref[...]], axis=0)
    v = jnp.concatenate([vp_ref[...], vc_ref[...]], axis=0)
    kr = pltpu.roll(k, HEAD_DIM, 1)
    vr = pltpu.roll(v, HEAD_DIM, 1)
    lo_kv = lax.broadcasted_iota(jnp.int32, k.shape, 1) < HEAD_DIM
    kdup = [jnp.where(lo_kv, k, kr).astype(BF16), jnp.where(lo_kv, kr, k).astype(BF16)]
    vdup = [jnp.where(lo_kv, v, vr).astype(BF16), jnp.where(lo_kv, vr, v).astype(BF16)]

    qi = lax.broadcasted_iota(jnp.int32, (blk, 2 * blk), 0)
    ki = lax.broadcasted_iota(jnp.int32, (blk, 2 * blk), 1)
    dist = qi + blk - ki
    valid = (dist >= 0) & (dist < blk) & ((n > 0) | (ki >= blk))
    distf = dist.astype(F32)
    lo_q = lax.broadcasted_iota(jnp.int32, (blk, LANES), 1) < HEAD_DIM

    heads = range(SWA_HEADS)
    qm = []
    for h in heads:
        j, e = divmod(h, 2)
        q2 = q_ref[:, j * LANES:(j + 1) * LANES] * (scale * LOG2E)
        qm.append(jnp.where(lo_q if e == 0 else ~lo_q, q2, 0.0).astype(BF16))
    logits = [_dot(qm[h], kdup[h // SWA_GROUP], NT) for h in heads]
    ps, inv_denoms = [], []
    for h in heads:
        slope = 2.0 ** (-8.0 * (h + 1) / SWA_HEADS) * LOG2E
        s = jnp.where(valid, logits[h] - slope * distf, -jnp.inf)
        sink = sink_ref[h] * LOG2E
        m = jnp.maximum(jnp.max(s, axis=-1, keepdims=True), sink)
        p = jnp.exp2(s - m)
        inv_denoms.append(1.0 / (jnp.sum(p, axis=-1, keepdims=True) + jnp.exp2(sink - m)))
        ps.append(p.astype(BF16))
    outs = [_dot(ps[h], vdup[h // SWA_GROUP]) * inv_denoms[h] for h in heads]
    for j in range(SWA_HEADS // 2):
        o_ref[:, j * LANES:(j + 1) * LANES] = jnp.where(lo_q, outs[2 * j], outs[2 * j + 1]).astype(o_ref.dtype)


def _swa(qkv, sinks):
    b, s, _ = qkv.shape
    blk = SWA_BLOCK
    nq = SWA_HEADS * HEAD_DIM
    kcol = nq // LANES
    vcol = kcol + 1
    prev = lambda bi, n: jnp.maximum(n - 1, 0)
    return pl.pallas_call(
        _swa_kernel,
        out_shape=jax.ShapeDtypeStruct((b, s, nq), BF16),
        grid=(b, s // blk),
        in_specs=[
            pl.BlockSpec(memory_space=pltpu.SMEM),
            pl.BlockSpec((None, blk, nq), lambda bi, n: (bi, n, 0)),
            pl.BlockSpec((None, blk, LANES), lambda bi, n: (bi, prev(bi, n), kcol)),
            pl.BlockSpec((None, blk, LANES), lambda bi, n: (bi, n, kcol)),
            pl.BlockSpec((None, blk, LANES), lambda bi, n: (bi, prev(bi, n), vcol)),
            pl.BlockSpec((None, blk, LANES), lambda bi, n: (bi, n, vcol)),
        ],
        out_specs=pl.BlockSpec((None, blk, nq), lambda bi, n: (bi, n, 0)),
        compiler_params=_params("parallel", "arbitrary"),
        name="swa",
    )(sinks, qkv, qkv, qkv, qkv, qkv)


def _tri_inverse_minus_eye(lows, ri, ci):
    same = lambda w: (ri ^ ci) < w
    base = 8
    x = [jnp.where(same(base), -low, 0.0) for low in lows]
    p2 = [_dot_bf(xi, xi) for xi in x]
    e = [xi + pi + _dot_bf(xi, pi) for xi, pi in zip(x, p2)]
    p4 = [_dot_bf(pi, pi) for pi in p2]
    e = [ei + pi + _dot_bf(ei, pi) for ei, pi in zip(e, p4)]
    w = base * 2
    while w <= RWKV_CHUNK:
        off = [jnp.where(same(w) & ~same(w // 2), low, 0.0) for low in lows]
        wm = [oi + _dot_bf(ei, oi) for ei, oi in zip(e, off)]
        e = [ei - wi - _dot_bf(wi, ei) for ei, wi in zip(e, wm)]
        w *= 2
    return e


def _rwkv_kernel(h_ref, mu_ref, w0_ref, w2_ref, a0_ref, a2_ref, g2_ref, kk_ref, ka_ref,
                 rk_ref, lnw_ref, lnb_ref, o_ref, state_ref, last_ref):
    c = RWKV_CHUNK
    tb = h_ref.shape[0]
    nchunk = tb // c
    dim = o_ref.shape[-1]
    npair = dim // LANES

    @pl.when(pl.program_id(1) == 0)
    def _():
        state_ref[...] = jnp.zeros_like(state_ref)
        last_ref[...] = jnp.zeros_like(last_ref)

    h = h_ref[...]
    row = lax.broadcasted_iota(jnp.int32, h.shape, 0)
    shifted = jnp.where(row == 0, last_ref[...], pltpu.roll(h, 1, 0))
    last_ref[...] = h[tb - 1:tb, :]
    hs = h + (shifted - h) * mu_ref[...]
    r = hs[:, 0:dim]
    k = hs[:, dim:2 * dim]
    v = hs[:, 2 * dim:3 * dim]
    xwa = hs[:, 3 * dim:3 * dim + LANES]
    xg = hs[:, 3 * dim + LANES:3 * dim + 2 * LANES]

    wl = w0_ref[...] + _dot_x3(jnp.tanh(xwa), w2_ref[...])
    logw = -jax.nn.sigmoid(wl) * float(np.exp(-0.5))
    a = jax.nn.sigmoid(a0_ref[...] + _dot_x3(xwa, a2_ref[...]))
    gate = _dot_bf(jax.nn.sigmoid(xg), g2_ref[...])

    ri = lax.broadcasted_iota(jnp.int32, (LANES, LANES), 0)
    ci = lax.broadcasted_iota(jnp.int32, (LANES, LANES), 1)
    ones_bd = ((ri ^ ci) < HEAD_DIM).astype(BF16)
    ones_bd2 = jnp.concatenate([ones_bd, ones_bd], axis=0)

    def head_sum(x):
        cols = []
        for j in range(npair):
            hi, lo = _hi_lo(x[:, j * LANES:(j + 1) * LANES])
            cols.append(_dot(jnp.concatenate([hi, lo], axis=1), ones_bd2))
        return jnp.concatenate(cols, axis=1)

    kk = k * kk_ref[...]
    kk = kk / jnp.maximum(jnp.sqrt(head_sum(kk * kk)), L2_EPS)
    k2 = k * (1.0 + (a - 1.0) * ka_ref[...])
    bvec = kk * a

    ti = lax.broadcasted_iota(jnp.int32, (tb, tb), 0)
    si = lax.broadcasted_iota(jnp.int32, (tb, tb), 1)
    tri = (((ti ^ si) < c) & (si <= ti)).astype(BF16)
    cum = _dot(jnp.concatenate([tri] * 3, axis=1), jnp.concatenate(_split3(logw), axis=0))
    e_neg = jnp.exp(-cum)
    alpha = kk * jnp.exp(cum - logw)
    beta = bvec * e_neg
    kappa = k2 * e_neg
    rho = r * jnp.exp(cum)
    cum_end = jnp.concatenate(
        [jnp.broadcast_to(cum[(n + 1) * c - 1:(n + 1) * c, :], (c, dim)) for n in range(nchunk)], axis=0)
    to_end = jnp.exp(cum_end - cum)
    beta_e = bvec * to_end
    kappa_e = k2 * to_end
    w_end = jnp.exp(cum_end)

    same_head = (ri ^ ci) < c
    strict = same_head & (ci < ri)
    incl = same_head & (ci <= ri)
    eye = ri == ci
    lo = lax.broadcasted_iota(jnp.int32, (c, LANES), 1) < HEAD_DIM

    def stack(x2):
        return jnp.concatenate([jnp.where(lo, x2, 0.0), jnp.where(lo, 0.0, x2)], axis=0)

    units = [(n, j) for n in range(nchunk) for j in range(npair)]
    blk = lambda x, n, j: x[n * c:(n + 1) * c, j * LANES:(j + 1) * LANES]
    a_s = [stack(blk(alpha, n, j)) for n, j in units]
    rho_s = [stack(blk(rho, n, j)) for n, j in units]
    v_s = [stack(blk(v, n, j)) for n, j in units]
    ends = [jnp.concatenate([stack(blk(beta_e, n, j)), stack(blk(kappa_e, n, j))], axis=0) for n, j in units]
    bk = [jnp.concatenate([blk(beta, n, j)] * 2 + [blk(kappa, n, j)] * 2, axis=0) for n, j in units]
    sc = [_dot_bf(jnp.concatenate([ai, ri_], axis=0), bi, NT) for ai, ri_, bi in zip(a_s, rho_s, bk)]
    l_ab = [jnp.where(strict, s[:LANES, :LANES], 0.0) for s in sc]
    l_ak = [jnp.where(strict, s[:LANES, LANES:], 0.0) for s in sc]
    r_b = [jnp.where(incl, s[LANES:, :LANES], 0.0) for s in sc]
    r_k = [jnp.where(incl, s[LANES:, LANES:], 0.0) for s in sc]
    e_inv = _tri_inverse_minus_eye(l_ab, ri, ci)
    lkv = [_dot_bf(li, vi) for li, vi in zip(l_ak, v_s)]
    p_m = [-(ai + _dot_bf(ei, ai)) for ei, ai in zip(e_inv, a_s)]
    q_m = [-(xi + _dot_bf(ei, xi)) for ei, xi in zip(e_inv, lkv)]
    m_m = [jnp.where(eye, blk(w_end, n, j)[:1, :], 0.0) + _dot_bf(pi, ei[:LANES], TN)
           for (n, j), pi, ei in zip(units, p_m, ends)]
    n_m = [_dot_bf(jnp.concatenate([qi, vi], axis=0), ei, TN) for qi, vi, ei in zip(q_m, v_s, ends)]
    g_m = [ri_ + _dot_bf(rb, pi) for ri_, rb, pi in zip(rho_s, r_b, p_m)]
    h_m = [_dot_bf(jnp.concatenate([rb, rk], axis=1), jnp.concatenate([qi, vi], axis=0))
           for rb, rk, qi, vi in zip(r_b, r_k, q_m, v_s)]

    state = [state_ref[j] for j in range(npair)]
    ys = []
    for n in range(nchunk):
        idx = [n * npair + j for j in range(npair)]
        nxt = [_dot_bf(state[j], m_m[i]) + n_m[i] for j, i in enumerate(idx)]
        y = [_dot_bf(g_m[i], state[j], NT) + h_m[i] for j, i in enumerate(idx)]
        ys.append(jnp.concatenate([yi[:c] + yi[c:] for yi in y], axis=1))
        state = nxt
    for j in range(npair):
        state_ref[j] = state[j]
    y = jnp.concatenate(ys, axis=0)

    mean = head_sum(y) * (1.0 / HEAD_DIM)
    d = y - mean
    var = head_sum(d * d) * (1.0 / HEAD_DIM)
    y = d * lax.rsqrt(var + GN_EPS) * lnw_ref[...] + lnb_ref[...]
    y = y + head_sum(r * k2 * rk_ref[...]) * v
    o_ref[...] = (y * gate).astype(o_ref.dtype)


def _rwkv(hb, mu, w0, w2p, a0, a2p, g2, k_k, k_a, r_k, ln_w, ln_b, *, tb=4 * RWKV_CHUNK):
    b, s, cols = hb.shape
    dim = w0.shape[-1]
    c = tb
    row = lambda x: x.reshape(1, -1)
    vec = lambda n: pl.BlockSpec((1, n), lambda bi, t: (0, 0))
    mat = lambda m: pl.BlockSpec(m.shape, lambda bi, t: (0, 0))
    return pl.pallas_call(
        _rwkv_kernel,
        out_shape=jax.ShapeDtypeStruct((b, s, dim), BF16),
        grid=(b, s // c),
        in_specs=[
            pl.BlockSpec((None, c, cols), lambda bi, t: (bi, t, 0)),
            vec(cols), vec(dim), mat(w2p), vec(dim), mat(a2p), mat(g2),
            vec(dim), vec(dim), vec(dim), vec(dim), vec(dim),
        ],
        out_specs=pl.BlockSpec((None, c, dim), lambda bi, t: (bi, t, 0)),
        scratch_shapes=[pltpu.VMEM((dim // LANES, LANES, LANES), F32), pltpu.VMEM((1, cols), F32)],
        compiler_params=_params("parallel", "arbitrary"),
        name="rwkv7",
    )(hb, row(mu), row(w0), w2p, row(a0), a2p, g2, row(k_k), row(k_a), row(r_k), row(ln_w), row(ln_b))


FOX_AUG = 3
FOX_BLOCK = 512


def _fox_in_kernel(x_ref, g_ref, w_ref, bf_ref, sel_ref, qa_ref, ka_ref, va_ref, carry_ref):
    tc = x_ref.shape[0]
    nheads = qa_ref.shape[-1] // LANES
    dim = nheads * HEAD_DIM
    scale = HEAD_DIM ** -0.5 * LOG2E

    @pl.when(pl.program_id(1) == 0)
    def _():
        carry_ref[...] = jnp.zeros_like(carry_ref)

    proj = _dot(_rms(x_ref[...], g_ref[...]).astype(BF16), w_ref[...])
    z = proj[:, 3 * dim:] + bf_ref[...]
    logf = jnp.minimum(z, 0.0) - jnp.log(1.0 + jnp.exp(-jnp.abs(z)))
    lane = lax.broadcasted_iota(jnp.int32, z.shape, 1)
    logf = jnp.where(lane < nheads, logf, 0.0)
    ti = lax.broadcasted_iota(jnp.int32, (tc, tc), 0)
    si = lax.broadcasted_iota(jnp.int32, (tc, tc), 1)
    cg = _sel_dot((si <= ti).astype(BF16), logf) + carry_ref[...]
    carry_ref[...] = cg[tc - 1:tc, :]

    hi, mid, low = _split3(cg * LOG2E)
    pieces = (hi.astype(F32) + pltpu.roll(mid.astype(F32), nheads, 1)
              + pltpu.roll(low.astype(F32), 2 * nheads, 1)).astype(BF16)
    q_c = _dot(pieces, sel_ref[0])
    k_c = -_dot(pieces, sel_ref[1])

    lane = lax.broadcasted_iota(jnp.int32, (tc, LANES), 1)
    lo = lane < HEAD_DIM
    q_one = ((lane >= HEAD_DIM + FOX_AUG) & (lane < HEAD_DIM + 2 * FOX_AUG)).astype(F32)
    k_one = ((lane >= HEAD_DIM) & (lane < HEAD_DIM + FOX_AUG)).astype(F32)
    v_one = (lane == HEAD_DIM).astype(F32)
    sources = ((0, q_c, q_one, qa_ref), (1, k_c, k_one, ka_ref), (2, None, v_one, va_ref))
    for j in range(nheads // 2):
        for part, c_aug, one, o_ref in sources:
            x = proj[:, part * dim + j * LANES:part * dim + (j + 1) * LANES]
            if part == 0:
                x = x * scale
            xr = pltpu.roll(x, HEAD_DIM, 1)
            for e, xe in ((0, x), (1, xr)):
                hs = slice((2 * j + e) * LANES, (2 * j + e + 1) * LANES)
                if c_aug is None:
                    o_ref[2 * j + e, 0] = jnp.where(lo, xe, one).T.astype(BF16)
                else:
                    o_ref[:, hs] = jnp.where(lo, xe, c_aug[:, hs] + one).astype(BF16)


def _fox_select_matrices(nheads):
    sel = np.zeros((2, LANES, nheads * LANES), np.float32)
    for h in range(nheads):
        for i in range(FOX_AUG):
            sel[0, i * nheads + h, h * LANES + HEAD_DIM + i] = 1.0
            sel[1, i * nheads + h, h * LANES + HEAD_DIM + FOX_AUG + i] = 1.0
    return jnp.asarray(sel, BF16)


def _fox_in(x, g, w_in, b_f, *, batch, tc):
    t, d = x.shape
    s = t // batch
    nheads = b_f.shape[-1]
    assert FOX_AUG * nheads <= LANES
    w = jnp.pad(w_in, ((0, 0), (0, LANES - nheads))).astype(BF16)
    bf = jnp.zeros((1, LANES), F32).at[0, :nheads].set(b_f)
    sel = _fox_select_matrices(nheads)
    wide = nheads * LANES
    nt = s // tc
    resident = pl.Buffered(1)
    return pl.pallas_call(
        _fox_in_kernel,
        out_shape=(jax.ShapeDtypeStruct((batch, s, wide), BF16),
                   jax.ShapeDtypeStruct((batch, s, wide), BF16),
                   jax.ShapeDtypeStruct((batch, nheads, nt, LANES, tc), BF16)),
        grid=(batch, nt),
        in_specs=[
            pl.BlockSpec((tc, d), lambda bi, ti: (bi * nt + ti, 0)),
            pl.BlockSpec((1, d), lambda bi, ti: (0, 0)),
            pl.BlockSpec(w.shape, lambda bi, ti: (0, 0), pipeline_mode=resident),
            pl.BlockSpec((1, LANES), lambda bi, ti: (0, 0)),
            pl.BlockSpec(sel.shape, lambda bi, ti: (0, 0, 0), pipeline_mode=resident),
        ],
        out_specs=(pl.BlockSpec((None, tc, wide), lambda bi, ti: (bi, ti, 0)),
                   pl.BlockSpec((None, tc, wide), lambda bi, ti: (bi, ti, 0)),
                   pl.BlockSpec((None, nheads, 1, LANES, tc), lambda bi, ti: (bi, 0, ti, 0, 0))),
        scratch_shapes=[pltpu.VMEM((1, LANES), F32)],
        compiler_params=_params("parallel", "arbitrary"),
        name="fox_in",
    )(x, g.reshape(1, d), w, bf, sel)


def _fox_attn_kernel(q_ref, k_ref, v_ref, o_ref, m_ref, acc_ref, sa_ref, sb_ref):
    tq = q_ref.shape[0]
    nh = q_ref.shape[1] // LANES
    qi = pl.program_id(2)
    m_ref[...] = jnp.full_like(m_ref, -jnp.inf)
    acc_ref[...] = jnp.zeros_like(acc_ref)
    keys = lax.broadcasted_iota(jnp.int32, (tq, tq), 0)
    queries = lax.broadcasted_iota(jnp.int32, (tq, tq), 1)

    def logits_into(dst_ref, kb):
        start = pl.multiple_of(kb * tq, tq)
        for e in range(nh):
            hs = slice(e * LANES, (e + 1) * LANES)
            dst_ref[e] = _dot(k_ref[pl.ds(start, tq), hs], q_ref[:, hs], NT)

    def consume(src_ref, kb, diagonal):
        for e in range(nh):
            s = src_ref[e]
            if diagonal:
                s = jnp.where(keys <= queries, s, -jnp.inf)
            m_prev = m_ref[e]
            m_new = jnp.maximum(m_prev, jnp.max(s, axis=0, keepdims=True))
            p = jnp.exp2(s - m_new).astype(BF16)
            acc_ref[e] = jnp.exp2(m_prev - m_new) * acc_ref[e] + _dot(v_ref[e, kb], p)
            m_ref[e] = m_new

    logits_into(sa_ref, 0)

    def two_blocks(t, carry):
        kb = 2 * t
        logits_into(sb_ref, kb + 1)
        consume(sa_ref, kb, False)
        logits_into(sa_ref, kb + 2)
        consume(sb_ref, kb + 1, False)
        return carry

    lax.fori_loop(0, qi // 2, two_blocks, 0)

    @pl.when(qi % 2 == 0)
    def _():
        consume(sa_ref, qi, True)

    @pl.when(qi % 2 == 1)
    def _():
        logits_into(sb_ref, qi)
        consume(sa_ref, qi - 1, False)
        consume(sb_ref, qi, True)

    for pair in range(nh // 2):
        outs = []
        for e in (2 * pair, 2 * pair + 1):
            acc = acc_ref[e]
            outs.append(acc[:HEAD_DIM] / acc[HEAD_DIM:HEAD_DIM + 1])
        o_ref[:, pair * LANES:(pair + 1) * LANES] = jnp.concatenate(outs, axis=0).T.astype(o_ref.dtype)


def _fox_attn(q_aug, k_aug, v_aug, *, tq, nh=4):
    b, s, wide = q_aug.shape
    nheads = wide // LANES
    return pl.pallas_call(
        _fox_attn_kernel,
        out_shape=jax.ShapeDtypeStruct((b, s, nheads * HEAD_DIM), BF16),
        grid=(b, nheads // nh, s // tq),
        in_specs=[
            pl.BlockSpec((None, tq, nh * LANES), lambda bi, g, qi: (bi, qi, g)),
            pl.BlockSpec((None, s, nh * LANES), lambda bi, g, qi: (bi, 0, g)),
            pl.BlockSpec((None, nh, s // tq, LANES, tq), lambda bi, g, qi: (bi, g, 0, 0, 0)),
        ],
        out_specs=pl.BlockSpec((None, tq, nh * HEAD_DIM), lambda bi, g, qi: (bi, qi, g)),
        scratch_shapes=[pltpu.VMEM((nh, 1, tq), F32), pltpu.VMEM((nh, LANES, tq), F32),
                        pltpu.VMEM((nh, tq, tq), F32), pltpu.VMEM((nh, tq, tq), F32)],
        compiler_params=_params("parallel", "parallel", "arbitrary"),
        name="fox_attn",
    )(q_aug, k_aug, v_aug)


def kernel(x, p, ffn1_norm, ffn1_w_gu, ffn1_w_down, mix_norm, ffn2_norm, ffn2_w_gu, ffn2_w_down, ple_norm, ple_w_gate, ple_w_proj, even_w_in, even_w_out, swa_sinks, rwkv_mu, rwkv_w0, rwkv_w2, rwkv_a0, rwkv_a2, rwkv_g2, rwkv_k_k, rwkv_k_a, rwkv_r_k, rwkv_ln_w, rwkv_ln_b, fox_w_in, fox_b_f, fox_w_out, final_norm):
    b, s, d = x.shape
    depth = p.shape[0]
    t = b * s
    bf = lambda w: w.astype(BF16)
    swa_q = SWA_HEADS * HEAD_DIM
    swa_cols = swa_q + 2 * (SWA_HEADS // SWA_GROUP) * HEAD_DIM
    rwkv_dim = rwkv_w0.shape[-1]
    lora = rwkv_w2.shape[1]
    fox_heads = fox_b_f.shape[-1]
    fox_dim = fox_heads * HEAD_DIM

    ffn1_gu, ffn1_down, ffn2_gu, ffn2_down = bf(ffn1_w_gu), bf(ffn1_w_down), bf(ffn2_w_gu), bf(ffn2_w_down)
    ple_gate, ple_proj = bf(ple_w_gate), bf(ple_w_proj)
    even_in, even_out, fox_out = bf(even_w_in), bf(even_w_out), bf(fox_w_out)

    x = x.reshape(t, d)
    for i in range(depth):
        j = i // 2
        if i % 2 == 0:
            widths = (swa_cols, even_w_in.shape[-1] - swa_cols)
            x, qkv, hb = _ffn(x, ffn1_norm[i], ffn1_gu, ffn1_down, i, proj=(mix_norm[i], even_in, j, widths))
            ya = _swa(qkv.reshape(b, s, swa_cols), swa_sinks[j])
            zeros = jnp.zeros((lora, rwkv_dim), F32)
            w2p = jnp.concatenate([rwkv_w2[j], zeros], axis=0)
            a2p = jnp.concatenate([zeros, rwkv_a2[j]], axis=0)
            yb = _rwkv(hb.reshape(b, s, -1), rwkv_mu[j], rwkv_w0[j], w2p, rwkv_a0[j], a2p,
                       bf(rwkv_g2[j]), rwkv_k_k[j], rwkv_k_a[j], rwkv_r_k[j].reshape(-1),
                       rwkv_ln_w[j], rwkv_ln_b[j])
            mixed = ([ya.reshape(t, swa_q), yb.reshape(t, rwkv_dim)], even_out, j)
        else:
            (x,) = _ffn(x, ffn1_norm[i], ffn1_gu, ffn1_down, i)
            q_aug, k_aug, v_aug = _fox_in(x, mix_norm[i], fox_w_in[j], fox_b_f[j], batch=b, tc=FOX_BLOCK)
            yc = _fox_attn(q_aug, k_aug, v_aug, tq=FOX_BLOCK)
            mixed = ([yc.reshape(t, fox_dim)], fox_out, j)
        x = _post_mix(*mixed, x, ffn2_norm[i], ffn2_gu, ffn2_down, ple_norm[i], ple_gate, p[i].reshape(t, -1),
                      ple_proj, final_norm, i, final=(i == depth - 1))
    return x.reshape(b, s, d)
```

```python
import functools

import jax
import jax.numpy as jnp
import numpy as np
from jax import lax
from jax.experimental import pallas as pl
from jax.experimental.pallas import tpu as pltpu

F32 = jnp.float32
BF16 = jnp.bfloat16

LANES = 128
HEAD_DIM = 64
SWA_HEADS = 8
SWA_GROUP = 4
SWA_BLOCK = 128
RWKV_CHUNK = 64
NORM_EPS = 1e-6
GN_EPS = 64e-5
L2_EPS = 1e-12
LOG2E = float(np.log2(np.e))
VMEM_LIMIT = 56 * 1024 * 1024

NN = (((1,), (0,)), ((), ()))
NT = (((1,), (1,)), ((), ()))
TN = (((0,), (0,)), ((), ()))


def _dot(a, b, dims=NN):
    return lax.dot_general(a, b, dims, preferred_element_type=F32)


def _dot_bf(a, b, dims=NN):
    return _dot(a.astype(BF16), b.astype(BF16), dims)


def _hi_lo(x):
    hi = x.astype(BF16)
    lo = (x - hi.astype(F32)).astype(BF16)
    return hi, lo


def _dot_x3(a, b, dims=NN):
    ah, al = _hi_lo(a)
    bh, bl = _hi_lo(b)
    return _dot(ah, bh, dims) + (_dot(ah, bl, dims) + _dot(al, bh, dims))


def _split3(x):
    hi = x.astype(BF16)
    r1 = x - hi.astype(F32)
    mid = r1.astype(BF16)
    lo = (r1 - mid.astype(F32)).astype(BF16)
    return hi, mid, lo


def _sel_dot(sel, x):
    hi, mid, lo = _split3(x)
    return _dot(sel, hi) + (_dot(sel, mid) + _dot(sel, lo))


def _dot_sel(x, sel):
    hi, mid, lo = _split3(x)
    return _dot(hi, sel) + (_dot(mid, sel) + _dot(lo, sel))


def _rms(x, g):
    ms = jnp.mean(x * x, axis=-1, keepdims=True)
    return x * lax.rsqrt(ms + NORM_EPS) * g


def _params(*sem):
    return pltpu.CompilerParams(dimension_semantics=sem, vmem_limit_bytes=VMEM_LIMIT)


def _layer_block(layer, shape, block=(0, 0)):
    return pl.BlockSpec((None,) + tuple(shape), lambda i: (layer,) + tuple(block), pipeline_mode=pl.Buffered(1))


def _swiglu_half_step(x, g, wg, wu, wd):
    hn = _rms(x, g).astype(BF16)
    gate = _dot(hn, wg)
    up = _dot(hn, wu)
    act = (gate * jax.nn.sigmoid(gate) * up).astype(BF16)
    return x + 0.5 * _dot(act, wd)


def _ffn_kernel(n_proj, x_ref, g_ref, wg_ref, wu_ref, wd_ref, *refs):
    x = _swiglu_half_step(x_ref[...], g_ref[...], wg_ref[...], wu_ref[...], wd_ref[...])
    if n_proj:
        gm_ref, w_ref = refs[:2]
        o_ref, proj_refs = refs[2], refs[3:]
        proj = _dot(_rms(x, gm_ref[...]).astype(BF16), w_ref[...])
        col = 0
        for p_ref in proj_refs:
            p_ref[...] = proj[:, col:col + p_ref.shape[1]]
            col += p_ref.shape[1]
    else:
        o_ref = refs[0]
    o_ref[...] = x


def _ffn(x, g, w_gu, w_down, layer, proj=None, *, tm=512):
    t, d = x.shape
    dff = w_down.shape[1]
    row = lambda n: pl.BlockSpec((tm, n), lambda i: (i, 0))
    vec = pl.BlockSpec((1, d), lambda i: (0, 0))
    in_specs = [row(d), vec, _layer_block(layer, (d, dff)), _layer_block(layer, (d, dff), (0, 1)),
                _layer_block(layer, (dff, d))]
    args = [x, g.reshape(1, d), w_gu, w_gu, w_down]
    out_shape = [jax.ShapeDtypeStruct((t, d), F32)]
    out_specs = [row(d)]
    widths = ()
    if proj is not None:
        gm, w, w_layer, widths = proj
        assert sum(widths) == w.shape[2]
        in_specs += [vec, _layer_block(w_layer, w.shape[1:])]
        args += [gm.reshape(1, d), w]
        out_shape += [jax.ShapeDtypeStruct((t, n), F32) for n in widths]
        out_specs += [row(n) for n in widths]
    return pl.pallas_call(
        functools.partial(_ffn_kernel, len(widths)),
        out_shape=out_shape,
        grid=(t // tm,),
        in_specs=in_specs,
        out_specs=out_specs,
        compiler_params=_params("parallel"),
        name="ffn",
    )(*args)


def _post_mix_kernel(n_in, final, *refs):
    a_refs, w_refs = refs[:n_in], refs[n_in:2 * n_in]
    (x_ref, g2_ref, wg_ref, wu_ref, wd_ref, gp_ref, wpg_ref, p_ref, wpp_ref, fn_ref, o_ref) = refs[2 * n_in:]
    x = x_ref[...]
    for a_ref, w_ref in zip(a_refs, w_refs):
        x = x + _dot(a_ref[...], w_ref[...])
    x = _swiglu_half_step(x, g2_ref[...], wg_ref[...], wu_ref[...], wd_ref[...])
    gate = jax.nn.sigmoid(_dot(_rms(x, gp_ref[...]).astype(BF16), wpg_ref[...]))
    x = x + gate * _dot(p_ref[...].astype(BF16), wpp_ref[...])
    if final:
        x = _rms(x, fn_ref[...])
    o_ref[...] = x


def _post_mix(a_list, w_out, out_layer, x, g2, w_gu, w_down, gp, w_gate, p, w_proj, final_g, layer, *, final,
              tm=512):
    t, d = x.shape
    dff = w_down.shape[1]
    row = lambda n: pl.BlockSpec((tm, n), lambda i: (i, 0))
    vec = pl.BlockSpec((1, d), lambda i: (0, 0))
    assert all(a.shape[1] == a_list[0].shape[1] for a in a_list)
    in_specs = [row(a.shape[1]) for a in a_list]
    in_specs += [_layer_block(out_layer, (a.shape[1], d), (k, 0)) for k, a in enumerate(a_list)]
    in_specs += [row(d), vec, _layer_block(layer, (d, dff)), _layer_block(layer, (d, dff), (0, 1)),
                 _layer_block(layer, (dff, d)), vec, _layer_block(layer, (d, d)), row(p.shape[1]),
                 _layer_block(layer, w_proj.shape[1:]), vec]
    return pl.pallas_call(
        functools.partial(_post_mix_kernel, len(a_list), final),
        out_shape=jax.ShapeDtypeStruct((t, d), F32),
        grid=(t // tm,),
        in_specs=in_specs,
        out_specs=row(d),
        compiler_params=_params("parallel"),
        name="post_mix",
    )(*a_list, *([w_out] * len(a_list)), x, g2.reshape(1, d), w_gu, w_gu, w_down, gp.reshape(1, d), w_gate, p,
      w_proj, final_g.reshape(1, d))


def _swa_kernel(sink_ref, q_ref, kp_ref, kc_ref, vp_ref, vc_ref, o_ref):
    n = pl.program_id(1)
    blk = SWA_BLOCK
    nsub = q_ref.shape[0] // blk
    scale = HEAD_DIM ** -0.5
    k = jnp.concatenate([kp_ref[...], kc_ref[...]], axis=0)
    v = jnp.concatenate([vp_ref[...], vc_ref[...]], axis=0)
    kr = pltpu.roll(k, HEAD_DIM, 1)
    vr = pltpu.roll(v, HEAD_DIM, 1)
    lo_kv = lax.broadcasted_iota(jnp.int32, k.shape, 1) < HEAD_DIM
    kdup = [jnp.where(lo_kv, k, kr).astype(BF16), jnp.where(lo_kv, kr, k).astype(BF16)]
    vdup = [jnp.where(lo_kv, v, vr).astype(BF16), jnp.where(lo_kv, vr, v).astype(BF16)]

    qi = lax.broadcasted_iota(jnp.int32, (blk, 2 * blk), 0)
    ki = lax.broadcasted_iota(jnp.int32, (blk, 2 * blk), 1)
    dist = qi + blk - ki
    in_window = (dist >= 0) & (dist < blk)
    distf = dist.astype(F32)
    lo_q = lax.broadcasted_iota(jnp.int32, (blk, LANES), 1) < HEAD_DIM

    units = [(u, h) for u in range(nsub) for h in range(SWA_HEADS)]
    keys_of = lambda x, u: x[u * blk:(u + 2) * blk]
    qm = []
    for u, h in units:
        j, e = divmod(h, 2)
        q2 = q_ref[u * blk:(u + 1) * blk, j * LANES:(j + 1) * LANES] * (scale * LOG2E)
        qm.append(jnp.where(lo_q if e == 0 else ~lo_q, q2, 0.0).astype(BF16))
    logits = [_dot(qi_, keys_of(kdup[h // SWA_GROUP], u), NT) for qi_, (u, h) in zip(qm, units)]
    ps, inv_denoms = [], []
    for s, (u, h) in zip(logits, units):
        valid = in_window & ((n > 0) | (ki >= blk)) if u == 0 else in_window
        slope = 2.0 ** (-8.0 * (h + 1) / SWA_HEADS) * LOG2E
        s = jnp.where(valid, s - slope * distf, -jnp.inf)
        sink = sink_ref[h] * LOG2E
        m = jnp.maximum(jnp.max(s, axis=-1, keepdims=True), sink)
        p = jnp.exp2(s - m)
        inv_denoms.append(1.0 / (jnp.sum(p, axis=-1, keepdims=True) + jnp.exp2(sink - m)))
        ps.append(p.astype(BF16))
    outs = [_dot(p, keys_of(vdup[h // SWA_GROUP], u)) * inv
            for p, inv, (u, h) in zip(ps, inv_denoms, units)]
    for u in range(nsub):
        for j in range(SWA_HEADS // 2):
            pair = jnp.where(lo_q, outs[u * SWA_HEADS + 2 * j], outs[u * SWA_HEADS + 2 * j + 1])
            o_ref[u * blk:(u + 1) * blk, j * LANES:(j + 1) * LANES] = pair.astype(o_ref.dtype)


def _swa(qkv, sinks, *, nsub=2):
    b, s, _ = qkv.shape
    blk = SWA_BLOCK
    tq = nsub * blk
    nq = SWA_HEADS * HEAD_DIM
    kcol = nq // LANES
    vcol = kcol + 1
    prev = lambda n: jnp.maximum(nsub * n - 1, 0)
    return pl.pallas_call(
        _swa_kernel,
        out_shape=jax.ShapeDtypeStruct((b, s, nq), BF16),
        grid=(b, s // tq),
        in_specs=[
            pl.BlockSpec(memory_space=pltpu.SMEM),
            pl.BlockSpec((None, tq, nq), lambda bi, n: (bi, n, 0)),
            pl.BlockSpec((None, blk, LANES), lambda bi, n: (bi, prev(n), kcol)),
            pl.BlockSpec((None, tq, LANES), lambda bi, n: (bi, n, kcol)),
            pl.BlockSpec((None, blk, LANES), lambda bi, n: (bi, prev(n), vcol)),
            pl.BlockSpec((None, tq, LANES), lambda bi, n: (bi, n, vcol)),
        ],
        out_specs=pl.BlockSpec((None, tq, nq), lambda bi, n: (bi, n, 0)),
        compiler_params=_params("parallel", "arbitrary"),
        name="swa",
    )(sinks, qkv, qkv, qkv, qkv, qkv)


def _tri_inverse_minus_eye(lows, ri, ci):
    same = lambda w: (ri ^ ci) < w
    base = 8
    x = [jnp.where(same(base), -low, 0.0) for low in lows]
    p2 = [_dot_bf(xi, xi) for xi in x]
    e = [xi + pi + _dot_bf(xi, pi) for xi, pi in zip(x, p2)]
    p4 = [_dot_bf(pi, pi) for pi in p2]
    e = [ei + pi + _dot_bf(ei, pi) for ei, pi in zip(e, p4)]
    w = base * 2
    while w <= RWKV_CHUNK:
        off = [jnp.where(same(w) & ~same(w // 2), low, 0.0) for low in lows]
        wm = [oi + _dot_bf(ei, oi) for ei, oi in zip(e, off)]
        e = [ei - wi - _dot_bf(wi, ei) for ei, wi in zip(e, wm)]
        w *= 2
    return e


def _rwkv_kernel(h_ref, mu_ref, w0_ref, w2_ref, a0_ref, a2_ref, g2_ref, kk_ref, ka_ref,
                 rk_ref, lnw_ref, lnb_ref, o_ref, state_ref, last_ref):
    c = RWKV_CHUNK
    tb = h_ref.shape[0]
    nchunk = tb // c
    dim = o_ref.shape[-1]
    npair = dim // LANES

    @pl.when(pl.program_id(1) == 0)
    def _():
        state_ref[...] = jnp.zeros_like(state_ref)
        last_ref[...] = jnp.zeros_like(last_ref)

    h = h_ref[...]
    row = lax.broadcasted_iota(jnp.int32, h.shape, 0)
    shifted = jnp.where(row == 0, last_ref[...], pltpu.roll(h, 1, 0))
    last_ref[...] = h[tb - 1:tb, :]
    hs = h + (shifted - h) * mu_ref[...]
    r = hs[:, 0:dim]
    k = hs[:, dim:2 * dim]
    v = hs[:, 2 * dim:3 * dim]
    xwa = hs[:, 3 * dim:3 * dim + LANES]
    xg = hs[:, 3 * dim + LANES:3 * dim + 2 * LANES]

    wl = w0_ref[...] + _dot_x3(jnp.tanh(xwa), w2_ref[...])
    logw = -jax.nn.sigmoid(wl) * float(np.exp(-0.5))
    a = jax.nn.sigmoid(a0_ref[...] + _dot_x3(xwa, a2_ref[...]))
    gate = _dot_bf(jax.nn.sigmoid(xg), g2_ref[...])

    ri = lax.broadcasted_iota(jnp.int32, (LANES, LANES), 0)
    ci = lax.broadcasted_iota(jnp.int32, (LANES, LANES), 1)
    ones_bd = ((ri ^ ci) < HEAD_DIM).astype(BF16)
    ones_bd2 = jnp.concatenate([ones_bd, ones_bd], axis=0)

    def head_sum(x):
        cols = []
        for j in range(npair):
            hi, lo = _hi_lo(x[:, j * LANES:(j + 1) * LANES])
            cols.append(_dot(jnp.concatenate([hi, lo], axis=1), ones_bd2))
        return jnp.concatenate(cols, axis=1)

    kk = k * kk_ref[...]
    kk = kk / jnp.maximum(jnp.sqrt(head_sum(kk * kk)), L2_EPS)
    k2 = k * (1.0 + (a - 1.0) * ka_ref[...])
    bvec = kk * a

    ti = lax.broadcasted_iota(jnp.int32, (tb, tb), 0)
    si = lax.broadcasted_iota(jnp.int32, (tb, tb), 1)
    tri = (((ti ^ si) < c) & (si <= ti)).astype(BF16)
    cum = _dot(jnp.concatenate([tri] * 3, axis=1), jnp.concatenate(_split3(logw), axis=0))
    e_neg = jnp.exp(-cum)
    alpha = kk * jnp.exp(cum - logw)
    beta = bvec * e_neg
    kappa = k2 * e_neg
    rho = r * jnp.exp(cum)
    cum_end = jnp.concatenate(
        [jnp.broadcast_to(cum[(n + 1) * c - 1:(n + 1) * c, :], (c, dim)) for n in range(nchunk)], axis=0)
    to_end = jnp.exp(cum_end - cum)
    beta_e = bvec * to_end
    kappa_e = k2 * to_end
    w_end = jnp.exp(cum_end)

    same_head = (ri ^ ci) < c
    strict = same_head & (ci < ri)
    incl = same_head & (ci <= ri)
    eye = ri == ci
    lo = lax.broadcasted_iota(jnp.int32, (c, LANES), 1) < HEAD_DIM

    def stack(x2):
        return jnp.concatenate([jnp.where(lo, x2, 0.0), jnp.where(lo, 0.0, x2)], axis=0)

    units = [(n, j) for n in range(nchunk) for j in range(npair)]
    blk = lambda x, n, j: x[n * c:(n + 1) * c, j * LANES:(j + 1) * LANES]
    a_s = [stack(blk(alpha, n, j)) for n, j in units]
    rho_s = [stack(blk(rho, n, j)) for n, j in units]
    v_s = [stack(blk(v, n, j)) for n, j in units]
    ends = [jnp.concatenate([stack(blk(beta_e, n, j)), stack(blk(kappa_e, n, j))], axis=0) for n, j in units]
    bk = [jnp.concatenate([blk(beta, n, j)] * 2 + [blk(kappa, n, j)] * 2, axis=0) for n, j in units]
    sc = [_dot_bf(jnp.concatenate([ai, ri_], axis=0), bi, NT) for ai, ri_, bi in zip(a_s, rho_s, bk)]
    l_ab = [jnp.where(strict, s[:LANES, :LANES], 0.0) for s in sc]
    l_ak = [jnp.where(strict, s[:LANES, LANES:], 0.0) for s in sc]
    r_b = [jnp.where(incl, s[LANES:, :LANES], 0.0) for s in sc]
    r_k = [jnp.where(incl, s[LANES:, LANES:], 0.0) for s in sc]
    e_inv = _tri_inverse_minus_eye(l_ab, ri, ci)
    lkv = [_dot_bf(li, vi) for li, vi in zip(l_ak, v_s)]
    p_m = [-(ai + _dot_bf(ei, ai)) for ei, ai in zip(e_inv, a_s)]
    q_m = [-(xi + _dot_bf(ei, xi)) for ei, xi in zip(e_inv, lkv)]
    m_m = [jnp.where(eye, blk(w_end, n, j)[:1, :], 0.0) + _dot_bf(pi, ei[:LANES], TN)
           for (n, j), pi, ei in zip(units, p_m, ends)]
    n_m = [_dot_bf(jnp.concatenate([qi, vi], axis=0), ei, TN) for qi, vi, ei in zip(q_m, v_s, ends)]
    g_m = [ri_ + _dot_bf(rb, pi) for ri_, rb, pi in zip(rho_s, r_b, p_m)]
    h_m = [_dot_bf(jnp.concatenate([rb, rk], axis=1), jnp.concatenate([qi, vi], axis=0))
           for rb, rk, qi, vi in zip(r_b, r_k, q_m, v_s)]

    state = [state_ref[j] for j in range(npair)]
    ys = []
    for n in range(nchunk):
        idx = [n * npair + j for j in range(npair)]
        nxt = [_dot_bf(state[j], m_m[i]) + n_m[i] for j, i in enumerate(idx)]
        y = [_dot_bf(g_m[i], state[j], NT) + h_m[i] for j, i in enumerate(idx)]
        ys.append(jnp.concatenate([yi[:c] + yi[c:] for yi in y], axis=1))
        state = nxt
    for j in range(npair):
        state_ref[j] = state[j]
    y = jnp.concatenate(ys, axis=0)

    mean = head_sum(y) * (1.0 / HEAD_DIM)
    d = y - mean
    var = head_sum(d * d) * (1.0 / HEAD_DIM)
    y = d * lax.rsqrt(var + GN_EPS) * lnw_ref[...] + lnb_ref[...]
    y = y + head_sum(r * k2 * rk_ref[...]) * v
    o_ref[...] = (y * gate).astype(o_ref.dtype)


def _rwkv(hb, mu, w0, w2p, a0, a2p, g2, k_k, k_a, r_k, ln_w, ln_b, *, tb=4 * RWKV_CHUNK):
    b, s, cols = hb.shape
    dim = w0.shape[-1]
    c = tb
    row = lambda x: x.reshape(1, -1)
    vec = lambda n: pl.BlockSpec((1, n), lambda bi, t: (0, 0))
    mat = lambda m: pl.BlockSpec(m.shape, lambda bi, t: (0, 0))
    return pl.pallas_call(
        _rwkv_kernel,
        out_shape=jax.ShapeDtypeStruct((b, s, dim), BF16),
        grid=(b, s // c),
        in_specs=[
            pl.BlockSpec((None, c, cols), lambda bi, t: (bi, t, 0)),
            vec(cols), vec(dim), mat(w2p), vec(dim), mat(a2p), mat(g2),
            vec(dim), vec(dim), vec(dim), vec(dim), vec(dim),
        ],
        out_specs=pl.BlockSpec((None, c, dim), lambda bi, t: (bi, t, 0)),
        scratch_shapes=[pltpu.VMEM((dim // LANES, LANES, LANES), F32), pltpu.VMEM((1, cols), F32)],
        compiler_params=_params("parallel", "arbitrary"),
        name="rwkv7",
    )(hb, row(mu), row(w0), w2p, row(a0), a2p, g2, row(k_k), row(k_a), row(r_k), row(ln_w), row(ln_b))


FOX_AUG = 3
FOX_BLOCK = 512


def _fox_in_kernel(x_ref, g_ref, w_ref, bf_ref, sel_ref, qa_ref, ka_ref, va_ref, carry_ref):
    tc = x_ref.shape[0]
    nheads = qa_ref.shape[-1] // LANES
    dim = nheads * HEAD_DIM
    scale = HEAD_DIM ** -0.5 * LOG2E

    @pl.when(pl.program_id(1) == 0)
    def _():
        carry_ref[...] = jnp.zeros_like(carry_ref)

    proj = _dot(_rms(x_ref[...], g_ref[...]).astype(BF16), w_ref[...])
    z = proj[:, 3 * dim:] + bf_ref[...]
    logf = jnp.minimum(z, 0.0) - jnp.log(1.0 + jnp.exp(-jnp.abs(z)))
    lane = lax.broadcasted_iota(jnp.int32, z.shape, 1)
    logf = jnp.where(lane < nheads, logf, 0.0)
    ti = lax.broadcasted_iota(jnp.int32, (tc, tc), 0)
    si = lax.broadcasted_iota(jnp.int32, (tc, tc), 1)
    cg = _sel_dot((si <= ti).astype(BF16), logf) + carry_ref[...]
    carry_ref[...] = cg[tc - 1:tc, :]

    hi, mid, low = _split3(cg * LOG2E)
    pieces = (hi.astype(F32) + pltpu.roll(mid.astype(F32), nheads, 1)
              + pltpu.roll(low.astype(F32), 2 * nheads, 1)).astype(BF16)
    q_c = _dot(pieces, sel_ref[0])
    k_c = -_dot(pieces, sel_ref[1])

    lane = lax.broadcasted_iota(jnp.int32, (tc, LANES), 1)
    lo = lane < HEAD_DIM
    q_one = ((lane >= HEAD_DIM + FOX_AUG) & (lane < HEAD_DIM + 2 * FOX_AUG)).astype(F32)
    k_one = ((lane >= HEAD_DIM) & (lane < HEAD_DIM + FOX_AUG)).astype(F32)
    v_one = (lane == HEAD_DIM).astype(F32)
    sources = ((0, q_c, q_one, qa_ref), (1, k_c, k_one, ka_ref), (2, None, v_one, va_ref))
    for j in range(nheads // 2):
        for part, c_aug, one, o_ref in sources:
            x = proj[:, part * dim + j * LANES:part * dim + (j + 1) * LANES]
            if part == 0:
                x = x * scale
            xr = pltpu.roll(x, HEAD_DIM, 1)
            for e, xe in ((0, x), (1, xr)):
                hs = slice((2 * j + e) * LANES, (2 * j + e + 1) * LANES)
                if c_aug is None:
                    o_ref[2 * j + e, 0] = jnp.where(lo, xe, one).T.astype(BF16)
                else:
                    o_ref[:, hs] = jnp.where(lo, xe, c_aug[:, hs] + one).astype(BF16)


def _fox_select_matrices(nheads):
    sel = np.zeros((2, LANES, nheads * LANES), np.float32)
    for h in range(nheads):
        for i in range(FOX_AUG):
            sel[0, i * nheads + h, h * LANES + HEAD_DIM + i] = 1.0
            sel[1, i * nheads + h, h * LANES + HEAD_DIM + FOX_AUG + i] = 1.0
    return jnp.asarray(sel, BF16)


def _fox_in(x, g, w_in, b_f, *, batch, tc):
    t, d = x.shape
    s = t // batch
    nheads = b_f.shape[-1]
    assert FOX_AUG * nheads <= LANES
    w = jnp.pad(w_in, ((0, 0), (0, LANES - nheads))).astype(BF16)
    bf = jnp.zeros((1, LANES), F32).at[0, :nheads].set(b_f)
    sel = _fox_select_matrices(nheads)
    wide = nheads * LANES
    nt = s // tc
    resident = pl.Buffered(1)
    return pl.pallas_call(
        _fox_in_kernel,
        out_shape=(jax.ShapeDtypeStruct((batch, s, wide), BF16),
                   jax.ShapeDtypeStruct((batch, s, wide), BF16),
                   jax.ShapeDtypeStruct((batch, nheads, nt, LANES, tc), BF16)),
        grid=(batch, nt),
        in_specs=[
            pl.BlockSpec((tc, d), lambda bi, ti: (bi * nt + ti, 0)),
            pl.BlockSpec((1, d), lambda bi, ti: (0, 0)),
            pl.BlockSpec(w.shape, lambda bi, ti: (0, 0), pipeline_mode=resident),
            pl.BlockSpec((1, LANES), lambda bi, ti: (0, 0)),
            pl.BlockSpec(sel.shape, lambda bi, ti: (0, 0, 0), pipeline_mode=resident),
        ],
        out_specs=(pl.BlockSpec((None, tc, wide), lambda bi, ti: (bi, ti, 0)),
                   pl.BlockSpec((None, tc, wide), lambda bi, ti: (bi, ti, 0)),
                   pl.BlockSpec((None, nheads, 1, LANES, tc), lambda bi, ti: (bi, 0, ti, 0, 0))),
        scratch_shapes=[pltpu.VMEM((1, LANES), F32)],
        compiler_params=_params("parallel", "arbitrary"),
        name="fox_in",
    )(x, g.reshape(1, d), w, bf, sel)


def _fox_attn_kernel(q_ref, k_ref, v_ref, o_ref, m_ref, acc_ref, sa_ref, sb_ref):
    tq = o_ref.shape[0]
    nh = q_ref.shape[1] // LANES
    qi = pl.program_id(2)
    m_ref[...] = jnp.full_like(m_ref, -jnp.inf)
    acc_ref[...] = jnp.zeros_like(acc_ref)
    keys = lax.broadcasted_iota(jnp.int32, (tq, tq), 0)
    queries = lax.broadcasted_iota(jnp.int32, (tq, tq), 1)

    def logits_into(dst_ref, qb, kb, heads=None):
        q0 = pl.multiple_of(qb * tq, tq)
        k0 = pl.multiple_of(kb * tq, tq)
        for e in range(nh) if heads is None else heads:
            hs = slice(e * LANES, (e + 1) * LANES)
            dst_ref[e] = _dot(k_ref[pl.ds(k0, tq), hs], q_ref[pl.ds(q0, tq), hs], NT)

    def consume(src_ref, kb, diagonal, after_head=None):
        for e in range(nh):
            s = src_ref[e]
            if diagonal:
                s = jnp.where(keys <= queries, s, -jnp.inf)
            m_prev = m_ref[e]
            m_new = jnp.maximum(m_prev, jnp.max(s, axis=0, keepdims=True))
            p = jnp.exp2(s - m_new).astype(BF16)
            acc_ref[e] = jnp.exp2(m_prev - m_new) * acc_ref[e] + _dot(v_ref[e, kb], p)
            m_ref[e] = m_new
            if after_head is not None:
                after_head(e)

    @pl.when(qi == 0)
    def _():
        logits_into(sa_ref, 0, 0)

    def two_blocks(t, carry):
        kb = 2 * t
        logits_into(sb_ref, qi, kb + 1)
        consume(sa_ref, kb, False)
        logits_into(sa_ref, qi, kb + 2)
        consume(sb_ref, kb + 1, False)
        return carry

    lax.fori_loop(0, qi // 2, two_blocks, 0)
    nxt = jnp.minimum(qi + 1, pl.num_programs(2) - 1)

    @pl.when(qi % 2 == 0)
    def _():
        consume(sa_ref, qi, True, after_head=lambda e: logits_into(sa_ref, nxt, 0, heads=(e,)))

    @pl.when(qi % 2 == 1)
    def _():
        logits_into(sb_ref, qi, qi)
        consume(sa_ref, qi - 1, False)
        logits_into(sa_ref, nxt, 0)
        consume(sb_ref, qi, True)

    for pair in range(nh // 2):
        outs = []
        for e in (2 * pair, 2 * pair + 1):
            acc = acc_ref[e]
            outs.append(acc[:HEAD_DIM] / acc[HEAD_DIM:HEAD_DIM + 1])
        o_ref[:, pair * LANES:(pair + 1) * LANES] = jnp.concatenate(outs, axis=0).T.astype(o_ref.dtype)


def _fox_attn(q_aug, k_aug, v_aug, *, tq, nh=4):
    b, s, wide = q_aug.shape
    nheads = wide // LANES
    return pl.pallas_call(
        _fox_attn_kernel,
        out_shape=jax.ShapeDtypeStruct((b, s, nheads * HEAD_DIM), BF16),
        grid=(b, nheads // nh, s // tq),
        in_specs=[
            pl.BlockSpec((None, s, nh * LANES), lambda bi, g, qi: (bi, 0, g)),
            pl.BlockSpec((None, s, nh * LANES), lambda bi, g, qi: (bi, 0, g)),
            pl.BlockSpec((None, nh, s // tq, LANES, tq), lambda bi, g, qi: (bi, g, 0, 0, 0)),
        ],
        out_specs=pl.BlockSpec((None, tq, nh * HEAD_DIM), lambda bi, g, qi: (bi, qi, g)),
        scratch_shapes=[pltpu.VMEM((nh, 1, tq), F32), pltpu.VMEM((nh, LANES, tq), F32),
                        pltpu.VMEM((nh, tq, tq), F32), pltpu.VMEM((nh, tq, tq), F32)],
        compiler_params=_params("parallel", "parallel", "arbitrary"),
        name="fox_attn",
    )(q_aug, k_aug, v_aug)


def kernel(x, p, ffn1_norm, ffn1_w_gu, ffn1_w_down, mix_norm, ffn2_norm, ffn2_w_gu, ffn2_w_down, ple_norm, ple_w_gate, ple_w_proj, even_w_in, even_w_out, swa_sinks, rwkv_mu, rwkv_w0, rwkv_w2, rwkv_a0, rwkv_a2, rwkv_g2, rwkv_k_k, rwkv_k_a, rwkv_r_k, rwkv_ln_w, rwkv_ln_b, fox_w_in, fox_b_f, fox_w_out, final_norm):
    b, s, d = x.shape
    depth = p.shape[0]
    t = b * s
    bf = lambda w: w.astype(BF16)
    swa_q = SWA_HEADS * HEAD_DIM
    swa_cols = swa_q + 2 * (SWA_HEADS // SWA_GROUP) * HEAD_DIM
    rwkv_dim = rwkv_w0.shape[-1]
    lora = rwkv_w2.shape[1]
    fox_heads = fox_b_f.shape[-1]
    fox_dim = fox_heads * HEAD_DIM

    ffn1_gu, ffn1_down, ffn2_gu, ffn2_down = bf(ffn1_w_gu), bf(ffn1_w_down), bf(ffn2_w_gu), bf(ffn2_w_down)
    ple_gate, ple_proj = bf(ple_w_gate), bf(ple_w_proj)
    even_in, even_out, fox_out = bf(even_w_in), bf(even_w_out), bf(fox_w_out)

    x = x.reshape(t, d)
    for i in range(depth):
        j = i // 2
        if i % 2 == 0:
            widths = (swa_cols, even_w_in.shape[-1] - swa_cols)
            x, qkv, hb = _ffn(x, ffn1_norm[i], ffn1_gu, ffn1_down, i, proj=(mix_norm[i], even_in, j, widths))
            ya = _swa(qkv.reshape(b, s, swa_cols), swa_sinks[j])
            zeros = jnp.zeros((lora, rwkv_dim), F32)
            w2p = jnp.concatenate([rwkv_w2[j], zeros], axis=0)
            a2p = jnp.concatenate([zeros, rwkv_a2[j]], axis=0)
            yb = _rwkv(hb.reshape(b, s, -1), rwkv_mu[j], rwkv_w0[j], w2p, rwkv_a0[j], a2p,
                       bf(rwkv_g2[j]), rwkv_k_k[j], rwkv_k_a[j], rwkv_r_k[j].reshape(-1),
                       rwkv_ln_w[j], rwkv_ln_b[j])
            mixed = ([ya.reshape(t, swa_q), yb.reshape(t, rwkv_dim)], even_out, j)
        else:
            (x,) = _ffn(x, ffn1_norm[i], ffn1_gu, ffn1_down, i)
            q_aug, k_aug, v_aug = _fox_in(x, mix_norm[i], fox_w_in[j], fox_b_f[j], batch=b, tc=FOX_BLOCK)
            yc = _fox_attn(q_aug, k_aug, v_aug, tq=FOX_BLOCK)
            mixed = ([yc.reshape(t, fox_dim)], fox_out, j)
        x = _post_mix(*mixed, x, ffn2_norm[i], ffn2_gu, ffn2_down, ple_norm[i], ple_gate, p[i].reshape(t, -1),
                      ple_proj, final_norm, i, final=(i == depth - 1))
    return x.reshape(b, s, d)
```

```python
import functools

import jax
import jax.numpy as jnp
import numpy as np
from jax import lax
from jax.experimental import pallas as pl
from jax.experimental.pallas import tpu as pltpu

F32 = jnp.float32
BF16 = jnp.bfloat16

LANES = 128
HEAD_DIM = 64
SWA_HEADS = 8
SWA_GROUP = 4
SWA_BLOCK = 128
RWKV_CHUNK = 64
NORM_EPS = 1e-6
GN_EPS = 64e-5
L2_EPS = 1e-12
LOG2E = float(np.log2(np.e))
VMEM_LIMIT = 56 * 1024 * 1024

NN = (((1,), (0,)), ((), ()))
NT = (((1,), (1,)), ((), ()))
TN = (((0,), (0,)), ((), ()))


def _dot(a, b, dims=NN):
    return lax.dot_general(a, b, dims, preferred_element_type=F32)


def _dot_bf(a, b, dims=NN):
    return _dot(a.astype(BF16), b.astype(BF16), dims)


def _hi_lo(x):
    hi = x.astype(BF16)
    lo = (x - hi.astype(F32)).astype(BF16)
    return hi, lo


def _dot_x3(a, b, dims=NN):
    ah, al = _hi_lo(a)
    bh, bl = _hi_lo(b)
    return _dot(ah, bh, dims) + (_dot(ah, bl, dims) + _dot(al, bh, dims))


def _split3(x):
    hi = x.astype(BF16)
    r1 = x - hi.astype(F32)
    mid = r1.astype(BF16)
    lo = (r1 - mid.astype(F32)).astype(BF16)
    return hi, mid, lo


def _sel_dot(sel, x):
    hi, mid, lo = _split3(x)
    return _dot(sel, hi) + (_dot(sel, mid) + _dot(sel, lo))


def _dot_sel(x, sel):
    hi, mid, lo = _split3(x)
    return _dot(hi, sel) + (_dot(mid, sel) + _dot(lo, sel))


def _rms(x, g):
    ms = jnp.mean(x * x, axis=-1, keepdims=True)
    return x * lax.rsqrt(ms + NORM_EPS) * g


def _params(*sem):
    return pltpu.CompilerParams(dimension_semantics=sem, vmem_limit_bytes=VMEM_LIMIT)


def _layer_block(layer, shape, block=(0, 0)):
    return pl.BlockSpec((None,) + tuple(shape), lambda i: (layer,) + tuple(block), pipeline_mode=pl.Buffered(1))


def _swiglu_half_step(x, g, wg, wu, wd):
    hn = _rms(x, g).astype(BF16)
    gate = _dot(hn, wg)
    up = _dot(hn, wu)
    act = (gate * jax.nn.sigmoid(gate) * up).astype(BF16)
    return x + 0.5 * _dot(act, wd)


def _ffn_kernel(n_proj, x_ref, g_ref, wg_ref, wu_ref, wd_ref, *refs):
    x = _swiglu_half_step(x_ref[...], g_ref[...], wg_ref[...], wu_ref[...], wd_ref[...])
    if n_proj:
        gm_ref, w_ref = refs[:2]
        o_ref, proj_refs = refs[2], refs[3:]
        proj = _dot(_rms(x, gm_ref[...]).astype(BF16), w_ref[...])
        col = 0
        for p_ref in proj_refs:
            p_ref[...] = proj[:, col:col + p_ref.shape[1]]
            col += p_ref.shape[1]
    else:
        o_ref = refs[0]
    o_ref[...] = x


def _ffn(x, g, w_gu, w_down, layer, proj=None, *, tm=512):
    t, d = x.shape
    dff = w_down.shape[1]
    row = lambda n: pl.BlockSpec((tm, n), lambda i: (i, 0))
    vec = pl.BlockSpec((1, d), lambda i: (0, 0))
    in_specs = [row(d), vec, _layer_block(layer, (d, dff)), _layer_block(layer, (d, dff), (0, 1)),
                _layer_block(layer, (dff, d))]
    args = [x, g.reshape(1, d), w_gu, w_gu, w_down]
    out_shape = [jax.ShapeDtypeStruct((t, d), F32)]
    out_specs = [row(d)]
    widths = ()
    if proj is not None:
        gm, w, w_layer, widths = proj
        assert sum(widths) == w.shape[2]
        in_specs += [vec, _layer_block(w_layer, w.shape[1:])]
        args += [gm.reshape(1, d), w]
        out_shape += [jax.ShapeDtypeStruct((t, n), F32) for n in widths]
        out_specs += [row(n) for n in widths]
    return pl.pallas_call(
        functools.partial(_ffn_kernel, len(widths)),
        out_shape=out_shape,
        grid=(t // tm,),
        in_specs=in_specs,
        out_specs=out_specs,
        compiler_params=_params("parallel"),
        name="ffn",
    )(*args)


def _post_mix_kernel(n_in, final, *refs):
    a_refs, w_refs = refs[:n_in], refs[n_in:2 * n_in]
    (x_ref, g2_ref, wg_ref, wu_ref, wd_ref, gp_ref, wpg_ref, p_ref, wpp_ref, fn_ref, o_ref) = refs[2 * n_in:]
    x = x_ref[...]
    for a_ref, w_ref in zip(a_refs, w_refs):
        x = x + _dot(a_ref[...], w_ref[...])
    x = _swiglu_half_step(x, g2_ref[...], wg_ref[...], wu_ref[...], wd_ref[...])
    gate = jax.nn.sigmoid(_dot(_rms(x, gp_ref[...]).astype(BF16), wpg_ref[...]))
    x = x + gate * _dot(p_ref[...].astype(BF16), wpp_ref[...])
    if final:
        x = _rms(x, fn_ref[...])
    o_ref[...] = x


def _post_mix(a_list, w_out, out_layer, x, g2, w_gu, w_down, gp, w_gate, p, w_proj, final_g, layer, *, final,
              tm=512):
    t, d = x.shape
    dff = w_down.shape[1]
    row = lambda n: pl.BlockSpec((tm, n), lambda i: (i, 0))
    vec = pl.BlockSpec((1, d), lambda i: (0, 0))
    assert all(a.shape[1] == a_list[0].shape[1] for a in a_list)
    in_specs = [row(a.shape[1]) for a in a_list]
    in_specs += [_layer_block(out_layer, (a.shape[1], d), (k, 0)) for k, a in enumerate(a_list)]
    in_specs += [row(d), vec, _layer_block(layer, (d, dff)), _layer_block(layer, (d, dff), (0, 1)),
                 _layer_block(layer, (dff, d)), vec, _layer_block(layer, (d, d)),
                 pl.BlockSpec((None, tm, p.shape[2]), lambda i: (layer, i, 0)),
                 _layer_block(layer, w_proj.shape[1:]), vec]
    return pl.pallas_call(
        functools.partial(_post_mix_kernel, len(a_list), final),
        out_shape=jax.ShapeDtypeStruct((t, d), F32),
        grid=(t // tm,),
        in_specs=in_specs,
        out_specs=row(d),
        compiler_params=_params("parallel"),
        name="post_mix",
    )(*a_list, *([w_out] * len(a_list)), x, g2.reshape(1, d), w_gu, w_gu, w_down, gp.reshape(1, d), w_gate, p,
      w_proj, final_g.reshape(1, d))


def _swa_kernel(sink_ref, q_ref, kp_ref, kc_ref, vp_ref, vc_ref, o_ref):
    n = pl.program_id(1)
    blk = SWA_BLOCK
    nsub = q_ref.shape[0] // blk
    scale = HEAD_DIM ** -0.5
    k = jnp.concatenate([kp_ref[...], kc_ref[...]], axis=0)
    v = jnp.concatenate([vp_ref[...], vc_ref[...]], axis=0)
    kr = pltpu.roll(k, HEAD_DIM, 1)
    vr = pltpu.roll(v, HEAD_DIM, 1)
    lo_kv = lax.broadcasted_iota(jnp.int32, k.shape, 1) < HEAD_DIM
    kdup = [jnp.where(lo_kv, k, kr).astype(BF16), jnp.where(lo_kv, kr, k).astype(BF16)]
    vdup = [jnp.where(lo_kv, v, vr).astype(BF16), jnp.where(lo_kv, vr, v).astype(BF16)]

    qi = lax.broadcasted_iota(jnp.int32, (blk, 2 * blk), 0)
    ki = lax.broadcasted_iota(jnp.int32, (blk, 2 * blk), 1)
    dist = qi + blk - ki
    in_window = (dist >= 0) & (dist < blk)
    distf = dist.astype(F32)
    lo_q = lax.broadcasted_iota(jnp.int32, (blk, LANES), 1) < HEAD_DIM

    units = [(u, h) for u in range(nsub) for h in range(SWA_HEADS)]
    keys_of = lambda x, u: x[u * blk:(u + 2) * blk]
    qm = []
    for u, h in units:
        j, e = divmod(h, 2)
        q2 = q_ref[u * blk:(u + 1) * blk, j * LANES:(j + 1) * LANES] * (scale * LOG2E)
        qm.append(jnp.where(lo_q if e == 0 else ~lo_q, q2, 0.0).astype(BF16))
    logits = [_dot(qi_, keys_of(kdup[h // SWA_GROUP], u), NT) for qi_, (u, h) in zip(qm, units)]
    ps, inv_denoms = [], []
    for s, (u, h) in zip(logits, units):
        valid = in_window & ((n > 0) | (ki >= blk)) if u == 0 else in_window
        slope = 2.0 ** (-8.0 * (h + 1) / SWA_HEADS) * LOG2E
        s = jnp.where(valid, s - slope * distf, -jnp.inf)
        sink = sink_ref[h] * LOG2E
        m = jnp.maximum(jnp.max(s, axis=-1, keepdims=True), sink)
        p = jnp.exp2(s - m)
        inv_denoms.append(1.0 / (jnp.sum(p, axis=-1, keepdims=True) + jnp.exp2(sink - m)))
        ps.append(p.astype(BF16))
    outs = [_dot(p, keys_of(vdup[h // SWA_GROUP], u)) * inv
            for p, inv, (u, h) in zip(ps, inv_denoms, units)]
    for u in range(nsub):
        for j in range(SWA_HEADS // 2):
            pair = jnp.where(lo_q, outs[u * SWA_HEADS + 2 * j], outs[u * SWA_HEADS + 2 * j + 1])
            o_ref[u * blk:(u + 1) * blk, j * LANES:(j + 1) * LANES] = pair.astype(o_ref.dtype)


def _swa(qkv, sinks, *, nsub=2):
    b, s, _ = qkv.shape
    blk = SWA_BLOCK
    tq = nsub * blk
    nq = SWA_HEADS * HEAD_DIM
    kcol = nq // LANES
    vcol = kcol + 1
    prev = lambda n: jnp.maximum(nsub * n - 1, 0)
    return pl.pallas_call(
        _swa_kernel,
        out_shape=jax.ShapeDtypeStruct((b, s, nq), BF16),
        grid=(b, s // tq),
        in_specs=[
            pl.BlockSpec(memory_space=pltpu.SMEM),
            pl.BlockSpec((None, tq, nq), lambda bi, n: (bi, n, 0)),
            pl.BlockSpec((None, blk, LANES), lambda bi, n: (bi, prev(n), kcol)),
            pl.BlockSpec((None, tq, LANES), lambda bi, n: (bi, n, kcol)),
            pl.BlockSpec((None, blk, LANES), lambda bi, n: (bi, prev(n), vcol)),
            pl.BlockSpec((None, tq, LANES), lambda bi, n: (bi, n, vcol)),
        ],
        out_specs=pl.BlockSpec((None, tq, nq), lambda bi, n: (bi, n, 0)),
        compiler_params=_params("parallel", "arbitrary"),
        name="swa",
    )(sinks, qkv, qkv, qkv, qkv, qkv)


def _tri_inverse_minus_eye(lows, ri, ci):
    same = lambda w: (ri ^ ci) < w
    base = 8
    x = [jnp.where(same(base), -low, 0.0) for low in lows]
    p2 = [_dot_bf(xi, xi) for xi in x]
    e = [xi + pi + _dot_bf(xi, pi) for xi, pi in zip(x, p2)]
    p4 = [_dot_bf(pi, pi) for pi in p2]
    e = [ei + pi + _dot_bf(ei, pi) for ei, pi in zip(e, p4)]
    w = base * 2
    while w <= RWKV_CHUNK:
        off = [jnp.where(same(w) & ~same(w // 2), low, 0.0) for low in lows]
        wm = [oi + _dot_bf(ei, oi) for ei, oi in zip(e, off)]
        e = [ei - wi - _dot_bf(wi, ei) for ei, wi in zip(e, wm)]
        w *= 2
    return e


def _rwkv_kernel(h_ref, mu_ref, w0_ref, w2_ref, a0_ref, a2_ref, g2_ref, kk_ref, ka_ref,
                 rk_ref, lnw_ref, lnb_ref, o_ref, state_ref, last_ref):
    c = RWKV_CHUNK
    tb = h_ref.shape[0]
    nchunk = tb // c
    dim = o_ref.shape[-1]
    npair = dim // LANES

    @pl.when(pl.program_id(1) == 0)
    def _():
        state_ref[...] = jnp.zeros_like(state_ref)
        last_ref[...] = jnp.zeros_like(last_ref)

    h = h_ref[...]
    row = lax.broadcasted_iota(jnp.int32, h.shape, 0)
    shifted = jnp.where(row == 0, last_ref[...], pltpu.roll(h, 1, 0))
    last_ref[...] = h[tb - 1:tb, :]
    hs = h + (shifted - h) * mu_ref[...]
    r = hs[:, 0:dim]
    k = hs[:, dim:2 * dim]
    v = hs[:, 2 * dim:3 * dim]
    xwa = hs[:, 3 * dim:3 * dim + LANES]
    xg = hs[:, 3 * dim + LANES:3 * dim + 2 * LANES]

    wl = w0_ref[...] + _dot_x3(jnp.tanh(xwa), w2_ref[...])
    logw = -jax.nn.sigmoid(wl) * float(np.exp(-0.5))
    a = jax.nn.sigmoid(a0_ref[...] + _dot_x3(xwa, a2_ref[...]))
    gate = _dot_bf(jax.nn.sigmoid(xg), g2_ref[...])

    ri = lax.broadcasted_iota(jnp.int32, (LANES, LANES), 0)
    ci = lax.broadcasted_iota(jnp.int32, (LANES, LANES), 1)
    ones_bd = ((ri ^ ci) < HEAD_DIM).astype(BF16)
    ones_bd2 = jnp.concatenate([ones_bd, ones_bd], axis=0)

    def head_sum(x):
        cols = []
        for j in range(npair):
            hi, lo = _hi_lo(x[:, j * LANES:(j + 1) * LANES])
            cols.append(_dot(jnp.concatenate([hi, lo], axis=1), ones_bd2))
        return jnp.concatenate(cols, axis=1)

    kk = k * kk_ref[...]
    kk = kk * jnp.minimum(lax.rsqrt(head_sum(kk * kk)), 1.0 / L2_EPS)
    k2 = k * (1.0 + (a - 1.0) * ka_ref[...])
    bvec = kk * a

    ti = lax.broadcasted_iota(jnp.int32, (tb, tb), 0)
    si = lax.broadcasted_iota(jnp.int32, (tb, tb), 1)
    tri = (((ti ^ si) < c) & (si <= ti)).astype(BF16)
    cum = _dot(jnp.concatenate([tri] * 3, axis=1), jnp.concatenate(_split3(logw), axis=0))
    e_neg = jnp.exp(-cum)
    alpha = kk * jnp.exp(cum - logw)
    beta = bvec * e_neg
    kappa = k2 * e_neg
    rho = r * jnp.exp(cum)
    cum_end = jnp.concatenate(
        [jnp.broadcast_to(cum[(n + 1) * c - 1:(n + 1) * c, :], (c, dim)) for n in range(nchunk)], axis=0)
    to_end = jnp.exp(cum_end - cum)
    beta_e = bvec * to_end
    kappa_e = k2 * to_end
    w_end = jnp.exp(cum_end)

    same_head = (ri ^ ci) < c
    strict = same_head & (ci < ri)
    incl = same_head & (ci <= ri)
    eye = ri == ci
    lo = lax.broadcasted_iota(jnp.int32, (c, LANES), 1) < HEAD_DIM

    def stack(x2):
        return jnp.concatenate([jnp.where(lo, x2, 0.0), jnp.where(lo, 0.0, x2)], axis=0)

    units = [(n, j) for n in range(nchunk) for j in range(npair)]
    blk = lambda x, n, j: x[n * c:(n + 1) * c, j * LANES:(j + 1) * LANES]
    a_s = [stack(blk(alpha, n, j)) for n, j in units]
    rho_s = [stack(blk(rho, n, j)) for n, j in units]
    v_s = [stack(blk(v, n, j)) for n, j in units]
    ends = [jnp.concatenate([stack(blk(beta_e, n, j)), stack(blk(kappa_e, n, j))], axis=0) for n, j in units]
    bk = [jnp.concatenate([blk(beta, n, j)] * 2 + [blk(kappa, n, j)] * 2, axis=0) for n, j in units]
    sc = [_dot_bf(jnp.concatenate([ai, ri_], axis=0), bi, NT) for ai, ri_, bi in zip(a_s, rho_s, bk)]
    l_ab = [jnp.where(strict, s[:LANES, :LANES], 0.0) for s in sc]
    l_ak = [jnp.where(strict, s[:LANES, LANES:], 0.0) for s in sc]
    r_b = [jnp.where(incl, s[LANES:, :LANES], 0.0) for s in sc]
    r_k = [jnp.where(incl, s[LANES:, LANES:], 0.0) for s in sc]
    e_inv = _tri_inverse_minus_eye(l_ab, ri, ci)
    lkv = [_dot_bf(li, vi) for li, vi in zip(l_ak, v_s)]
    p_m = [-(ai + _dot_bf(ei, ai)) for ei, ai in zip(e_inv, a_s)]
    q_m = [-(xi + _dot_bf(ei, xi)) for ei, xi in zip(e_inv, lkv)]
    m_m = [jnp.where(eye, blk(w_end, n, j)[:1, :], 0.0) + _dot_bf(pi, ei[:LANES], TN)
           for (n, j), pi, ei in zip(units, p_m, ends)]
    n_m = [_dot_bf(jnp.concatenate([qi, vi], axis=0), ei, TN) for qi, vi, ei in zip(q_m, v_s, ends)]
    g_m = [ri_ + _dot_bf(rb, pi) for ri_, rb, pi in zip(rho_s, r_b, p_m)]
    h_m = [_dot_bf(jnp.concatenate([rb, rk], axis=1), jnp.concatenate([qi, vi], axis=0))
           for rb, rk, qi, vi in zip(r_b, r_k, q_m, v_s)]

    state = [state_ref[j] for j in range(npair)]
    ys = []
    for n in range(nchunk):
        idx = [n * npair + j for j in range(npair)]
        nxt = [_dot_bf(state[j], m_m[i]) + n_m[i] for j, i in enumerate(idx)]
        y = [_dot_bf(g_m[i], state[j], NT) + h_m[i] for j, i in enumerate(idx)]
        ys.append(jnp.concatenate([yi[:c] + yi[c:] for yi in y], axis=1))
        state = nxt
    for j in range(npair):
        state_ref[j] = state[j]
    y = jnp.concatenate(ys, axis=0)

    mean = head_sum(y) * (1.0 / HEAD_DIM)
    d = y - mean
    var = head_sum(d * d) * (1.0 / HEAD_DIM)
    y = d * lax.rsqrt(var + GN_EPS) * lnw_ref[...] + lnb_ref[...]
    y = y + head_sum(r * k2 * rk_ref[...]) * v
    o_ref[...] = (y * gate).astype(o_ref.dtype)


def _rwkv(hb, mu, w0, w2p, a0, a2p, g2, k_k, k_a, r_k, ln_w, ln_b, *, tb=4 * RWKV_CHUNK):
    b, s, cols = hb.shape
    dim = w0.shape[-1]
    c = tb
    row = lambda x: x.reshape(1, -1)
    vec = lambda n: pl.BlockSpec((1, n), lambda bi, t: (0, 0))
    mat = lambda m: pl.BlockSpec(m.shape, lambda bi, t: (0, 0))
    return pl.pallas_call(
        _rwkv_kernel,
        out_shape=jax.ShapeDtypeStruct((b, s, dim), BF16),
        grid=(b, s // c),
        in_specs=[
            pl.BlockSpec((None, c, cols), lambda bi, t: (bi, t, 0)),
            vec(cols), vec(dim), mat(w2p), vec(dim), mat(a2p), mat(g2),
            vec(dim), vec(dim), vec(dim), vec(dim), vec(dim),
        ],
        out_specs=pl.BlockSpec((None, c, dim), lambda bi, t: (bi, t, 0)),
        scratch_shapes=[pltpu.VMEM((dim // LANES, LANES, LANES), F32), pltpu.VMEM((1, cols), F32)],
        compiler_params=_params("parallel", "arbitrary"),
        name="rwkv7",
    )(hb, row(mu), row(w0), w2p, row(a0), a2p, g2, row(k_k), row(k_a), row(r_k), row(ln_w), row(ln_b))


FOX_AUG = 3
FOX_BLOCK = 512


def _fox_in_kernel(x_ref, g_ref, w_ref, bf_ref, sel_ref, qa_ref, ka_ref, va_ref, carry_ref):
    tc = x_ref.shape[0]
    nheads = qa_ref.shape[-1] // LANES
    dim = nheads * HEAD_DIM
    scale = HEAD_DIM ** -0.5 * LOG2E

    @pl.when(pl.program_id(1) == 0)
    def _():
        carry_ref[...] = jnp.zeros_like(carry_ref)

    proj = _dot(_rms(x_ref[...], g_ref[...]).astype(BF16), w_ref[...])
    z = proj[:, 3 * dim:] + bf_ref[...]
    logf = jnp.minimum(z, 0.0) - jnp.log(1.0 + jnp.exp(-jnp.abs(z)))
    lane = lax.broadcasted_iota(jnp.int32, z.shape, 1)
    logf = jnp.where(lane < nheads, logf, 0.0)
    ti = lax.broadcasted_iota(jnp.int32, (tc, tc), 0)
    si = lax.broadcasted_iota(jnp.int32, (tc, tc), 1)
    cg = _sel_dot((si <= ti).astype(BF16), logf) + carry_ref[...]
    carry_ref[...] = cg[tc - 1:tc, :]

    hi, mid, low = _split3(cg * LOG2E)
    pieces = (hi.astype(F32) + pltpu.roll(mid.astype(F32), nheads, 1)
              + pltpu.roll(low.astype(F32), 2 * nheads, 1)
              + (lane == FOX_AUG * nheads).astype(F32)).astype(BF16)
    q_c = _dot(pieces, sel_ref[0])
    k_c = _dot(pieces, sel_ref[1])

    lane = lax.broadcasted_iota(jnp.int32, (tc, LANES), 1)
    lo = lane < HEAD_DIM
    v_one = (lane == HEAD_DIM).astype(F32)
    sources = ((0, q_c, qa_ref), (1, k_c, ka_ref), (2, None, va_ref))
    for j in range(nheads // 2):
        for part, c_aug, o_ref in sources:
            x = proj[:, part * dim + j * LANES:part * dim + (j + 1) * LANES]
            if part == 0:
                x = x * scale
            xr = pltpu.roll(x, HEAD_DIM, 1)
            for e, xe in ((0, x), (1, xr)):
                hs = slice((2 * j + e) * LANES, (2 * j + e + 1) * LANES)
                if c_aug is None:
                    o_ref[2 * j + e, 0] = jnp.where(lo, xe, v_one).T.astype(BF16)
                else:
                    o_ref[:, hs] = jnp.where(lo, xe, c_aug[:, hs]).astype(BF16)


def _fox_select_matrices(nheads):
    sel = np.zeros((2, LANES, nheads * LANES), np.float32)
    one_row = FOX_AUG * nheads
    for h in range(nheads):
        for i in range(FOX_AUG):
            sel[0, i * nheads + h, h * LANES + HEAD_DIM + i] = 1.0
            sel[0, one_row, h * LANES + HEAD_DIM + FOX_AUG + i] = 1.0
            sel[1, one_row, h * LANES + HEAD_DIM + i] = 1.0
            sel[1, i * nheads + h, h * LANES + HEAD_DIM + FOX_AUG + i] = -1.0
    return jnp.asarray(sel, BF16)


def _fox_in(x, g, w_in, b_f, *, batch, tc):
    t, d = x.shape
    s = t // batch
    nheads = b_f.shape[-1]
    assert FOX_AUG * nheads < LANES
    w = jnp.pad(w_in, ((0, 0), (0, LANES - nheads))).astype(BF16)
    bf = jnp.zeros((1, LANES), F32).at[0, :nheads].set(b_f)
    sel = _fox_select_matrices(nheads)
    wide = nheads * LANES
    nt = s // tc
    resident = pl.Buffered(1)
    return pl.pallas_call(
        _fox_in_kernel,
        out_shape=(jax.ShapeDtypeStruct((batch, s, wide), BF16),
                   jax.ShapeDtypeStruct((batch, s, wide), BF16),
                   jax.ShapeDtypeStruct((batch, nheads, nt, LANES, tc), BF16)),
        grid=(batch, nt),
        in_specs=[
            pl.BlockSpec((tc, d), lambda bi, ti: (bi * nt + ti, 0)),
            pl.BlockSpec((1, d), lambda bi, ti: (0, 0)),
            pl.BlockSpec(w.shape, lambda bi, ti: (0, 0), pipeline_mode=resident),
            pl.BlockSpec((1, LANES), lambda bi, ti: (0, 0)),
            pl.BlockSpec(sel.shape, lambda bi, ti: (0, 0, 0), pipeline_mode=resident),
        ],
        out_specs=(pl.BlockSpec((None, tc, wide), lambda bi, ti: (bi, ti, 0)),
                   pl.BlockSpec((None, tc, wide), lambda bi, ti: (bi, ti, 0)),
                   pl.BlockSpec((None, nheads, 1, LANES, tc), lambda bi, ti: (bi, 0, ti, 0, 0))),
        scratch_shapes=[pltpu.VMEM((1, LANES), F32)],
        compiler_params=_params("parallel", "arbitrary"),
        name="fox_in",
    )(x, g.reshape(1, d), w, bf, sel)


def _fox_attn_kernel(q_ref, k_ref, v_ref, o_ref, m_ref, acc_ref, sa_ref, sb_ref):
    tq = o_ref.shape[0]
    nh = q_ref.shape[1] // LANES
    qi = pl.program_id(2)
    m_ref[...] = jnp.full_like(m_ref, -jnp.inf)
    acc_ref[...] = jnp.zeros_like(acc_ref)
    keys = lax.broadcasted_iota(jnp.int32, (tq, tq), 0)
    queries = lax.broadcasted_iota(jnp.int32, (tq, tq), 1)

    def logits_into(dst_ref, qb, kb, heads=None):
        q0 = pl.multiple_of(qb * tq, tq)
        k0 = pl.multiple_of(kb * tq, tq)
        for e in range(nh) if heads is None else heads:
            hs = slice(e * LANES, (e + 1) * LANES)
            dst_ref[e] = _dot(k_ref[pl.ds(k0, tq), hs], q_ref[pl.ds(q0, tq), hs], NT)

    def consume(src_ref, kb, diagonal, after_head=None):
        for e in range(nh):
            s = src_ref[e]
            if diagonal:
                s = jnp.where(keys <= queries, s, -jnp.inf)
            m_prev = m_ref[e]
            m_new = jnp.maximum(m_prev, jnp.max(s, axis=0, keepdims=True))
            p = jnp.exp2(s - m_new).astype(BF16)
            acc_ref[e] = jnp.exp2(m_prev - m_new) * acc_ref[e] + _dot(v_ref[e, kb], p)
            m_ref[e] = m_new
            if after_head is not None:
                after_head(e)

    @pl.when(qi == 0)
    def _():
        logits_into(sa_ref, 0, 0)

    def two_blocks(t, carry):
        kb = 2 * t
        logits_into(sb_ref, qi, kb + 1)
        consume(sa_ref, kb, False)
        logits_into(sa_ref, qi, kb + 2)
        consume(sb_ref, kb + 1, False)
        return carry

    lax.fori_loop(0, qi // 2, two_blocks, 0)
    nxt = jnp.minimum(qi + 1, pl.num_programs(2) - 1)

    @pl.when(qi % 2 == 0)
    def _():
        consume(sa_ref, qi, True, after_head=lambda e: logits_into(sa_ref, nxt, 0, heads=(e,)))

    @pl.when(qi % 2 == 1)
    def _():
        logits_into(sb_ref, qi, qi)
        consume(sa_ref, qi - 1, False)
        logits_into(sa_ref, nxt, 0)
        consume(sb_ref, qi, True)

    for pair in range(nh // 2):
        outs = []
        for e in (2 * pair, 2 * pair + 1):
            acc = acc_ref[e]
            outs.append(acc[:HEAD_DIM] / acc[HEAD_DIM:HEAD_DIM + 1])
        o_ref[:, pair * LANES:(pair + 1) * LANES] = jnp.concatenate(outs, axis=0).T.astype(o_ref.dtype)


def _fox_attn(q_aug, k_aug, v_aug, *, tq, nh=4):
    b, s, wide = q_aug.shape
    nheads = wide // LANES
    return pl.pallas_call(
        _fox_attn_kernel,
        out_shape=jax.ShapeDtypeStruct((b, s, nheads * HEAD_DIM), BF16),
        grid=(b, nheads // nh, s // tq),
        in_specs=[
            pl.BlockSpec((None, s, nh * LANES), lambda bi, g, qi: (bi, 0, g)),
            pl.BlockSpec((None, s, nh * LANES), lambda bi, g, qi: (bi, 0, g)),
            pl.BlockSpec((None, nh, s // tq, LANES, tq), lambda bi, g, qi: (bi, g, 0, 0, 0)),
        ],
        out_specs=pl.BlockSpec((None, tq, nh * HEAD_DIM), lambda bi, g, qi: (bi, qi, g)),
        scratch_shapes=[pltpu.VMEM((nh, 1, tq), F32), pltpu.VMEM((nh, LANES, tq), F32),
                        pltpu.VMEM((nh, tq, tq), F32), pltpu.VMEM((nh, tq, tq), F32)],
        compiler_params=_params("parallel", "parallel", "arbitrary"),
        name="fox_attn",
    )(q_aug, k_aug, v_aug)


def kernel(x, p, ffn1_norm, ffn1_w_gu, ffn1_w_down, mix_norm, ffn2_norm, ffn2_w_gu, ffn2_w_down, ple_norm, ple_w_gate, ple_w_proj, even_w_in, even_w_out, swa_sinks, rwkv_mu, rwkv_w0, rwkv_w2, rwkv_a0, rwkv_a2, rwkv_g2, rwkv_k_k, rwkv_k_a, rwkv_r_k, rwkv_ln_w, rwkv_ln_b, fox_w_in, fox_b_f, fox_w_out, final_norm):
    b, s, d = x.shape
    depth = p.shape[0]
    t = b * s
    bf = lambda w: w.astype(BF16)
    swa_q = SWA_HEADS * HEAD_DIM
    swa_cols = swa_q + 2 * (SWA_HEADS // SWA_GROUP) * HEAD_DIM
    rwkv_dim = rwkv_w0.shape[-1]
    lora = rwkv_w2.shape[1]
    fox_heads = fox_b_f.shape[-1]
    fox_dim = fox_heads * HEAD_DIM

    ffn1_gu, ffn1_down, ffn2_gu, ffn2_down = bf(ffn1_w_gu), bf(ffn1_w_down), bf(ffn2_w_gu), bf(ffn2_w_down)
    ple_gate, ple_proj = bf(ple_w_gate), bf(ple_w_proj)
    even_in, even_out, fox_out = bf(even_w_in), bf(even_w_out), bf(fox_w_out)

    x = x.reshape(t, d)
    for i in range(depth):
        j = i // 2
        if i % 2 == 0:
            widths = (swa_cols, even_w_in.shape[-1] - swa_cols)
            x, qkv, hb = _ffn(x, ffn1_norm[i], ffn1_gu, ffn1_down, i, proj=(mix_norm[i], even_in, j, widths))
            ya = _swa(qkv.reshape(b, s, swa_cols), swa_sinks[j])
            zeros = jnp.zeros((lora, rwkv_dim), F32)
            w2p = jnp.concatenate([rwkv_w2[j], zeros], axis=0)
            a2p = jnp.concatenate([zeros, rwkv_a2[j]], axis=0)
            yb = _rwkv(hb.reshape(b, s, -1), rwkv_mu[j], rwkv_w0[j], w2p, rwkv_a0[j], a2p,
                       bf(rwkv_g2[j]), rwkv_k_k[j], rwkv_k_a[j], rwkv_r_k[j].reshape(-1),
                       rwkv_ln_w[j], rwkv_ln_b[j])
            mixed = ([ya.reshape(t, swa_q), yb.reshape(t, rwkv_dim)], even_out, j)
        else:
            (x,) = _ffn(x, ffn1_norm[i], ffn1_gu, ffn1_down, i)
            q_aug, k_aug, v_aug = _fox_in(x, mix_norm[i], fox_w_in[j], fox_b_f[j], batch=b, tc=FOX_BLOCK)
            yc = _fox_attn(q_aug, k_aug, v_aug, tq=FOX_BLOCK)
            mixed = ([yc.reshape(t, fox_dim)], fox_out, j)
        x = _post_mix(*mixed, x, ffn2_norm[i], ffn2_gu, ffn2_down, ple_norm[i], ple_gate, p.reshape(depth, t, -1),
                      ple_proj, final_norm, i, final=(i == depth - 1))
    return x.reshape(b, s, d)
```

```python
import functools

import jax
import jax.numpy as jnp
import numpy as np
from jax import lax
from jax.experimental import pallas as pl
from jax.experimental.pallas import tpu as pltpu

F32 = jnp.float32
BF16 = jnp.bfloat16

LANES = 128
HEAD_DIM = 64
SWA_HEADS = 8
SWA_GROUP = 4
SWA_BLOCK = 128
RWKV_CHUNK = 64
NORM_EPS = 1e-6
GN_EPS = 64e-5
L2_EPS = 1e-12
LOG2E = float(np.log2(np.e))
VMEM_LIMIT = 56 * 1024 * 1024

NN = (((1,), (0,)), ((), ()))
NT = (((1,), (1,)), ((), ()))
TN = (((0,), (0,)), ((), ()))


def _dot(a, b, dims=NN):
    return lax.dot_general(a, b, dims, preferred_element_type=F32)


def _dot_bf(a, b, dims=NN):
    return _dot(a.astype(BF16), b.astype(BF16), dims)


def _hi_lo(x):
    hi = x.astype(BF16)
    lo = (x - hi.astype(F32)).astype(BF16)
    return hi, lo


def _dot_x3(a, b, dims=NN):
    ah, al = _hi_lo(a)
    bh, bl = _hi_lo(b)
    return _dot(ah, bh, dims) + (_dot(ah, bl, dims) + _dot(al, bh, dims))


def _split3(x):
    hi = x.astype(BF16)
    r1 = x - hi.astype(F32)
    mid = r1.astype(BF16)
    lo = (r1 - mid.astype(F32)).astype(BF16)
    return hi, mid, lo


def _sel_dot(sel, x):
    hi, mid, lo = _split3(x)
    return _dot(sel, hi) + (_dot(sel, mid) + _dot(sel, lo))


def _dot_sel(x, sel):
    hi, mid, lo = _split3(x)
    return _dot(hi, sel) + (_dot(mid, sel) + _dot(lo, sel))


def _rms(x, g):
    ms = jnp.mean(x * x, axis=-1, keepdims=True)
    return x * lax.rsqrt(ms + NORM_EPS) * g


def _params(*sem):
    return pltpu.CompilerParams(dimension_semantics=sem, vmem_limit_bytes=VMEM_LIMIT)


LOAD_STEPS = 16


def _weight_chunk(w, layer, rows, block=0):
    chunk = rows // LOAD_STEPS
    assert chunk * LOAD_STEPS == rows and chunk % 16 == 0, (rows, chunk)
    return pl.BlockSpec((None, chunk, w.shape[2]),
                        lambda i: (layer, block * LOAD_STEPS + jnp.minimum(i, LOAD_STEPS - 1), 0))


def _stash(step, chunk_ref, copy_ref):
    rows = chunk_ref.shape[0]
    copy_ref[pl.ds(pl.multiple_of(step * rows, rows), rows), :] = chunk_ref[...].astype(BF16)


def _row_tile(tm, n):
    return pl.BlockSpec((tm, n), lambda i: (jnp.maximum(i - LOAD_STEPS, 0), 0))


def _swiglu_half_step(x, g, wgu_ref, wd_ref):
    dff = wd_ref.shape[0]
    hn = _rms(x, g).astype(BF16)
    gate = _dot(hn, wgu_ref[:, :dff])
    up = _dot(hn, wgu_ref[:, dff:])
    act = (gate * jax.nn.sigmoid(gate) * up).astype(BF16)
    return x + 0.5 * _dot(act, wd_ref[...])


def _ffn_kernel(n_proj, x_ref, g_ref, wgu_ref, wd_ref, *refs):
    n_w = 3 if n_proj else 2
    copies = refs[len(refs) - n_w:]
    refs = refs[:len(refs) - n_w]
    step = pl.program_id(0)

    @pl.when(step < LOAD_STEPS)
    def _():
        _stash(step, wgu_ref, copies[0])
        _stash(step, wd_ref, copies[1])
        if n_proj:
            _stash(step, refs[1], copies[2])

    @pl.when(step >= LOAD_STEPS)
    def _():
        x = _swiglu_half_step(x_ref[...], g_ref[...], copies[0], copies[1])
        if n_proj:
            gm_ref, o_ref, proj_refs = refs[0], refs[2], refs[3:]
            proj = _dot(_rms(x, gm_ref[...]).astype(BF16), copies[2][...])
            col = 0
            for p_ref in proj_refs:
                p_ref[...] = proj[:, col:col + p_ref.shape[1]]
                col += p_ref.shape[1]
        else:
            o_ref = refs[0]
        o_ref[...] = x


def _ffn(x, g, w_gu, w_down, layer, proj=None, *, tm=512):
    t, d = x.shape
    dff = w_down.shape[1]
    vec = pl.BlockSpec((1, d), lambda i: (0, 0))
    in_specs = [_row_tile(tm, d), vec, _weight_chunk(w_gu, layer, d), _weight_chunk(w_down, layer, dff)]
    args = [x, g.reshape(1, d), w_gu, w_down]
    scratch = [pltpu.VMEM((d, 2 * dff), BF16), pltpu.VMEM((dff, d), BF16)]
    out_shape = [jax.ShapeDtypeStruct((t, d), F32)]
    out_specs = [_row_tile(tm, d)]
    widths = ()
    if proj is not None:
        gm, w, w_layer, widths = proj
        assert sum(widths) == w.shape[2]
        in_specs += [vec, _weight_chunk(w, w_layer, d)]
        args += [gm.reshape(1, d), w]
        scratch += [pltpu.VMEM(w.shape[1:], BF16)]
        out_shape += [jax.ShapeDtypeStruct((t, n), F32) for n in widths]
        out_specs += [_row_tile(tm, n) for n in widths]
    return pl.pallas_call(
        functools.partial(_ffn_kernel, len(widths)),
        out_shape=out_shape,
        grid=(LOAD_STEPS + t // tm,),
        in_specs=in_specs,
        out_specs=out_specs,
        scratch_shapes=scratch,
        compiler_params=_params("arbitrary"),
        name="ffn",
    )(*args)


def _post_mix_kernel(n_in, final, *refs):
    a_refs, wo_refs = refs[:n_in], refs[n_in:2 * n_in]
    (x_ref, g2_ref, wgu_ref, wd_ref, gp_ref, wpg_ref, p_ref, wpp_ref, fn_ref, o_ref) = refs[2 * n_in:2 * n_in + 10]
    copies = refs[2 * n_in + 10:]
    wo_copies, (wgu_c, wd_c, wpg_c, wpp_c) = copies[:n_in], copies[n_in:]
    step = pl.program_id(0)

    @pl.when(step < LOAD_STEPS)
    def _():
        for src, dst in zip(wo_refs + (wgu_ref, wd_ref, wpg_ref, wpp_ref), wo_copies + (wgu_c, wd_c, wpg_c, wpp_c)):
            _stash(step, src, dst)

    @pl.when(step >= LOAD_STEPS)
    def _():
        x = x_ref[...]
        for a_ref, w_c in zip(a_refs, wo_copies):
            x = x + _dot(a_ref[...], w_c[...])
        x = _swiglu_half_step(x, g2_ref[...], wgu_c, wd_c)
        gate = jax.nn.sigmoid(_dot(_rms(x, gp_ref[...]).astype(BF16), wpg_c[...]))
        x = x + gate * _dot(p_ref[...].astype(BF16), wpp_c[...])
        if final:
            x = _rms(x, fn_ref[...])
        o_ref[...] = x


def _post_mix(a_list, w_out, out_layer, x, g2, w_gu, w_down, gp, w_gate, p, w_proj, final_g, layer, *, final,
              tm=512):
    t, d = x.shape
    dff = w_down.shape[1]
    vec = pl.BlockSpec((1, d), lambda i: (0, 0))
    width = a_list[0].shape[1]
    assert all(a.shape[1] == width for a in a_list)
    in_specs = [_row_tile(tm, width) for _ in a_list]
    in_specs += [_weight_chunk(w_out, out_layer, width, block=k) for k in range(len(a_list))]
    in_specs += [_row_tile(tm, d), vec, _weight_chunk(w_gu, layer, d), _weight_chunk(w_down, layer, dff), vec,
                 _weight_chunk(w_gate, layer, d),
                 pl.BlockSpec((None, tm, p.shape[2]), lambda i: (layer, jnp.maximum(i - LOAD_STEPS, 0), 0)),
                 _weight_chunk(w_proj, layer, w_proj.shape[1]), vec]
    scratch = [pltpu.VMEM((width, d), BF16) for _ in a_list]
    scratch += [pltpu.VMEM((d, 2 * dff), BF16), pltpu.VMEM((dff, d), BF16), pltpu.VMEM((d, d), BF16),
                pltpu.VMEM(w_proj.shape[1:], BF16)]
    return pl.pallas_call(
        functools.partial(_post_mix_kernel, len(a_list), final),
        out_shape=jax.ShapeDtypeStruct((t, d), F32),
        grid=(LOAD_STEPS + t // tm,),
        in_specs=in_specs,
        out_specs=_row_tile(tm, d),
        scratch_shapes=scratch,
        compiler_params=_params("arbitrary"),
        name="post_mix",
    )(*a_list, *([w_out] * len(a_list)), x, g2.reshape(1, d), w_gu, w_down, gp.reshape(1, d), w_gate, p,
      w_proj, final_g.reshape(1, d))


def _swa_kernel(sink_ref, q_ref, kp_ref, kc_ref, vp_ref, vc_ref, o_ref):
    n = pl.program_id(1)
    blk = SWA_BLOCK
    nsub = q_ref.shape[0] // blk
    scale = HEAD_DIM ** -0.5
    k = jnp.concatenate([kp_ref[...], kc_ref[...]], axis=0)
    v = jnp.concatenate([vp_ref[...], vc_ref[...]], axis=0)
    kr = pltpu.roll(k, HEAD_DIM, 1)
    vr = pltpu.roll(v, HEAD_DIM, 1)
    lo_kv = lax.broadcasted_iota(jnp.int32, k.shape, 1) < HEAD_DIM
    kdup = [jnp.where(lo_kv, k, kr).astype(BF16), jnp.where(lo_kv, kr, k).astype(BF16)]
    vdup = [jnp.where(lo_kv, v, vr).astype(BF16), jnp.where(lo_kv, vr, v).astype(BF16)]

    qi = lax.broadcasted_iota(jnp.int32, (blk, 2 * blk), 0)
    ki = lax.broadcasted_iota(jnp.int32, (blk, 2 * blk), 1)
    dist = qi + blk - ki
    in_window = (dist >= 0) & (dist < blk)
    distf = dist.astype(F32)
    lo_q = lax.broadcasted_iota(jnp.int32, (blk, LANES), 1) < HEAD_DIM

    units = [(u, h) for u in range(nsub) for h in range(SWA_HEADS)]
    keys_of = lambda x, u: x[u * blk:(u + 2) * blk]
    qm = []
    for u, h in units:
        j, e = divmod(h, 2)
        q2 = q_ref[u * blk:(u + 1) * blk, j * LANES:(j + 1) * LANES] * (scale * LOG2E)
        qm.append(jnp.where(lo_q if e == 0 else ~lo_q, q2, 0.0).astype(BF16))
    logits = [_dot(qi_, keys_of(kdup[h // SWA_GROUP], u), NT) for qi_, (u, h) in zip(qm, units)]
    ps, inv_denoms = [], []
    for s, (u, h) in zip(logits, units):
        valid = in_window & ((n > 0) | (ki >= blk)) if u == 0 else in_window
        slope = 2.0 ** (-8.0 * (h + 1) / SWA_HEADS) * LOG2E
        s = jnp.where(valid, s - slope * distf, -jnp.inf)
        sink = sink_ref[h] * LOG2E
        m = jnp.maximum(jnp.max(s, axis=-1, keepdims=True), sink)
        p = jnp.exp2(s - m)
        inv_denoms.append(1.0 / (jnp.sum(p, axis=-1, keepdims=True) + jnp.exp2(sink - m)))
        ps.append(p.astype(BF16))
    outs = [_dot(p, keys_of(vdup[h // SWA_GROUP], u)) * inv
            for p, inv, (u, h) in zip(ps, inv_denoms, units)]
    for u in range(nsub):
        for j in range(SWA_HEADS // 2):
            pair = jnp.where(lo_q, outs[u * SWA_HEADS + 2 * j], outs[u * SWA_HEADS + 2 * j + 1])
            o_ref[u * blk:(u + 1) * blk, j * LANES:(j + 1) * LANES] = pair.astype(o_ref.dtype)


def _swa(qkv, sinks, *, nsub=2):
    b, s, _ = qkv.shape
    blk = SWA_BLOCK
    tq = nsub * blk
    nq = SWA_HEADS * HEAD_DIM
    kcol = nq // LANES
    vcol = kcol + 1
    prev = lambda n: jnp.maximum(nsub * n - 1, 0)
    return pl.pallas_call(
        _swa_kernel,
        out_shape=jax.ShapeDtypeStruct((b, s, nq), BF16),
        grid=(b, s // tq),
        in_specs=[
            pl.BlockSpec(memory_space=pltpu.SMEM),
            pl.BlockSpec((None, tq, nq), lambda bi, n: (bi, n, 0)),
            pl.BlockSpec((None, blk, LANES), lambda bi, n: (bi, prev(n), kcol)),
            pl.BlockSpec((None, tq, LANES), lambda bi, n: (bi, n, kcol)),
            pl.BlockSpec((None, blk, LANES), lambda bi, n: (bi, prev(n), vcol)),
            pl.BlockSpec((None, tq, LANES), lambda bi, n: (bi, n, vcol)),
        ],
        out_specs=pl.BlockSpec((None, tq, nq), lambda bi, n: (bi, n, 0)),
        compiler_params=_params("parallel", "arbitrary"),
        name="swa",
    )(sinks, qkv, qkv, qkv, qkv, qkv)


def _tri_inverse_minus_eye(lows, ri, ci):
    same = lambda w: (ri ^ ci) < w
    base = 8
    x = [jnp.where(same(base), -low, 0.0) for low in lows]
    p2 = [_dot_bf(xi, xi) for xi in x]
    e = [xi + pi + _dot_bf(xi, pi) for xi, pi in zip(x, p2)]
    p4 = [_dot_bf(pi, pi) for pi in p2]
    e = [ei + pi + _dot_bf(ei, pi) for ei, pi in zip(e, p4)]
    w = base * 2
    while w <= RWKV_CHUNK:
        off = [jnp.where(same(w) & ~same(w // 2), low, 0.0) for low in lows]
        wm = [oi + _dot_bf(ei, oi) for ei, oi in zip(e, off)]
        e = [ei - wi - _dot_bf(wi, ei) for ei, wi in zip(e, wm)]
        w *= 2
    return e


def _rwkv_kernel(h_ref, mu_ref, w0_ref, w2_ref, a0_ref, a2_ref, g2_ref, kk_ref, ka_ref,
                 rk_ref, lnw_ref, lnb_ref, o_ref, state_ref, last_ref):
    c = RWKV_CHUNK
    tb = h_ref.shape[0]
    nchunk = tb // c
    dim = o_ref.shape[-1]
    npair = dim // LANES

    @pl.when(pl.program_id(1) == 0)
    def _():
        state_ref[...] = jnp.zeros_like(state_ref)
        last_ref[...] = jnp.zeros_like(last_ref)

    h = h_ref[...]
    row = lax.broadcasted_iota(jnp.int32, h.shape, 0)
    shifted = jnp.where(row == 0, last_ref[...], pltpu.roll(h, 1, 0))
    last_ref[...] = h[tb - 1:tb, :]
    hs = h + (shifted - h) * mu_ref[...]
    r = hs[:, 0:dim]
    k = hs[:, dim:2 * dim]
    v = hs[:, 2 * dim:3 * dim]
    xwa = hs[:, 3 * dim:3 * dim + LANES]
    xg = hs[:, 3 * dim + LANES:3 * dim + 2 * LANES]

    wl = w0_ref[...] + _dot_x3(jnp.tanh(xwa), w2_ref[...])
    logw = -jax.nn.sigmoid(wl) * float(np.exp(-0.5))
    a = jax.nn.sigmoid(a0_ref[...] + _dot_x3(xwa, a2_ref[...]))
    gate = _dot_bf(jax.nn.sigmoid(xg), g2_ref[...])

    ri = lax.broadcasted_iota(jnp.int32, (LANES, LANES), 0)
    ci = lax.broadcasted_iota(jnp.int32, (LANES, LANES), 1)
    ones_bd = ((ri ^ ci) < HEAD_DIM).astype(BF16)
    ones_bd2 = jnp.concatenate([ones_bd, ones_bd], axis=0)

    def head_sum(x):
        cols = []
        for j in range(npair):
            hi, lo = _hi_lo(x[:, j * LANES:(j + 1) * LANES])
            cols.append(_dot(jnp.concatenate([hi, lo], axis=1), ones_bd2))
        return jnp.concatenate(cols, axis=1)

    kk = k * kk_ref[...]
    kk = kk * jnp.minimum(lax.rsqrt(head_sum(kk * kk)), 1.0 / L2_EPS)
    k2 = k * (1.0 + (a - 1.0) * ka_ref[...])
    bvec = kk * a

    ti = lax.broadcasted_iota(jnp.int32, (tb, tb), 0)
    si = lax.broadcasted_iota(jnp.int32, (tb, tb), 1)
    tri = (((ti ^ si) < c) & (si <= ti)).astype(BF16)
    cum = _dot(jnp.concatenate([tri] * 3, axis=1), jnp.concatenate(_split3(logw), axis=0))
    e_neg = jnp.exp(-cum)
    alpha = kk * jnp.exp(cum - logw)
    beta = bvec * e_neg
    kappa = k2 * e_neg
    rho = r * jnp.exp(cum)
    cum_end = jnp.concatenate(
        [jnp.broadcast_to(cum[(n + 1) * c - 1:(n + 1) * c, :], (c, dim)) for n in range(nchunk)], axis=0)
    to_end = jnp.exp(cum_end - cum)
    beta_e = bvec * to_end
    kappa_e = k2 * to_end
    w_end = jnp.exp(cum_end)

    same_head = (ri ^ ci) < c
    strict = same_head & (ci < ri)
    incl = same_head & (ci <= ri)
    eye = ri == ci
    lo = lax.broadcasted_iota(jnp.int32, (c, LANES), 1) < HEAD_DIM

    def stack(x2):
        return jnp.concatenate([jnp.where(lo, x2, 0.0), jnp.where(lo, 0.0, x2)], axis=0)

    units = [(n, j) for n in range(nchunk) for j in range(npair)]
    blk = lambda x, n, j: x[n * c:(n + 1) * c, j * LANES:(j + 1) * LANES]
    a_s = [stack(blk(alpha, n, j)) for n, j in units]
    rho_s = [stack(blk(rho, n, j)) for n, j in units]
    v_s = [stack(blk(v, n, j)) for n, j in units]
    ends = [jnp.concatenate([stack(blk(beta_e, n, j)), stack(blk(kappa_e, n, j))], axis=0) for n, j in units]
    bk = [jnp.concatenate([blk(beta, n, j)] * 2 + [blk(kappa, n, j)] * 2, axis=0) for n, j in units]
    sc = [_dot_bf(jnp.concatenate([ai, ri_], axis=0), bi, NT) for ai, ri_, bi in zip(a_s, rho_s, bk)]
    l_ab = [jnp.where(strict, s[:LANES, :LANES], 0.0) for s in sc]
    l_ak = [jnp.where(strict, s[:LANES, LANES:], 0.0) for s in sc]
    r_b = [jnp.where(incl, s[LANES:, :LANES], 0.0) for s in sc]
    r_k = [jnp.where(incl, s[LANES:, LANES:], 0.0) for s in sc]
    e_inv = _tri_inverse_minus_eye(l_ab, ri, ci)
    lkv = [_dot_bf(li, vi) for li, vi in zip(l_ak, v_s)]
    p_m = [-(ai + _dot_bf(ei, ai)) for ei, ai in zip(e_inv, a_s)]
    q_m = [-(xi + _dot_bf(ei, xi)) for ei, xi in zip(e_inv, lkv)]
    m_m = [jnp.where(eye, blk(w_end, n, j)[:1, :], 0.0) + _dot_bf(pi, ei[:LANES], TN)
           for (n, j), pi, ei in zip(units, p_m, ends)]
    n_m = [_dot_bf(jnp.concatenate([qi, vi], axis=0), ei, TN) for qi, vi, ei in zip(q_m, v_s, ends)]
    g_m = [ri_ + _dot_bf(rb, pi) for ri_, rb, pi in zip(rho_s, r_b, p_m)]
    h_m = [_dot_bf(jnp.concatenate([rb, rk], axis=1), jnp.concatenate([qi, vi], axis=0))
           for rb, rk, qi, vi in zip(r_b, r_k, q_m, v_s)]

    state = [state_ref[j] for j in range(npair)]
    ys = []
    for n in range(nchunk):
        idx = [n * npair + j for j in range(npair)]
        nxt = [_dot_bf(state[j], m_m[i]) + n_m[i] for j, i in enumerate(idx)]
        y = [_dot_bf(g_m[i], state[j], NT) + h_m[i] for j, i in enumerate(idx)]
        ys.append(jnp.concatenate([yi[:c] + yi[c:] for yi in y], axis=1))
        state = nxt
    for j in range(npair):
        state_ref[j] = state[j]
    y = jnp.concatenate(ys, axis=0)

    mean = head_sum(y) * (1.0 / HEAD_DIM)
    d = y - mean
    var = head_sum(d * d) * (1.0 / HEAD_DIM)
    y = d * lax.rsqrt(var + GN_EPS) * lnw_ref[...] + lnb_ref[...]
    y = y + head_sum(r * k2 * rk_ref[...]) * v
    o_ref[...] = (y * gate).astype(o_ref.dtype)


def _rwkv(hb, mu, w0, w2p, a0, a2p, g2, k_k, k_a, r_k, ln_w, ln_b, *, tb=4 * RWKV_CHUNK):
    b, s, cols = hb.shape
    dim = w0.shape[-1]
    c = tb
    row = lambda x: x.reshape(1, -1)
    vec = lambda n: pl.BlockSpec((1, n), lambda bi, t: (0, 0))
    mat = lambda m: pl.BlockSpec(m.shape, lambda bi, t: (0, 0))
    return pl.pallas_call(
        _rwkv_kernel,
        out_shape=jax.ShapeDtypeStruct((b, s, dim), BF16),
        grid=(b, s // c),
        in_specs=[
            pl.BlockSpec((None, c, cols), lambda bi, t: (bi, t, 0)),
            vec(cols), vec(dim), mat(w2p), vec(dim), mat(a2p), mat(g2),
            vec(dim), vec(dim), vec(dim), vec(dim), vec(dim),
        ],
        out_specs=pl.BlockSpec((None, c, dim), lambda bi, t: (bi, t, 0)),
        scratch_shapes=[pltpu.VMEM((dim // LANES, LANES, LANES), F32), pltpu.VMEM((1, cols), F32)],
        compiler_params=_params("parallel", "arbitrary"),
        name="rwkv7",
    )(hb, row(mu), row(w0), w2p, row(a0), a2p, g2, row(k_k), row(k_a), row(r_k), row(ln_w), row(ln_b))


FOX_AUG = 3
FOX_BLOCK = 512


def _fox_in_kernel(x_ref, g_ref, w_ref, bf_ref, sel_ref, qa_ref, ka_ref, va_ref, carry_ref):
    tc = x_ref.shape[0]
    nheads = qa_ref.shape[-1] // LANES
    dim = nheads * HEAD_DIM
    scale = HEAD_DIM ** -0.5 * LOG2E

    @pl.when(pl.program_id(1) == 0)
    def _():
        carry_ref[...] = jnp.zeros_like(carry_ref)

    proj = _dot(_rms(x_ref[...], g_ref[...]).astype(BF16), w_ref[...])
    z = proj[:, 3 * dim:] + bf_ref[...]
    logf = jnp.minimum(z, 0.0) - jnp.log(1.0 + jnp.exp(-jnp.abs(z)))
    lane = lax.broadcasted_iota(jnp.int32, z.shape, 1)
    logf = jnp.where(lane < nheads, logf, 0.0)
    ti = lax.broadcasted_iota(jnp.int32, (tc, tc), 0)
    si = lax.broadcasted_iota(jnp.int32, (tc, tc), 1)
    cg = _sel_dot((si <= ti).astype(BF16), logf) + carry_ref[...]
    carry_ref[...] = cg[tc - 1:tc, :]

    hi, mid, low = _split3(cg * LOG2E)
    pieces = (hi.astype(F32) + pltpu.roll(mid.astype(F32), nheads, 1)
              + pltpu.roll(low.astype(F32), 2 * nheads, 1)
              + (lane == FOX_AUG * nheads).astype(F32)).astype(BF16)
    q_c = _dot(pieces, sel_ref[0])
    k_c = _dot(pieces, sel_ref[1])

    lane = lax.broadcasted_iota(jnp.int32, (tc, LANES), 1)
    lo = lane < HEAD_DIM
    v_one = (lane == HEAD_DIM).astype(F32)
    sources = ((0, q_c, qa_ref), (1, k_c, ka_ref), (2, None, va_ref))
    for j in range(nheads // 2):
        for part, c_aug, o_ref in sources:
            x = proj[:, part * dim + j * LANES:part * dim + (j + 1) * LANES]
            if part == 0:
                x = x * scale
            xr = pltpu.roll(x, HEAD_DIM, 1)
            for e, xe in ((0, x), (1, xr)):
                hs = slice((2 * j + e) * LANES, (2 * j + e + 1) * LANES)
                if c_aug is None:
                    o_ref[2 * j + e, 0] = jnp.where(lo, xe, v_one).T.astype(BF16)
                else:
                    o_ref[:, hs] = jnp.where(lo, xe, c_aug[:, hs]).astype(BF16)


def _fox_select_matrices(nheads):
    sel = np.zeros((2, LANES, nheads * LANES), np.float32)
    one_row = FOX_AUG * nheads
    for h in range(nheads):
        for i in range(FOX_AUG):
            sel[0, i * nheads + h, h * LANES + HEAD_DIM + i] = 1.0
            sel[0, one_row, h * LANES + HEAD_DIM + FOX_AUG + i] = 1.0
            sel[1, one_row, h * LANES + HEAD_DIM + i] = 1.0
            sel[1, i * nheads + h, h * LANES + HEAD_DIM + FOX_AUG + i] = -1.0
    return jnp.asarray(sel, BF16)


def _fox_in(x, g, w_in, b_f, *, batch, tc):
    t, d = x.shape
    s = t // batch
    nheads = b_f.shape[-1]
    assert FOX_AUG * nheads < LANES
    w = jnp.pad(w_in, ((0, 0), (0, LANES - nheads))).astype(BF16)
    bf = jnp.zeros((1, LANES), F32).at[0, :nheads].set(b_f)
    sel = _fox_select_matrices(nheads)
    wide = nheads * LANES
    nt = s // tc
    resident = pl.Buffered(1)
    return pl.pallas_call(
        _fox_in_kernel,
        out_shape=(jax.ShapeDtypeStruct((batch, s, wide), BF16),
                   jax.ShapeDtypeStruct((batch, s, wide), BF16),
                   jax.ShapeDtypeStruct((batch, nheads, nt, LANES, tc), BF16)),
        grid=(batch, nt),
        in_specs=[
            pl.BlockSpec((tc, d), lambda bi, ti: (bi * nt + ti, 0)),
            pl.BlockSpec((1, d), lambda bi, ti: (0, 0)),
            pl.BlockSpec(w.shape, lambda bi, ti: (0, 0), pipeline_mode=resident),
            pl.BlockSpec((1, LANES), lambda bi, ti: (0, 0)),
            pl.BlockSpec(sel.shape, lambda bi, ti: (0, 0, 0), pipeline_mode=resident),
        ],
        out_specs=(pl.BlockSpec((None, tc, wide), lambda bi, ti: (bi, ti, 0)),
                   pl.BlockSpec((None, tc, wide), lambda bi, ti: (bi, ti, 0)),
                   pl.BlockSpec((None, nheads, 1, LANES, tc), lambda bi, ti: (bi, 0, ti, 0, 0))),
        scratch_shapes=[pltpu.VMEM((1, LANES), F32)],
        compiler_params=_params("parallel", "arbitrary"),
        name="fox_in",
    )(x, g.reshape(1, d), w, bf, sel)


def _fox_attn_kernel(q_ref, k_ref, v_ref, o_ref, m_ref, acc_ref, sa_ref, sb_ref):
    tq = o_ref.shape[0]
    nh = q_ref.shape[1] // LANES
    qi = pl.program_id(2)
    m_ref[...] = jnp.full_like(m_ref, -jnp.inf)
    acc_ref[...] = jnp.zeros_like(acc_ref)
    keys = lax.broadcasted_iota(jnp.int32, (tq, tq), 0)
    queries = lax.broadcasted_iota(jnp.int32, (tq, tq), 1)

    def logits_into(dst_ref, qb, kb, heads=None):
        q0 = pl.multiple_of(qb * tq, tq)
        k0 = pl.multiple_of(kb * tq, tq)
        for e in range(nh) if heads is None else heads:
            hs = slice(e * LANES, (e + 1) * LANES)
            dst_ref[e] = _dot(k_ref[pl.ds(k0, tq), hs], q_ref[pl.ds(q0, tq), hs], NT)

    def consume(src_ref, kb, diagonal, after_head=None):
        for e in range(nh):
            s = src_ref[e]
            if diagonal:
                s = jnp.where(keys <= queries, s, -jnp.inf)
            m_prev = m_ref[e]
            m_new = jnp.maximum(m_prev, jnp.max(s, axis=0, keepdims=True))
            p = jnp.exp2(s - m_new).astype(BF16)
            acc_ref[e] = jnp.exp2(m_prev - m_new) * acc_ref[e] + _dot(v_ref[e, kb], p)
            m_ref[e] = m_new
            if after_head is not None:
                after_head(e)

    @pl.when(qi == 0)
    def _():
        logits_into(sa_ref, 0, 0)

    def two_blocks(t, carry):
        kb = 2 * t
        logits_into(sb_ref, qi, kb + 1)
        consume(sa_ref, kb, False)
        logits_into(sa_ref, qi, kb + 2)
        consume(sb_ref, kb + 1, False)
        return carry

    lax.fori_loop(0, qi // 2, two_blocks, 0)
    nxt = jnp.minimum(qi + 1, pl.num_programs(2) - 1)

    @pl.when(qi % 2 == 0)
    def _():
        consume(sa_ref, qi, True, after_head=lambda e: logits_into(sa_ref, nxt, 0, heads=(e,)))

    @pl.when(qi % 2 == 1)
    def _():
        logits_into(sb_ref, qi, qi)
        consume(sa_ref, qi - 1, False)
        logits_into(sa_ref, nxt, 0)
        consume(sb_ref, qi, True)

    for pair in range(nh // 2):
        outs = []
        for e in (2 * pair, 2 * pair + 1):
            acc = acc_ref[e]
            outs.append(acc[:HEAD_DIM] / acc[HEAD_DIM:HEAD_DIM + 1])
        o_ref[:, pair * LANES:(pair + 1) * LANES] = jnp.concatenate(outs, axis=0).T.astype(o_ref.dtype)


def _fox_attn(q_aug, k_aug, v_aug, *, tq, nh=4):
    b, s, wide = q_aug.shape
    nheads = wide // LANES
    return pl.pallas_call(
        _fox_attn_kernel,
        out_shape=jax.ShapeDtypeStruct((b, s, nheads * HEAD_DIM), BF16),
        grid=(b, nheads // nh, s // tq),
        in_specs=[
            pl.BlockSpec((None, s, nh * LANES), lambda bi, g, qi: (bi, 0, g)),
            pl.BlockSpec((None, s, nh * LANES), lambda bi, g, qi: (bi, 0, g)),
            pl.BlockSpec((None, nh, s // tq, LANES, tq), lambda bi, g, qi: (bi, g, 0, 0, 0)),
        ],
        out_specs=pl.BlockSpec((None, tq, nh * HEAD_DIM), lambda bi, g, qi: (bi, qi, g)),
        scratch_shapes=[pltpu.VMEM((nh, 1, tq), F32), pltpu.VMEM((nh, LANES, tq), F32),
                        pltpu.VMEM((nh, tq, tq), F32), pltpu.VMEM((nh, tq, tq), F32)],
        compiler_params=_params("parallel", "parallel", "arbitrary"),
        name="fox_attn",
    )(q_aug, k_aug, v_aug)


def kernel(x, p, ffn1_norm, ffn1_w_gu, ffn1_w_down, mix_norm, ffn2_norm, ffn2_w_gu, ffn2_w_down, ple_norm, ple_w_gate, ple_w_proj, even_w_in, even_w_out, swa_sinks, rwkv_mu, rwkv_w0, rwkv_w2, rwkv_a0, rwkv_a2, rwkv_g2, rwkv_k_k, rwkv_k_a, rwkv_r_k, rwkv_ln_w, rwkv_ln_b, fox_w_in, fox_b_f, fox_w_out, final_norm):
    b, s, d = x.shape
    depth = p.shape[0]
    t = b * s
    bf = lambda w: w.astype(BF16)
    swa_q = SWA_HEADS * HEAD_DIM
    swa_cols = swa_q + 2 * (SWA_HEADS // SWA_GROUP) * HEAD_DIM
    rwkv_dim = rwkv_w0.shape[-1]
    lora = rwkv_w2.shape[1]
    fox_heads = fox_b_f.shape[-1]
    fox_dim = fox_heads * HEAD_DIM

    x = x.reshape(t, d)
    for i in range(depth):
        j = i // 2
        if i % 2 == 0:
            widths = (swa_cols, even_w_in.shape[-1] - swa_cols)
            x, qkv, hb = _ffn(x, ffn1_norm[i], ffn1_w_gu, ffn1_w_down, i, proj=(mix_norm[i], even_w_in, j, widths))
            ya = _swa(qkv.reshape(b, s, swa_cols), swa_sinks[j])
            zeros = jnp.zeros((lora, rwkv_dim), F32)
            w2p = jnp.concatenate([rwkv_w2[j], zeros], axis=0)
            a2p = jnp.concatenate([zeros, rwkv_a2[j]], axis=0)
            yb = _rwkv(hb.reshape(b, s, -1), rwkv_mu[j], rwkv_w0[j], w2p, rwkv_a0[j], a2p,
                       bf(rwkv_g2[j]), rwkv_k_k[j], rwkv_k_a[j], rwkv_r_k[j].reshape(-1),
                       rwkv_ln_w[j], rwkv_ln_b[j])
            mixed = ([ya.reshape(t, swa_q), yb.reshape(t, rwkv_dim)], even_w_out, j)
        else:
            (x,) = _ffn(x, ffn1_norm[i], ffn1_w_gu, ffn1_w_down, i)
            q_aug, k_aug, v_aug = _fox_in(x, mix_norm[i], fox_w_in[j], fox_b_f[j], batch=b, tc=FOX_BLOCK)
            yc = _fox_attn(q_aug, k_aug, v_aug, tq=FOX_BLOCK)
            mixed = ([yc.reshape(t, fox_dim)], fox_w_out, j)
        x = _post_mix(*mixed, x, ffn2_norm[i], ffn2_w_gu, ffn2_w_down, ple_norm[i], ple_w_gate,
                      p.reshape(depth, t, -1), ple_w_proj, final_norm, i, final=(i == depth - 1))
    return x.reshape(b, s, d)
```

```python
import functools

import jax
import jax.numpy as jnp
import numpy as np
from jax import lax
from jax.experimental import pallas as pl
from jax.experimental.pallas import tpu as pltpu

F32 = jnp.float32
BF16 = jnp.bfloat16

LANES = 128
HEAD_DIM = 64
SWA_HEADS = 8
SWA_GROUP = 4
SWA_BLOCK = 128
RWKV_CHUNK = 64
NORM_EPS = 1e-6
GN_EPS = 64e-5
L2_EPS = 1e-12
LOG2E = float(np.log2(np.e))
VMEM_LIMIT = 56 * 1024 * 1024

NN = (((1,), (0,)), ((), ()))
NT = (((1,), (1,)), ((), ()))
TN = (((0,), (0,)), ((), ()))


def _dot(a, b, dims=NN):
    return lax.dot_general(a, b, dims, preferred_element_type=F32)


def _dot_bf(a, b, dims=NN):
    return _dot(a.astype(BF16), b.astype(BF16), dims)


def _hi_lo(x):
    hi = x.astype(BF16)
    lo = (x - hi.astype(F32)).astype(BF16)
    return hi, lo


def _dot_x3(a, b, dims=NN):
    ah, al = _hi_lo(a)
    bh, bl = _hi_lo(b)
    return _dot(ah, bh, dims) + (_dot(ah, bl, dims) + _dot(al, bh, dims))


def _split3(x):
    hi = x.astype(BF16)
    r1 = x - hi.astype(F32)
    mid = r1.astype(BF16)
    lo = (r1 - mid.astype(F32)).astype(BF16)
    return hi, mid, lo


def _sel_dot(sel, x):
    hi, mid, lo = _split3(x)
    return _dot(sel, hi) + (_dot(sel, mid) + _dot(sel, lo))


def _dot_sel(x, sel):
    hi, mid, lo = _split3(x)
    return _dot(hi, sel) + (_dot(mid, sel) + _dot(lo, sel))


def _rms(x, g):
    ms = jnp.mean(x * x, axis=-1, keepdims=True)
    return x * lax.rsqrt(ms + NORM_EPS) * g


def _params(*sem):
    return pltpu.CompilerParams(dimension_semantics=sem, vmem_limit_bytes=VMEM_LIMIT)


LOAD_STEPS = 16


def _weight_chunk(w, layer, rows, block=0):
    chunk = rows // LOAD_STEPS
    assert chunk * LOAD_STEPS == rows and chunk % 16 == 0, (rows, chunk)
    return pl.BlockSpec((None, chunk, w.shape[2]),
                        lambda i: (layer, block * LOAD_STEPS + jnp.minimum(i, LOAD_STEPS - 1), 0))


def _stash(step, chunk_ref, copy_ref):
    rows = chunk_ref.shape[0]
    copy_ref[pl.ds(pl.multiple_of(step * rows, rows), rows), :] = chunk_ref[...].astype(BF16)


def _row_tile(tm, n):
    return pl.BlockSpec((tm, n), lambda i: (jnp.maximum(i - LOAD_STEPS, 0), 0))


def _swiglu_half_step(x, g, wgu_ref, wd_ref):
    dff = wd_ref.shape[0]
    hn = _rms(x, g).astype(BF16)
    gate = _dot(hn, wgu_ref[:, :dff])
    up = _dot(hn, wgu_ref[:, dff:])
    act = (gate * jax.nn.sigmoid(gate) * up).astype(BF16)
    return x + 0.5 * _dot(act, wd_ref[...])


def _ffn_kernel(n_proj, x_ref, g_ref, wgu_ref, wd_ref, *refs):
    n_w = 3 if n_proj else 2
    copies = refs[len(refs) - n_w:]
    refs = refs[:len(refs) - n_w]
    step = pl.program_id(0)

    @pl.when(step < LOAD_STEPS)
    def _():
        _stash(step, wgu_ref, copies[0])
        _stash(step, wd_ref, copies[1])
        if n_proj:
            _stash(step, refs[1], copies[2])

    @pl.when(step >= LOAD_STEPS)
    def _():
        x = _swiglu_half_step(x_ref[...], g_ref[...], copies[0], copies[1])
        if n_proj:
            gm_ref, o_ref, proj_refs = refs[0], refs[2], refs[3:]
            proj = _dot(_rms(x, gm_ref[...]).astype(BF16), copies[2][...])
            col = 0
            for p_ref in proj_refs:
                p_ref[...] = proj[:, col:col + p_ref.shape[1]]
                col += p_ref.shape[1]
        else:
            o_ref = refs[0]
        o_ref[...] = x


def _ffn(x, g, w_gu, w_down, layer, proj=None, *, tm=512):
    t, d = x.shape
    dff = w_down.shape[1]
    vec = pl.BlockSpec((1, d), lambda i: (0, 0))
    in_specs = [_row_tile(tm, d), vec, _weight_chunk(w_gu, layer, d), _weight_chunk(w_down, layer, dff)]
    args = [x, g.reshape(1, d), w_gu, w_down]
    scratch = [pltpu.VMEM((d, 2 * dff), BF16), pltpu.VMEM((dff, d), BF16)]
    out_shape = [jax.ShapeDtypeStruct((t, d), F32)]
    out_specs = [_row_tile(tm, d)]
    widths = ()
    if proj is not None:
        gm, w, w_layer, widths = proj
        assert sum(widths) == w.shape[2]
        in_specs += [vec, _weight_chunk(w, w_layer, d)]
        args += [gm.reshape(1, d), w]
        scratch += [pltpu.VMEM(w.shape[1:], BF16)]
        out_shape += [jax.ShapeDtypeStruct((t, n), F32) for n in widths]
        out_specs += [_row_tile(tm, n) for n in widths]
    return pl.pallas_call(
        functools.partial(_ffn_kernel, len(widths)),
        out_shape=out_shape,
        grid=(LOAD_STEPS + t // tm,),
        in_specs=in_specs,
        out_specs=out_specs,
        scratch_shapes=scratch,
        compiler_params=_params("arbitrary"),
        name="ffn",
    )(*args)


def _post_mix_kernel(n_in, final, *refs):
    a_refs, wo_refs = refs[:n_in], refs[n_in:2 * n_in]
    (x_ref, g2_ref, wgu_ref, wd_ref, gp_ref, wpg_ref, p_ref, wpp_ref, fn_ref, o_ref) = refs[2 * n_in:2 * n_in + 10]
    copies = refs[2 * n_in + 10:]
    wo_copies, (wgu_c, wd_c, wpg_c, wpp_c) = copies[:n_in], copies[n_in:]
    step = pl.program_id(0)

    @pl.when(step < LOAD_STEPS)
    def _():
        for src, dst in zip(wo_refs + (wgu_ref, wd_ref, wpg_ref, wpp_ref), wo_copies + (wgu_c, wd_c, wpg_c, wpp_c)):
            _stash(step, src, dst)

    @pl.when(step >= LOAD_STEPS)
    def _():
        x = x_ref[...]
        for a_ref, w_c in zip(a_refs, wo_copies):
            x = x + _dot(a_ref[...], w_c[...])
        x = _swiglu_half_step(x, g2_ref[...], wgu_c, wd_c)
        gate = jax.nn.sigmoid(_dot(_rms(x, gp_ref[...]).astype(BF16), wpg_c[...]))
        x = x + gate * _dot(p_ref[...].astype(BF16), wpp_c[...])
        if final:
            x = _rms(x, fn_ref[...])
        o_ref[...] = x


def _post_mix(a_list, w_out, out_layer, x, g2, w_gu, w_down, gp, w_gate, p, w_proj, final_g, layer, *, final,
              tm=512):
    t, d = x.shape
    dff = w_down.shape[1]
    vec = pl.BlockSpec((1, d), lambda i: (0, 0))
    width = a_list[0].shape[1]
    assert all(a.shape[1] == width for a in a_list)
    in_specs = [_row_tile(tm, width) for _ in a_list]
    in_specs += [_weight_chunk(w_out, out_layer, width, block=k) for k in range(len(a_list))]
    in_specs += [_row_tile(tm, d), vec, _weight_chunk(w_gu, layer, d), _weight_chunk(w_down, layer, dff), vec,
                 _weight_chunk(w_gate, layer, d),
                 pl.BlockSpec((None, tm, p.shape[2]), lambda i: (layer, jnp.maximum(i - LOAD_STEPS, 0), 0)),
                 _weight_chunk(w_proj, layer, w_proj.shape[1]), vec]
    scratch = [pltpu.VMEM((width, d), BF16) for _ in a_list]
    scratch += [pltpu.VMEM((d, 2 * dff), BF16), pltpu.VMEM((dff, d), BF16), pltpu.VMEM((d, d), BF16),
                pltpu.VMEM(w_proj.shape[1:], BF16)]
    return pl.pallas_call(
        functools.partial(_post_mix_kernel, len(a_list), final),
        out_shape=jax.ShapeDtypeStruct((t, d), F32),
        grid=(LOAD_STEPS + t // tm,),
        in_specs=in_specs,
        out_specs=_row_tile(tm, d),
        scratch_shapes=scratch,
        compiler_params=_params("arbitrary"),
        name="post_mix",
    )(*a_list, *([w_out] * len(a_list)), x, g2.reshape(1, d), w_gu, w_down, gp.reshape(1, d), w_gate, p,
      w_proj, final_g.reshape(1, d))


def _swa_kernel(sink_ref, q_ref, kp_ref, kc_ref, vp_ref, vc_ref, o_ref):
    n = pl.program_id(1)
    blk = SWA_BLOCK
    nsub = q_ref.shape[0] // blk
    scale = HEAD_DIM ** -0.5
    k = jnp.concatenate([kp_ref[...], kc_ref[...]], axis=0)
    v = jnp.concatenate([vp_ref[...], vc_ref[...]], axis=0)
    kr = pltpu.roll(k, HEAD_DIM, 1)
    vr = pltpu.roll(v, HEAD_DIM, 1)
    lo_kv = lax.broadcasted_iota(jnp.int32, k.shape, 1) < HEAD_DIM
    kdup = [jnp.where(lo_kv, k, kr).astype(BF16), jnp.where(lo_kv, kr, k).astype(BF16)]
    vdup = [jnp.where(lo_kv, v, vr).astype(BF16), jnp.where(lo_kv, vr, v).astype(BF16)]

    qi = lax.broadcasted_iota(jnp.int32, (blk, 2 * blk), 0)
    ki = lax.broadcasted_iota(jnp.int32, (blk, 2 * blk), 1)
    dist = qi + blk - ki
    in_window = (dist >= 0) & (dist < blk)
    distf = dist.astype(F32)
    lo_q = lax.broadcasted_iota(jnp.int32, (blk, LANES), 1) < HEAD_DIM

    units = [(u, h) for u in range(nsub) for h in range(SWA_HEADS)]
    keys_of = lambda x, u: x[u * blk:(u + 2) * blk]
    qm = []
    for u, h in units:
        j, e = divmod(h, 2)
        q2 = q_ref[u * blk:(u + 1) * blk, j * LANES:(j + 1) * LANES] * (scale * LOG2E)
        qm.append(jnp.where(lo_q if e == 0 else ~lo_q, q2, 0.0).astype(BF16))
    logits = [_dot(qi_, keys_of(kdup[h // SWA_GROUP], u), NT) for qi_, (u, h) in zip(qm, units)]
    ps, inv_denoms = [], []
    for s, (u, h) in zip(logits, units):
        valid = in_window & ((n > 0) | (ki >= blk)) if u == 0 else in_window
        slope = 2.0 ** (-8.0 * (h + 1) / SWA_HEADS) * LOG2E
        s = jnp.where(valid, s - slope * distf, -jnp.inf)
        sink = sink_ref[h] * LOG2E
        m = jnp.maximum(jnp.max(s, axis=-1, keepdims=True), sink)
        p = jnp.exp2(s - m)
        inv_denoms.append(1.0 / (jnp.sum(p, axis=-1, keepdims=True) + jnp.exp2(sink - m)))
        ps.append(p.astype(BF16))
    outs = [_dot(p, keys_of(vdup[h // SWA_GROUP], u)) * inv
            for p, inv, (u, h) in zip(ps, inv_denoms, units)]
    for u in range(nsub):
        for j in range(SWA_HEADS // 2):
            pair = jnp.where(lo_q, outs[u * SWA_HEADS + 2 * j], outs[u * SWA_HEADS + 2 * j + 1])
            o_ref[u * blk:(u + 1) * blk, j * LANES:(j + 1) * LANES] = pair.astype(o_ref.dtype)


def _swa(qkv, sinks, *, nsub=2):
    b, s, _ = qkv.shape
    blk = SWA_BLOCK
    tq = nsub * blk
    nq = SWA_HEADS * HEAD_DIM
    kcol = nq // LANES
    vcol = kcol + 1
    prev = lambda n: jnp.maximum(nsub * n - 1, 0)
    return pl.pallas_call(
        _swa_kernel,
        out_shape=jax.ShapeDtypeStruct((b, s, nq), BF16),
        grid=(b, s // tq),
        in_specs=[
            pl.BlockSpec(memory_space=pltpu.SMEM),
            pl.BlockSpec((None, tq, nq), lambda bi, n: (bi, n, 0)),
            pl.BlockSpec((None, blk, LANES), lambda bi, n: (bi, prev(n), kcol)),
            pl.BlockSpec((None, tq, LANES), lambda bi, n: (bi, n, kcol)),
            pl.BlockSpec((None, blk, LANES), lambda bi, n: (bi, prev(n), vcol)),
            pl.BlockSpec((None, tq, LANES), lambda bi, n: (bi, n, vcol)),
        ],
        out_specs=pl.BlockSpec((None, tq, nq), lambda bi, n: (bi, n, 0)),
        compiler_params=_params("parallel", "arbitrary"),
        name="swa",
    )(sinks, qkv, qkv, qkv, qkv, qkv)


def _tri_inverse_minus_eye(lows, ri, ci):
    same = lambda w: (ri ^ ci) < w
    base = 8
    x = [jnp.where(same(base), -low, 0.0) for low in lows]
    p2 = [_dot_bf(xi, xi) for xi in x]
    e = [xi + pi + _dot_bf(xi, pi) for xi, pi in zip(x, p2)]
    p4 = [_dot_bf(pi, pi) for pi in p2]
    e = [ei + pi + _dot_bf(ei, pi) for ei, pi in zip(e, p4)]
    w = base * 2
    while w <= RWKV_CHUNK:
        off = [jnp.where(same(w) & ~same(w // 2), low, 0.0) for low in lows]
        wm = [oi + _dot_bf(ei, oi) for ei, oi in zip(e, off)]
        e = [ei - wi - _dot_bf(wi, ei) for ei, wi in zip(e, wm)]
        w *= 2
    return e


def _rwkv_kernel(h_ref, mu_ref, w0_ref, w2_ref, a0_ref, a2_ref, g2_ref, kk_ref, ka_ref,
                 rk_ref, lnw_ref, lnb_ref, o_ref, state_ref, last_ref):
    c = RWKV_CHUNK
    tb = h_ref.shape[0]
    nchunk = tb // c
    dim = o_ref.shape[-1]
    npair = dim // LANES

    @pl.when(pl.program_id(1) == 0)
    def _():
        state_ref[...] = jnp.zeros_like(state_ref)
        last_ref[...] = jnp.zeros_like(last_ref)

    h = h_ref[...]
    row = lax.broadcasted_iota(jnp.int32, h.shape, 0)
    shifted = jnp.where(row == 0, last_ref[...], pltpu.roll(h, 1, 0))
    last_ref[...] = h[tb - 1:tb, :]
    hs = h + (shifted - h) * mu_ref[...]
    r = hs[:, 0:dim]
    k = hs[:, dim:2 * dim]
    v = hs[:, 2 * dim:3 * dim]
    xwa = hs[:, 3 * dim:3 * dim + LANES]
    xg = hs[:, 3 * dim + LANES:3 * dim + 2 * LANES]

    wl = w0_ref[...] + _dot_x3(jnp.tanh(xwa), w2_ref[...])
    logw = -jax.nn.sigmoid(wl) * float(np.exp(-0.5))
    a = jax.nn.sigmoid(a0_ref[...] + _dot_x3(xwa, a2_ref[...]))
    gate = _dot_bf(jax.nn.sigmoid(xg), g2_ref[...])

    ri = lax.broadcasted_iota(jnp.int32, (LANES, LANES), 0)
    ci = lax.broadcasted_iota(jnp.int32, (LANES, LANES), 1)
    ones_bd = ((ri ^ ci) < HEAD_DIM).astype(BF16)
    ones_bd2 = jnp.concatenate([ones_bd, ones_bd], axis=0)

    def head_sum(x):
        cols = []
        for j in range(npair):
            hi, lo = _hi_lo(x[:, j * LANES:(j + 1) * LANES])
            cols.append(_dot(jnp.concatenate([hi, lo], axis=1), ones_bd2))
        return jnp.concatenate(cols, axis=1)

    kk = k * kk_ref[...]
    kk = kk * jnp.minimum(lax.rsqrt(head_sum(kk * kk)), 1.0 / L2_EPS)
    k2 = k * (1.0 + (a - 1.0) * ka_ref[...])
    bvec = kk * a

    ti = lax.broadcasted_iota(jnp.int32, (tb, tb), 0)
    si = lax.broadcasted_iota(jnp.int32, (tb, tb), 1)
    tri = (((ti ^ si) < c) & (si <= ti)).astype(BF16)
    cum = _dot(jnp.concatenate([tri] * 3, axis=1), jnp.concatenate(_split3(logw), axis=0))
    e_neg = jnp.exp(-cum)
    alpha = kk * jnp.exp(cum - logw)
    beta = bvec * e_neg
    kappa = k2 * e_neg
    rho = r * jnp.exp(cum)
    cum_end = jnp.concatenate(
        [jnp.broadcast_to(cum[(n + 1) * c - 1:(n + 1) * c, :], (c, dim)) for n in range(nchunk)], axis=0)
    to_end = jnp.exp(cum_end - cum)
    beta_e = bvec * to_end
    kappa_e = k2 * to_end
    w_end = jnp.exp(cum_end)

    same_head = (ri ^ ci) < c
    strict = same_head & (ci < ri)
    incl = same_head & (ci <= ri)
    eye = ri == ci
    lo = lax.broadcasted_iota(jnp.int32, (c, LANES), 1) < HEAD_DIM

    def stack(x2):
        return jnp.concatenate([jnp.where(lo, x2, 0.0), jnp.where(lo, 0.0, x2)], axis=0)

    units = [(n, j) for n in range(nchunk) for j in range(npair)]
    blk = lambda x, n, j: x[n * c:(n + 1) * c, j * LANES:(j + 1) * LANES]
    a_s = [stack(blk(alpha, n, j)) for n, j in units]
    rho_s = [stack(blk(rho, n, j)) for n, j in units]
    v_s = [stack(blk(v, n, j)) for n, j in units]
    ends = [jnp.concatenate([stack(blk(beta_e, n, j)), stack(blk(kappa_e, n, j))], axis=0) for n, j in units]
    bk = [jnp.concatenate([blk(beta, n, j)] * 2 + [blk(kappa, n, j)] * 2, axis=0) for n, j in units]
    sc = [_dot_bf(jnp.concatenate([ai, ri_], axis=0), bi, NT) for ai, ri_, bi in zip(a_s, rho_s, bk)]
    l_ab = [jnp.where(strict, s[:LANES, :LANES], 0.0) for s in sc]
    l_ak = [jnp.where(strict, s[:LANES, LANES:], 0.0) for s in sc]
    r_b = [jnp.where(incl, s[LANES:, :LANES], 0.0) for s in sc]
    r_k = [jnp.where(incl, s[LANES:, LANES:], 0.0) for s in sc]
    e_inv = _tri_inverse_minus_eye(l_ab, ri, ci)
    lkv = [_dot_bf(li, vi) for li, vi in zip(l_ak, v_s)]
    p_m = [-(ai + _dot_bf(ei, ai)) for ei, ai in zip(e_inv, a_s)]
    q_m = [-(xi + _dot_bf(ei, xi)) for ei, xi in zip(e_inv, lkv)]
    m_m = [jnp.where(eye, blk(w_end, n, j)[:1, :], 0.0) + _dot_bf(pi, ei[:LANES], TN)
           for (n, j), pi, ei in zip(units, p_m, ends)]
    n_m = [_dot_bf(jnp.concatenate([qi, vi], axis=0), ei, TN) for qi, vi, ei in zip(q_m, v_s, ends)]
    g_m = [ri_ + _dot_bf(rb, pi) for ri_, rb, pi in zip(rho_s, r_b, p_m)]
    h_m = [_dot_bf(jnp.concatenate([rb, rk], axis=1), jnp.concatenate([qi, vi], axis=0))
           for rb, rk, qi, vi in zip(r_b, r_k, q_m, v_s)]

    state = [state_ref[j] for j in range(npair)]
    ys = []
    for n in range(nchunk):
        idx = [n * npair + j for j in range(npair)]
        nxt = [_dot_bf(state[j], m_m[i]) + n_m[i] for j, i in enumerate(idx)]
        y = [_dot_bf(g_m[i], state[j], NT) + h_m[i] for j, i in enumerate(idx)]
        ys.append(jnp.concatenate([yi[:c] + yi[c:] for yi in y], axis=1))
        state = nxt
    for j in range(npair):
        state_ref[j] = state[j]
    y = jnp.concatenate(ys, axis=0)

    mean = head_sum(y) * (1.0 / HEAD_DIM)
    d = y - mean
    var = head_sum(d * d) * (1.0 / HEAD_DIM)
    y = d * lax.rsqrt(var + GN_EPS) * lnw_ref[...] + lnb_ref[...]
    y = y + head_sum(r * k2 * rk_ref[...]) * v
    o_ref[...] = (y * gate).astype(o_ref.dtype)


def _rwkv(hb, mu, w0, w2p, a0, a2p, g2, k_k, k_a, r_k, ln_w, ln_b, *, tb=4 * RWKV_CHUNK):
    b, s, cols = hb.shape
    dim = w0.shape[-1]
    c = tb
    row = lambda x: x.reshape(1, -1)
    vec = lambda n: pl.BlockSpec((1, n), lambda bi, t: (0, 0))
    mat = lambda m: pl.BlockSpec(m.shape, lambda bi, t: (0, 0))
    return pl.pallas_call(
        _rwkv_kernel,
        out_shape=jax.ShapeDtypeStruct((b, s, dim), BF16),
        grid=(b, s // c),
        in_specs=[
            pl.BlockSpec((None, c, cols), lambda bi, t: (bi, t, 0)),
            vec(cols), vec(dim), mat(w2p), vec(dim), mat(a2p), mat(g2),
            vec(dim), vec(dim), vec(dim), vec(dim), vec(dim),
        ],
        out_specs=pl.BlockSpec((None, c, dim), lambda bi, t: (bi, t, 0)),
        scratch_shapes=[pltpu.VMEM((dim // LANES, LANES, LANES), F32), pltpu.VMEM((1, cols), F32)],
        compiler_params=_params("parallel", "arbitrary"),
        name="rwkv7",
    )(hb, row(mu), row(w0), w2p, row(a0), a2p, g2, row(k_k), row(k_a), row(r_k), row(ln_w), row(ln_b))


FOX_AUG = 3
FOX_BLOCK = 512


def _fox_in_kernel(x_ref, g_ref, w_ref, bf_ref, sel_ref, qa_ref, ka_ref, va_ref, carry_ref):
    tc = x_ref.shape[0]
    nheads = qa_ref.shape[-1] // LANES
    dim = nheads * HEAD_DIM
    scale = HEAD_DIM ** -0.5 * LOG2E

    @pl.when(pl.program_id(1) == 0)
    def _():
        carry_ref[...] = jnp.zeros_like(carry_ref)

    hn = _rms(x_ref[...], g_ref[...]).astype(BF16)
    z = _dot(hn, w_ref[:, 3 * dim:]) + bf_ref[...]
    proj_q = _dot(hn, w_ref[:, :dim])
    logf = jnp.minimum(z, 0.0) - jnp.log(1.0 + jnp.exp(-jnp.abs(z)))
    lane = lax.broadcasted_iota(jnp.int32, z.shape, 1)
    logf = jnp.where(lane < nheads, logf, 0.0)
    ti = lax.broadcasted_iota(jnp.int32, (tc, tc), 0)
    si = lax.broadcasted_iota(jnp.int32, (tc, tc), 1)
    cg = _sel_dot((si <= ti).astype(BF16), logf) + carry_ref[...]
    carry_ref[...] = cg[tc - 1:tc, :]

    hi, mid, low = _split3(cg * LOG2E)
    pieces = (hi.astype(F32) + pltpu.roll(mid.astype(F32), nheads, 1)
              + pltpu.roll(low.astype(F32), 2 * nheads, 1)
              + (lane == FOX_AUG * nheads).astype(F32)).astype(BF16)
    q_c = _dot(pieces, sel_ref[0])
    k_c = _dot(pieces, sel_ref[1])

    lane = lax.broadcasted_iota(jnp.int32, (tc, LANES), 1)
    lo = lane < HEAD_DIM
    v_one = (lane == HEAD_DIM).astype(F32)

    def assemble(proj, c_aug, o_ref):
        for j in range(nheads // 2):
            x = proj[:, j * LANES:(j + 1) * LANES]
            xr = pltpu.roll(x, HEAD_DIM, 1)
            for e, xe in ((0, x), (1, xr)):
                hs = slice((2 * j + e) * LANES, (2 * j + e + 1) * LANES)
                if c_aug is None:
                    o_ref[2 * j + e, 0] = jnp.where(lo, xe, v_one).T.astype(BF16)
                else:
                    o_ref[:, hs] = jnp.where(lo, xe, c_aug[:, hs]).astype(BF16)

    proj_k = _dot(hn, w_ref[:, dim:2 * dim])
    assemble(proj_q * scale, q_c, qa_ref)
    proj_v = _dot(hn, w_ref[:, 2 * dim:3 * dim])
    assemble(proj_k, k_c, ka_ref)
    assemble(proj_v, None, va_ref)


def _fox_select_matrices(nheads):
    sel = np.zeros((2, LANES, nheads * LANES), np.float32)
    one_row = FOX_AUG * nheads
    for h in range(nheads):
        for i in range(FOX_AUG):
            sel[0, i * nheads + h, h * LANES + HEAD_DIM + i] = 1.0
            sel[0, one_row, h * LANES + HEAD_DIM + FOX_AUG + i] = 1.0
            sel[1, one_row, h * LANES + HEAD_DIM + i] = 1.0
            sel[1, i * nheads + h, h * LANES + HEAD_DIM + FOX_AUG + i] = -1.0
    return jnp.asarray(sel, BF16)


def _fox_in(x, g, w_in, b_f, *, batch, tc):
    t, d = x.shape
    s = t // batch
    nheads = b_f.shape[-1]
    assert FOX_AUG * nheads < LANES
    w = jnp.pad(w_in, ((0, 0), (0, LANES - nheads))).astype(BF16)
    bf = jnp.zeros((1, LANES), F32).at[0, :nheads].set(b_f)
    sel = _fox_select_matrices(nheads)
    wide = nheads * LANES
    nt = s // tc
    resident = pl.Buffered(1)
    return pl.pallas_call(
        _fox_in_kernel,
        out_shape=(jax.ShapeDtypeStruct((batch, s, wide), BF16),
                   jax.ShapeDtypeStruct((batch, s, wide), BF16),
                   jax.ShapeDtypeStruct((batch, nheads, nt, LANES, tc), BF16)),
        grid=(batch, nt),
        in_specs=[
            pl.BlockSpec((tc, d), lambda bi, ti: (bi * nt + ti, 0)),
            pl.BlockSpec((1, d), lambda bi, ti: (0, 0)),
            pl.BlockSpec(w.shape, lambda bi, ti: (0, 0), pipeline_mode=resident),
            pl.BlockSpec((1, LANES), lambda bi, ti: (0, 0)),
            pl.BlockSpec(sel.shape, lambda bi, ti: (0, 0, 0), pipeline_mode=resident),
        ],
        out_specs=(pl.BlockSpec((None, tc, wide), lambda bi, ti: (bi, ti, 0)),
                   pl.BlockSpec((None, tc, wide), lambda bi, ti: (bi, ti, 0)),
                   pl.BlockSpec((None, nheads, 1, LANES, tc), lambda bi, ti: (bi, 0, ti, 0, 0))),
        scratch_shapes=[pltpu.VMEM((1, LANES), F32)],
        compiler_params=_params("parallel", "arbitrary"),
        name="fox_in",
    )(x, g.reshape(1, d), w, bf, sel)


def _fox_attn_kernel(q_ref, k_ref, v_ref, o_ref, m_ref, acc_ref, sa_ref, sb_ref):
    tq = o_ref.shape[0]
    nh = q_ref.shape[1] // LANES
    qi = pl.program_id(2)
    m_ref[...] = jnp.full_like(m_ref, -jnp.inf)
    acc_ref[...] = jnp.zeros_like(acc_ref)
    keys = lax.broadcasted_iota(jnp.int32, (tq, tq), 0)
    queries = lax.broadcasted_iota(jnp.int32, (tq, tq), 1)

    def logits_into(dst_ref, qb, kb, heads=None):
        q0 = pl.multiple_of(qb * tq, tq)
        k0 = pl.multiple_of(kb * tq, tq)
        for e in range(nh) if heads is None else heads:
            hs = slice(e * LANES, (e + 1) * LANES)
            dst_ref[e] = _dot(k_ref[pl.ds(k0, tq), hs], q_ref[pl.ds(q0, tq), hs], NT)

    def consume(src_ref, kb, diagonal, after_head=None):
        for e in range(nh):
            s = src_ref[e]
            if diagonal:
                s = jnp.where(keys <= queries, s, -jnp.inf)
            m_prev = m_ref[e]
            m_new = jnp.maximum(m_prev, jnp.max(s, axis=0, keepdims=True))
            p = jnp.exp2(s - m_new).astype(BF16)
            acc_ref[e] = jnp.exp2(m_prev - m_new) * acc_ref[e] + _dot(v_ref[e, kb], p)
            m_ref[e] = m_new
            if after_head is not None:
                after_head(e)

    @pl.when(qi == 0)
    def _():
        logits_into(sa_ref, 0, 0)

    def two_blocks(t, carry):
        kb = 2 * t
        logits_into(sb_ref, qi, kb + 1)
        consume(sa_ref, kb, False)
        logits_into(sa_ref, qi, kb + 2)
        consume(sb_ref, kb + 1, False)
        return carry

    lax.fori_loop(0, qi // 2, two_blocks, 0)
    nxt = jnp.minimum(qi + 1, pl.num_programs(2) - 1)

    @pl.when(qi % 2 == 0)
    def _():
        consume(sa_ref, qi, True, after_head=lambda e: logits_into(sa_ref, nxt, 0, heads=(e,)))

    @pl.when(qi % 2 == 1)
    def _():
        logits_into(sb_ref, qi, qi)
        consume(sa_ref, qi - 1, False)
        logits_into(sa_ref, nxt, 0)
        consume(sb_ref, qi, True)

    for pair in range(nh // 2):
        outs = []
        for e in (2 * pair, 2 * pair + 1):
            acc = acc_ref[e]
            outs.append(acc[:HEAD_DIM] / acc[HEAD_DIM:HEAD_DIM + 1])
        o_ref[:, pair * LANES:(pair + 1) * LANES] = jnp.concatenate(outs, axis=0).T.astype(o_ref.dtype)


def _fox_attn(q_aug, k_aug, v_aug, *, tq, nh=4):
    b, s, wide = q_aug.shape
    nheads = wide // LANES
    return pl.pallas_call(
        _fox_attn_kernel,
        out_shape=jax.ShapeDtypeStruct((b, s, nheads * HEAD_DIM), BF16),
        grid=(b, nheads // nh, s // tq),
        in_specs=[
            pl.BlockSpec((None, s, nh * LANES), lambda bi, g, qi: (bi, 0, g)),
            pl.BlockSpec((None, s, nh * LANES), lambda bi, g, qi: (bi, 0, g)),
            pl.BlockSpec((None, nh, s // tq, LANES, tq), lambda bi, g, qi: (bi, g, 0, 0, 0)),
        ],
        out_specs=pl.BlockSpec((None, tq, nh * HEAD_DIM), lambda bi, g, qi: (bi, qi, g)),
        scratch_shapes=[pltpu.VMEM((nh, 1, tq), F32), pltpu.VMEM((nh, LANES, tq), F32),
                        pltpu.VMEM((nh, tq, tq), F32), pltpu.VMEM((nh, tq, tq), F32)],
        compiler_params=_params("parallel", "parallel", "arbitrary"),
        name="fox_attn",
    )(q_aug, k_aug, v_aug)


def kernel(x, p, ffn1_norm, ffn1_w_gu, ffn1_w_down, mix_norm, ffn2_norm, ffn2_w_gu, ffn2_w_down, ple_norm, ple_w_gate, ple_w_proj, even_w_in, even_w_out, swa_sinks, rwkv_mu, rwkv_w0, rwkv_w2, rwkv_a0, rwkv_a2, rwkv_g2, rwkv_k_k, rwkv_k_a, rwkv_r_k, rwkv_ln_w, rwkv_ln_b, fox_w_in, fox_b_f, fox_w_out, final_norm):
    b, s, d = x.shape
    depth = p.shape[0]
    t = b * s
    bf = lambda w: w.astype(BF16)
    swa_q = SWA_HEADS * HEAD_DIM
    swa_cols = swa_q + 2 * (SWA_HEADS // SWA_GROUP) * HEAD_DIM
    rwkv_dim = rwkv_w0.shape[-1]
    lora = rwkv_w2.shape[1]
    fox_heads = fox_b_f.shape[-1]
    fox_dim = fox_heads * HEAD_DIM

    x = x.reshape(t, d)
    for i in range(depth):
        j = i // 2
        if i % 2 == 0:
            widths = (swa_cols, even_w_in.shape[-1] - swa_cols)
            x, qkv, hb = _ffn(x, ffn1_norm[i], ffn1_w_gu, ffn1_w_down, i, proj=(mix_norm[i], even_w_in, j, widths))
            ya = _swa(qkv.reshape(b, s, swa_cols), swa_sinks[j])
            zeros = jnp.zeros((lora, rwkv_dim), F32)
            w2p = jnp.concatenate([rwkv_w2[j], zeros], axis=0)
            a2p = jnp.concatenate([zeros, rwkv_a2[j]], axis=0)
            yb = _rwkv(hb.reshape(b, s, -1), rwkv_mu[j], rwkv_w0[j], w2p, rwkv_a0[j], a2p,
                       bf(rwkv_g2[j]), rwkv_k_k[j], rwkv_k_a[j], rwkv_r_k[j].reshape(-1),
                       rwkv_ln_w[j], rwkv_ln_b[j])
            mixed = ([ya.reshape(t, swa_q), yb.reshape(t, rwkv_dim)], even_w_out, j)
        else:
            (x,) = _ffn(x, ffn1_norm[i], ffn1_w_gu, ffn1_w_down, i)
            q_aug, k_aug, v_aug = _fox_in(x, mix_norm[i], fox_w_in[j], fox_b_f[j], batch=b, tc=FOX_BLOCK)
            yc = _fox_attn(q_aug, k_aug, v_aug, tq=FOX_BLOCK)
            mixed = ([yc.reshape(t, fox_dim)], fox_w_out, j)
        x = _post_mix(*mixed, x, ffn2_norm[i], ffn2_w_gu, ffn2_w_down, ple_norm[i], ple_w_gate,
                      p.reshape(depth, t, -1), ple_w_proj, final_norm, i, final=(i == depth - 1))
    return x.reshape(b, s, d)
```

```python
import functools

import jax
import jax.numpy as jnp
import numpy as np
from jax import lax
from jax.experimental import pallas as pl
from jax.experimental.pallas import tpu as pltpu

F32 = jnp.float32
BF16 = jnp.bfloat16

LANES = 128
HEAD_DIM = 64
SWA_HEADS = 8
SWA_GROUP = 4
SWA_BLOCK = 128
RWKV_CHUNK = 64
NORM_EPS = 1e-6
GN_EPS = 64e-5
L2_EPS = 1e-12
LOG2E = float(np.log2(np.e))
VMEM_LIMIT = 56 * 1024 * 1024

NN = (((1,), (0,)), ((), ()))
NT = (((1,), (1,)), ((), ()))
TN = (((0,), (0,)), ((), ()))


def _dot(a, b, dims=NN):
    return lax.dot_general(a, b, dims, preferred_element_type=F32)


def _dot_bf(a, b, dims=NN):
    return _dot(a.astype(BF16), b.astype(BF16), dims)


def _hi_lo(x):
    hi = x.astype(BF16)
    lo = (x - hi.astype(F32)).astype(BF16)
    return hi, lo


def _dot_x3(a, b, dims=NN):
    ah, al = _hi_lo(a)
    bh, bl = _hi_lo(b)
    return _dot(ah, bh, dims) + (_dot(ah, bl, dims) + _dot(al, bh, dims))


def _split3(x):
    hi = x.astype(BF16)
    r1 = x - hi.astype(F32)
    mid = r1.astype(BF16)
    lo = (r1 - mid.astype(F32)).astype(BF16)
    return hi, mid, lo


def _sel_dot(sel, x):
    hi, mid, lo = _split3(x)
    return _dot(sel, hi) + (_dot(sel, mid) + _dot(sel, lo))


def _dot_sel(x, sel):
    hi, mid, lo = _split3(x)
    return _dot(hi, sel) + (_dot(mid, sel) + _dot(lo, sel))


def _rms(x, g):
    ms = jnp.mean(x * x, axis=-1, keepdims=True)
    return x * lax.rsqrt(ms + NORM_EPS) * g


def _params(*sem):
    return pltpu.CompilerParams(dimension_semantics=sem, vmem_limit_bytes=VMEM_LIMIT)


LOAD_STEPS = 16


def _weight_chunk(w, layer, rows, block=0):
    chunk = rows // LOAD_STEPS
    assert chunk * LOAD_STEPS == rows and chunk % 16 == 0, (rows, chunk)
    return pl.BlockSpec((None, chunk, w.shape[2]),
                        lambda i: (layer, block * LOAD_STEPS + jnp.minimum(i, LOAD_STEPS - 1), 0))


def _stash(step, chunk_ref, copy_ref):
    rows = chunk_ref.shape[0]
    copy_ref[pl.ds(pl.multiple_of(step * rows, rows), rows), :] = chunk_ref[...].astype(BF16)


def _row_tile(tm, n):
    return pl.BlockSpec((tm, n), lambda i: (jnp.maximum(i - LOAD_STEPS, 0), 0))


def _swiglu_half_step(x, g, wgu_ref, wd_ref):
    dff = wd_ref.shape[0]
    hn = _rms(x, g).astype(BF16)
    gate = _dot(hn, wgu_ref[:, :dff])
    up = _dot(hn, wgu_ref[:, dff:])
    act = (gate * jax.nn.sigmoid(gate) * up).astype(BF16)
    return x + 0.5 * _dot(act, wd_ref[...])


def _ffn_kernel(n_proj, x_ref, g_ref, wgu_ref, wd_ref, *refs):
    n_w = 3 if n_proj else 2
    copies = refs[len(refs) - n_w:]
    refs = refs[:len(refs) - n_w]
    step = pl.program_id(0)

    @pl.when(step < LOAD_STEPS)
    def _():
        _stash(step, wgu_ref, copies[0])
        _stash(step, wd_ref, copies[1])
        if n_proj:
            _stash(step, refs[1], copies[2])

    @pl.when(step >= LOAD_STEPS)
    def _():
        x = _swiglu_half_step(x_ref[...], g_ref[...], copies[0], copies[1])
        if n_proj:
            gm_ref, o_ref, proj_refs = refs[0], refs[2], refs[3:]
            proj = _dot(_rms(x, gm_ref[...]).astype(BF16), copies[2][...])
            col = 0
            for p_ref in proj_refs:
                p_ref[...] = proj[:, col:col + p_ref.shape[1]]
                col += p_ref.shape[1]
        else:
            o_ref = refs[0]
        o_ref[...] = x


def _ffn(x, g, w_gu, w_down, layer, proj=None, *, tm=512):
    t, d = x.shape
    dff = w_down.shape[1]
    vec = pl.BlockSpec((1, d), lambda i: (0, 0))
    in_specs = [_row_tile(tm, d), vec, _weight_chunk(w_gu, layer, d), _weight_chunk(w_down, layer, dff)]
    args = [x, g.reshape(1, d), w_gu, w_down]
    scratch = [pltpu.VMEM((d, 2 * dff), BF16), pltpu.VMEM((dff, d), BF16)]
    out_shape = [jax.ShapeDtypeStruct((t, d), F32)]
    out_specs = [_row_tile(tm, d)]
    widths = ()
    if proj is not None:
        gm, w, w_layer, widths = proj
        assert sum(widths) == w.shape[2]
        in_specs += [vec, _weight_chunk(w, w_layer, d)]
        args += [gm.reshape(1, d), w]
        scratch += [pltpu.VMEM(w.shape[1:], BF16)]
        out_shape += [jax.ShapeDtypeStruct((t, n), F32) for n in widths]
        out_specs += [_row_tile(tm, n) for n in widths]
    return pl.pallas_call(
        functools.partial(_ffn_kernel, len(widths)),
        out_shape=out_shape,
        grid=(LOAD_STEPS + t // tm,),
        in_specs=in_specs,
        out_specs=out_specs,
        scratch_shapes=scratch,
        compiler_params=_params("arbitrary"),
        name="ffn",
    )(*args)


def _post_mix_kernel(n_in, final, *refs):
    a_refs, wo_refs = refs[:n_in], refs[n_in:2 * n_in]
    (x_ref, g2_ref, wgu_ref, wd_ref, gp_ref, wpg_ref, p_ref, wpp_ref, fn_ref, o_ref) = refs[2 * n_in:2 * n_in + 10]
    copies = refs[2 * n_in + 10:]
    wo_copies, (wgu_c, wd_c, wpg_c, wpp_c) = copies[:n_in], copies[n_in:]
    step = pl.program_id(0)

    @pl.when(step < LOAD_STEPS)
    def _():
        for src, dst in zip(wo_refs + (wgu_ref, wd_ref, wpg_ref, wpp_ref), wo_copies + (wgu_c, wd_c, wpg_c, wpp_c)):
            _stash(step, src, dst)

    @pl.when(step >= LOAD_STEPS)
    def _():
        x = x_ref[...]
        for a_ref, w_c in zip(a_refs, wo_copies):
            x = x + _dot(a_ref[...], w_c[...])
        x = _swiglu_half_step(x, g2_ref[...], wgu_c, wd_c)
        gate = jax.nn.sigmoid(_dot(_rms(x, gp_ref[...]).astype(BF16), wpg_c[...]))
        x = x + gate * _dot(p_ref[...].astype(BF16), wpp_c[...])
        if final:
            x = _rms(x, fn_ref[...])
        o_ref[...] = x


def _post_mix(a_list, w_out, out_layer, x, g2, w_gu, w_down, gp, w_gate, p, w_proj, final_g, layer, *, final,
              tm=512):
    t, d = x.shape
    dff = w_down.shape[1]
    vec = pl.BlockSpec((1, d), lambda i: (0, 0))
    width = a_list[0].shape[1]
    assert all(a.shape[1] == width for a in a_list)
    in_specs = [_row_tile(tm, width) for _ in a_list]
    in_specs += [_weight_chunk(w_out, out_layer, width, block=k) for k in range(len(a_list))]
    in_specs += [_row_tile(tm, d), vec, _weight_chunk(w_gu, layer, d), _weight_chunk(w_down, layer, dff), vec,
                 _weight_chunk(w_gate, layer, d),
                 pl.BlockSpec((None, tm, p.shape[2]), lambda i: (layer, jnp.maximum(i - LOAD_STEPS, 0), 0)),
                 _weight_chunk(w_proj, layer, w_proj.shape[1]), vec]
    scratch = [pltpu.VMEM((width, d), BF16) for _ in a_list]
    scratch += [pltpu.VMEM((d, 2 * dff), BF16), pltpu.VMEM((dff, d), BF16), pltpu.VMEM((d, d), BF16),
                pltpu.VMEM(w_proj.shape[1:], BF16)]
    return pl.pallas_call(
        functools.partial(_post_mix_kernel, len(a_list), final),
        out_shape=jax.ShapeDtypeStruct((t, d), F32),
        grid=(LOAD_STEPS + t // tm,),
        in_specs=in_specs,
        out_specs=_row_tile(tm, d),
        scratch_shapes=scratch,
        compiler_params=_params("arbitrary"),
        name="post_mix",
    )(*a_list, *([w_out] * len(a_list)), x, g2.reshape(1, d), w_gu, w_down, gp.reshape(1, d), w_gate, p,
      w_proj, final_g.reshape(1, d))


def _swa_kernel(sink_ref, q_ref, kp_ref, kc_ref, vp_ref, vc_ref, o_ref):
    n = pl.program_id(1)
    blk = SWA_BLOCK
    nsub = q_ref.shape[0] // blk
    scale = HEAD_DIM ** -0.5
    k = jnp.concatenate([kp_ref[...], kc_ref[...]], axis=0)
    v = jnp.concatenate([vp_ref[...], vc_ref[...]], axis=0)
    kr = pltpu.roll(k, HEAD_DIM, 1)
    vr = pltpu.roll(v, HEAD_DIM, 1)
    lo_kv = lax.broadcasted_iota(jnp.int32, k.shape, 1) < HEAD_DIM
    kdup = [jnp.where(lo_kv, k, kr).astype(BF16), jnp.where(lo_kv, kr, k).astype(BF16)]
    vdup = [jnp.where(lo_kv, v, vr).astype(BF16), jnp.where(lo_kv, vr, v).astype(BF16)]

    qi = lax.broadcasted_iota(jnp.int32, (blk, 2 * blk), 0)
    ki = lax.broadcasted_iota(jnp.int32, (blk, 2 * blk), 1)
    dist = qi + blk - ki
    in_window = (dist >= 0) & (dist < blk)
    distf = dist.astype(F32)
    lo_q = lax.broadcasted_iota(jnp.int32, (blk, LANES), 1) < HEAD_DIM

    units = [(u, h) for u in range(nsub) for h in range(SWA_HEADS)]
    keys_of = lambda x, u: x[u * blk:(u + 2) * blk]
    qm = []
    for u, h in units:
        j, e = divmod(h, 2)
        q2 = q_ref[u * blk:(u + 1) * blk, j * LANES:(j + 1) * LANES] * (scale * LOG2E)
        qm.append(jnp.where(lo_q if e == 0 else ~lo_q, q2, 0.0).astype(BF16))
    logits = [_dot(qi_, keys_of(kdup[h // SWA_GROUP], u), NT) for qi_, (u, h) in zip(qm, units)]
    ps, inv_denoms = [], []
    for s, (u, h) in zip(logits, units):
        valid = in_window & ((n > 0) | (ki >= blk)) if u == 0 else in_window
        slope = 2.0 ** (-8.0 * (h + 1) / SWA_HEADS) * LOG2E
        s = jnp.where(valid, s - slope * distf, -jnp.inf)
        sink = sink_ref[h] * LOG2E
        m = jnp.maximum(jnp.max(s, axis=-1, keepdims=True), sink)
        p = jnp.exp2(s - m)
        inv_denoms.append(1.0 / (jnp.sum(p, axis=-1, keepdims=True) + jnp.exp2(sink - m)))
        ps.append(p.astype(BF16))
    outs = [_dot(p, keys_of(vdup[h // SWA_GROUP], u)) * inv
            for p, inv, (u, h) in zip(ps, inv_denoms, units)]
    for u in range(nsub):
        for j in range(SWA_HEADS // 2):
            pair = jnp.where(lo_q, outs[u * SWA_HEADS + 2 * j], outs[u * SWA_HEADS + 2 * j + 1])
            o_ref[u * blk:(u + 1) * blk, j * LANES:(j + 1) * LANES] = pair.astype(o_ref.dtype)


def _swa(qkv, sinks, *, nsub=2):
    b, s, _ = qkv.shape
    blk = SWA_BLOCK
    tq = nsub * blk
    nq = SWA_HEADS * HEAD_DIM
    kcol = nq // LANES
    vcol = kcol + 1
    prev = lambda n: jnp.maximum(nsub * n - 1, 0)
    return pl.pallas_call(
        _swa_kernel,
        out_shape=jax.ShapeDtypeStruct((b, s, nq), BF16),
        grid=(b, s // tq),
        in_specs=[
            pl.BlockSpec(memory_space=pltpu.SMEM),
            pl.BlockSpec((None, tq, nq), lambda bi, n: (bi, n, 0)),
            pl.BlockSpec((None, blk, LANES), lambda bi, n: (bi, prev(n), kcol)),
            pl.BlockSpec((None, tq, LANES), lambda bi, n: (bi, n, kcol)),
            pl.BlockSpec((None, blk, LANES), lambda bi, n: (bi, prev(n), vcol)),
            pl.BlockSpec((None, tq, LANES), lambda bi, n: (bi, n, vcol)),
        ],
        out_specs=pl.BlockSpec((None, tq, nq), lambda bi, n: (bi, n, 0)),
        compiler_params=_params("parallel", "arbitrary"),
        name="swa",
    )(sinks, qkv, qkv, qkv, qkv, qkv)


def _tri_inverse_minus_eye(lows, ri, ci):
    same = lambda w: (ri ^ ci) < w
    base = 8
    x = [jnp.where(same(base), -low, 0.0) for low in lows]
    p2 = [_dot_bf(xi, xi) for xi in x]
    e = [xi + pi + _dot_bf(xi, pi) for xi, pi in zip(x, p2)]
    p4 = [_dot_bf(pi, pi) for pi in p2]
    e = [ei + pi + _dot_bf(ei, pi) for ei, pi in zip(e, p4)]
    w = base * 2
    while w <= RWKV_CHUNK:
        off = [jnp.where(same(w) & ~same(w // 2), low, 0.0) for low in lows]
        wm = [oi + _dot_bf(ei, oi) for ei, oi in zip(e, off)]
        e = [ei - wi - _dot_bf(wi, ei) for ei, wi in zip(e, wm)]
        w *= 2
    return e


def _rwkv_kernel(h_ref, mu_ref, w0_ref, w2_ref, a0_ref, a2_ref, g2_ref, kk_ref, ka_ref,
                 rk_ref, lnw_ref, lnb_ref, o_ref, state_ref, last_ref):
    c = RWKV_CHUNK
    tb = h_ref.shape[0]
    nchunk = tb // c
    dim = o_ref.shape[-1]
    npair = dim // LANES

    @pl.when(pl.program_id(1) == 0)
    def _():
        state_ref[...] = jnp.zeros_like(state_ref)
        last_ref[...] = jnp.zeros_like(last_ref)

    h = h_ref[...]
    row = lax.broadcasted_iota(jnp.int32, h.shape, 0)
    shifted = jnp.where(row == 0, last_ref[...], pltpu.roll(h, 1, 0))
    last_ref[...] = h[tb - 1:tb, :]
    hs = h + (shifted - h) * mu_ref[...]
    r = hs[:, 0:dim]
    k = hs[:, dim:2 * dim]
    v = hs[:, 2 * dim:3 * dim]
    xwa = hs[:, 3 * dim:3 * dim + LANES]
    xg = hs[:, 3 * dim + LANES:3 * dim + 2 * LANES]

    wl = w0_ref[...] + _dot_x3(jnp.tanh(xwa), w2_ref[...])
    logw = -jax.nn.sigmoid(wl) * float(np.exp(-0.5))
    a = jax.nn.sigmoid(a0_ref[...] + _dot_x3(xwa, a2_ref[...]))
    gate = _dot_bf(jax.nn.sigmoid(xg), g2_ref[...])

    ri = lax.broadcasted_iota(jnp.int32, (LANES, LANES), 0)
    ci = lax.broadcasted_iota(jnp.int32, (LANES, LANES), 1)
    ones_bd = ((ri ^ ci) < HEAD_DIM).astype(BF16)
    ones_bd2 = jnp.concatenate([ones_bd, ones_bd], axis=0)

    def head_sum(x):
        cols = []
        for j in range(npair):
            hi, lo = _hi_lo(x[:, j * LANES:(j + 1) * LANES])
            cols.append(_dot(jnp.concatenate([hi, lo], axis=1), ones_bd2))
        return jnp.concatenate(cols, axis=1)

    kk = k * kk_ref[...]
    kk = kk * jnp.minimum(lax.rsqrt(head_sum(kk * kk)), 1.0 / L2_EPS)
    k2 = k * (1.0 + (a - 1.0) * ka_ref[...])
    bvec = kk * a

    ti = lax.broadcasted_iota(jnp.int32, (tb, tb), 0)
    si = lax.broadcasted_iota(jnp.int32, (tb, tb), 1)
    tri = (((ti ^ si) < c) & (si <= ti)).astype(BF16)
    cum = _dot(jnp.concatenate([tri] * 3, axis=1), jnp.concatenate(_split3(logw), axis=0))
    e_neg = jnp.exp(-cum)
    alpha = kk * jnp.exp(cum - logw)
    beta = bvec * e_neg
    kappa = k2 * e_neg
    rho = r * jnp.exp(cum)
    cum_end = jnp.concatenate(
        [jnp.broadcast_to(cum[(n + 1) * c - 1:(n + 1) * c, :], (c, dim)) for n in range(nchunk)], axis=0)
    to_end = jnp.exp(cum_end - cum)
    beta_e = bvec * to_end
    kappa_e = k2 * to_end
    w_end = jnp.exp(cum_end)

    same_head = (ri ^ ci) < c
    strict = same_head & (ci < ri)
    incl = same_head & (ci <= ri)
    eye = ri == ci
    lo = lax.broadcasted_iota(jnp.int32, (c, LANES), 1) < HEAD_DIM

    def stack(x2):
        return jnp.concatenate([jnp.where(lo, x2, 0.0), jnp.where(lo, 0.0, x2)], axis=0)

    units = [(n, j) for n in range(nchunk) for j in range(npair)]
    blk = lambda x, n, j: x[n * c:(n + 1) * c, j * LANES:(j + 1) * LANES]
    a_s = [stack(blk(alpha, n, j)) for n, j in units]
    rho_s = [stack(blk(rho, n, j)) for n, j in units]
    v_s = [stack(blk(v, n, j)) for n, j in units]
    ends = [jnp.concatenate([stack(blk(beta_e, n, j)), stack(blk(kappa_e, n, j))], axis=0) for n, j in units]
    bk = [jnp.concatenate([blk(beta, n, j)] * 2 + [blk(kappa, n, j)] * 2, axis=0) for n, j in units]
    sc = [_dot_bf(jnp.concatenate([ai, ri_], axis=0), bi, NT) for ai, ri_, bi in zip(a_s, rho_s, bk)]
    l_ab = [jnp.where(strict, s[:LANES, :LANES], 0.0) for s in sc]
    l_ak = [jnp.where(strict, s[:LANES, LANES:], 0.0) for s in sc]
    r_b = [jnp.where(incl, s[LANES:, :LANES], 0.0) for s in sc]
    r_k = [jnp.where(incl, s[LANES:, LANES:], 0.0) for s in sc]
    e_inv = _tri_inverse_minus_eye(l_ab, ri, ci)
    lkv = [_dot_bf(li, vi) for li, vi in zip(l_ak, v_s)]
    p_m = [-(ai + _dot_bf(ei, ai)) for ei, ai in zip(e_inv, a_s)]
    q_m = [-(xi + _dot_bf(ei, xi)) for ei, xi in zip(e_inv, lkv)]
    m_m = [jnp.where(eye, blk(w_end, n, j)[:1, :], 0.0) + _dot_bf(pi, ei[:LANES], TN)
           for (n, j), pi, ei in zip(units, p_m, ends)]
    n_m = [_dot_bf(jnp.concatenate([qi, vi], axis=0), ei, TN) for qi, vi, ei in zip(q_m, v_s, ends)]
    g_m = [ri_ + _dot_bf(rb, pi) for ri_, rb, pi in zip(rho_s, r_b, p_m)]
    h_m = [_dot_bf(jnp.concatenate([rb, rk], axis=1), jnp.concatenate([qi, vi], axis=0))
           for rb, rk, qi, vi in zip(r_b, r_k, q_m, v_s)]

    state = [state_ref[j] for j in range(npair)]
    ys = []
    for n in range(nchunk):
        idx = [n * npair + j for j in range(npair)]
        nxt = [_dot_bf(state[j], m_m[i]) + n_m[i] for j, i in enumerate(idx)]
        y = [_dot_bf(g_m[i], state[j], NT) + h_m[i] for j, i in enumerate(idx)]
        ys.append(jnp.concatenate([yi[:c] + yi[c:] for yi in y], axis=1))
        state = nxt
    for j in range(npair):
        state_ref[j] = state[j]
    y = jnp.concatenate(ys, axis=0)

    mean = head_sum(y) * (1.0 / HEAD_DIM)
    d = y - mean
    var = head_sum(d * d) * (1.0 / HEAD_DIM)
    y = d * lax.rsqrt(var + GN_EPS) * lnw_ref[...] + lnb_ref[...]
    y = y + head_sum(r * k2 * rk_ref[...]) * v
    o_ref[...] = (y * gate).astype(o_ref.dtype)


def _rwkv(hb, mu, w0, w2p, a0, a2p, g2, k_k, k_a, r_k, ln_w, ln_b, *, tb=4 * RWKV_CHUNK):
    b, s, cols = hb.shape
    dim = w0.shape[-1]
    c = tb
    row = lambda x: x.reshape(1, -1)
    vec = lambda n: pl.BlockSpec((1, n), lambda bi, t: (0, 0))
    mat = lambda m: pl.BlockSpec(m.shape, lambda bi, t: (0, 0))
    return pl.pallas_call(
        _rwkv_kernel,
        out_shape=jax.ShapeDtypeStruct((b, s, dim), BF16),
        grid=(b, s // c),
        in_specs=[
            pl.BlockSpec((None, c, cols), lambda bi, t: (bi, t, 0)),
            vec(cols), vec(dim), mat(w2p), vec(dim), mat(a2p), mat(g2),
            vec(dim), vec(dim), vec(dim), vec(dim), vec(dim),
        ],
        out_specs=pl.BlockSpec((None, c, dim), lambda bi, t: (bi, t, 0)),
        scratch_shapes=[pltpu.VMEM((dim // LANES, LANES, LANES), F32), pltpu.VMEM((1, cols), F32)],
        compiler_params=_params("parallel", "arbitrary"),
        name="rwkv7",
    )(hb, row(mu), row(w0), w2p, row(a0), a2p, g2, row(k_k), row(k_a), row(r_k), row(ln_w), row(ln_b))


FOX_AUG = 3
FOX_BLOCK = 512


def _fox_in_kernel(x_ref, g_ref, w_ref, wvt_ref, bf_ref, sel_ref, qa_ref, ka_ref, va_ref, carry_ref):
    tc = x_ref.shape[0]
    nheads = qa_ref.shape[-1] // LANES
    dim = nheads * HEAD_DIM
    scale = HEAD_DIM ** -0.5 * LOG2E

    @pl.when(pl.program_id(1) == 0)
    def _():
        carry_ref[...] = jnp.zeros_like(carry_ref)

    hn = _rms(x_ref[...], g_ref[...]).astype(BF16)
    z = _dot(hn, w_ref[:, 2 * dim:]) + bf_ref[...]
    proj_q = _dot(hn, w_ref[:, :dim])
    logf = jnp.minimum(z, 0.0) - jnp.log(1.0 + jnp.exp(-jnp.abs(z)))
    lane = lax.broadcasted_iota(jnp.int32, z.shape, 1)
    logf = jnp.where(lane < nheads, logf, 0.0)
    ti = lax.broadcasted_iota(jnp.int32, (tc, tc), 0)
    si = lax.broadcasted_iota(jnp.int32, (tc, tc), 1)
    cg = _sel_dot((si <= ti).astype(BF16), logf) + carry_ref[...]
    carry_ref[...] = cg[tc - 1:tc, :]

    hi, mid, low = _split3(cg * LOG2E)
    pieces = (hi.astype(F32) + pltpu.roll(mid.astype(F32), nheads, 1)
              + pltpu.roll(low.astype(F32), 2 * nheads, 1)
              + (lane == FOX_AUG * nheads).astype(F32)).astype(BF16)
    q_c = _dot(pieces, sel_ref[0])
    k_c = _dot(pieces, sel_ref[1])

    lane = lax.broadcasted_iota(jnp.int32, (tc, LANES), 1)
    lo = lane < HEAD_DIM

    def assemble(proj, c_aug, o_ref):
        for j in range(nheads // 2):
            x = proj[:, j * LANES:(j + 1) * LANES]
            xr = pltpu.roll(x, HEAD_DIM, 1)
            for e, xe in ((0, x), (1, xr)):
                hs = slice((2 * j + e) * LANES, (2 * j + e + 1) * LANES)
                o_ref[:, hs] = jnp.where(lo, xe, c_aug[:, hs]).astype(BF16)

    proj_k = _dot(hn, w_ref[:, dim:2 * dim])
    assemble(proj_q * scale, q_c, qa_ref)
    v_t = _dot(wvt_ref[...], hn, NT)
    assemble(proj_k, k_c, ka_ref)
    one_rows = (lax.broadcasted_iota(jnp.int32, (LANES - HEAD_DIM, tc), 0) == 0).astype(F32)
    for h in range(nheads):
        va_ref[h, 0] = jnp.concatenate([v_t[h * HEAD_DIM:(h + 1) * HEAD_DIM], one_rows], axis=0).astype(BF16)


def _fox_select_matrices(nheads):
    sel = np.zeros((2, LANES, nheads * LANES), np.float32)
    one_row = FOX_AUG * nheads
    for h in range(nheads):
        for i in range(FOX_AUG):
            sel[0, i * nheads + h, h * LANES + HEAD_DIM + i] = 1.0
            sel[0, one_row, h * LANES + HEAD_DIM + FOX_AUG + i] = 1.0
            sel[1, one_row, h * LANES + HEAD_DIM + i] = 1.0
            sel[1, i * nheads + h, h * LANES + HEAD_DIM + FOX_AUG + i] = -1.0
    return jnp.asarray(sel, BF16)


def _fox_in(x, g, w_in, b_f, *, batch, tc):
    t, d = x.shape
    s = t // batch
    nheads = b_f.shape[-1]
    assert FOX_AUG * nheads < LANES
    dim = nheads * HEAD_DIM
    w = jnp.concatenate([w_in[:, :2 * dim], jnp.pad(w_in[:, 3 * dim:], ((0, 0), (0, LANES - nheads)))],
                        axis=1).astype(BF16)
    w_vt = w_in[:, 2 * dim:3 * dim].T.astype(BF16)
    bf = jnp.zeros((1, LANES), F32).at[0, :nheads].set(b_f)
    sel = _fox_select_matrices(nheads)
    wide = nheads * LANES
    nt = s // tc
    resident = pl.Buffered(1)
    return pl.pallas_call(
        _fox_in_kernel,
        out_shape=(jax.ShapeDtypeStruct((batch, s, wide), BF16),
                   jax.ShapeDtypeStruct((batch, s, wide), BF16),
                   jax.ShapeDtypeStruct((batch, nheads, nt, LANES, tc), BF16)),
        grid=(batch, nt),
        in_specs=[
            pl.BlockSpec((tc, d), lambda bi, ti: (bi * nt + ti, 0)),
            pl.BlockSpec((1, d), lambda bi, ti: (0, 0)),
            pl.BlockSpec(w.shape, lambda bi, ti: (0, 0), pipeline_mode=resident),
            pl.BlockSpec(w_vt.shape, lambda bi, ti: (0, 0), pipeline_mode=resident),
            pl.BlockSpec((1, LANES), lambda bi, ti: (0, 0)),
            pl.BlockSpec(sel.shape, lambda bi, ti: (0, 0, 0), pipeline_mode=resident),
        ],
        out_specs=(pl.BlockSpec((None, tc, wide), lambda bi, ti: (bi, ti, 0)),
                   pl.BlockSpec((None, tc, wide), lambda bi, ti: (bi, ti, 0)),
                   pl.BlockSpec((None, nheads, 1, LANES, tc), lambda bi, ti: (bi, 0, ti, 0, 0))),
        scratch_shapes=[pltpu.VMEM((1, LANES), F32)],
        compiler_params=_params("parallel", "arbitrary"),
        name="fox_in",
    )(x, g.reshape(1, d), w, w_vt, bf, sel)


def _fox_attn_kernel(q_ref, k_ref, v_ref, o_ref, m_ref, acc_ref, sa_ref, sb_ref):
    tq = o_ref.shape[0]
    nh = q_ref.shape[1] // LANES
    qi = pl.program_id(2)
    m_ref[...] = jnp.full_like(m_ref, -jnp.inf)
    acc_ref[...] = jnp.zeros_like(acc_ref)
    keys = lax.broadcasted_iota(jnp.int32, (tq, tq), 0)
    queries = lax.broadcasted_iota(jnp.int32, (tq, tq), 1)

    def logits_into(dst_ref, qb, kb, heads=None):
        q0 = pl.multiple_of(qb * tq, tq)
        k0 = pl.multiple_of(kb * tq, tq)
        for e in range(nh) if heads is None else heads:
            hs = slice(e * LANES, (e + 1) * LANES)
            dst_ref[e] = _dot(k_ref[pl.ds(k0, tq), hs], q_ref[pl.ds(q0, tq), hs], NT)

    def consume(src_ref, kb, diagonal, after_head=None):
        for e in range(nh):
            s = src_ref[e]
            if diagonal:
                s = jnp.where(keys <= queries, s, -jnp.inf)
            m_prev = m_ref[e]
            m_new = jnp.maximum(m_prev, jnp.max(s, axis=0, keepdims=True))
            p = jnp.exp2(s - m_new).astype(BF16)
            acc_ref[e] = jnp.exp2(m_prev - m_new) * acc_ref[e] + _dot(v_ref[e, kb], p)
            m_ref[e] = m_new
            if after_head is not None:
                after_head(e)

    @pl.when(qi == 0)
    def _():
        logits_into(sa_ref, 0, 0)

    def two_blocks(t, carry):
        kb = 2 * t
        logits_into(sb_ref, qi, kb + 1)
        consume(sa_ref, kb, False)
        logits_into(sa_ref, qi, kb + 2)
        consume(sb_ref, kb + 1, False)
        return carry

    lax.fori_loop(0, qi // 2, two_blocks, 0)
    nxt = jnp.minimum(qi + 1, pl.num_programs(2) - 1)

    @pl.when(qi % 2 == 0)
    def _():
        consume(sa_ref, qi, True, after_head=lambda e: logits_into(sa_ref, nxt, 0, heads=(e,)))

    @pl.when(qi % 2 == 1)
    def _():
        logits_into(sb_ref, qi, qi)
        consume(sa_ref, qi - 1, False)
        logits_into(sa_ref, nxt, 0)
        consume(sb_ref, qi, True)

    for pair in range(nh // 2):
        outs = []
        for e in (2 * pair, 2 * pair + 1):
            acc = acc_ref[e]
            outs.append(acc[:HEAD_DIM] / acc[HEAD_DIM:HEAD_DIM + 1])
        o_ref[:, pair * LANES:(pair + 1) * LANES] = jnp.concatenate(outs, axis=0).T.astype(o_ref.dtype)


def _fox_attn(q_aug, k_aug, v_aug, *, tq, nh=4):
    b, s, wide = q_aug.shape
    nheads = wide // LANES
    return pl.pallas_call(
        _fox_attn_kernel,
        out_shape=jax.ShapeDtypeStruct((b, s, nheads * HEAD_DIM), BF16),
        grid=(b, nheads // nh, s // tq),
        in_specs=[
            pl.BlockSpec((None, s, nh * LANES), lambda bi, g, qi: (bi, 0, g)),
            pl.BlockSpec((None, s, nh * LANES), lambda bi, g, qi: (bi, 0, g)),
            pl.BlockSpec((None, nh, s // tq, LANES, tq), lambda bi, g, qi: (bi, g, 0, 0, 0)),
        ],
        out_specs=pl.BlockSpec((None, tq, nh * HEAD_DIM), lambda bi, g, qi: (bi, qi, g)),
        scratch_shapes=[pltpu.VMEM((nh, 1, tq), F32), pltpu.VMEM((nh, LANES, tq), F32),
                        pltpu.VMEM((nh, tq, tq), F32), pltpu.VMEM((nh, tq, tq), F32)],
        compiler_params=_params("parallel", "parallel", "arbitrary"),
        name="fox_attn",
    )(q_aug, k_aug, v_aug)


def kernel(x, p, ffn1_norm, ffn1_w_gu, ffn1_w_down, mix_norm, ffn2_norm, ffn2_w_gu, ffn2_w_down, ple_norm, ple_w_gate, ple_w_proj, even_w_in, even_w_out, swa_sinks, rwkv_mu, rwkv_w0, rwkv_w2, rwkv_a0, rwkv_a2, rwkv_g2, rwkv_k_k, rwkv_k_a, rwkv_r_k, rwkv_ln_w, rwkv_ln_b, fox_w_in, fox_b_f, fox_w_out, final_norm):
    b, s, d = x.shape
    depth = p.shape[0]
    t = b * s
    bf = lambda w: w.astype(BF16)
    swa_q = SWA_HEADS * HEAD_DIM
    swa_cols = swa_q + 2 * (SWA_HEADS // SWA_GROUP) * HEAD_DIM
    rwkv_dim = rwkv_w0.shape[-1]
    lora = rwkv_w2.shape[1]
    fox_heads = fox_b_f.shape[-1]
    fox_dim = fox_heads * HEAD_DIM

    x = x.reshape(t, d)
    for i in range(depth):
        j = i // 2
        if i % 2 == 0:
            widths = (swa_cols, even_w_in.shape[-1] - swa_cols)
            x, qkv, hb = _ffn(x, ffn1_norm[i], ffn1_w_gu, ffn1_w_down, i, proj=(mix_norm[i], even_w_in, j, widths))
            ya = _swa(qkv.reshape(b, s, swa_cols), swa_sinks[j])
            zeros = jnp.zeros((lora, rwkv_dim), F32)
            w2p = jnp.concatenate([rwkv_w2[j], zeros], axis=0)
            a2p = jnp.concatenate([zeros, rwkv_a2[j]], axis=0)
            yb = _rwkv(hb.reshape(b, s, -1), rwkv_mu[j], rwkv_w0[j], w2p, rwkv_a0[j], a2p,
                       bf(rwkv_g2[j]), rwkv_k_k[j], rwkv_k_a[j], rwkv_r_k[j].reshape(-1),
                       rwkv_ln_w[j], rwkv_ln_b[j])
            mixed = ([ya.reshape(t, swa_q), yb.reshape(t, rwkv_dim)], even_w_out, j)
        else:
            (x,) = _ffn(x, ffn1_norm[i], ffn1_w_gu, ffn1_w_down, i)
            q_aug, k_aug, v_aug = _fox_in(x, mix_norm[i], fox_w_in[j], fox_b_f[j], batch=b, tc=FOX_BLOCK)
            yc = _fox_attn(q_aug, k_aug, v_aug, tq=FOX_BLOCK)
            mixed = ([yc.reshape(t, fox_dim)], fox_w_out, j)
        x = _post_mix(*mixed, x, ffn2_norm[i], ffn2_w_gu, ffn2_w_down, ple_norm[i], ple_w_gate,
                      p.reshape(depth, t, -1), ple_w_proj, final_norm, i, final=(i == depth - 1))
    return x.reshape(b, s, d)
```

```python
import functools

import jax
import jax.numpy as jnp
import numpy as np
from jax import lax
from jax.experimental import pallas as pl
from jax.experimental.pallas import tpu as pltpu

F32 = jnp.float32
BF16 = jnp.bfloat16

LANES = 128
HEAD_DIM = 64
SWA_HEADS = 8
SWA_GROUP = 4
SWA_BLOCK = 128
RWKV_CHUNK = 64
NORM_EPS = 1e-6
GN_EPS = 64e-5
L2_EPS = 1e-12
LOG2E = float(np.log2(np.e))
VMEM_LIMIT = 56 * 1024 * 1024

NN = (((1,), (0,)), ((), ()))
NT = (((1,), (1,)), ((), ()))
TN = (((0,), (0,)), ((), ()))


def _dot(a, b, dims=NN):
    return lax.dot_general(a, b, dims, preferred_element_type=F32)


def _dot_bf(a, b, dims=NN):
    return _dot(a.astype(BF16), b.astype(BF16), dims)


def _hi_lo(x):
    hi = x.astype(BF16)
    lo = (x - hi.astype(F32)).astype(BF16)
    return hi, lo


def _dot_x3(a, b, dims=NN):
    ah, al = _hi_lo(a)
    bh, bl = _hi_lo(b)
    return _dot(ah, bh, dims) + (_dot(ah, bl, dims) + _dot(al, bh, dims))


def _split3(x):
    hi = x.astype(BF16)
    r1 = x - hi.astype(F32)
    mid = r1.astype(BF16)
    lo = (r1 - mid.astype(F32)).astype(BF16)
    return hi, mid, lo


def _sel_dot(sel, x):
    hi, mid, lo = _split3(x)
    return _dot(sel, hi) + (_dot(sel, mid) + _dot(sel, lo))


def _rms(x, g):
    ms = jnp.mean(x * x, axis=-1, keepdims=True)
    return x * lax.rsqrt(ms + NORM_EPS) * g


def _params(*sem):
    return pltpu.CompilerParams(dimension_semantics=sem, vmem_limit_bytes=VMEM_LIMIT)


LOAD_STEPS = 16


def _weight_chunk(w, layer, rows, block=0):
    chunk = rows // LOAD_STEPS
    assert chunk * LOAD_STEPS == rows and chunk % 16 == 0, (rows, chunk)
    return pl.BlockSpec((None, chunk, w.shape[2]),
                        lambda i: (layer, block * LOAD_STEPS + jnp.minimum(i, LOAD_STEPS - 1), 0))


def _stash(step, chunk_ref, copy_ref):
    rows = chunk_ref.shape[0]
    copy_ref[pl.ds(pl.multiple_of(step * rows, rows), rows), :] = chunk_ref[...].astype(BF16)


def _row_tile(tm, n):
    return pl.BlockSpec((tm, n), lambda i: (jnp.maximum(i - LOAD_STEPS, 0), 0))


def _swiglu_half_step(x, g, wgu_ref, wd_ref):
    dff = wd_ref.shape[0]
    hn = _rms(x, g).astype(BF16)
    gate = _dot(hn, wgu_ref[:, :dff])
    up = _dot(hn, wgu_ref[:, dff:])
    act = (gate * jax.nn.sigmoid(gate) * up).astype(BF16)
    return x + 0.5 * _dot(act, wd_ref[...])


def _ffn_kernel(n_proj, x_ref, g_ref, wgu_ref, wd_ref, *refs):
    n_w = 3 if n_proj else 2
    copies = refs[len(refs) - n_w:]
    refs = refs[:len(refs) - n_w]
    step = pl.program_id(0)

    @pl.when(step < LOAD_STEPS)
    def _():
        _stash(step, wgu_ref, copies[0])
        _stash(step, wd_ref, copies[1])
        if n_proj:
            _stash(step, refs[1], copies[2])

    @pl.when(step >= LOAD_STEPS)
    def _():
        x = _swiglu_half_step(x_ref[...], g_ref[...], copies[0], copies[1])
        if n_proj:
            gm_ref, o_ref, proj_refs = refs[0], refs[2], refs[3:]
            proj = _dot(_rms(x, gm_ref[...]).astype(BF16), copies[2][...])
            col = 0
            for p_ref in proj_refs:
                p_ref[...] = proj[:, col:col + p_ref.shape[1]]
                col += p_ref.shape[1]
        else:
            o_ref = refs[0]
        o_ref[...] = x


def _ffn(x, g, w_gu, w_down, layer, proj=None, *, tm=512):
    t, d = x.shape
    dff = w_down.shape[1]
    vec = pl.BlockSpec((1, d), lambda i: (0, 0))
    in_specs = [_row_tile(tm, d), vec, _weight_chunk(w_gu, layer, d), _weight_chunk(w_down, layer, dff)]
    args = [x, g.reshape(1, d), w_gu, w_down]
    scratch = [pltpu.VMEM((d, 2 * dff), BF16), pltpu.VMEM((dff, d), BF16)]
    out_shape = [jax.ShapeDtypeStruct((t, d), F32)]
    out_specs = [_row_tile(tm, d)]
    widths = ()
    if proj is not None:
        gm, w, w_layer, widths = proj
        assert sum(widths) == w.shape[2]
        in_specs += [vec, _weight_chunk(w, w_layer, d)]
        args += [gm.reshape(1, d), w]
        scratch += [pltpu.VMEM(w.shape[1:], BF16)]
        out_shape += [jax.ShapeDtypeStruct((t, n), F32) for n in widths]
        out_specs += [_row_tile(tm, n) for n in widths]
    return pl.pallas_call(
        functools.partial(_ffn_kernel, len(widths)),
        out_shape=out_shape,
        grid=(LOAD_STEPS + t // tm,),
        in_specs=in_specs,
        out_specs=out_specs,
        scratch_shapes=scratch,
        compiler_params=_params("arbitrary"),
        name="ffn",
    )(*args)


def _post_mix_kernel(n_in, final, *refs):
    a_refs, wo_refs = refs[:n_in], refs[n_in:2 * n_in]
    (x_ref, g2_ref, wgu_ref, wd_ref, gp_ref, wpg_ref, p_ref, wpp_ref, fn_ref, o_ref) = refs[2 * n_in:2 * n_in + 10]
    copies = refs[2 * n_in + 10:]
    wo_copies, (wgu_c, wd_c, wpg_c, wpp_c) = copies[:n_in], copies[n_in:]
    step = pl.program_id(0)

    @pl.when(step < LOAD_STEPS)
    def _():
        for src, dst in zip(wo_refs + (wgu_ref, wd_ref, wpg_ref, wpp_ref), wo_copies + (wgu_c, wd_c, wpg_c, wpp_c)):
            _stash(step, src, dst)

    @pl.when(step >= LOAD_STEPS)
    def _():
        x = x_ref[...]
        for a_ref, w_c in zip(a_refs, wo_copies):
            x = x + _dot(a_ref[...], w_c[...])
        x = _swiglu_half_step(x, g2_ref[...], wgu_c, wd_c)
        gate = jax.nn.sigmoid(_dot(_rms(x, gp_ref[...]).astype(BF16), wpg_c[...]))
        x = x + gate * _dot(p_ref[...].astype(BF16), wpp_c[...])
        if final:
            x = _rms(x, fn_ref[...])
        o_ref[...] = x


def _post_mix(a_list, w_out, out_layer, x, g2, w_gu, w_down, gp, w_gate, p, w_proj, final_g, layer, *, final,
              tm=512):
    t, d = x.shape
    dff = w_down.shape[1]
    vec = pl.BlockSpec((1, d), lambda i: (0, 0))
    width = a_list[0].shape[1]
    assert all(a.shape[1] == width for a in a_list)
    in_specs = [_row_tile(tm, width) for _ in a_list]
    in_specs += [_weight_chunk(w_out, out_layer, width, block=k) for k in range(len(a_list))]
    in_specs += [_row_tile(tm, d), vec, _weight_chunk(w_gu, layer, d), _weight_chunk(w_down, layer, dff), vec,
                 _weight_chunk(w_gate, layer, d),
                 pl.BlockSpec((None, tm, p.shape[2]), lambda i: (layer, jnp.maximum(i - LOAD_STEPS, 0), 0)),
                 _weight_chunk(w_proj, layer, w_proj.shape[1]), vec]
    scratch = [pltpu.VMEM((width, d), BF16) for _ in a_list]
    scratch += [pltpu.VMEM((d, 2 * dff), BF16), pltpu.VMEM((dff, d), BF16), pltpu.VMEM((d, d), BF16),
                pltpu.VMEM(w_proj.shape[1:], BF16)]
    return pl.pallas_call(
        functools.partial(_post_mix_kernel, len(a_list), final),
        out_shape=jax.ShapeDtypeStruct((t, d), F32),
        grid=(LOAD_STEPS + t // tm,),
        in_specs=in_specs,
        out_specs=_row_tile(tm, d),
        scratch_shapes=scratch,
        compiler_params=_params("arbitrary"),
        name="post_mix",
    )(*a_list, *([w_out] * len(a_list)), x, g2.reshape(1, d), w_gu, w_down, gp.reshape(1, d), w_gate, p,
      w_proj, final_g.reshape(1, d))


def _swa_kernel(sink_ref, q_ref, kp_ref, kc_ref, vp_ref, vc_ref, o_ref):
    n = pl.program_id(1)
    blk = SWA_BLOCK
    nsub = q_ref.shape[0] // blk
    scale = HEAD_DIM ** -0.5
    k = jnp.concatenate([kp_ref[...], kc_ref[...]], axis=0)
    v = jnp.concatenate([vp_ref[...], vc_ref[...]], axis=0)
    kr = pltpu.roll(k, HEAD_DIM, 1)
    vr = pltpu.roll(v, HEAD_DIM, 1)
    lo_kv = lax.broadcasted_iota(jnp.int32, k.shape, 1) < HEAD_DIM
    kdup = [jnp.where(lo_kv, k, kr).astype(BF16), jnp.where(lo_kv, kr, k).astype(BF16)]
    vdup = [jnp.where(lo_kv, v, vr).astype(BF16), jnp.where(lo_kv, vr, v).astype(BF16)]

    qi = lax.broadcasted_iota(jnp.int32, (blk, 2 * blk), 0)
    ki = lax.broadcasted_iota(jnp.int32, (blk, 2 * blk), 1)
    dist = qi + blk - ki
    in_window = (dist >= 0) & (dist < blk)
    distf = dist.astype(F32)
    lo_q = lax.broadcasted_iota(jnp.int32, (blk, LANES), 1) < HEAD_DIM

    units = [(u, h) for u in range(nsub) for h in range(SWA_HEADS)]
    keys_of = lambda x, u: x[u * blk:(u + 2) * blk]
    qm = []
    for u, h in units:
        j, e = divmod(h, 2)
        q2 = q_ref[u * blk:(u + 1) * blk, j * LANES:(j + 1) * LANES] * (scale * LOG2E)
        qm.append(jnp.where(lo_q if e == 0 else ~lo_q, q2, 0.0).astype(BF16))
    logits = [_dot(qi_, keys_of(kdup[h // SWA_GROUP], u), NT) for qi_, (u, h) in zip(qm, units)]
    ps, inv_denoms = [], []
    for s, (u, h) in zip(logits, units):
        valid = in_window & ((n > 0) | (ki >= blk)) if u == 0 else in_window
        slope = 2.0 ** (-8.0 * (h + 1) / SWA_HEADS) * LOG2E
        s = jnp.where(valid, s - slope * distf, -jnp.inf)
        sink = sink_ref[h] * LOG2E
        m = jnp.maximum(jnp.max(s, axis=-1, keepdims=True), sink)
        p = jnp.exp2(s - m)
        inv_denoms.append(1.0 / (jnp.sum(p, axis=-1, keepdims=True) + jnp.exp2(sink - m)))
        ps.append(p.astype(BF16))
    outs = [_dot(p, keys_of(vdup[h // SWA_GROUP], u)) * inv
            for p, inv, (u, h) in zip(ps, inv_denoms, units)]
    for u in range(nsub):
        for j in range(SWA_HEADS // 2):
            pair = jnp.where(lo_q, outs[u * SWA_HEADS + 2 * j], outs[u * SWA_HEADS + 2 * j + 1])
            o_ref[u * blk:(u + 1) * blk, j * LANES:(j + 1) * LANES] = pair.astype(o_ref.dtype)


def _swa(qkv, sinks, *, nsub=2):
    b, s, _ = qkv.shape
    blk = SWA_BLOCK
    tq = nsub * blk
    nq = SWA_HEADS * HEAD_DIM
    kcol = nq // LANES
    vcol = kcol + 1
    prev = lambda n: jnp.maximum(nsub * n - 1, 0)
    return pl.pallas_call(
        _swa_kernel,
        out_shape=jax.ShapeDtypeStruct((b, s, nq), BF16),
        grid=(b, s // tq),
        in_specs=[
            pl.BlockSpec(memory_space=pltpu.SMEM),
            pl.BlockSpec((None, tq, nq), lambda bi, n: (bi, n, 0)),
            pl.BlockSpec((None, blk, LANES), lambda bi, n: (bi, prev(n), kcol)),
            pl.BlockSpec((None, tq, LANES), lambda bi, n: (bi, n, kcol)),
            pl.BlockSpec((None, blk, LANES), lambda bi, n: (bi, prev(n), vcol)),
            pl.BlockSpec((None, tq, LANES), lambda bi, n: (bi, n, vcol)),
        ],
        out_specs=pl.BlockSpec((None, tq, nq), lambda bi, n: (bi, n, 0)),
        compiler_params=_params("parallel", "arbitrary"),
        name="swa",
    )(sinks, qkv, qkv, qkv, qkv, qkv)


def _tri_inverse_minus_eye(lows, ri, ci):
    same = lambda w: (ri ^ ci) < w
    base = 8
    x = [jnp.where(same(base), -low, 0.0) for low in lows]
    p2 = [_dot_bf(xi, xi) for xi in x]
    e = [xi + pi + _dot_bf(xi, pi) for xi, pi in zip(x, p2)]
    p4 = [_dot_bf(pi, pi) for pi in p2]
    e = [ei + pi + _dot_bf(ei, pi) for ei, pi in zip(e, p4)]
    w = base * 2
    while w <= RWKV_CHUNK:
        off = [jnp.where(same(w) & ~same(w // 2), low, 0.0) for low in lows]
        wm = [oi + _dot_bf(ei, oi) for ei, oi in zip(e, off)]
        e = [ei - wi - _dot_bf(wi, ei) for ei, wi in zip(e, wm)]
        w *= 2
    return e


def _rwkv_kernel(h_ref, mu_ref, w0_ref, w2_ref, a0_ref, a2_ref, g2_ref, kk_ref, ka_ref,
                 rk_ref, lnw_ref, lnb_ref, o_ref, state_ref, last_ref):
    c = RWKV_CHUNK
    tb = h_ref.shape[0]
    nchunk = tb // c
    dim = o_ref.shape[-1]
    npair = dim // LANES

    @pl.when(pl.program_id(1) == 0)
    def _():
        state_ref[...] = jnp.zeros_like(state_ref)
        last_ref[...] = jnp.zeros_like(last_ref)

    h = h_ref[...]
    row = lax.broadcasted_iota(jnp.int32, h.shape, 0)
    shifted = jnp.where(row == 0, last_ref[...], pltpu.roll(h, 1, 0))
    last_ref[...] = h[tb - 1:tb, :]
    hs = h + (shifted - h) * mu_ref[...]
    r = hs[:, 0:dim]
    k = hs[:, dim:2 * dim]
    v = hs[:, 2 * dim:3 * dim]
    xwa = hs[:, 3 * dim:3 * dim + LANES]
    xg = hs[:, 3 * dim + LANES:3 * dim + 2 * LANES]

    wl = w0_ref[...] + _dot_x3(jnp.tanh(xwa), w2_ref[...])
    logw = -jax.nn.sigmoid(wl) * float(np.exp(-0.5))
    a = jax.nn.sigmoid(a0_ref[...] + _dot_x3(xwa, a2_ref[...]))
    gate = _dot_bf(jax.nn.sigmoid(xg), g2_ref[...])

    ri = lax.broadcasted_iota(jnp.int32, (LANES, LANES), 0)
    ci = lax.broadcasted_iota(jnp.int32, (LANES, LANES), 1)
    ones_bd = ((ri ^ ci) < HEAD_DIM).astype(BF16)
    ones_bd2 = jnp.concatenate([ones_bd, ones_bd], axis=0)

    def head_sum(x):
        cols = []
        for j in range(npair):
            hi, lo = _hi_lo(x[:, j * LANES:(j + 1) * LANES])
            cols.append(_dot(jnp.concatenate([hi, lo], axis=1), ones_bd2))
        return jnp.concatenate(cols, axis=1)

    kk = k * kk_ref[...]
    kk = kk * jnp.minimum(lax.rsqrt(head_sum(kk * kk)), 1.0 / L2_EPS)
    k2 = k * (1.0 + (a - 1.0) * ka_ref[...])
    bvec = kk * a

    ti = lax.broadcasted_iota(jnp.int32, (tb, tb), 0)
    si = lax.broadcasted_iota(jnp.int32, (tb, tb), 1)
    tri = (((ti ^ si) < c) & (si <= ti)).astype(BF16)
    cum = _dot(jnp.concatenate([tri] * 3, axis=1), jnp.concatenate(_split3(logw), axis=0))
    e_neg = jnp.exp(-cum)
    alpha = kk * jnp.exp(cum - logw)
    beta = bvec * e_neg
    kappa = k2 * e_neg
    rho = r * jnp.exp(cum)
    cum_end = jnp.concatenate(
        [jnp.broadcast_to(cum[(n + 1) * c - 1:(n + 1) * c, :], (c, dim)) for n in range(nchunk)], axis=0)
    to_end = jnp.exp(cum_end - cum)
    beta_e = bvec * to_end
    kappa_e = k2 * to_end
    w_end = jnp.exp(cum_end)

    same_head = (ri ^ ci) < c
    strict = same_head & (ci < ri)
    incl = same_head & (ci <= ri)
    eye = ri == ci
    lo = lax.broadcasted_iota(jnp.int32, (c, LANES), 1) < HEAD_DIM

    def stack(x2):
        return jnp.concatenate([jnp.where(lo, x2, 0.0), jnp.where(lo, 0.0, x2)], axis=0)

    units = [(n, j) for n in range(nchunk) for j in range(npair)]
    blk = lambda x, n, j: x[n * c:(n + 1) * c, j * LANES:(j + 1) * LANES]
    a_s = [stack(blk(alpha, n, j)) for n, j in units]
    rho_s = [stack(blk(rho, n, j)) for n, j in units]
    v_s = [stack(blk(v, n, j)) for n, j in units]
    ends = [jnp.concatenate([stack(blk(beta_e, n, j)), stack(blk(kappa_e, n, j))], axis=0) for n, j in units]
    bk = [jnp.concatenate([blk(beta, n, j)] * 2 + [blk(kappa, n, j)] * 2, axis=0) for n, j in units]
    sc = [_dot_bf(jnp.concatenate([ai, ri_], axis=0), bi, NT) for ai, ri_, bi in zip(a_s, rho_s, bk)]
    l_ab = [jnp.where(strict, s[:LANES, :LANES], 0.0) for s in sc]
    l_ak = [jnp.where(strict, s[:LANES, LANES:], 0.0) for s in sc]
    r_b = [jnp.where(incl, s[LANES:, :LANES], 0.0) for s in sc]
    r_k = [jnp.where(incl, s[LANES:, LANES:], 0.0) for s in sc]
    e_inv = _tri_inverse_minus_eye(l_ab, ri, ci)
    lkv = [_dot_bf(li, vi) for li, vi in zip(l_ak, v_s)]
    p_m = [-(ai + _dot_bf(ei, ai)) for ei, ai in zip(e_inv, a_s)]
    q_m = [-(xi + _dot_bf(ei, xi)) for ei, xi in zip(e_inv, lkv)]
    m_m = [jnp.where(eye, blk(w_end, n, j)[:1, :], 0.0) + _dot_bf(pi, ei[:LANES], TN)
           for (n, j), pi, ei in zip(units, p_m, ends)]
    n_m = [_dot_bf(jnp.concatenate([qi, vi], axis=0), ei, TN) for qi, vi, ei in zip(q_m, v_s, ends)]
    g_m = [ri_ + _dot_bf(rb, pi) for ri_, rb, pi in zip(rho_s, r_b, p_m)]
    h_m = [_dot_bf(jnp.concatenate([rb, rk], axis=1), jnp.concatenate([qi, vi], axis=0))
           for rb, rk, qi, vi in zip(r_b, r_k, q_m, v_s)]

    state = [state_ref[j] for j in range(npair)]
    ys = []
    for n in range(nchunk):
        idx = [n * npair + j for j in range(npair)]
        nxt = [_dot_bf(state[j], m_m[i]) + n_m[i] for j, i in enumerate(idx)]
        y = [_dot_bf(g_m[i], state[j], NT) + h_m[i] for j, i in enumerate(idx)]
        ys.append(jnp.concatenate([yi[:c] + yi[c:] for yi in y], axis=1))
        state = nxt
    for j in range(npair):
        state_ref[j] = state[j]
    y = jnp.concatenate(ys, axis=0)

    mean = head_sum(y) * (1.0 / HEAD_DIM)
    d = y - mean
    var = head_sum(d * d) * (1.0 / HEAD_DIM)
    y = d * lax.rsqrt(var + GN_EPS) * lnw_ref[...] + lnb_ref[...]
    y = y + head_sum(r * k2 * rk_ref[...]) * v
    o_ref[...] = (y * gate).astype(o_ref.dtype)


def _rwkv(hb, mu, w0, w2p, a0, a2p, g2, k_k, k_a, r_k, ln_w, ln_b, *, tb=4 * RWKV_CHUNK):
    b, s, cols = hb.shape
    dim = w0.shape[-1]
    c = tb
    row = lambda x: x.reshape(1, -1)
    vec = lambda n: pl.BlockSpec((1, n), lambda bi, t: (0, 0))
    mat = lambda m: pl.BlockSpec(m.shape, lambda bi, t: (0, 0))
    return pl.pallas_call(
        _rwkv_kernel,
        out_shape=jax.ShapeDtypeStruct((b, s, dim), BF16),
        grid=(b, s // c),
        in_specs=[
            pl.BlockSpec((None, c, cols), lambda bi, t: (bi, t, 0)),
            vec(cols), vec(dim), mat(w2p), vec(dim), mat(a2p), mat(g2),
            vec(dim), vec(dim), vec(dim), vec(dim), vec(dim),
        ],
        out_specs=pl.BlockSpec((None, c, dim), lambda bi, t: (bi, t, 0)),
        scratch_shapes=[pltpu.VMEM((dim // LANES, LANES, LANES), F32), pltpu.VMEM((1, cols), F32)],
        compiler_params=_params("parallel", "arbitrary"),
        name="rwkv7",
    )(hb, row(mu), row(w0), w2p, row(a0), a2p, g2, row(k_k), row(k_a), row(r_k), row(ln_w), row(ln_b))


FOX_AUG = 3
FOX_BLOCK = 512


def _fox_in_kernel(x_ref, g_ref, w_ref, wvt_ref, bf_ref, sel_ref, qa_ref, ka_ref, va_ref, carry_ref):
    tc = x_ref.shape[0]
    nheads = qa_ref.shape[-1] // LANES
    dim = nheads * HEAD_DIM
    scale = HEAD_DIM ** -0.5 * LOG2E

    @pl.when(pl.program_id(1) == 0)
    def _():
        carry_ref[...] = jnp.zeros_like(carry_ref)

    hn = _rms(x_ref[...], g_ref[...]).astype(BF16)
    z = _dot(hn, w_ref[:, 2 * dim:]) + bf_ref[...]
    proj_q = _dot(hn, w_ref[:, :dim])
    logf = jnp.minimum(z, 0.0) - jnp.log(1.0 + jnp.exp(-jnp.abs(z)))
    lane = lax.broadcasted_iota(jnp.int32, z.shape, 1)
    logf = jnp.where(lane < nheads, logf, 0.0)
    ti = lax.broadcasted_iota(jnp.int32, (tc, tc), 0)
    si = lax.broadcasted_iota(jnp.int32, (tc, tc), 1)
    cg = _sel_dot((si <= ti).astype(BF16), logf) + carry_ref[...]
    carry_ref[...] = cg[tc - 1:tc, :]

    hi, mid, low = _split3(cg * LOG2E)
    pieces = (hi.astype(F32) + pltpu.roll(mid.astype(F32), nheads, 1)
              + pltpu.roll(low.astype(F32), 2 * nheads, 1)
              + (lane == FOX_AUG * nheads).astype(F32)).astype(BF16)
    q_c = _dot(pieces, sel_ref[0])
    k_c = _dot(pieces, sel_ref[1])

    lane = lax.broadcasted_iota(jnp.int32, (tc, LANES), 1)
    lo = lane < HEAD_DIM

    def assemble(proj, c_aug, o_ref):
        for j in range(nheads // 2):
            x = proj[:, j * LANES:(j + 1) * LANES]
            xr = pltpu.roll(x, HEAD_DIM, 1)
            for e, xe in ((0, x), (1, xr)):
                hs = slice((2 * j + e) * LANES, (2 * j + e + 1) * LANES)
                o_ref[:, hs] = jnp.where(lo, xe, c_aug[:, hs]).astype(BF16)

    proj_k = _dot(hn, w_ref[:, dim:2 * dim])
    assemble(proj_q * scale, q_c, qa_ref)
    v_t = _dot(wvt_ref[...], hn, NT)
    assemble(proj_k, k_c, ka_ref)
    one_rows = (lax.broadcasted_iota(jnp.int32, (LANES - HEAD_DIM, tc), 0) == 0).astype(F32)
    for h in range(nheads):
        va_ref[h, 0] = jnp.concatenate([v_t[h * HEAD_DIM:(h + 1) * HEAD_DIM], one_rows], axis=0).astype(BF16)


def _fox_select_matrices(nheads):
    sel = np.zeros((2, LANES, nheads * LANES), np.float32)
    one_row = FOX_AUG * nheads
    for h in range(nheads):
        for i in range(FOX_AUG):
            sel[0, i * nheads + h, h * LANES + HEAD_DIM + i] = 1.0
            sel[0, one_row, h * LANES + HEAD_DIM + FOX_AUG + i] = 1.0
            sel[1, one_row, h * LANES + HEAD_DIM + i] = 1.0
            sel[1, i * nheads + h, h * LANES + HEAD_DIM + FOX_AUG + i] = -1.0
    return jnp.asarray(sel, BF16)


def _fox_in(x, g, w_in, b_f, *, batch, tc):
    t, d = x.shape
    s = t // batch
    nheads = b_f.shape[-1]
    assert FOX_AUG * nheads < LANES
    dim = nheads * HEAD_DIM
    w = jnp.concatenate([w_in[:, :2 * dim], jnp.pad(w_in[:, 3 * dim:], ((0, 0), (0, LANES - nheads)))],
                        axis=1).astype(BF16)
    w_vt = w_in[:, 2 * dim:3 * dim].T.astype(BF16)
    bf = jnp.zeros((1, LANES), F32).at[0, :nheads].set(b_f)
    sel = _fox_select_matrices(nheads)
    wide = nheads * LANES
    nt = s // tc
    resident = pl.Buffered(1)
    return pl.pallas_call(
        _fox_in_kernel,
        out_shape=(jax.ShapeDtypeStruct((batch, s, wide), BF16),
                   jax.ShapeDtypeStruct((batch, s, wide), BF16),
                   jax.ShapeDtypeStruct((batch, nheads, nt, LANES, tc), BF16)),
        grid=(batch, nt),
        in_specs=[
            pl.BlockSpec((tc, d), lambda bi, ti: (bi * nt + ti, 0)),
            pl.BlockSpec((1, d), lambda bi, ti: (0, 0)),
            pl.BlockSpec(w.shape, lambda bi, ti: (0, 0), pipeline_mode=resident),
            pl.BlockSpec(w_vt.shape, lambda bi, ti: (0, 0), pipeline_mode=resident),
            pl.BlockSpec((1, LANES), lambda bi, ti: (0, 0)),
            pl.BlockSpec(sel.shape, lambda bi, ti: (0, 0, 0), pipeline_mode=resident),
        ],
        out_specs=(pl.BlockSpec((None, tc, wide), lambda bi, ti: (bi, ti, 0)),
                   pl.BlockSpec((None, tc, wide), lambda bi, ti: (bi, ti, 0)),
                   pl.BlockSpec((None, nheads, 1, LANES, tc), lambda bi, ti: (bi, 0, ti, 0, 0))),
        scratch_shapes=[pltpu.VMEM((1, LANES), F32)],
        compiler_params=_params("parallel", "arbitrary"),
        name="fox_in",
    )(x, g.reshape(1, d), w, w_vt, bf, sel)


def _fox_attn_kernel(q_ref, k_ref, v_ref, o_ref, m_ref, acc_ref, sa_ref, sb_ref):
    tq = o_ref.shape[0]
    nh = q_ref.shape[1] // LANES
    qi = pl.program_id(2)
    m_ref[...] = jnp.full_like(m_ref, -jnp.inf)
    acc_ref[...] = jnp.zeros_like(acc_ref)
    keys = lax.broadcasted_iota(jnp.int32, (tq, tq), 0)
    queries = lax.broadcasted_iota(jnp.int32, (tq, tq), 1)

    def logits_into(dst_ref, qb, kb, heads=None):
        q0 = pl.multiple_of(qb * tq, tq)
        k0 = pl.multiple_of(kb * tq, tq)
        for e in range(nh) if heads is None else heads:
            hs = slice(e * LANES, (e + 1) * LANES)
            dst_ref[e] = _dot(k_ref[pl.ds(k0, tq), hs], q_ref[pl.ds(q0, tq), hs], NT)

    def consume(src_ref, kb, diagonal, after_head=None):
        for e in range(nh):
            s = src_ref[e]
            if diagonal:
                s = jnp.where(keys <= queries, s, -jnp.inf)
            m_prev = m_ref[e]
            m_new = jnp.maximum(m_prev, jnp.max(s, axis=0, keepdims=True))
            p = jnp.exp2(s - m_new).astype(BF16)
            acc_ref[e] = jnp.exp2(m_prev - m_new) * acc_ref[e] + _dot(v_ref[e, kb], p)
            m_ref[e] = m_new
            if after_head is not None:
                after_head(e)

    @pl.when(qi == 0)
    def _():
        logits_into(sa_ref, 0, 0)

    def two_blocks(t, carry):
        kb = 2 * t
        logits_into(sb_ref, qi, kb + 1)
        consume(sa_ref, kb, False)
        logits_into(sa_ref, qi, kb + 2)
        consume(sb_ref, kb + 1, False)
        return carry

    lax.fori_loop(0, qi // 2, two_blocks, 0)
    nxt = jnp.minimum(qi + 1, pl.num_programs(2) - 1)

    @pl.when(qi % 2 == 0)
    def _():
        consume(sa_ref, qi, True, after_head=lambda e: logits_into(sa_ref, nxt, 0, heads=(e,)))

    @pl.when(qi % 2 == 1)
    def _():
        logits_into(sb_ref, qi, qi)
        consume(sa_ref, qi - 1, False)
        logits_into(sa_ref, nxt, 0)
        consume(sb_ref, qi, True)

    for pair in range(nh // 2):
        outs = []
        for e in (2 * pair, 2 * pair + 1):
            acc = acc_ref[e]
            outs.append(acc[:HEAD_DIM] / acc[HEAD_DIM:HEAD_DIM + 1])
        o_ref[:, pair * LANES:(pair + 1) * LANES] = jnp.concatenate(outs, axis=0).T.astype(o_ref.dtype)


def _fox_attn(q_aug, k_aug, v_aug, *, tq, nh=4):
    b, s, wide = q_aug.shape
    nheads = wide // LANES
    return pl.pallas_call(
        _fox_attn_kernel,
        out_shape=jax.ShapeDtypeStruct((b, s, nheads * HEAD_DIM), BF16),
        grid=(b, nheads // nh, s // tq),
        in_specs=[
            pl.BlockSpec((None, s, nh * LANES), lambda bi, g, qi: (bi, 0, g)),
            pl.BlockSpec((None, s, nh * LANES), lambda bi, g, qi: (bi, 0, g)),
            pl.BlockSpec((None, nh, s // tq, LANES, tq), lambda bi, g, qi: (bi, g, 0, 0, 0)),
        ],
        out_specs=pl.BlockSpec((None, tq, nh * HEAD_DIM), lambda bi, g, qi: (bi, qi, g)),
        scratch_shapes=[pltpu.VMEM((nh, 1, tq), F32), pltpu.VMEM((nh, LANES, tq), F32),
                        pltpu.VMEM((nh, tq, tq), F32), pltpu.VMEM((nh, tq, tq), F32)],
        compiler_params=_params("parallel", "parallel", "arbitrary"),
        name="fox_attn",
    )(q_aug, k_aug, v_aug)


def kernel(x, p, ffn1_norm, ffn1_w_gu, ffn1_w_down, mix_norm, ffn2_norm, ffn2_w_gu, ffn2_w_down, ple_norm, ple_w_gate, ple_w_proj, even_w_in, even_w_out, swa_sinks, rwkv_mu, rwkv_w0, rwkv_w2, rwkv_a0, rwkv_a2, rwkv_g2, rwkv_k_k, rwkv_k_a, rwkv_r_k, rwkv_ln_w, rwkv_ln_b, fox_w_in, fox_b_f, fox_w_out, final_norm):
    b, s, d = x.shape
    depth = p.shape[0]
    t = b * s
    bf = lambda w: w.astype(BF16)
    swa_q = SWA_HEADS * HEAD_DIM
    swa_cols = swa_q + 2 * (SWA_HEADS // SWA_GROUP) * HEAD_DIM
    rwkv_dim = rwkv_w0.shape[-1]
    lora = rwkv_w2.shape[1]
    fox_heads = fox_b_f.shape[-1]
    fox_dim = fox_heads * HEAD_DIM

    x = x.reshape(t, d)
    for i in range(depth):
        j = i // 2
        if i % 2 == 0:
            widths = (swa_cols, even_w_in.shape[-1] - swa_cols)
            x, qkv, hb = _ffn(x, ffn1_norm[i], ffn1_w_gu, ffn1_w_down, i, proj=(mix_norm[i], even_w_in, j, widths))
            ya = _swa(qkv.reshape(b, s, swa_cols), swa_sinks[j])
            zeros = jnp.zeros((lora, rwkv_dim), F32)
            w2p = jnp.concatenate([rwkv_w2[j], zeros], axis=0)
            a2p = jnp.concatenate([zeros, rwkv_a2[j]], axis=0)
            yb = _rwkv(hb.reshape(b, s, -1), rwkv_mu[j], rwkv_w0[j], w2p, rwkv_a0[j], a2p,
                       bf(rwkv_g2[j]), rwkv_k_k[j], rwkv_k_a[j], rwkv_r_k[j].reshape(-1),
                       rwkv_ln_w[j], rwkv_ln_b[j])
            mixed = ([ya.reshape(t, swa_q), yb.reshape(t, rwkv_dim)], even_w_out, j)
        else:
            (x,) = _ffn(x, ffn1_norm[i], ffn1_w_gu, ffn1_w_down, i)
            q_aug, k_aug, v_aug = _fox_in(x, mix_norm[i], fox_w_in[j], fox_b_f[j], batch=b, tc=FOX_BLOCK)
            yc = _fox_attn(q_aug, k_aug, v_aug, tq=FOX_BLOCK)
            mixed = ([yc.reshape(t, fox_dim)], fox_w_out, j)
        x = _post_mix(*mixed, x, ffn2_norm[i], ffn2_w_gu, ffn2_w_down, ple_norm[i], ple_w_gate,
                      p.reshape(depth, t, -1), ple_w_proj, final_norm, i, final=(i == depth - 1))
    return x.reshape(b, s, d)
```

```python
import functools

import jax
import jax.numpy as jnp
import numpy as np
from jax import lax
from jax.experimental import pallas as pl
from jax.experimental.pallas import tpu as pltpu

F32 = jnp.float32
BF16 = jnp.bfloat16

LANES = 128
HEAD_DIM = 64
SWA_HEADS = 8
SWA_GROUP = 4
SWA_BLOCK = 128
RWKV_CHUNK = 64
NORM_EPS = 1e-6
GN_EPS = 64e-5
L2_EPS = 1e-12
LOG2E = float(np.log2(np.e))
VMEM_LIMIT = 56 * 1024 * 1024

NN = (((1,), (0,)), ((), ()))
NT = (((1,), (1,)), ((), ()))
TN = (((0,), (0,)), ((), ()))


def _dot(a, b, dims=NN):
    return lax.dot_general(a, b, dims, preferred_element_type=F32)


def _dot_bf(a, b, dims=NN):
    return _dot(a.astype(BF16), b.astype(BF16), dims)


def _hi_lo(x):
    hi = x.astype(BF16)
    lo = (x - hi.astype(F32)).astype(BF16)
    return hi, lo


def _dot_x3(a, b, dims=NN):
    ah, al = _hi_lo(a)
    bh, bl = _hi_lo(b)
    return _dot(ah, bh, dims) + (_dot(ah, bl, dims) + _dot(al, bh, dims))


def _split3(x):
    hi = x.astype(BF16)
    r1 = x - hi.astype(F32)
    mid = r1.astype(BF16)
    lo = (r1 - mid.astype(F32)).astype(BF16)
    return hi, mid, lo


def _sel_dot(sel, x):
    hi, mid, lo = _split3(x)
    return _dot(sel, hi) + (_dot(sel, mid) + _dot(sel, lo))


def _rms(x, g):
    ms = jnp.mean(x * x, axis=-1, keepdims=True)
    return x * lax.rsqrt(ms + NORM_EPS) * g


def _params(*sem):
    return pltpu.CompilerParams(dimension_semantics=sem, vmem_limit_bytes=VMEM_LIMIT)


LOAD_STEPS = 16


def _weight_chunk(w, layer, rows, block=0):
    chunk = rows // LOAD_STEPS
    assert chunk * LOAD_STEPS == rows and chunk % 16 == 0, (rows, chunk)
    return pl.BlockSpec((None, chunk, w.shape[2]),
                        lambda i: (layer, block * LOAD_STEPS + jnp.minimum(i, LOAD_STEPS - 1), 0))


def _stash(step, chunk_ref, copy_ref):
    rows = chunk_ref.shape[0]
    copy_ref[pl.ds(pl.multiple_of(step * rows, rows), rows), :] = chunk_ref[...].astype(BF16)


def _row_tile(tm, n):
    return pl.BlockSpec((tm, n), lambda i: (jnp.maximum(i - LOAD_STEPS, 0), 0))


def _swiglu_half_step(x, g, wgu_ref, wd_ref):
    dff = wd_ref.shape[0]
    hn = _rms(x, g).astype(BF16)
    gate = _dot(hn, wgu_ref[:, :dff])
    up = _dot(hn, wgu_ref[:, dff:])
    act = (gate * jax.nn.sigmoid(gate) * up).astype(BF16)
    return x + 0.5 * _dot(act, wd_ref[...])


def _ffn_kernel(n_proj, x_ref, g_ref, wgu_ref, wd_ref, *refs):
    n_w = 3 if n_proj else 2
    copies = refs[len(refs) - n_w:]
    refs = refs[:len(refs) - n_w]
    step = pl.program_id(0)

    @pl.when(step < LOAD_STEPS)
    def _():
        _stash(step, wgu_ref, copies[0])
        _stash(step, wd_ref, copies[1])
        if n_proj:
            _stash(step, refs[1], copies[2])

    @pl.when(step >= LOAD_STEPS)
    def _():
        x = _swiglu_half_step(x_ref[...], g_ref[...], copies[0], copies[1])
        if n_proj:
            gm_ref, o_ref, proj_refs = refs[0], refs[2], refs[3:]
            proj = _dot(_rms(x, gm_ref[...]).astype(BF16), copies[2][...])
            col = 0
            for p_ref in proj_refs:
                p_ref[...] = proj[:, col:col + p_ref.shape[1]]
                col += p_ref.shape[1]
        else:
            o_ref = refs[0]
        o_ref[...] = x


def _ffn(x, g, w_gu, w_down, layer, proj=None, *, tm=512):
    t, d = x.shape
    dff = w_down.shape[1]
    vec = pl.BlockSpec((1, d), lambda i: (0, 0))
    in_specs = [_row_tile(tm, d), vec, _weight_chunk(w_gu, layer, d), _weight_chunk(w_down, layer, dff)]
    args = [x, g.reshape(1, d), w_gu, w_down]
    scratch = [pltpu.VMEM((d, 2 * dff), BF16), pltpu.VMEM((dff, d), BF16)]
    out_shape = [jax.ShapeDtypeStruct((t, d), F32)]
    out_specs = [_row_tile(tm, d)]
    widths = ()
    if proj is not None:
        gm, w, w_layer, widths = proj
        assert sum(widths) == w.shape[2]
        in_specs += [vec, _weight_chunk(w, w_layer, d)]
        args += [gm.reshape(1, d), w]
        scratch += [pltpu.VMEM(w.shape[1:], BF16)]
        out_shape += [jax.ShapeDtypeStruct((t, n), F32) for n in widths]
        out_specs += [_row_tile(tm, n) for n in widths]
    return pl.pallas_call(
        functools.partial(_ffn_kernel, len(widths)),
        out_shape=out_shape,
        grid=(LOAD_STEPS + t // tm,),
        in_specs=in_specs,
        out_specs=out_specs,
        scratch_shapes=scratch,
        compiler_params=_params("arbitrary"),
        name="ffn",
    )(*args)


def _post_mix_kernel(n_in, final, *refs):
    a_refs, wo_refs = refs[:n_in], refs[n_in:2 * n_in]
    (x_ref, g2_ref, wgu_ref, wd_ref, gp_ref, wpg_ref, p_ref, wpp_ref, fn_ref, o_ref) = refs[2 * n_in:2 * n_in + 10]
    copies = refs[2 * n_in + 10:]
    wo_copies, (wgu_c, wd_c, wpg_c, wpp_c) = copies[:n_in], copies[n_in:]
    step = pl.program_id(0)

    @pl.when(step < LOAD_STEPS)
    def _():
        for src, dst in zip(wo_refs + (wgu_ref, wd_ref, wpg_ref, wpp_ref), wo_copies + (wgu_c, wd_c, wpg_c, wpp_c)):
            _stash(step, src, dst)

    @pl.when(step >= LOAD_STEPS)
    def _():
        x = x_ref[...]
        for a_ref, w_c in zip(a_refs, wo_copies):
            x = x + _dot(a_ref[...], w_c[...])
        x = _swiglu_half_step(x, g2_ref[...], wgu_c, wd_c)
        gate = jax.nn.sigmoid(_dot(_rms(x, gp_ref[...]).astype(BF16), wpg_c[...]))
        x = x + gate * _dot(p_ref[...].astype(BF16), wpp_c[...])
        if final:
            x = _rms(x, fn_ref[...])
        o_ref[...] = x


def _post_mix(a_list, w_out, out_layer, x, g2, w_gu, w_down, gp, w_gate, p, w_proj, final_g, layer, *, final,
              tm=512):
    t, d = x.shape
    dff = w_down.shape[1]
    vec = pl.BlockSpec((1, d), lambda i: (0, 0))
    width = a_list[0].shape[1]
    assert all(a.shape[1] == width for a in a_list)
    in_specs = [_row_tile(tm, width) for _ in a_list]
    in_specs += [_weight_chunk(w_out, out_layer, width, block=k) for k in range(len(a_list))]
    in_specs += [_row_tile(tm, d), vec, _weight_chunk(w_gu, layer, d), _weight_chunk(w_down, layer, dff), vec,
                 _weight_chunk(w_gate, layer, d),
                 pl.BlockSpec((None, tm, p.shape[2]), lambda i: (layer, jnp.maximum(i - LOAD_STEPS, 0), 0)),
                 _weight_chunk(w_proj, layer, w_proj.shape[1]), vec]
    scratch = [pltpu.VMEM((width, d), BF16) for _ in a_list]
    scratch += [pltpu.VMEM((d, 2 * dff), BF16), pltpu.VMEM((dff, d), BF16), pltpu.VMEM((d, d), BF16),
                pltpu.VMEM(w_proj.shape[1:], BF16)]
    return pl.pallas_call(
        functools.partial(_post_mix_kernel, len(a_list), final),
        out_shape=jax.ShapeDtypeStruct((t, d), F32),
        grid=(LOAD_STEPS + t // tm,),
        in_specs=in_specs,
        out_specs=_row_tile(tm, d),
        scratch_shapes=scratch,
        compiler_params=_params("arbitrary"),
        name="post_mix",
    )(*a_list, *([w_out] * len(a_list)), x, g2.reshape(1, d), w_gu, w_down, gp.reshape(1, d), w_gate, p,
      w_proj, final_g.reshape(1, d))


def _swa_kernel(sink_ref, q_ref, kp_ref, kc_ref, vp_ref, vc_ref, o_ref):
    n = pl.program_id(1)
    blk = SWA_BLOCK
    nsub = q_ref.shape[0] // blk
    scale = HEAD_DIM ** -0.5
    k = jnp.concatenate([kp_ref[...], kc_ref[...]], axis=0)
    v = jnp.concatenate([vp_ref[...], vc_ref[...]], axis=0)
    kr = pltpu.roll(k, HEAD_DIM, 1)
    vr = pltpu.roll(v, HEAD_DIM, 1)
    lo_kv = lax.broadcasted_iota(jnp.int32, k.shape, 1) < HEAD_DIM
    kdup = [jnp.where(lo_kv, k, kr).astype(BF16), jnp.where(lo_kv, kr, k).astype(BF16)]
    vdup = [jnp.where(lo_kv, v, vr).astype(BF16), jnp.where(lo_kv, vr, v).astype(BF16)]

    qi = lax.broadcasted_iota(jnp.int32, (blk, 2 * blk), 0)
    ki = lax.broadcasted_iota(jnp.int32, (blk, 2 * blk), 1)
    dist = qi + blk - ki
    in_window = (dist >= 0) & (dist < blk)
    distf = dist.astype(F32)
    lo_q = lax.broadcasted_iota(jnp.int32, (blk, LANES), 1) < HEAD_DIM

    units = [(u, h) for u in range(nsub) for h in range(SWA_HEADS)]
    keys_of = lambda x, u: x[u * blk:(u + 2) * blk]
    qm = []
    for u, h in units:
        j, e = divmod(h, 2)
        q2 = q_ref[u * blk:(u + 1) * blk, j * LANES:(j + 1) * LANES] * (scale * LOG2E)
        qm.append(jnp.where(lo_q if e == 0 else ~lo_q, q2, 0.0).astype(BF16))
    logits = [_dot(qi_, keys_of(kdup[h // SWA_GROUP], u), NT) for qi_, (u, h) in zip(qm, units)]
    ps, inv_denoms = [], []
    for s, (u, h) in zip(logits, units):
        valid = in_window & ((n > 0) | (ki >= blk)) if u == 0 else in_window
        slope = 2.0 ** (-8.0 * (h + 1) / SWA_HEADS) * LOG2E
        s = jnp.where(valid, s - slope * distf, -jnp.inf)
        sink = sink_ref[h] * LOG2E
        m = jnp.maximum(jnp.max(s, axis=-1, keepdims=True), sink)
        p = jnp.exp2(s - m)
        inv_denoms.append(1.0 / (jnp.sum(p, axis=-1, keepdims=True) + jnp.exp2(sink - m)))
        ps.append(p.astype(BF16))
    outs = [_dot(p, keys_of(vdup[h // SWA_GROUP], u)) * inv
            for p, inv, (u, h) in zip(ps, inv_denoms, units)]
    for u in range(nsub):
        for j in range(SWA_HEADS // 2):
            pair = jnp.where(lo_q, outs[u * SWA_HEADS + 2 * j], outs[u * SWA_HEADS + 2 * j + 1])
            o_ref[u * blk:(u + 1) * blk, j * LANES:(j + 1) * LANES] = pair.astype(o_ref.dtype)


def _swa(qkv, sinks, *, nsub=2):
    b, s, _ = qkv.shape
    blk = SWA_BLOCK
    tq = nsub * blk
    nq = SWA_HEADS * HEAD_DIM
    kcol = nq // LANES
    vcol = kcol + 1
    prev = lambda n: jnp.maximum(nsub * n - 1, 0)
    return pl.pallas_call(
        _swa_kernel,
        out_shape=jax.ShapeDtypeStruct((b, s, nq), BF16),
        grid=(b, s // tq),
        in_specs=[
            pl.BlockSpec(memory_space=pltpu.SMEM),
            pl.BlockSpec((None, tq, nq), lambda bi, n: (bi, n, 0)),
            pl.BlockSpec((None, blk, LANES), lambda bi, n: (bi, prev(n), kcol)),
            pl.BlockSpec((None, tq, LANES), lambda bi, n: (bi, n, kcol)),
            pl.BlockSpec((None, blk, LANES), lambda bi, n: (bi, prev(n), vcol)),
            pl.BlockSpec((None, tq, LANES), lambda bi, n: (bi, n, vcol)),
        ],
        out_specs=pl.BlockSpec((None, tq, nq), lambda bi, n: (bi, n, 0)),
        compiler_params=_params("parallel", "arbitrary"),
        name="swa",
    )(sinks, qkv, qkv, qkv, qkv, qkv)


def _tri_inverse_minus_eye(lows, ri, ci):
    same = lambda w: (ri ^ ci) < w
    base = 8
    x = [jnp.where(same(base), -low, 0.0) for low in lows]
    p2 = [_dot_bf(xi, xi) for xi in x]
    e = [xi + pi + _dot_bf(xi, pi) for xi, pi in zip(x, p2)]
    p4 = [_dot_bf(pi, pi) for pi in p2]
    e = [ei + pi + _dot_bf(ei, pi) for ei, pi in zip(e, p4)]
    w = base * 2
    while w <= RWKV_CHUNK:
        off = [jnp.where(same(w) & ~same(w // 2), low, 0.0) for low in lows]
        wm = [oi + _dot_bf(ei, oi) for ei, oi in zip(e, off)]
        e = [ei - wi - _dot_bf(wi, ei) for ei, wi in zip(e, wm)]
        w *= 2
    return e


def _rwkv_kernel(h_ref, mu_ref, w0_ref, w2_ref, a0_ref, a2_ref, g2_ref, kk_ref, ka_ref,
                 rk_ref, lnw_ref, lnb_ref, o_ref, state_ref, last_ref):
    c = RWKV_CHUNK
    tb = h_ref.shape[0]
    nchunk = tb // c
    dim = o_ref.shape[-1]
    npair = dim // LANES

    @pl.when(pl.program_id(1) == 0)
    def _():
        state_ref[...] = jnp.zeros_like(state_ref)
        last_ref[...] = jnp.zeros_like(last_ref)

    h = h_ref[...]
    row = lax.broadcasted_iota(jnp.int32, h.shape, 0)
    shifted = jnp.where(row == 0, last_ref[...], pltpu.roll(h, 1, 0))
    last_ref[...] = h[tb - 1:tb, :]
    hs = h + (shifted - h) * mu_ref[...]
    r = hs[:, 0:dim]
    k = hs[:, dim:2 * dim]
    v = hs[:, 2 * dim:3 * dim]
    xwa = hs[:, 3 * dim:3 * dim + LANES]
    xg = hs[:, 3 * dim + LANES:3 * dim + 2 * LANES]

    wl = w0_ref[...] + _dot_x3(jnp.tanh(xwa), w2_ref[...])
    logw = -jax.nn.sigmoid(wl) * float(np.exp(-0.5))
    a = jax.nn.sigmoid(a0_ref[...] + _dot_x3(xwa, a2_ref[...]))
    gate = _dot_bf(jax.nn.sigmoid(xg), g2_ref[...])

    ri = lax.broadcasted_iota(jnp.int32, (LANES, LANES), 0)
    ci = lax.broadcasted_iota(jnp.int32, (LANES, LANES), 1)
    ones_bd = ((ri ^ ci) < HEAD_DIM).astype(BF16)
    ones_bd2 = jnp.concatenate([ones_bd, ones_bd], axis=0)

    def head_sum(x):
        cols = []
        for j in range(npair):
            hi, lo = _hi_lo(x[:, j * LANES:(j + 1) * LANES])
            cols.append(_dot(jnp.concatenate([hi, lo], axis=1), ones_bd2))
        return jnp.concatenate(cols, axis=1)

    kk = k * kk_ref[...]
    kk = kk * jnp.minimum(lax.rsqrt(head_sum(kk * kk)), 1.0 / L2_EPS)
    k2 = k * (1.0 + (a - 1.0) * ka_ref[...])
    bvec = kk * a

    ti = lax.broadcasted_iota(jnp.int32, (tb, tb), 0)
    si = lax.broadcasted_iota(jnp.int32, (tb, tb), 1)
    tri = (((ti ^ si) < c) & (si <= ti)).astype(BF16)
    cum = _dot(jnp.concatenate([tri] * 3, axis=1), jnp.concatenate(_split3(logw), axis=0))
    e_neg = jnp.exp(-cum)
    alpha = kk * jnp.exp(cum - logw)
    beta = bvec * e_neg
    kappa = k2 * e_neg
    rho = r * jnp.exp(cum)
    cum_end = jnp.concatenate(
        [jnp.broadcast_to(cum[(n + 1) * c - 1:(n + 1) * c, :], (c, dim)) for n in range(nchunk)], axis=0)
    to_end = jnp.exp(cum_end - cum)
    beta_e = bvec * to_end
    kappa_e = k2 * to_end
    w_end = jnp.exp(cum_end)

    same_head = (ri ^ ci) < c
    strict = same_head & (ci < ri)
    incl = same_head & (ci <= ri)
    eye = ri == ci
    lo = lax.broadcasted_iota(jnp.int32, (c, LANES), 1) < HEAD_DIM

    def stack(x2):
        return jnp.concatenate([jnp.where(lo, x2, 0.0), jnp.where(lo, 0.0, x2)], axis=0)

    units = [(n, j) for n in range(nchunk) for j in range(npair)]
    blk = lambda x, n, j: x[n * c:(n + 1) * c, j * LANES:(j + 1) * LANES]
    a_s = [stack(blk(alpha, n, j)) for n, j in units]
    rho_s = [stack(blk(rho, n, j)) for n, j in units]
    v_s = [stack(blk(v, n, j)) for n, j in units]
    ends = [jnp.concatenate([stack(blk(beta_e, n, j)), stack(blk(kappa_e, n, j))], axis=0) for n, j in units]
    bk = [jnp.concatenate([blk(beta, n, j)] * 2 + [blk(kappa, n, j)] * 2, axis=0) for n, j in units]
    sc = [_dot_bf(jnp.concatenate([ai, ri_], axis=0), bi, NT) for ai, ri_, bi in zip(a_s, rho_s, bk)]
    l_ab = [jnp.where(strict, s[:LANES, :LANES], 0.0) for s in sc]
    l_ak = [jnp.where(strict, s[:LANES, LANES:], 0.0) for s in sc]
    r_b = [jnp.where(incl, s[LANES:, :LANES], 0.0) for s in sc]
    r_k = [jnp.where(incl, s[LANES:, LANES:], 0.0) for s in sc]
    e_inv = _tri_inverse_minus_eye(l_ab, ri, ci)
    lkv = [_dot_bf(li, vi) for li, vi in zip(l_ak, v_s)]
    p_m = [-(ai + _dot_bf(ei, ai)) for ei, ai in zip(e_inv, a_s)]
    q_m = [-(xi + _dot_bf(ei, xi)) for ei, xi in zip(e_inv, lkv)]
    m_m = [jnp.where(eye, blk(w_end, n, j)[:1, :], 0.0) + _dot_bf(pi, ei[:LANES], TN)
           for (n, j), pi, ei in zip(units, p_m, ends)]
    n_m = [_dot_bf(jnp.concatenate([qi, vi], axis=0), ei, TN) for qi, vi, ei in zip(q_m, v_s, ends)]
    g_m = [ri_ + _dot_bf(rb, pi) for ri_, rb, pi in zip(rho_s, r_b, p_m)]
    h_m = [_dot_bf(jnp.concatenate([rb, rk], axis=1), jnp.concatenate([qi, vi], axis=0))
           for rb, rk, qi, vi in zip(r_b, r_k, q_m, v_s)]

    state = [state_ref[j] for j in range(npair)]
    ys = []
    for n in range(nchunk):
        idx = [n * npair + j for j in range(npair)]
        nxt = [_dot_bf(state[j], m_m[i]) + n_m[i] for j, i in enumerate(idx)]
        y = [_dot_bf(g_m[i], state[j], NT) + h_m[i] for j, i in enumerate(idx)]
        ys.append(jnp.concatenate([yi[:c] + yi[c:] for yi in y], axis=1))
        state = nxt
    for j in range(npair):
        state_ref[j] = state[j]
    y = jnp.concatenate(ys, axis=0)

    mean = head_sum(y) * (1.0 / HEAD_DIM)
    d = y - mean
    var = head_sum(d * d) * (1.0 / HEAD_DIM)
    y = d * lax.rsqrt(var + GN_EPS) * lnw_ref[...] + lnb_ref[...]
    y = y + head_sum(r * k2 * rk_ref[...]) * v
    o_ref[...] = (y * gate).astype(o_ref.dtype)


def _rwkv(hb, mu, w0, w2p, a0, a2p, g2, k_k, k_a, r_k, ln_w, ln_b, *, tb=4 * RWKV_CHUNK):
    b, s, cols = hb.shape
    dim = w0.shape[-1]
    c = tb
    row = lambda x: x.reshape(1, -1)
    vec = lambda n: pl.BlockSpec((1, n), lambda bi, t: (0, 0))
    mat = lambda m: pl.BlockSpec(m.shape, lambda bi, t: (0, 0))
    return pl.pallas_call(
        _rwkv_kernel,
        out_shape=jax.ShapeDtypeStruct((b, s, dim), BF16),
        grid=(b, s // c),
        in_specs=[
            pl.BlockSpec((None, c, cols), lambda bi, t: (bi, t, 0)),
            vec(cols), vec(dim), mat(w2p), vec(dim), mat(a2p), mat(g2),
            vec(dim), vec(dim), vec(dim), vec(dim), vec(dim),
        ],
        out_specs=pl.BlockSpec((None, c, dim), lambda bi, t: (bi, t, 0)),
        scratch_shapes=[pltpu.VMEM((dim // LANES, LANES, LANES), F32), pltpu.VMEM((1, cols), F32)],
        compiler_params=_params("parallel", "arbitrary"),
        name="rwkv7",
    )(hb, row(mu), row(w0), w2p, row(a0), a2p, g2, row(k_k), row(k_a), row(r_k), row(ln_w), row(ln_b))


FOX_AUG = 3
FOX_BLOCK = 512


def _fox_in_kernel(x_ref, g_ref, w_ref, wvt_ref, bf_ref, sel_ref, qa_ref, ka_ref, va_ref, carry_ref):
    tc = x_ref.shape[0]
    nheads = qa_ref.shape[-1] // LANES
    dim = nheads * HEAD_DIM
    scale = HEAD_DIM ** -0.5 * LOG2E

    @pl.when(pl.program_id(1) == 0)
    def _():
        carry_ref[...] = jnp.zeros_like(carry_ref)

    hn = _rms(x_ref[...], g_ref[...]).astype(BF16)
    z = _dot(hn, w_ref[:, 2 * dim:]) + bf_ref[...]
    proj_q = _dot(hn, w_ref[:, :dim])
    logf = jnp.minimum(z, 0.0) - jnp.log(1.0 + jnp.exp(-jnp.abs(z)))
    lane = lax.broadcasted_iota(jnp.int32, z.shape, 1)
    logf = jnp.where(lane < nheads, logf, 0.0)
    ti = lax.broadcasted_iota(jnp.int32, (tc, tc), 0)
    si = lax.broadcasted_iota(jnp.int32, (tc, tc), 1)
    cg = _sel_dot((si <= ti).astype(BF16), logf) + carry_ref[...]
    carry_ref[...] = cg[tc - 1:tc, :]

    hi, mid, low = _split3(cg * LOG2E)
    pieces = (hi.astype(F32) + pltpu.roll(mid.astype(F32), nheads, 1)
              + pltpu.roll(low.astype(F32), 2 * nheads, 1)
              + (lane == FOX_AUG * nheads).astype(F32)).astype(BF16)
    q_c = _dot(pieces, sel_ref[0])
    k_c = _dot(pieces, sel_ref[1])

    lane = lax.broadcasted_iota(jnp.int32, (tc, LANES), 1)
    lo = lane < HEAD_DIM

    def assemble(proj, c_aug, o_ref):
        for j in range(nheads // 2):
            x = proj[:, j * LANES:(j + 1) * LANES]
            xr = pltpu.roll(x, HEAD_DIM, 1)
            for e, xe in ((0, x), (1, xr)):
                hs = slice((2 * j + e) * LANES, (2 * j + e + 1) * LANES)
                o_ref[:, hs] = jnp.where(lo, xe, c_aug[:, hs]).astype(BF16)

    proj_k = _dot(hn, w_ref[:, dim:2 * dim])
    assemble(proj_q * scale, q_c, qa_ref)
    v_t = _dot(wvt_ref[...], hn, NT)
    assemble(proj_k, k_c, ka_ref)
    one_rows = (lax.broadcasted_iota(jnp.int32, (LANES - HEAD_DIM, tc), 0) == 0).astype(F32)
    for h in range(nheads):
        va_ref[h, 0] = jnp.concatenate([v_t[h * HEAD_DIM:(h + 1) * HEAD_DIM], one_rows], axis=0).astype(BF16)


def _fox_select_matrices(nheads):
    sel = np.zeros((2, LANES, nheads * LANES), np.float32)
    one_row = FOX_AUG * nheads
    for h in range(nheads):
        for i in range(FOX_AUG):
            sel[0, i * nheads + h, h * LANES + HEAD_DIM + i] = 1.0
            sel[0, one_row, h * LANES + HEAD_DIM + FOX_AUG + i] = 1.0
            sel[1, one_row, h * LANES + HEAD_DIM + i] = 1.0
            sel[1, i * nheads + h, h * LANES + HEAD_DIM + FOX_AUG + i] = -1.0
    return jnp.asarray(sel, BF16)


def _fox_in(x, g, w_in, b_f, *, batch, tc):
    t, d = x.shape
    s = t // batch
    nheads = b_f.shape[-1]
    assert FOX_AUG * nheads < LANES
    dim = nheads * HEAD_DIM
    w = jnp.concatenate([w_in[:, :2 * dim], jnp.pad(w_in[:, 3 * dim:], ((0, 0), (0, LANES - nheads)))],
                        axis=1).astype(BF16)
    w_vt = w_in[:, 2 * dim:3 * dim].T.astype(BF16)
    bf = jnp.zeros((1, LANES), F32).at[0, :nheads].set(b_f)
    sel = _fox_select_matrices(nheads)
    wide = nheads * LANES
    nt = s // tc
    resident = pl.Buffered(1)
    return pl.pallas_call(
        _fox_in_kernel,
        out_shape=(jax.ShapeDtypeStruct((batch, s, wide), BF16),
                   jax.ShapeDtypeStruct((batch, s, wide), BF16),
                   jax.ShapeDtypeStruct((batch, nheads, nt, LANES, tc), BF16)),
        grid=(batch, nt),
        in_specs=[
            pl.BlockSpec((tc, d), lambda bi, ti: (bi * nt + ti, 0)),
            pl.BlockSpec((1, d), lambda bi, ti: (0, 0)),
            pl.BlockSpec(w.shape, lambda bi, ti: (0, 0), pipeline_mode=resident),
            pl.BlockSpec(w_vt.shape, lambda bi, ti: (0, 0), pipeline_mode=resident),
            pl.BlockSpec((1, LANES), lambda bi, ti: (0, 0)),
            pl.BlockSpec(sel.shape, lambda bi, ti: (0, 0, 0), pipeline_mode=resident),
        ],
        out_specs=(pl.BlockSpec((None, tc, wide), lambda bi, ti: (bi, ti, 0)),
                   pl.BlockSpec((None, tc, wide), lambda bi, ti: (bi, ti, 0)),
                   pl.BlockSpec((None, nheads, 1, LANES, tc), lambda bi, ti: (bi, 0, ti, 0, 0))),
        scratch_shapes=[pltpu.VMEM((1, LANES), F32)],
        compiler_params=_params("parallel", "arbitrary"),
        name="fox_in",
    )(x, g.reshape(1, d), w, w_vt, bf, sel)


def _fox_attn_kernel(q_ref, k_ref, v_ref, o_ref, m_ref, acc_ref, sa_ref, sb_ref):
    tq = o_ref.shape[0] // 2
    nh = q_ref.shape[1] // LANES
    step = pl.program_id(2)
    last_q = 2 * pl.num_programs(2) - 1
    keys = lax.broadcasted_iota(jnp.int32, (tq, tq), 0)
    queries = lax.broadcasted_iota(jnp.int32, (tq, tq), 1)

    def logits_into(dst_ref, qb, kb, heads=None):
        q0 = pl.multiple_of(qb * tq, tq)
        k0 = pl.multiple_of(kb * tq, tq)
        for e in range(nh) if heads is None else heads:
            hs = slice(e * LANES, (e + 1) * LANES)
            dst_ref[e] = _dot(k_ref[pl.ds(k0, tq), hs], q_ref[pl.ds(q0, tq), hs], NT)

    def consume(src_ref, kb, diagonal, after_head=None):
        for e in range(nh):
            s = src_ref[e]
            if diagonal:
                s = jnp.where(keys <= queries, s, -jnp.inf)
            m_prev = m_ref[e]
            m_new = jnp.maximum(m_prev, jnp.max(s, axis=0, keepdims=True))
            p = jnp.exp2(s - m_new).astype(BF16)
            acc_ref[e] = jnp.exp2(m_prev - m_new) * acc_ref[e] + _dot(v_ref[e, kb], p)
            m_ref[e] = m_new
            if after_head is not None:
                after_head(e)

    @pl.when(step == 0)
    def _():
        logits_into(sa_ref, 0, 0)

    for half in range(2):
        qi = 2 * step + half
        m_ref[...] = jnp.full_like(m_ref, -jnp.inf)
        acc_ref[...] = jnp.zeros_like(acc_ref)

        def two_blocks(t, carry, qi=qi):
            kb = 2 * t
            logits_into(sb_ref, qi, kb + 1)
            consume(sa_ref, kb, False)
            logits_into(sa_ref, qi, kb + 2)
            consume(sb_ref, kb + 1, False)
            return carry

        lax.fori_loop(0, step, two_blocks, 0)
        if half == 0:
            consume(sa_ref, qi, True, after_head=lambda e, qi=qi: logits_into(sa_ref, qi + 1, 0, heads=(e,)))
        else:
            nxt = jnp.minimum(qi + 1, last_q)
            logits_into(sb_ref, qi, qi)
            consume(sa_ref, qi - 1, False)
            logits_into(sa_ref, nxt, 0)
            consume(sb_ref, qi, True)

        for pair in range(nh // 2):
            outs = []
            for e in (2 * pair, 2 * pair + 1):
                acc = acc_ref[e]
                outs.append(acc[:HEAD_DIM] / acc[HEAD_DIM:HEAD_DIM + 1])
            o_ref[half * tq:(half + 1) * tq, pair * LANES:(pair + 1) * LANES] = (
                jnp.concatenate(outs, axis=0).T.astype(o_ref.dtype))


def _fox_attn(q_aug, k_aug, v_aug, *, tq, nh=4):
    b, s, wide = q_aug.shape
    nheads = wide // LANES
    assert (s // tq) % 2 == 0
    return pl.pallas_call(
        _fox_attn_kernel,
        out_shape=jax.ShapeDtypeStruct((b, s, nheads * HEAD_DIM), BF16),
        grid=(b, nheads // nh, s // (2 * tq)),
        in_specs=[
            pl.BlockSpec((None, s, nh * LANES), lambda bi, g, qi: (bi, 0, g)),
            pl.BlockSpec((None, s, nh * LANES), lambda bi, g, qi: (bi, 0, g)),
            pl.BlockSpec((None, nh, s // tq, LANES, tq), lambda bi, g, qi: (bi, g, 0, 0, 0)),
        ],
        out_specs=pl.BlockSpec((None, 2 * tq, nh * HEAD_DIM), lambda bi, g, qi: (bi, qi, g)),
        scratch_shapes=[pltpu.VMEM((nh, 1, tq), F32), pltpu.VMEM((nh, LANES, tq), F32),
                        pltpu.VMEM((nh, tq, tq), F32), pltpu.VMEM((nh, tq, tq), F32)],
        compiler_params=_params("parallel", "parallel", "arbitrary"),
        name="fox_attn",
    )(q_aug, k_aug, v_aug)


def kernel(x, p, ffn1_norm, ffn1_w_gu, ffn1_w_down, mix_norm, ffn2_norm, ffn2_w_gu, ffn2_w_down, ple_norm, ple_w_gate, ple_w_proj, even_w_in, even_w_out, swa_sinks, rwkv_mu, rwkv_w0, rwkv_w2, rwkv_a0, rwkv_a2, rwkv_g2, rwkv_k_k, rwkv_k_a, rwkv_r_k, rwkv_ln_w, rwkv_ln_b, fox_w_in, fox_b_f, fox_w_out, final_norm):
    b, s, d = x.shape
    depth = p.shape[0]
    t = b * s
    bf = lambda w: w.astype(BF16)
    swa_q = SWA_HEADS * HEAD_DIM
    swa_cols = swa_q + 2 * (SWA_HEADS // SWA_GROUP) * HEAD_DIM
    rwkv_dim = rwkv_w0.shape[-1]
    lora = rwkv_w2.shape[1]
    fox_heads = fox_b_f.shape[-1]
    fox_dim = fox_heads * HEAD_DIM

    x = x.reshape(t, d)
    for i in range(depth):
        j = i // 2
        if i % 2 == 0:
            widths = (swa_cols, even_w_in.shape[-1] - swa_cols)
            x, qkv, hb = _ffn(x, ffn1_norm[i], ffn1_w_gu, ffn1_w_down, i, proj=(mix_norm[i], even_w_in, j, widths))
            ya = _swa(qkv.reshape(b, s, swa_cols), swa_sinks[j])
            zeros = jnp.zeros((lora, rwkv_dim), F32)
            w2p = jnp.concatenate([rwkv_w2[j], zeros], axis=0)
            a2p = jnp.concatenate([zeros, rwkv_a2[j]], axis=0)
            yb = _rwkv(hb.reshape(b, s, -1), rwkv_mu[j], rwkv_w0[j], w2p, rwkv_a0[j], a2p,
                       bf(rwkv_g2[j]), rwkv_k_k[j], rwkv_k_a[j], rwkv_r_k[j].reshape(-1),
                       rwkv_ln_w[j], rwkv_ln_b[j])
            mixed = ([ya.reshape(t, swa_q), yb.reshape(t, rwkv_dim)], even_w_out, j)
        else:
            (x,) = _ffn(x, ffn1_norm[i], ffn1_w_gu, ffn1_w_down, i)
            q_aug, k_aug, v_aug = _fox_in(x, mix_norm[i], fox_w_in[j], fox_b_f[j], batch=b, tc=FOX_BLOCK)
            yc = _fox_attn(q_aug, k_aug, v_aug, tq=FOX_BLOCK)
            mixed = ([yc.reshape(t, fox_dim)], fox_w_out, j)
        x = _post_mix(*mixed, x, ffn2_norm[i], ffn2_w_gu, ffn2_w_down, ple_norm[i], ple_w_gate,
                      p.reshape(depth, t, -1), ple_w_proj, final_norm, i, final=(i == depth - 1))
    return x.reshape(b, s, d)
```

```python
import functools

import jax
import jax.numpy as jnp
import numpy as np
from jax import lax
from jax.experimental import pallas as pl
from jax.experimental.pallas import tpu as pltpu

F32 = jnp.float32
BF16 = jnp.bfloat16

LANES = 128
HEAD_DIM = 64
SWA_HEADS = 8
SWA_GROUP = 4
SWA_BLOCK = 128
RWKV_CHUNK = 64
NORM_EPS = 1e-6
GN_EPS = 64e-5
L2_EPS = 1e-12
LOG2E = float(np.log2(np.e))
VMEM_LIMIT = 56 * 1024 * 1024

NN = (((1,), (0,)), ((), ()))
NT = (((1,), (1,)), ((), ()))
TN = (((0,), (0,)), ((), ()))


def _dot(a, b, dims=NN):
    return lax.dot_general(a, b, dims, preferred_element_type=F32)


def _dot_bf(a, b, dims=NN):
    return _dot(a.astype(BF16), b.astype(BF16), dims)


def _hi_lo(x):
    hi = x.astype(BF16)
    lo = (x - hi.astype(F32)).astype(BF16)
    return hi, lo


def _dot_x3(a, b, dims=NN):
    ah, al = _hi_lo(a)
    bh, bl = _hi_lo(b)
    return _dot(ah, bh, dims) + (_dot(ah, bl, dims) + _dot(al, bh, dims))


def _split3(x):
    hi = x.astype(BF16)
    r1 = x - hi.astype(F32)
    mid = r1.astype(BF16)
    lo = (r1 - mid.astype(F32)).astype(BF16)
    return hi, mid, lo


def _sel_dot(sel, x):
    hi, mid, lo = _split3(x)
    return _dot(sel, hi) + (_dot(sel, mid) + _dot(sel, lo))


def _rms(x, g):
    ms = jnp.mean(x * x, axis=-1, keepdims=True)
    return x * lax.rsqrt(ms + NORM_EPS) * g


def _params(*sem):
    return pltpu.CompilerParams(dimension_semantics=sem, vmem_limit_bytes=VMEM_LIMIT)


LOAD_STEPS = 16


def _weight_chunk(w, layer, rows, block=0):
    chunk = rows // LOAD_STEPS
    assert chunk * LOAD_STEPS == rows and chunk % 16 == 0, (rows, chunk)
    return pl.BlockSpec((None, chunk, w.shape[2]),
                        lambda i: (layer, block * LOAD_STEPS + jnp.minimum(i, LOAD_STEPS - 1), 0))


def _stash(step, chunk_ref, copy_ref):
    rows = chunk_ref.shape[0]
    copy_ref[pl.ds(pl.multiple_of(step * rows, rows), rows), :] = chunk_ref[...].astype(BF16)


def _row_tile(tm, n):
    return pl.BlockSpec((tm, n), lambda i: (jnp.maximum(i - LOAD_STEPS, 0), 0))


def _swiglu_half_step(x, g, wgu_ref, wd_ref):
    dff = wd_ref.shape[0]
    hn = _rms(x, g).astype(BF16)
    gate = _dot(hn, wgu_ref[:, :dff])
    up = _dot(hn, wgu_ref[:, dff:])
    act = (gate * jax.nn.sigmoid(gate) * up).astype(BF16)
    return x + 0.5 * _dot(act, wd_ref[...])


def _ffn_kernel(n_proj, x_ref, g_ref, wgu_ref, wd_ref, *refs):
    n_w = 3 if n_proj else 2
    copies = refs[len(refs) - n_w:]
    refs = refs[:len(refs) - n_w]
    step = pl.program_id(0)

    @pl.when(step < LOAD_STEPS)
    def _():
        _stash(step, wgu_ref, copies[0])
        _stash(step, wd_ref, copies[1])
        if n_proj:
            _stash(step, refs[1], copies[2])

    @pl.when(step >= LOAD_STEPS)
    def _():
        x = _swiglu_half_step(x_ref[...], g_ref[...], copies[0], copies[1])
        if n_proj:
            gm_ref, o_ref, proj_refs = refs[0], refs[2], refs[3:]
            proj = _dot(_rms(x, gm_ref[...]).astype(BF16), copies[2][...])
            col = 0
            for p_ref in proj_refs:
                p_ref[...] = proj[:, col:col + p_ref.shape[1]]
                col += p_ref.shape[1]
        else:
            o_ref = refs[0]
        o_ref[...] = x


def _ffn(x, g, w_gu, w_down, layer, proj=None, *, tm=512):
    t, d = x.shape
    dff = w_down.shape[1]
    vec = pl.BlockSpec((1, d), lambda i: (0, 0))
    in_specs = [_row_tile(tm, d), vec, _weight_chunk(w_gu, layer, d), _weight_chunk(w_down, layer, dff)]
    args = [x, g.reshape(1, d), w_gu, w_down]
    scratch = [pltpu.VMEM((d, 2 * dff), BF16), pltpu.VMEM((dff, d), BF16)]
    out_shape = [jax.ShapeDtypeStruct((t, d), F32)]
    out_specs = [_row_tile(tm, d)]
    widths = ()
    if proj is not None:
        gm, w, w_layer, widths = proj
        assert sum(widths) == w.shape[2]
        in_specs += [vec, _weight_chunk(w, w_layer, d)]
        args += [gm.reshape(1, d), w]
        scratch += [pltpu.VMEM(w.shape[1:], BF16)]
        out_shape += [jax.ShapeDtypeStruct((t, n), F32) for n in widths]
        out_specs += [_row_tile(tm, n) for n in widths]
    return pl.pallas_call(
        functools.partial(_ffn_kernel, len(widths)),
        out_shape=out_shape,
        grid=(LOAD_STEPS + t // tm,),
        in_specs=in_specs,
        out_specs=out_specs,
        scratch_shapes=scratch,
        compiler_params=_params("arbitrary"),
        name="ffn",
    )(*args)


def _post_mix_kernel(n_in, final, *refs):
    a_refs, wo_refs = refs[:n_in], refs[n_in:2 * n_in]
    (x_ref, g2_ref, wgu_ref, wd_ref, gp_ref, wpg_ref, p_ref, wpp_ref, fn_ref, o_ref) = refs[2 * n_in:2 * n_in + 10]
    copies = refs[2 * n_in + 10:]
    wo_copies, (wgu_c, wd_c, wpg_c, wpp_c) = copies[:n_in], copies[n_in:]
    step = pl.program_id(0)

    @pl.when(step < LOAD_STEPS)
    def _():
        for src, dst in zip(wo_refs + (wgu_ref, wd_ref, wpg_ref, wpp_ref), wo_copies + (wgu_c, wd_c, wpg_c, wpp_c)):
            _stash(step, src, dst)

    @pl.when(step >= LOAD_STEPS)
    def _():
        x = x_ref[...]
        for a_ref, w_c in zip(a_refs, wo_copies):
            x = x + _dot(a_ref[...], w_c[...])
        x = _swiglu_half_step(x, g2_ref[...], wgu_c, wd_c)
        gate = jax.nn.sigmoid(_dot(_rms(x, gp_ref[...]).astype(BF16), wpg_c[...]))
        x = x + gate * _dot(p_ref[...].astype(BF16), wpp_c[...])
        if final:
            x = _rms(x, fn_ref[...])
        o_ref[...] = x


def _post_mix(a_list, w_out, out_layer, x, g2, w_gu, w_down, gp, w_gate, p, w_proj, final_g, layer, *, final,
              tm=512):
    t, d = x.shape
    dff = w_down.shape[1]
    vec = pl.BlockSpec((1, d), lambda i: (0, 0))
    width = a_list[0].shape[1]
    assert all(a.shape[1] == width for a in a_list)
    in_specs = [_row_tile(tm, width) for _ in a_list]
    in_specs += [_weight_chunk(w_out, out_layer, width, block=k) for k in range(len(a_list))]
    in_specs += [_row_tile(tm, d), vec, _weight_chunk(w_gu, layer, d), _weight_chunk(w_down, layer, dff), vec,
                 _weight_chunk(w_gate, layer, d),
                 pl.BlockSpec((None, tm, p.shape[2]), lambda i: (layer, jnp.maximum(i - LOAD_STEPS, 0), 0)),
                 _weight_chunk(w_proj, layer, w_proj.shape[1]), vec]
    scratch = [pltpu.VMEM((width, d), BF16) for _ in a_list]
    scratch += [pltpu.VMEM((d, 2 * dff), BF16), pltpu.VMEM((dff, d), BF16), pltpu.VMEM((d, d), BF16),
                pltpu.VMEM(w_proj.shape[1:], BF16)]
    return pl.pallas_call(
        functools.partial(_post_mix_kernel, len(a_list), final),
        out_shape=jax.ShapeDtypeStruct((t, d), F32),
        grid=(LOAD_STEPS + t // tm,),
        in_specs=in_specs,
        out_specs=_row_tile(tm, d),
        scratch_shapes=scratch,
        compiler_params=_params("arbitrary"),
        name="post_mix",
    )(*a_list, *([w_out] * len(a_list)), x, g2.reshape(1, d), w_gu, w_down, gp.reshape(1, d), w_gate, p,
      w_proj, final_g.reshape(1, d))


def _swa_kernel(sink_ref, q_ref, kp_ref, kc_ref, vp_ref, vc_ref, o_ref):
    n = pl.program_id(1)
    blk = SWA_BLOCK
    nsub = q_ref.shape[0] // blk
    scale = HEAD_DIM ** -0.5
    k = jnp.concatenate([kp_ref[...], kc_ref[...]], axis=0)
    v = jnp.concatenate([vp_ref[...], vc_ref[...]], axis=0)
    kr = pltpu.roll(k, HEAD_DIM, 1)
    vr = pltpu.roll(v, HEAD_DIM, 1)
    lo_kv = lax.broadcasted_iota(jnp.int32, k.shape, 1) < HEAD_DIM
    kdup = [jnp.where(lo_kv, k, kr).astype(BF16), jnp.where(lo_kv, kr, k).astype(BF16)]
    vdup = [jnp.where(lo_kv, v, vr).astype(BF16), jnp.where(lo_kv, vr, v).astype(BF16)]

    qi = lax.broadcasted_iota(jnp.int32, (blk, 2 * blk), 0)
    ki = lax.broadcasted_iota(jnp.int32, (blk, 2 * blk), 1)
    dist = qi + blk - ki
    in_window = (dist >= 0) & (dist < blk)
    distf = dist.astype(F32)
    lo_q = lax.broadcasted_iota(jnp.int32, (blk, LANES), 1) < HEAD_DIM

    units = [(u, h) for u in range(nsub) for h in range(SWA_HEADS)]
    keys_of = lambda x, u: x[u * blk:(u + 2) * blk]
    qm = []
    for u, h in units:
        j, e = divmod(h, 2)
        q2 = q_ref[u * blk:(u + 1) * blk, j * LANES:(j + 1) * LANES] * (scale * LOG2E)
        qm.append(jnp.where(lo_q if e == 0 else ~lo_q, q2, 0.0).astype(BF16))
    logits = [_dot(qi_, keys_of(kdup[h // SWA_GROUP], u), NT) for qi_, (u, h) in zip(qm, units)]
    ps, inv_denoms = [], []
    for s, (u, h) in zip(logits, units):
        valid = in_window & ((n > 0) | (ki >= blk)) if u == 0 else in_window
        slope = 2.0 ** (-8.0 * (h + 1) / SWA_HEADS) * LOG2E
        s = jnp.where(valid, s - slope * distf, -jnp.inf)
        sink = sink_ref[h] * LOG2E
        m = jnp.maximum(jnp.max(s, axis=-1, keepdims=True), sink)
        p = jnp.exp2(s - m)
        inv_denoms.append(1.0 / (jnp.sum(p, axis=-1, keepdims=True) + jnp.exp2(sink - m)))
        ps.append(p.astype(BF16))
    outs = [_dot(p, keys_of(vdup[h // SWA_GROUP], u)) * inv
            for p, inv, (u, h) in zip(ps, inv_denoms, units)]
    for u in range(nsub):
        for j in range(SWA_HEADS // 2):
            pair = jnp.where(lo_q, outs[u * SWA_HEADS + 2 * j], outs[u * SWA_HEADS + 2 * j + 1])
            o_ref[u * blk:(u + 1) * blk, j * LANES:(j + 1) * LANES] = pair.astype(o_ref.dtype)


def _swa(qkv, sinks, *, nsub=4):
    b, s, _ = qkv.shape
    blk = SWA_BLOCK
    tq = nsub * blk
    nq = SWA_HEADS * HEAD_DIM
    kcol = nq // LANES
    vcol = kcol + 1
    prev = lambda n: jnp.maximum(nsub * n - 1, 0)
    return pl.pallas_call(
        _swa_kernel,
        out_shape=jax.ShapeDtypeStruct((b, s, nq), BF16),
        grid=(b, s // tq),
        in_specs=[
            pl.BlockSpec(memory_space=pltpu.SMEM),
            pl.BlockSpec((None, tq, nq), lambda bi, n: (bi, n, 0)),
            pl.BlockSpec((None, blk, LANES), lambda bi, n: (bi, prev(n), kcol)),
            pl.BlockSpec((None, tq, LANES), lambda bi, n: (bi, n, kcol)),
            pl.BlockSpec((None, blk, LANES), lambda bi, n: (bi, prev(n), vcol)),
            pl.BlockSpec((None, tq, LANES), lambda bi, n: (bi, n, vcol)),
        ],
        out_specs=pl.BlockSpec((None, tq, nq), lambda bi, n: (bi, n, 0)),
        compiler_params=_params("parallel", "arbitrary"),
        name="swa",
    )(sinks, qkv, qkv, qkv, qkv, qkv)


def _tri_inverse_minus_eye(lows, ri, ci):
    same = lambda w: (ri ^ ci) < w
    base = 8
    x = [jnp.where(same(base), -low, 0.0) for low in lows]
    p2 = [_dot_bf(xi, xi) for xi in x]
    e = [xi + pi + _dot_bf(xi, pi) for xi, pi in zip(x, p2)]
    p4 = [_dot_bf(pi, pi) for pi in p2]
    e = [ei + pi + _dot_bf(ei, pi) for ei, pi in zip(e, p4)]
    w = base * 2
    while w <= RWKV_CHUNK:
        off = [jnp.where(same(w) & ~same(w // 2), low, 0.0) for low in lows]
        wm = [oi + _dot_bf(ei, oi) for ei, oi in zip(e, off)]
        e = [ei - wi - _dot_bf(wi, ei) for ei, wi in zip(e, wm)]
        w *= 2
    return e


def _rwkv_kernel(tb, h_ref, *refs):
    state_ref, last_ref = refs[-2:]

    @pl.when(pl.program_id(1) == 0)
    def _():
        state_ref[...] = jnp.zeros_like(state_ref)
        last_ref[...] = jnp.zeros_like(last_ref)

    def tile(i, carry):
        _rwkv_tile(pl.ds(pl.multiple_of(i * tb, tb), tb), h_ref, *refs)
        return carry

    lax.fori_loop(0, h_ref.shape[0] // tb, tile, 0)


def _rwkv_tile(rows, h_ref, mu_ref, w0_ref, w2_ref, a0_ref, a2_ref, g2_ref, kk_ref, ka_ref,
               rk_ref, lnw_ref, lnb_ref, o_ref, state_ref, last_ref):
    c = RWKV_CHUNK
    tb = rows.size
    nchunk = tb // c
    dim = o_ref.shape[-1]
    npair = dim // LANES

    h = h_ref[rows, :]
    row = lax.broadcasted_iota(jnp.int32, h.shape, 0)
    shifted = jnp.where(row == 0, last_ref[...], pltpu.roll(h, 1, 0))
    last_ref[...] = h[tb - 1:tb, :]
    hs = h + (shifted - h) * mu_ref[...]
    r = hs[:, 0:dim]
    k = hs[:, dim:2 * dim]
    v = hs[:, 2 * dim:3 * dim]
    xwa = hs[:, 3 * dim:3 * dim + LANES]
    xg = hs[:, 3 * dim + LANES:3 * dim + 2 * LANES]

    wl = w0_ref[...] + _dot_x3(jnp.tanh(xwa), w2_ref[...])
    logw = -jax.nn.sigmoid(wl) * float(np.exp(-0.5))
    a = jax.nn.sigmoid(a0_ref[...] + _dot_x3(xwa, a2_ref[...]))
    gate = _dot_bf(jax.nn.sigmoid(xg), g2_ref[...])

    ri = lax.broadcasted_iota(jnp.int32, (LANES, LANES), 0)
    ci = lax.broadcasted_iota(jnp.int32, (LANES, LANES), 1)
    ones_bd = ((ri ^ ci) < HEAD_DIM).astype(BF16)
    ones_bd2 = jnp.concatenate([ones_bd, ones_bd], axis=0)

    def head_sum(x):
        cols = []
        for j in range(npair):
            hi, lo = _hi_lo(x[:, j * LANES:(j + 1) * LANES])
            cols.append(_dot(jnp.concatenate([hi, lo], axis=1), ones_bd2))
        return jnp.concatenate(cols, axis=1)

    kk = k * kk_ref[...]
    kk = kk * jnp.minimum(lax.rsqrt(head_sum(kk * kk)), 1.0 / L2_EPS)
    k2 = k * (1.0 + (a - 1.0) * ka_ref[...])
    bvec = kk * a

    ti = lax.broadcasted_iota(jnp.int32, (tb, tb), 0)
    si = lax.broadcasted_iota(jnp.int32, (tb, tb), 1)
    tri = (((ti ^ si) < c) & (si <= ti)).astype(BF16)
    cum = _dot(jnp.concatenate([tri] * 3, axis=1), jnp.concatenate(_split3(logw), axis=0))
    e_neg = jnp.exp(-cum)
    alpha = kk * jnp.exp(cum - logw)
    beta = bvec * e_neg
    kappa = k2 * e_neg
    rho = r * jnp.exp(cum)
    cum_end = jnp.concatenate(
        [jnp.broadcast_to(cum[(n + 1) * c - 1:(n + 1) * c, :], (c, dim)) for n in range(nchunk)], axis=0)
    to_end = jnp.exp(cum_end - cum)
    beta_e = bvec * to_end
    kappa_e = k2 * to_end
    w_end = jnp.exp(cum_end)

    same_head = (ri ^ ci) < c
    strict = same_head & (ci < ri)
    incl = same_head & (ci <= ri)
    eye = ri == ci
    lo = lax.broadcasted_iota(jnp.int32, (c, LANES), 1) < HEAD_DIM

    def stack(x2):
        return jnp.concatenate([jnp.where(lo, x2, 0.0), jnp.where(lo, 0.0, x2)], axis=0)

    units = [(n, j) for n in range(nchunk) for j in range(npair)]
    blk = lambda x, n, j: x[n * c:(n + 1) * c, j * LANES:(j + 1) * LANES]
    a_s = [stack(blk(alpha, n, j)) for n, j in units]
    rho_s = [stack(blk(rho, n, j)) for n, j in units]
    v_s = [stack(blk(v, n, j)) for n, j in units]
    ends = [jnp.concatenate([stack(blk(beta_e, n, j)), stack(blk(kappa_e, n, j))], axis=0) for n, j in units]
    bk = [jnp.concatenate([blk(beta, n, j)] * 2 + [blk(kappa, n, j)] * 2, axis=0) for n, j in units]
    sc = [_dot_bf(jnp.concatenate([ai, ri_], axis=0), bi, NT) for ai, ri_, bi in zip(a_s, rho_s, bk)]
    l_ab = [jnp.where(strict, s[:LANES, :LANES], 0.0) for s in sc]
    l_ak = [jnp.where(strict, s[:LANES, LANES:], 0.0) for s in sc]
    r_b = [jnp.where(incl, s[LANES:, :LANES], 0.0) for s in sc]
    r_k = [jnp.where(incl, s[LANES:, LANES:], 0.0) for s in sc]
    e_inv = _tri_inverse_minus_eye(l_ab, ri, ci)
    lkv = [_dot_bf(li, vi) for li, vi in zip(l_ak, v_s)]
    p_m = [-(ai + _dot_bf(ei, ai)) for ei, ai in zip(e_inv, a_s)]
    q_m = [-(xi + _dot_bf(ei, xi)) for ei, xi in zip(e_inv, lkv)]
    m_m = [jnp.where(eye, blk(w_end, n, j)[:1, :], 0.0) + _dot_bf(pi, ei[:LANES], TN)
           for (n, j), pi, ei in zip(units, p_m, ends)]
    n_m = [_dot_bf(jnp.concatenate([qi, vi], axis=0), ei, TN) for qi, vi, ei in zip(q_m, v_s, ends)]
    g_m = [ri_ + _dot_bf(rb, pi) for ri_, rb, pi in zip(rho_s, r_b, p_m)]
    h_m = [_dot_bf(jnp.concatenate([rb, rk], axis=1), jnp.concatenate([qi, vi], axis=0))
           for rb, rk, qi, vi in zip(r_b, r_k, q_m, v_s)]

    state = [state_ref[j] for j in range(npair)]
    ys = []
    for n in range(nchunk):
        idx = [n * npair + j for j in range(npair)]
        nxt = [_dot_bf(state[j], m_m[i]) + n_m[i] for j, i in enumerate(idx)]
        y = [_dot_bf(g_m[i], state[j], NT) + h_m[i] for j, i in enumerate(idx)]
        ys.append(jnp.concatenate([yi[:c] + yi[c:] for yi in y], axis=1))
        state = nxt
    for j in range(npair):
        state_ref[j] = state[j]
    y = jnp.concatenate(ys, axis=0)

    mean = head_sum(y) * (1.0 / HEAD_DIM)
    d = y - mean
    var = head_sum(d * d) * (1.0 / HEAD_DIM)
    y = d * lax.rsqrt(var + GN_EPS) * lnw_ref[...] + lnb_ref[...]
    y = y + head_sum(r * k2 * rk_ref[...]) * v
    o_ref[rows, :] = (y * gate).astype(o_ref.dtype)


def _rwkv(hb, mu, w0, w2p, a0, a2p, g2, k_k, k_a, r_k, ln_w, ln_b, *, tb=4 * RWKV_CHUNK, tiles=2):
    b, s, cols = hb.shape
    dim = w0.shape[-1]
    c = tb * tiles
    row = lambda x: x.reshape(1, -1)
    vec = lambda n: pl.BlockSpec((1, n), lambda bi, t: (0, 0))
    mat = lambda m: pl.BlockSpec(m.shape, lambda bi, t: (0, 0))
    return pl.pallas_call(
        functools.partial(_rwkv_kernel, tb),
        out_shape=jax.ShapeDtypeStruct((b, s, dim), BF16),
        grid=(b, s // c),
        in_specs=[
            pl.BlockSpec((None, c, cols), lambda bi, t: (bi, t, 0)),
            vec(cols), vec(dim), mat(w2p), vec(dim), mat(a2p), mat(g2),
            vec(dim), vec(dim), vec(dim), vec(dim), vec(dim),
        ],
        out_specs=pl.BlockSpec((None, c, dim), lambda bi, t: (bi, t, 0)),
        scratch_shapes=[pltpu.VMEM((dim // LANES, LANES, LANES), F32), pltpu.VMEM((1, cols), F32)],
        compiler_params=_params("parallel", "arbitrary"),
        name="rwkv7",
    )(hb, row(mu), row(w0), w2p, row(a0), a2p, g2, row(k_k), row(k_a), row(r_k), row(ln_w), row(ln_b))


FOX_AUG = 3
FOX_BLOCK = 512


def _fox_in_kernel(x_ref, g_ref, w_ref, wvt_ref, bf_ref, sel_ref, qa_ref, ka_ref, va_ref, carry_ref):
    tc = x_ref.shape[0]
    nheads = qa_ref.shape[-1] // LANES
    dim = nheads * HEAD_DIM
    scale = HEAD_DIM ** -0.5 * LOG2E

    @pl.when(pl.program_id(1) == 0)
    def _():
        carry_ref[...] = jnp.zeros_like(carry_ref)

    hn = _rms(x_ref[...], g_ref[...]).astype(BF16)
    z = _dot(hn, w_ref[:, 2 * dim:]) + bf_ref[...]
    proj_q = _dot(hn, w_ref[:, :dim])
    logf = jnp.minimum(z, 0.0) - jnp.log(1.0 + jnp.exp(-jnp.abs(z)))
    lane = lax.broadcasted_iota(jnp.int32, z.shape, 1)
    logf = jnp.where(lane < nheads, logf, 0.0)
    ti = lax.broadcasted_iota(jnp.int32, (tc, tc), 0)
    si = lax.broadcasted_iota(jnp.int32, (tc, tc), 1)
    cg = _sel_dot((si <= ti).astype(BF16), logf) + carry_ref[...]
    carry_ref[...] = cg[tc - 1:tc, :]

    hi, mid, low = _split3(cg * LOG2E)
    pieces = (hi.astype(F32) + pltpu.roll(mid.astype(F32), nheads, 1)
              + pltpu.roll(low.astype(F32), 2 * nheads, 1)
              + (lane == FOX_AUG * nheads).astype(F32)).astype(BF16)
    q_c = _dot(pieces, sel_ref[0])
    k_c = _dot(pieces, sel_ref[1])

    lane = lax.broadcasted_iota(jnp.int32, (tc, LANES), 1)
    lo = lane < HEAD_DIM

    def assemble(proj, c_aug, o_ref):
        for j in range(nheads // 2):
            x = proj[:, j * LANES:(j + 1) * LANES]
            xr = pltpu.roll(x, HEAD_DIM, 1)
            for e, xe in ((0, x), (1, xr)):
                hs = slice((2 * j + e) * LANES, (2 * j + e + 1) * LANES)
                o_ref[:, hs] = jnp.where(lo, xe, c_aug[:, hs]).astype(BF16)

    proj_k = _dot(hn, w_ref[:, dim:2 * dim])
    assemble(proj_q * scale, q_c, qa_ref)
    v_t = _dot(wvt_ref[...], hn, NT)
    assemble(proj_k, k_c, ka_ref)
    one_rows = (lax.broadcasted_iota(jnp.int32, (LANES - HEAD_DIM, tc), 0) == 0).astype(F32)
    for h in range(nheads):
        va_ref[h, 0] = jnp.concatenate([v_t[h * HEAD_DIM:(h + 1) * HEAD_DIM], one_rows], axis=0).astype(BF16)


def _fox_select_matrices(nheads):
    sel = np.zeros((2, LANES, nheads * LANES), np.float32)
    one_row = FOX_AUG * nheads
    for h in range(nheads):
        for i in range(FOX_AUG):
            sel[0, i * nheads + h, h * LANES + HEAD_DIM + i] = 1.0
            sel[0, one_row, h * LANES + HEAD_DIM + FOX_AUG + i] = 1.0
            sel[1, one_row, h * LANES + HEAD_DIM + i] = 1.0
            sel[1, i * nheads + h, h * LANES + HEAD_DIM + FOX_AUG + i] = -1.0
    return jnp.asarray(sel, BF16)


def _fox_in(x, g, w_in, b_f, *, batch, tc):
    t, d = x.shape
    s = t // batch
    nheads = b_f.shape[-1]
    assert FOX_AUG * nheads < LANES
    dim = nheads * HEAD_DIM
    w = jnp.concatenate([w_in[:, :2 * dim], jnp.pad(w_in[:, 3 * dim:], ((0, 0), (0, LANES - nheads)))],
                        axis=1).astype(BF16)
    w_vt = w_in[:, 2 * dim:3 * dim].T.astype(BF16)
    bf = jnp.zeros((1, LANES), F32).at[0, :nheads].set(b_f)
    sel = _fox_select_matrices(nheads)
    wide = nheads * LANES
    nt = s // tc
    resident = pl.Buffered(1)
    return pl.pallas_call(
        _fox_in_kernel,
        out_shape=(jax.ShapeDtypeStruct((batch, s, wide), BF16),
                   jax.ShapeDtypeStruct((batch, s, wide), BF16),
                   jax.ShapeDtypeStruct((batch, nheads, nt, LANES, tc), BF16)),
        grid=(batch, nt),
        in_specs=[
            pl.BlockSpec((tc, d), lambda bi, ti: (bi * nt + ti, 0)),
            pl.BlockSpec((1, d), lambda bi, ti: (0, 0)),
            pl.BlockSpec(w.shape, lambda bi, ti: (0, 0), pipeline_mode=resident),
            pl.BlockSpec(w_vt.shape, lambda bi, ti: (0, 0), pipeline_mode=resident),
            pl.BlockSpec((1, LANES), lambda bi, ti: (0, 0)),
            pl.BlockSpec(sel.shape, lambda bi, ti: (0, 0, 0), pipeline_mode=resident),
        ],
        out_specs=(pl.BlockSpec((None, tc, wide), lambda bi, ti: (bi, ti, 0)),
                   pl.BlockSpec((None, tc, wide), lambda bi, ti: (bi, ti, 0)),
                   pl.BlockSpec((None, nheads, 1, LANES, tc), lambda bi, ti: (bi, 0, ti, 0, 0))),
        scratch_shapes=[pltpu.VMEM((1, LANES), F32)],
        compiler_params=_params("parallel", "arbitrary"),
        name="fox_in",
    )(x, g.reshape(1, d), w, w_vt, bf, sel)


def _fox_attn_kernel(q_ref, k_ref, v_ref, o_ref, m_ref, acc_ref, sa_ref, sb_ref):
    tq = sa_ref.shape[1]
    nh = q_ref.shape[1] // LANES
    nq = o_ref.shape[0] // tq
    keys = lax.broadcasted_iota(jnp.int32, (tq, tq), 0)
    queries = lax.broadcasted_iota(jnp.int32, (tq, tq), 1)

    def logits_into(dst_ref, qb, kb, heads=None):
        q0 = pl.multiple_of(qb * tq, tq)
        k0 = pl.multiple_of(kb * tq, tq)
        for e in range(nh) if heads is None else heads:
            hs = slice(e * LANES, (e + 1) * LANES)
            dst_ref[e] = _dot(k_ref[pl.ds(k0, tq), hs], q_ref[pl.ds(q0, tq), hs], NT)

    def consume(src_ref, kb, diagonal, after_head=None):
        for e in range(nh):
            s = src_ref[e]
            if diagonal:
                s = jnp.where(keys <= queries, s, -jnp.inf)
            m_prev = m_ref[e]
            m_new = jnp.maximum(m_prev, jnp.max(s, axis=0, keepdims=True))
            p = jnp.exp2(s - m_new).astype(BF16)
            acc_ref[e] = jnp.exp2(m_prev - m_new) * acc_ref[e] + _dot(v_ref[e, kb], p)
            m_ref[e] = m_new
            if after_head is not None:
                after_head(e)

    logits_into(sa_ref, 0, 0)

    def two_query_blocks(step, carry):
        for half in range(2):
            qi = 2 * step + half
            m_ref[...] = jnp.full_like(m_ref, -jnp.inf)
            acc_ref[...] = jnp.zeros_like(acc_ref)

            def two_blocks(t, carry, qi=qi):
                kb = 2 * t
                logits_into(sb_ref, qi, kb + 1)
                consume(sa_ref, kb, False)
                logits_into(sa_ref, qi, kb + 2)
                consume(sb_ref, kb + 1, False)
                return carry

            lax.fori_loop(0, step, two_blocks, 0)
            if half == 0:
                consume(sa_ref, qi, True, after_head=lambda e, qi=qi: logits_into(sa_ref, qi + 1, 0, heads=(e,)))
            else:
                nxt = jnp.minimum(qi + 1, nq - 1)
                logits_into(sb_ref, qi, qi)
                consume(sa_ref, qi - 1, False)
                logits_into(sa_ref, nxt, 0)
                consume(sb_ref, qi, True)

            rows = pl.ds(pl.multiple_of(qi * tq, tq), tq)
            for pair in range(nh // 2):
                outs = []
                for e in (2 * pair, 2 * pair + 1):
                    acc = acc_ref[e]
                    outs.append(acc[:HEAD_DIM] / acc[HEAD_DIM:HEAD_DIM + 1])
                o_ref[rows, pair * LANES:(pair + 1) * LANES] = jnp.concatenate(outs, axis=0).T.astype(o_ref.dtype)
        return carry

    lax.fori_loop(0, nq // 2, two_query_blocks, 0)


def _fox_attn(q_aug, k_aug, v_aug, *, tq, nh=4):
    b, s, wide = q_aug.shape
    nheads = wide // LANES
    assert (s // tq) % 2 == 0
    return pl.pallas_call(
        _fox_attn_kernel,
        out_shape=jax.ShapeDtypeStruct((b, s, nheads * HEAD_DIM), BF16),
        grid=(b, nheads // nh),
        in_specs=[
            pl.BlockSpec((None, s, nh * LANES), lambda bi, g: (bi, 0, g)),
            pl.BlockSpec((None, s, nh * LANES), lambda bi, g: (bi, 0, g)),
            pl.BlockSpec((None, nh, s // tq, LANES, tq), lambda bi, g: (bi, g, 0, 0, 0)),
        ],
        out_specs=pl.BlockSpec((None, s, nh * HEAD_DIM), lambda bi, g: (bi, 0, g)),
        scratch_shapes=[pltpu.VMEM((nh, 1, tq), F32), pltpu.VMEM((nh, LANES, tq), F32),
                        pltpu.VMEM((nh, tq, tq), F32), pltpu.VMEM((nh, tq, tq), F32)],
        compiler_params=_params("parallel", "parallel"),
        name="fox_attn",
    )(q_aug, k_aug, v_aug)


def kernel(x, p, ffn1_norm, ffn1_w_gu, ffn1_w_down, mix_norm, ffn2_norm, ffn2_w_gu, ffn2_w_down, ple_norm, ple_w_gate, ple_w_proj, even_w_in, even_w_out, swa_sinks, rwkv_mu, rwkv_w0, rwkv_w2, rwkv_a0, rwkv_a2, rwkv_g2, rwkv_k_k, rwkv_k_a, rwkv_r_k, rwkv_ln_w, rwkv_ln_b, fox_w_in, fox_b_f, fox_w_out, final_norm):
    b, s, d = x.shape
    depth = p.shape[0]
    t = b * s
    bf = lambda w: w.astype(BF16)
    swa_q = SWA_HEADS * HEAD_DIM
    swa_cols = swa_q + 2 * (SWA_HEADS // SWA_GROUP) * HEAD_DIM
    rwkv_dim = rwkv_w0.shape[-1]
    lora = rwkv_w2.shape[1]
    fox_heads = fox_b_f.shape[-1]
    fox_dim = fox_heads * HEAD_DIM

    x = x.reshape(t, d)
    for i in range(depth):
        j = i // 2
        if i % 2 == 0:
            widths = (swa_cols, even_w_in.shape[-1] - swa_cols)
            x, qkv, hb = _ffn(x, ffn1_norm[i], ffn1_w_gu, ffn1_w_down, i, proj=(mix_norm[i], even_w_in, j, widths))
            ya = _swa(qkv.reshape(b, s, swa_cols), swa_sinks[j])
            zeros = jnp.zeros((lora, rwkv_dim), F32)
            w2p = jnp.concatenate([rwkv_w2[j], zeros], axis=0)
            a2p = jnp.concatenate([zeros, rwkv_a2[j]], axis=0)
            yb = _rwkv(hb.reshape(b, s, -1), rwkv_mu[j], rwkv_w0[j], w2p, rwkv_a0[j], a2p,
                       bf(rwkv_g2[j]), rwkv_k_k[j], rwkv_k_a[j], rwkv_r_k[j].reshape(-1),
                       rwkv_ln_w[j], rwkv_ln_b[j])
            mixed = ([ya.reshape(t, swa_q), yb.reshape(t, rwkv_dim)], even_w_out, j)
        else:
            (x,) = _ffn(x, ffn1_norm[i], ffn1_w_gu, ffn1_w_down, i)
            q_aug, k_aug, v_aug = _fox_in(x, mix_norm[i], fox_w_in[j], fox_b_f[j], batch=b, tc=FOX_BLOCK)
            yc = _fox_attn(q_aug, k_aug, v_aug, tq=FOX_BLOCK)
            mixed = ([yc.reshape(t, fox_dim)], fox_w_out, j)
        x = _post_mix(*mixed, x, ffn2_norm[i], ffn2_w_gu, ffn2_w_down, ple_norm[i], ple_w_gate,
                      p.reshape(depth, t, -1), ple_w_proj, final_norm, i, final=(i == depth - 1))
    return x.reshape(b, s, d)
```

```python
import functools

import jax
import jax.numpy as jnp
import numpy as np
from jax import lax
from jax.experimental import pallas as pl
from jax.experimental.pallas import tpu as pltpu

F32 = jnp.float32
BF16 = jnp.bfloat16

LANES = 128
HEAD_DIM = 64
SWA_HEADS = 8
SWA_GROUP = 4
SWA_BLOCK = 128
RWKV_CHUNK = 64
NORM_EPS = 1e-6
GN_EPS = 64e-5
L2_EPS = 1e-12
LOG2E = float(np.log2(np.e))
VMEM_LIMIT = 56 * 1024 * 1024

NN = (((1,), (0,)), ((), ()))
NT = (((1,), (1,)), ((), ()))
TN = (((0,), (0,)), ((), ()))


def _dot(a, b, dims=NN):
    return lax.dot_general(a, b, dims, preferred_element_type=F32)


def _dot_bf(a, b, dims=NN):
    return _dot(a.astype(BF16), b.astype(BF16), dims)


def _hi_lo(x):
    hi = x.astype(BF16)
    lo = (x - hi.astype(F32)).astype(BF16)
    return hi, lo


def _dot_x3(a, b, dims=NN):
    ah, al = _hi_lo(a)
    bh, bl = _hi_lo(b)
    return _dot(ah, bh, dims) + (_dot(ah, bl, dims) + _dot(al, bh, dims))


def _split3(x):
    hi = x.astype(BF16)
    r1 = x - hi.astype(F32)
    mid = r1.astype(BF16)
    lo = (r1 - mid.astype(F32)).astype(BF16)
    return hi, mid, lo


def _sel_dot(sel, x):
    hi, mid, lo = _split3(x)
    return _dot(sel, hi) + (_dot(sel, mid) + _dot(sel, lo))


def _rms(x, g):
    ms = jnp.mean(x * x, axis=-1, keepdims=True)
    return x * lax.rsqrt(ms + NORM_EPS) * g


def _params(*sem):
    return pltpu.CompilerParams(dimension_semantics=sem, vmem_limit_bytes=VMEM_LIMIT)


LOAD_STEPS = 16


def _weight_chunk(w, layer, rows, block=0):
    chunk = rows // LOAD_STEPS
    assert chunk * LOAD_STEPS == rows and chunk % 16 == 0, (rows, chunk)
    return pl.BlockSpec((None, chunk, w.shape[2]),
                        lambda i: (layer, block * LOAD_STEPS + jnp.minimum(i, LOAD_STEPS - 1), 0))


def _stash(step, chunk_ref, copy_ref):
    rows = chunk_ref.shape[0]
    copy_ref[pl.ds(pl.multiple_of(step * rows, rows), rows), :] = chunk_ref[...].astype(BF16)


def _row_tile(tm, n):
    return pl.BlockSpec((tm, n), lambda i: (jnp.maximum(i - LOAD_STEPS, 0), 0))


def _swiglu_half_step(x, g, wgu_ref, wd_ref):
    dff = wd_ref.shape[0]
    hn = _rms(x, g).astype(BF16)
    gate = _dot(hn, wgu_ref[:, :dff])
    up = _dot(hn, wgu_ref[:, dff:])
    act = (gate * jax.nn.sigmoid(gate) * up).astype(BF16)
    return x + 0.5 * _dot(act, wd_ref[...])


def _ffn_kernel(n_proj, x_ref, g_ref, wgu_ref, wd_ref, *refs):
    n_w = 3 if n_proj else 2
    copies = refs[len(refs) - n_w:]
    refs = refs[:len(refs) - n_w]
    step = pl.program_id(0)

    @pl.when(step < LOAD_STEPS)
    def _():
        _stash(step, wgu_ref, copies[0])
        _stash(step, wd_ref, copies[1])
        if n_proj:
            _stash(step, refs[1], copies[2])

    @pl.when(step >= LOAD_STEPS)
    def _():
        x = _swiglu_half_step(x_ref[...], g_ref[...], copies[0], copies[1])
        if n_proj:
            gm_ref, o_ref, proj_refs = refs[0], refs[2], refs[3:]
            proj = _dot(_rms(x, gm_ref[...]).astype(BF16), copies[2][...])
            col = 0
            for p_ref in proj_refs:
                p_ref[...] = proj[:, col:col + p_ref.shape[1]]
                col += p_ref.shape[1]
        else:
            o_ref = refs[0]
        o_ref[...] = x


def _ffn(x, g, w_gu, w_down, layer, proj=None, *, tm=512):
    t, d = x.shape
    dff = w_down.shape[1]
    vec = pl.BlockSpec((1, d), lambda i: (0, 0))
    in_specs = [_row_tile(tm, d), vec, _weight_chunk(w_gu, layer, d), _weight_chunk(w_down, layer, dff)]
    args = [x, g.reshape(1, d), w_gu, w_down]
    scratch = [pltpu.VMEM((d, 2 * dff), BF16), pltpu.VMEM((dff, d), BF16)]
    out_shape = [jax.ShapeDtypeStruct((t, d), F32)]
    out_specs = [_row_tile(tm, d)]
    widths = ()
    if proj is not None:
        gm, w, w_layer, widths = proj
        assert sum(widths) == w.shape[2]
        in_specs += [vec, _weight_chunk(w, w_layer, d)]
        args += [gm.reshape(1, d), w]
        scratch += [pltpu.VMEM(w.shape[1:], BF16)]
        out_shape += [jax.ShapeDtypeStruct((t, n), F32) for n in widths]
        out_specs += [_row_tile(tm, n) for n in widths]
    return pl.pallas_call(
        functools.partial(_ffn_kernel, len(widths)),
        out_shape=out_shape,
        grid=(LOAD_STEPS + t // tm,),
        in_specs=in_specs,
        out_specs=out_specs,
        scratch_shapes=scratch,
        compiler_params=_params("arbitrary"),
        name="ffn",
    )(*args)


def _post_mix_kernel(n_in, final, *refs):
    a_refs, wo_refs = refs[:n_in], refs[n_in:2 * n_in]
    (x_ref, g2_ref, wgu_ref, wd_ref, gp_ref, wpg_ref, p_ref, wpp_ref, fn_ref, o_ref) = refs[2 * n_in:2 * n_in + 10]
    copies = refs[2 * n_in + 10:]
    wo_copies, (wgu_c, wd_c, wpg_c, wpp_c) = copies[:n_in], copies[n_in:]
    step = pl.program_id(0)

    @pl.when(step < LOAD_STEPS)
    def _():
        for src, dst in zip(wo_refs + (wgu_ref, wd_ref, wpg_ref, wpp_ref), wo_copies + (wgu_c, wd_c, wpg_c, wpp_c)):
            _stash(step, src, dst)

    @pl.when(step >= LOAD_STEPS)
    def _():
        x = x_ref[...]
        for a_ref, w_c in zip(a_refs, wo_copies):
            x = x + _dot(a_ref[...], w_c[...])
        x = _swiglu_half_step(x, g2_ref[...], wgu_c, wd_c)
        gate = jax.nn.sigmoid(_dot(_rms(x, gp_ref[...]).astype(BF16), wpg_c[...]))
        x = x + gate * _dot(p_ref[...].astype(BF16), wpp_c[...])
        if final:
            x = _rms(x, fn_ref[...])
        o_ref[...] = x


def _post_mix(a_list, w_out, out_layer, x, g2, w_gu, w_down, gp, w_gate, p, w_proj, final_g, layer, *, final,
              tm=512):
    t, d = x.shape
    dff = w_down.shape[1]
    vec = pl.BlockSpec((1, d), lambda i: (0, 0))
    width = a_list[0].shape[1]
    assert all(a.shape[1] == width for a in a_list)
    in_specs = [_row_tile(tm, width) for _ in a_list]
    in_specs += [_weight_chunk(w_out, out_layer, width, block=k) for k in range(len(a_list))]
    in_specs += [_row_tile(tm, d), vec, _weight_chunk(w_gu, layer, d), _weight_chunk(w_down, layer, dff), vec,
                 _weight_chunk(w_gate, layer, d),
                 pl.BlockSpec((None, tm, p.shape[2]), lambda i: (layer, jnp.maximum(i - LOAD_STEPS, 0), 0)),
                 _weight_chunk(w_proj, layer, w_proj.shape[1]), vec]
    scratch = [pltpu.VMEM((width, d), BF16) for _ in a_list]
    scratch += [pltpu.VMEM((d, 2 * dff), BF16), pltpu.VMEM((dff, d), BF16), pltpu.VMEM((d, d), BF16),
                pltpu.VMEM(w_proj.shape[1:], BF16)]
    return pl.pallas_call(
        functools.partial(_post_mix_kernel, len(a_list), final),
        out_shape=jax.ShapeDtypeStruct((t, d), F32),
        grid=(LOAD_STEPS + t // tm,),
        in_specs=in_specs,
        out_specs=_row_tile(tm, d),
        scratch_shapes=scratch,
        compiler_params=_params("arbitrary"),
        name="post_mix",
    )(*a_list, *([w_out] * len(a_list)), x, g2.reshape(1, d), w_gu, w_down, gp.reshape(1, d), w_gate, p,
      w_proj, final_g.reshape(1, d))


def _swa_kernel(sink_ref, q_ref, kp_ref, kc_ref, vp_ref, vc_ref, o_ref):
    n = pl.program_id(1)
    blk = SWA_BLOCK
    nsub = q_ref.shape[0] // blk
    scale = HEAD_DIM ** -0.5
    k = jnp.concatenate([kp_ref[...], kc_ref[...]], axis=0)
    v = jnp.concatenate([vp_ref[...], vc_ref[...]], axis=0)
    kr = pltpu.roll(k, HEAD_DIM, 1)
    vr = pltpu.roll(v, HEAD_DIM, 1)
    lo_kv = lax.broadcasted_iota(jnp.int32, k.shape, 1) < HEAD_DIM
    kdup = [jnp.where(lo_kv, k, kr).astype(BF16), jnp.where(lo_kv, kr, k).astype(BF16)]
    vdup = [jnp.where(lo_kv, v, vr).astype(BF16), jnp.where(lo_kv, vr, v).astype(BF16)]

    qi = lax.broadcasted_iota(jnp.int32, (blk, 2 * blk), 0)
    ki = lax.broadcasted_iota(jnp.int32, (blk, 2 * blk), 1)
    dist = qi + blk - ki
    in_window = (dist >= 0) & (dist < blk)
    distf = dist.astype(F32)
    lo_q = lax.broadcasted_iota(jnp.int32, (blk, LANES), 1) < HEAD_DIM

    units = [(u, h) for u in range(nsub) for h in range(SWA_HEADS)]
    keys_of = lambda x, u: x[u * blk:(u + 2) * blk]
    qm = []
    for u, h in units:
        j, e = divmod(h, 2)
        q2 = q_ref[u * blk:(u + 1) * blk, j * LANES:(j + 1) * LANES] * (scale * LOG2E)
        qm.append(jnp.where(lo_q if e == 0 else ~lo_q, q2, 0.0).astype(BF16))
    logits = [_dot(qi_, keys_of(kdup[h // SWA_GROUP], u), NT) for qi_, (u, h) in zip(qm, units)]
    ps, inv_denoms = [], []
    for s, (u, h) in zip(logits, units):
        valid = in_window & ((n > 0) | (ki >= blk)) if u == 0 else in_window
        slope = 2.0 ** (-8.0 * (h + 1) / SWA_HEADS) * LOG2E
        s = jnp.where(valid, s - slope * distf, -jnp.inf)
        sink = sink_ref[h] * LOG2E
        m = jnp.maximum(jnp.max(s, axis=-1, keepdims=True), sink)
        p = jnp.exp2(s - m)
        inv_denoms.append(1.0 / (jnp.sum(p, axis=-1, keepdims=True) + jnp.exp2(sink - m)))
        ps.append(p.astype(BF16))
    outs = [_dot(p, keys_of(vdup[h // SWA_GROUP], u)) * inv
            for p, inv, (u, h) in zip(ps, inv_denoms, units)]
    for u in range(nsub):
        for j in range(SWA_HEADS // 2):
            pair = jnp.where(lo_q, outs[u * SWA_HEADS + 2 * j], outs[u * SWA_HEADS + 2 * j + 1])
            o_ref[u * blk:(u + 1) * blk, j * LANES:(j + 1) * LANES] = pair.astype(o_ref.dtype)


def _swa(qkv, sinks, *, nsub=2):
    b, s, _ = qkv.shape
    blk = SWA_BLOCK
    tq = nsub * blk
    nq = SWA_HEADS * HEAD_DIM
    kcol = nq // LANES
    vcol = kcol + 1
    prev = lambda n: jnp.maximum(nsub * n - 1, 0)
    return pl.pallas_call(
        _swa_kernel,
        out_shape=jax.ShapeDtypeStruct((b, s, nq), BF16),
        grid=(b, s // tq),
        in_specs=[
            pl.BlockSpec(memory_space=pltpu.SMEM),
            pl.BlockSpec((None, tq, nq), lambda bi, n: (bi, n, 0)),
            pl.BlockSpec((None, blk, LANES), lambda bi, n: (bi, prev(n), kcol)),
            pl.BlockSpec((None, tq, LANES), lambda bi, n: (bi, n, kcol)),
            pl.BlockSpec((None, blk, LANES), lambda bi, n: (bi, prev(n), vcol)),
            pl.BlockSpec((None, tq, LANES), lambda bi, n: (bi, n, vcol)),
        ],
        out_specs=pl.BlockSpec((None, tq, nq), lambda bi, n: (bi, n, 0)),
        compiler_params=_params("parallel", "arbitrary"),
        name="swa",
    )(sinks, qkv, qkv, qkv, qkv, qkv)


def _tri_inverse_minus_eye(lows, ri, ci):
    same = lambda w: (ri ^ ci) < w
    base = 8
    x = [jnp.where(same(base), -low, 0.0) for low in lows]
    p2 = [_dot_bf(xi, xi) for xi in x]
    e = [xi + pi + _dot_bf(xi, pi) for xi, pi in zip(x, p2)]
    p4 = [_dot_bf(pi, pi) for pi in p2]
    e = [ei + pi + _dot_bf(ei, pi) for ei, pi in zip(e, p4)]
    w = base * 2
    while w <= RWKV_CHUNK:
        off = [jnp.where(same(w) & ~same(w // 2), low, 0.0) for low in lows]
        wm = [oi + _dot_bf(ei, oi) for ei, oi in zip(e, off)]
        e = [ei - wi - _dot_bf(wi, ei) for ei, wi in zip(e, wm)]
        w *= 2
    return e


def _rwkv_kernel(h_ref, mu_ref, w0_ref, w2_ref, a0_ref, a2_ref, g2_ref, kk_ref, ka_ref,
                 rk_ref, lnw_ref, lnb_ref, o_ref, state_ref, last_ref):
    c = RWKV_CHUNK
    tb = h_ref.shape[0]
    nchunk = tb // c
    dim = o_ref.shape[-1]
    npair = dim // LANES

    @pl.when(pl.program_id(1) == 0)
    def _():
        state_ref[...] = jnp.zeros_like(state_ref)
        last_ref[...] = jnp.zeros_like(last_ref)

    h = h_ref[...]
    row = lax.broadcasted_iota(jnp.int32, h.shape, 0)
    shifted = jnp.where(row == 0, last_ref[...], pltpu.roll(h, 1, 0))
    last_ref[...] = h[tb - 1:tb, :]
    hs = h + (shifted - h) * mu_ref[...]
    r = hs[:, 0:dim]
    k = hs[:, dim:2 * dim]
    v = hs[:, 2 * dim:3 * dim]
    xwa = hs[:, 3 * dim:3 * dim + LANES]
    xg = hs[:, 3 * dim + LANES:3 * dim + 2 * LANES]

    wl = w0_ref[...] + _dot_x3(jnp.tanh(xwa), w2_ref[...])
    logw = -jax.nn.sigmoid(wl) * float(np.exp(-0.5))
    a = jax.nn.sigmoid(a0_ref[...] + _dot_x3(xwa, a2_ref[...]))
    gate = _dot_bf(jax.nn.sigmoid(xg), g2_ref[...])

    ri = lax.broadcasted_iota(jnp.int32, (LANES, LANES), 0)
    ci = lax.broadcasted_iota(jnp.int32, (LANES, LANES), 1)
    ones_bd = ((ri ^ ci) < HEAD_DIM).astype(BF16)
    ones_bd2 = jnp.concatenate([ones_bd, ones_bd], axis=0)

    def head_sum(x):
        cols = []
        for j in range(npair):
            hi, lo = _hi_lo(x[:, j * LANES:(j + 1) * LANES])
            cols.append(_dot(jnp.concatenate([hi, lo], axis=1), ones_bd2))
        return jnp.concatenate(cols, axis=1)

    kk = k * kk_ref[...]
    kk = kk * jnp.minimum(lax.rsqrt(head_sum(kk * kk)), 1.0 / L2_EPS)
    k2 = k * (1.0 + (a - 1.0) * ka_ref[...])
    bvec = kk * a

    ti = lax.broadcasted_iota(jnp.int32, (tb, tb), 0)
    si = lax.broadcasted_iota(jnp.int32, (tb, tb), 1)
    tri = (((ti ^ si) < c) & (si <= ti)).astype(BF16)
    cum = _dot(jnp.concatenate([tri] * 3, axis=1), jnp.concatenate(_split3(logw), axis=0))
    e_neg = jnp.exp(-cum)
    alpha = kk * jnp.exp(cum - logw)
    beta = bvec * e_neg
    kappa = k2 * e_neg
    rho = r * jnp.exp(cum)
    cum_end = jnp.concatenate(
        [jnp.broadcast_to(cum[(n + 1) * c - 1:(n + 1) * c, :], (c, dim)) for n in range(nchunk)], axis=0)
    to_end = jnp.exp(cum_end - cum)
    beta_e = bvec * to_end
    kappa_e = k2 * to_end
    w_end = jnp.exp(cum_end)

    same_head = (ri ^ ci) < c
    strict = same_head & (ci < ri)
    incl = same_head & (ci <= ri)
    eye = ri == ci
    lo = lax.broadcasted_iota(jnp.int32, (c, LANES), 1) < HEAD_DIM

    def stack(x2):
        return jnp.concatenate([jnp.where(lo, x2, 0.0), jnp.where(lo, 0.0, x2)], axis=0)

    units = [(n, j) for n in range(nchunk) for j in range(npair)]
    blk = lambda x, n, j: x[n * c:(n + 1) * c, j * LANES:(j + 1) * LANES]
    a_s = [stack(blk(alpha, n, j)) for n, j in units]
    rho_s = [stack(blk(rho, n, j)) for n, j in units]
    v_s = [stack(blk(v, n, j)) for n, j in units]
    ends = [jnp.concatenate([stack(blk(beta_e, n, j)), stack(blk(kappa_e, n, j))], axis=0) for n, j in units]
    bk = [jnp.concatenate([blk(beta, n, j)] * 2 + [blk(kappa, n, j)] * 2, axis=0) for n, j in units]
    sc = [_dot_bf(jnp.concatenate([ai, ri_], axis=0), bi, NT) for ai, ri_, bi in zip(a_s, rho_s, bk)]
    l_ab = [jnp.where(strict, s[:LANES, :LANES], 0.0) for s in sc]
    l_ak = [jnp.where(strict, s[:LANES, LANES:], 0.0) for s in sc]
    r_b = [jnp.where(incl, s[LANES:, :LANES], 0.0) for s in sc]
    r_k = [jnp.where(incl, s[LANES:, LANES:], 0.0) for s in sc]
    e_inv = _tri_inverse_minus_eye(l_ab, ri, ci)
    lkv = [_dot_bf(li, vi) for li, vi in zip(l_ak, v_s)]
    p_m = [-(ai + _dot_bf(ei, ai)) for ei, ai in zip(e_inv, a_s)]
    q_m = [-(xi + _dot_bf(ei, xi)) for ei, xi in zip(e_inv, lkv)]
    m_m = [jnp.where(eye, blk(w_end, n, j)[:1, :], 0.0) + _dot_bf(pi, ei[:LANES], TN)
           for (n, j), pi, ei in zip(units, p_m, ends)]
    n_m = [_dot_bf(jnp.concatenate([qi, vi], axis=0), ei, TN) for qi, vi, ei in zip(q_m, v_s, ends)]
    g_m = [ri_ + _dot_bf(rb, pi) for ri_, rb, pi in zip(rho_s, r_b, p_m)]
    h_m = [_dot_bf(jnp.concatenate([rb, rk], axis=1), jnp.concatenate([qi, vi], axis=0))
           for rb, rk, qi, vi in zip(r_b, r_k, q_m, v_s)]

    state = [state_ref[j] for j in range(npair)]
    ys = []
    for n in range(nchunk):
        idx = [n * npair + j for j in range(npair)]
        nxt = [_dot_bf(state[j], m_m[i]) + n_m[i] for j, i in enumerate(idx)]
        y = [_dot_bf(g_m[i], state[j], NT) + h_m[i] for j, i in enumerate(idx)]
        ys.append(jnp.concatenate([yi[:c] + yi[c:] for yi in y], axis=1))
        state = nxt
    for j in range(npair):
        state_ref[j] = state[j]
    y = jnp.concatenate(ys, axis=0)

    mean = head_sum(y) * (1.0 / HEAD_DIM)
    d = y - mean
    var = head_sum(d * d) * (1.0 / HEAD_DIM)
    y = d * lax.rsqrt(var + GN_EPS) * lnw_ref[...] + lnb_ref[...]
    y = y + head_sum(r * k2 * rk_ref[...]) * v
    o_ref[...] = (y * gate).astype(o_ref.dtype)


def _rwkv(hb, mu, w0, w2p, a0, a2p, g2, k_k, k_a, r_k, ln_w, ln_b, *, tb=4 * RWKV_CHUNK):
    b, s, cols = hb.shape
    dim = w0.shape[-1]
    c = tb
    row = lambda x: x.reshape(1, -1)
    vec = lambda n: pl.BlockSpec((1, n), lambda bi, t: (0, 0))
    mat = lambda m: pl.BlockSpec(m.shape, lambda bi, t: (0, 0))
    return pl.pallas_call(
        _rwkv_kernel,
        out_shape=jax.ShapeDtypeStruct((b, s, dim), BF16),
        grid=(b, s // c),
        in_specs=[
            pl.BlockSpec((None, c, cols), lambda bi, t: (bi, t, 0)),
            vec(cols), vec(dim), mat(w2p), vec(dim), mat(a2p), mat(g2),
            vec(dim), vec(dim), vec(dim), vec(dim), vec(dim),
        ],
        out_specs=pl.BlockSpec((None, c, dim), lambda bi, t: (bi, t, 0)),
        scratch_shapes=[pltpu.VMEM((dim // LANES, LANES, LANES), F32), pltpu.VMEM((1, cols), F32)],
        compiler_params=_params("parallel", "arbitrary"),
        name="rwkv7",
    )(hb, row(mu), row(w0), w2p, row(a0), a2p, g2, row(k_k), row(k_a), row(r_k), row(ln_w), row(ln_b))


FOX_AUG = 3
FOX_BLOCK = 512


def _fox_in_kernel(x_ref, g_ref, w_ref, wvt_ref, bf_ref, sel_ref, qa_ref, ka_ref, va_ref, carry_ref):
    tc = x_ref.shape[0]
    nheads = qa_ref.shape[-1] // LANES
    dim = nheads * HEAD_DIM
    scale = HEAD_DIM ** -0.5 * LOG2E

    @pl.when(pl.program_id(1) == 0)
    def _():
        carry_ref[...] = jnp.zeros_like(carry_ref)

    hn = _rms(x_ref[...], g_ref[...]).astype(BF16)
    z = _dot(hn, w_ref[:, 2 * dim:]) + bf_ref[...]
    proj_q = _dot(hn, w_ref[:, :dim])
    logf = jnp.minimum(z, 0.0) - jnp.log(1.0 + jnp.exp(-jnp.abs(z)))
    lane = lax.broadcasted_iota(jnp.int32, z.shape, 1)
    logf = jnp.where(lane < nheads, logf, 0.0)
    ti = lax.broadcasted_iota(jnp.int32, (tc, tc), 0)
    si = lax.broadcasted_iota(jnp.int32, (tc, tc), 1)
    cg = _sel_dot((si <= ti).astype(BF16), logf) + carry_ref[...]
    carry_ref[...] = cg[tc - 1:tc, :]

    hi, mid, low = _split3(cg * LOG2E)
    pieces = (hi.astype(F32) + pltpu.roll(mid.astype(F32), nheads, 1)
              + pltpu.roll(low.astype(F32), 2 * nheads, 1)
              + (lane == FOX_AUG * nheads).astype(F32)).astype(BF16)
    q_c = _dot(pieces, sel_ref[0])
    k_c = _dot(pieces, sel_ref[1])

    lane = lax.broadcasted_iota(jnp.int32, (tc, LANES), 1)
    lo = lane < HEAD_DIM

    def assemble(proj, c_aug, o_ref):
        for j in range(nheads // 2):
            x = proj[:, j * LANES:(j + 1) * LANES]
            xr = pltpu.roll(x, HEAD_DIM, 1)
            for e, xe in ((0, x), (1, xr)):
                hs = slice((2 * j + e) * LANES, (2 * j + e + 1) * LANES)
                o_ref[:, hs] = jnp.where(lo, xe, c_aug[:, hs]).astype(BF16)

    proj_k = _dot(hn, w_ref[:, dim:2 * dim])
    assemble(proj_q * scale, q_c, qa_ref)
    v_t = _dot(wvt_ref[...], hn, NT)
    assemble(proj_k, k_c, ka_ref)
    one_rows = (lax.broadcasted_iota(jnp.int32, (LANES - HEAD_DIM, tc), 0) == 0).astype(F32)
    for h in range(nheads):
        va_ref[h, 0] = jnp.concatenate([v_t[h * HEAD_DIM:(h + 1) * HEAD_DIM], one_rows], axis=0).astype(BF16)


def _fox_select_matrices(nheads):
    sel = np.zeros((2, LANES, nheads * LANES), np.float32)
    one_row = FOX_AUG * nheads
    for h in range(nheads):
        for i in range(FOX_AUG):
            sel[0, i * nheads + h, h * LANES + HEAD_DIM + i] = 1.0
            sel[0, one_row, h * LANES + HEAD_DIM + FOX_AUG + i] = 1.0
            sel[1, one_row, h * LANES + HEAD_DIM + i] = 1.0
            sel[1, i * nheads + h, h * LANES + HEAD_DIM + FOX_AUG + i] = -1.0
    return jnp.asarray(sel, BF16)


def _fox_in(x, g, w_in, b_f, *, batch, tc):
    t, d = x.shape
    s = t // batch
    nheads = b_f.shape[-1]
    assert FOX_AUG * nheads < LANES
    dim = nheads * HEAD_DIM
    w = jnp.concatenate([w_in[:, :2 * dim], jnp.pad(w_in[:, 3 * dim:], ((0, 0), (0, LANES - nheads)))],
                        axis=1).astype(BF16)
    w_vt = w_in[:, 2 * dim:3 * dim].T.astype(BF16)
    bf = jnp.zeros((1, LANES), F32).at[0, :nheads].set(b_f)
    sel = _fox_select_matrices(nheads)
    wide = nheads * LANES
    nt = s // tc
    resident = pl.Buffered(1)
    return pl.pallas_call(
        _fox_in_kernel,
        out_shape=(jax.ShapeDtypeStruct((batch, s, wide), BF16),
                   jax.ShapeDtypeStruct((batch, s, wide), BF16),
                   jax.ShapeDtypeStruct((batch, nheads, nt, LANES, tc), BF16)),
        grid=(batch, nt),
        in_specs=[
            pl.BlockSpec((tc, d), lambda bi, ti: (bi * nt + ti, 0)),
            pl.BlockSpec((1, d), lambda bi, ti: (0, 0)),
            pl.BlockSpec(w.shape, lambda bi, ti: (0, 0), pipeline_mode=resident),
            pl.BlockSpec(w_vt.shape, lambda bi, ti: (0, 0), pipeline_mode=resident),
            pl.BlockSpec((1, LANES), lambda bi, ti: (0, 0)),
            pl.BlockSpec(sel.shape, lambda bi, ti: (0, 0, 0), pipeline_mode=resident),
        ],
        out_specs=(pl.BlockSpec((None, tc, wide), lambda bi, ti: (bi, ti, 0)),
                   pl.BlockSpec((None, tc, wide), lambda bi, ti: (bi, ti, 0)),
                   pl.BlockSpec((None, nheads, 1, LANES, tc), lambda bi, ti: (bi, 0, ti, 0, 0))),
        scratch_shapes=[pltpu.VMEM((1, LANES), F32)],
        compiler_params=_params("parallel", "arbitrary"),
        name="fox_in",
    )(x, g.reshape(1, d), w, w_vt, bf, sel)


def _fox_attn_kernel(q_ref, k_ref, v_ref, o_ref, m_ref, acc_ref, sa_ref, sb_ref):
    tq = o_ref.shape[0] // 2
    nh = q_ref.shape[1] // LANES
    step = pl.program_id(2)
    last_q = 2 * pl.num_programs(2) - 1
    keys = lax.broadcasted_iota(jnp.int32, (tq, tq), 0)
    queries = lax.broadcasted_iota(jnp.int32, (tq, tq), 1)

    def logits_into(dst_ref, qb, kb, heads=None):
        q0 = pl.multiple_of(qb * tq, tq)
        k0 = pl.multiple_of(kb * tq, tq)
        for e in range(nh) if heads is None else heads:
            hs = slice(e * LANES, (e + 1) * LANES)
            dst_ref[e] = _dot(k_ref[pl.ds(k0, tq), hs], q_ref[pl.ds(q0, tq), hs], NT)

    def consume(src_ref, kb, diagonal, after_head=None):
        for e in range(nh):
            s = src_ref[e]
            if diagonal:
                s = jnp.where(keys <= queries, s, -jnp.inf)
            m_prev = m_ref[e]
            m_new = jnp.maximum(m_prev, jnp.max(s, axis=0, keepdims=True))
            p = jnp.exp2(s - m_new).astype(BF16)
            acc_ref[e] = jnp.exp2(m_prev - m_new) * acc_ref[e] + _dot(v_ref[e, kb], p)
            m_ref[e] = m_new
            if after_head is not None:
                after_head(e)

    @pl.when(step == 0)
    def _():
        logits_into(sa_ref, 0, 0)

    for half in range(2):
        qi = 2 * step + half
        m_ref[...] = jnp.full_like(m_ref, -jnp.inf)
        acc_ref[...] = jnp.zeros_like(acc_ref)

        def two_blocks(t, carry, qi=qi):
            kb = 2 * t
            logits_into(sb_ref, qi, kb + 1)
            consume(sa_ref, kb, False, after_head=lambda e: logits_into(sa_ref, qi, kb + 2, heads=(e,)))
            consume(sb_ref, kb + 1, False)
            return carry

        lax.fori_loop(0, step, two_blocks, 0)
        if half == 0:
            consume(sa_ref, qi, True, after_head=lambda e, qi=qi: logits_into(sa_ref, qi + 1, 0, heads=(e,)))
        else:
            nxt = jnp.minimum(qi + 1, last_q)
            logits_into(sb_ref, qi, qi)
            consume(sa_ref, qi - 1, False, after_head=lambda e, nxt=nxt: logits_into(sa_ref, nxt, 0, heads=(e,)))
            consume(sb_ref, qi, True)

        for pair in range(nh // 2):
            outs = []
            for e in (2 * pair, 2 * pair + 1):
                acc = acc_ref[e]
                outs.append(acc[:HEAD_DIM] / acc[HEAD_DIM:HEAD_DIM + 1])
            o_ref[half * tq:(half + 1) * tq, pair * LANES:(pair + 1) * LANES] = (
                jnp.concatenate(outs, axis=0).T.astype(o_ref.dtype))


def _fox_attn(q_aug, k_aug, v_aug, *, tq, nh=4):
    b, s, wide = q_aug.shape
    nheads = wide // LANES
    assert (s // tq) % 2 == 0
    return pl.pallas_call(
        _fox_attn_kernel,
        out_shape=jax.ShapeDtypeStruct((b, s, nheads * HEAD_DIM), BF16),
        grid=(b, nheads // nh, s // (2 * tq)),
        in_specs=[
            pl.BlockSpec((None, s, nh * LANES), lambda bi, g, qi: (bi, 0, g)),
            pl.BlockSpec((None, s, nh * LANES), lambda bi, g, qi: (bi, 0, g)),
            pl.BlockSpec((None, nh, s // tq, LANES, tq), lambda bi, g, qi: (bi, g, 0, 0, 0)),
        ],
        out_specs=pl.BlockSpec((None, 2 * tq, nh * HEAD_DIM), lambda bi, g, qi: (bi, qi, g)),
        scratch_shapes=[pltpu.VMEM((nh, 1, tq), F32), pltpu.VMEM((nh, LANES, tq), F32),
                        pltpu.VMEM((nh, tq, tq), F32), pltpu.VMEM((nh, tq, tq), F32)],
        compiler_params=_params("parallel", "parallel", "arbitrary"),
        name="fox_attn",
    )(q_aug, k_aug, v_aug)


def kernel(x, p, ffn1_norm, ffn1_w_gu, ffn1_w_down, mix_norm, ffn2_norm, ffn2_w_gu, ffn2_w_down, ple_norm, ple_w_gate, ple_w_proj, even_w_in, even_w_out, swa_sinks, rwkv_mu, rwkv_w0, rwkv_w2, rwkv_a0, rwkv_a2, rwkv_g2, rwkv_k_k, rwkv_k_a, rwkv_r_k, rwkv_ln_w, rwkv_ln_b, fox_w_in, fox_b_f, fox_w_out, final_norm):
    b, s, d = x.shape
    depth = p.shape[0]
    t = b * s
    bf = lambda w: w.astype(BF16)
    swa_q = SWA_HEADS * HEAD_DIM
    swa_cols = swa_q + 2 * (SWA_HEADS // SWA_GROUP) * HEAD_DIM
    rwkv_dim = rwkv_w0.shape[-1]
    lora = rwkv_w2.shape[1]
    fox_heads = fox_b_f.shape[-1]
    fox_dim = fox_heads * HEAD_DIM

    x = x.reshape(t, d)
    for i in range(depth):
        j = i // 2
        if i % 2 == 0:
            widths = (swa_cols, even_w_in.shape[-1] - swa_cols)
            x, qkv, hb = _ffn(x, ffn1_norm[i], ffn1_w_gu, ffn1_w_down, i, proj=(mix_norm[i], even_w_in, j, widths))
            ya = _swa(qkv.reshape(b, s, swa_cols), swa_sinks[j])
            zeros = jnp.zeros((lora, rwkv_dim), F32)
            w2p = jnp.concatenate([rwkv_w2[j], zeros], axis=0)
            a2p = jnp.concatenate([zeros, rwkv_a2[j]], axis=0)
            yb = _rwkv(hb.reshape(b, s, -1), rwkv_mu[j], rwkv_w0[j], w2p, rwkv_a0[j], a2p,
                       bf(rwkv_g2[j]), rwkv_k_k[j], rwkv_k_a[j], rwkv_r_k[j].reshape(-1),
                       rwkv_ln_w[j], rwkv_ln_b[j])
            mixed = ([ya.reshape(t, swa_q), yb.reshape(t, rwkv_dim)], even_w_out, j)
        else:
            (x,) = _ffn(x, ffn1_norm[i], ffn1_w_gu, ffn1_w_down, i)
            q_aug, k_aug, v_aug = _fox_in(x, mix_norm[i], fox_w_in[j], fox_b_f[j], batch=b, tc=FOX_BLOCK)
            yc = _fox_attn(q_aug, k_aug, v_aug, tq=FOX_BLOCK)
            mixed = ([yc.reshape(t, fox_dim)], fox_w_out, j)
        x = _post_mix(*mixed, x, ffn2_norm[i], ffn2_w_gu, ffn2_w_down, ple_norm[i], ple_w_gate,
                      p.reshape(depth, t, -1), ple_w_proj, final_norm, i, final=(i == depth - 1))
    return x.reshape(b, s, d)
```

```python
import functools

import jax
import jax.numpy as jnp
import numpy as np
from jax import lax
from jax.experimental import pallas as pl
from jax.experimental.pallas import tpu as pltpu

F32 = jnp.float32
BF16 = jnp.bfloat16

LANES = 128
HEAD_DIM = 64
SWA_HEADS = 8
SWA_GROUP = 4
SWA_BLOCK = 128
RWKV_CHUNK = 64
NORM_EPS = 1e-6
GN_EPS = 64e-5
L2_EPS = 1e-12
LOG2E = float(np.log2(np.e))
VMEM_LIMIT = 56 * 1024 * 1024

NN = (((1,), (0,)), ((), ()))
NT = (((1,), (1,)), ((), ()))
TN = (((0,), (0,)), ((), ()))


def _dot(a, b, dims=NN):
    return lax.dot_general(a, b, dims, preferred_element_type=F32)


def _dot_bf(a, b, dims=NN):
    return _dot(a.astype(BF16), b.astype(BF16), dims)


def _hi_lo(x):
    hi = x.astype(BF16)
    lo = (x - hi.astype(F32)).astype(BF16)
    return hi, lo


def _dot_x3(a, b, dims=NN):
    ah, al = _hi_lo(a)
    bh, bl = _hi_lo(b)
    return _dot(ah, bh, dims) + (_dot(ah, bl, dims) + _dot(al, bh, dims))


def _split3(x):
    hi = x.astype(BF16)
    r1 = x - hi.astype(F32)
    mid = r1.astype(BF16)
    lo = (r1 - mid.astype(F32)).astype(BF16)
    return hi, mid, lo


def _sel_dot(sel, x):
    hi, mid, lo = _split3(x)
    return _dot(sel, hi) + (_dot(sel, mid) + _dot(sel, lo))


def _rms(x, g):
    ms = jnp.mean(x * x, axis=-1, keepdims=True)
    return x * lax.rsqrt(ms + NORM_EPS) * g


def _params(*sem):
    return pltpu.CompilerParams(dimension_semantics=sem, vmem_limit_bytes=VMEM_LIMIT)


LOAD_STEPS = 16


def _weight_chunk(w, layer, rows, block=0):
    chunk = rows // LOAD_STEPS
    assert chunk * LOAD_STEPS == rows and chunk % 16 == 0, (rows, chunk)
    return pl.BlockSpec((None, chunk, w.shape[2]),
                        lambda i: (layer, block * LOAD_STEPS + jnp.minimum(i, LOAD_STEPS - 1), 0))


def _stash(step, chunk_ref, copy_ref):
    rows = chunk_ref.shape[0]
    copy_ref[pl.ds(pl.multiple_of(step * rows, rows), rows), :] = chunk_ref[...].astype(BF16)


def _row_tile(tm, n):
    return pl.BlockSpec((tm, n), lambda i: (jnp.maximum(i - LOAD_STEPS, 0), 0))


def _swiglu_half_step(x, g, wgu_ref, wd_ref):
    dff = wd_ref.shape[0]
    hn = _rms(x, g).astype(BF16)
    gate = _dot(hn, wgu_ref[:, :dff])
    up = _dot(hn, wgu_ref[:, dff:])
    act = (gate * jax.nn.sigmoid(gate) * up).astype(BF16)
    return x + 0.5 * _dot(act, wd_ref[...])


def _ffn_kernel(n_proj, x_ref, g_ref, wgu_ref, wd_ref, *refs):
    n_w = 3 if n_proj else 2
    copies = refs[len(refs) - n_w:]
    refs = refs[:len(refs) - n_w]
    step = pl.program_id(0)

    @pl.when(step < LOAD_STEPS)
    def _():
        _stash(step, wgu_ref, copies[0])
        _stash(step, wd_ref, copies[1])
        if n_proj:
            _stash(step, refs[1], copies[2])

    @pl.when(step >= LOAD_STEPS)
    def _():
        x = _swiglu_half_step(x_ref[...], g_ref[...], copies[0], copies[1])
        if n_proj:
            gm_ref, o_ref, proj_refs = refs[0], refs[2], refs[3:]
            proj = _dot(_rms(x, gm_ref[...]).astype(BF16), copies[2][...])
            col = 0
            for p_ref in proj_refs:
                p_ref[...] = proj[:, col:col + p_ref.shape[1]]
                col += p_ref.shape[1]
        else:
            o_ref = refs[0]
        o_ref[...] = x


def _ffn(x, g, w_gu, w_down, layer, proj=None, *, tm=512):
    t, d = x.shape
    dff = w_down.shape[1]
    vec = pl.BlockSpec((1, d), lambda i: (0, 0))
    in_specs = [_row_tile(tm, d), vec, _weight_chunk(w_gu, layer, d), _weight_chunk(w_down, layer, dff)]
    args = [x, g.reshape(1, d), w_gu, w_down]
    scratch = [pltpu.VMEM((d, 2 * dff), BF16), pltpu.VMEM((dff, d), BF16)]
    out_shape = [jax.ShapeDtypeStruct((t, d), F32)]
    out_specs = [_row_tile(tm, d)]
    widths = ()
    if proj is not None:
        gm, w, w_layer, widths = proj
        assert sum(widths) == w.shape[2]
        in_specs += [vec, _weight_chunk(w, w_layer, d)]
        args += [gm.reshape(1, d), w]
        scratch += [pltpu.VMEM(w.shape[1:], BF16)]
        out_shape += [jax.ShapeDtypeStruct((t, n), F32) for n in widths]
        out_specs += [_row_tile(tm, n) for n in widths]
    return pl.pallas_call(
        functools.partial(_ffn_kernel, len(widths)),
        out_shape=out_shape,
        grid=(LOAD_STEPS + t // tm,),
        in_specs=in_specs,
        out_specs=out_specs,
        scratch_shapes=scratch,
        compiler_params=_params("arbitrary"),
        name="ffn",
    )(*args)


def _post_mix_kernel(n_in, final, *refs):
    a_refs, wo_refs = refs[:n_in], refs[n_in:2 * n_in]
    (x_ref, g2_ref, wgu_ref, wd_ref, gp_ref, wpg_ref, p_ref, wpp_ref, fn_ref, o_ref) = refs[2 * n_in:2 * n_in + 10]
    copies = refs[2 * n_in + 10:]
    wo_copies, (wgu_c, wd_c, wpg_c, wpp_c) = copies[:n_in], copies[n_in:]
    step = pl.program_id(0)

    @pl.when(step < LOAD_STEPS)
    def _():
        for src, dst in zip(wo_refs + (wgu_ref, wd_ref, wpg_ref, wpp_ref), wo_copies + (wgu_c, wd_c, wpg_c, wpp_c)):
            _stash(step, src, dst)

    @pl.when(step >= LOAD_STEPS)
    def _():
        x = x_ref[...]
        for a_ref, w_c in zip(a_refs, wo_copies):
            x = x + _dot(a_ref[...], w_c[...])
        x = _swiglu_half_step(x, g2_ref[...], wgu_c, wd_c)
        gate = jax.nn.sigmoid(_dot(_rms(x, gp_ref[...]).astype(BF16), wpg_c[...]))
        x = x + gate * _dot(p_ref[...].astype(BF16), wpp_c[...])
        if final:
            x = _rms(x, fn_ref[...])
        o_ref[...] = x


def _post_mix(a_list, w_out, out_layer, x, g2, w_gu, w_down, gp, w_gate, p, w_proj, final_g, layer, *, final,
              tm=512):
    t, d = x.shape
    dff = w_down.shape[1]
    vec = pl.BlockSpec((1, d), lambda i: (0, 0))
    width = a_list[0].shape[1]
    assert all(a.shape[1] == width for a in a_list)
    in_specs = [_row_tile(tm, width) for _ in a_list]
    in_specs += [_weight_chunk(w_out, out_layer, width, block=k) for k in range(len(a_list))]
    in_specs += [_row_tile(tm, d), vec, _weight_chunk(w_gu, layer, d), _weight_chunk(w_down, layer, dff), vec,
                 _weight_chunk(w_gate, layer, d),
                 pl.BlockSpec((None, tm, p.shape[2]), lambda i: (layer, jnp.maximum(i - LOAD_STEPS, 0), 0)),
                 _weight_chunk(w_proj, layer, w_proj.shape[1]), vec]
    scratch = [pltpu.VMEM((width, d), BF16) for _ in a_list]
    scratch += [pltpu.VMEM((d, 2 * dff), BF16), pltpu.VMEM((dff, d), BF16), pltpu.VMEM((d, d), BF16),
                pltpu.VMEM(w_proj.shape[1:], BF16)]
    return pl.pallas_call(
        functools.partial(_post_mix_kernel, len(a_list), final),
        out_shape=jax.ShapeDtypeStruct((t, d), F32),
        grid=(LOAD_STEPS + t // tm,),
        in_specs=in_specs,
        out_specs=_row_tile(tm, d),
        scratch_shapes=scratch,
        compiler_params=_params("arbitrary"),
        name="post_mix",
    )(*a_list, *([w_out] * len(a_list)), x, g2.reshape(1, d), w_gu, w_down, gp.reshape(1, d), w_gate, p,
      w_proj, final_g.reshape(1, d))


def _swa_kernel(sink_ref, q_ref, kp_ref, kc_ref, vp_ref, vc_ref, o_ref):
    n = pl.program_id(1)
    blk = SWA_BLOCK
    nsub = q_ref.shape[0] // blk
    scale = HEAD_DIM ** -0.5
    k = jnp.concatenate([kp_ref[...], kc_ref[...]], axis=0)
    v = jnp.concatenate([vp_ref[...], vc_ref[...]], axis=0)
    kr = pltpu.roll(k, HEAD_DIM, 1)
    vr = pltpu.roll(v, HEAD_DIM, 1)
    lo_kv = lax.broadcasted_iota(jnp.int32, k.shape, 1) < HEAD_DIM
    kdup = [jnp.where(lo_kv, k, kr).astype(BF16), jnp.where(lo_kv, kr, k).astype(BF16)]
    vdup = [jnp.where(lo_kv, v, vr).astype(BF16), jnp.where(lo_kv, vr, v).astype(BF16)]

    qi = lax.broadcasted_iota(jnp.int32, (blk, 2 * blk), 0)
    ki = lax.broadcasted_iota(jnp.int32, (blk, 2 * blk), 1)
    dist = qi + blk - ki
    in_window = (dist >= 0) & (dist < blk)
    distf = dist.astype(F32)
    lo_q = lax.broadcasted_iota(jnp.int32, (blk, LANES), 1) < HEAD_DIM

    units = [(u, h) for u in range(nsub) for h in range(SWA_HEADS)]
    keys_of = lambda x, u: x[u * blk:(u + 2) * blk]
    qm = []
    for u, h in units:
        j, e = divmod(h, 2)
        q2 = q_ref[u * blk:(u + 1) * blk, j * LANES:(j + 1) * LANES] * (scale * LOG2E)
        qm.append(jnp.where(lo_q if e == 0 else ~lo_q, q2, 0.0).astype(BF16))
    logits = [_dot(qi_, keys_of(kdup[h // SWA_GROUP], u), NT) for qi_, (u, h) in zip(qm, units)]
    ps, inv_denoms = [], []
    for s, (u, h) in zip(logits, units):
        valid = in_window & ((n > 0) | (ki >= blk)) if u == 0 else in_window
        slope = 2.0 ** (-8.0 * (h + 1) / SWA_HEADS) * LOG2E
        s = jnp.where(valid, s - slope * distf, -jnp.inf)
        sink = sink_ref[h] * LOG2E
        m = jnp.maximum(jnp.max(s, axis=-1, keepdims=True), sink)
        p = jnp.exp2(s - m)
        inv_denoms.append(1.0 / (jnp.sum(p, axis=-1, keepdims=True) + jnp.exp2(sink - m)))
        ps.append(p.astype(BF16))
    outs = [_dot(p, keys_of(vdup[h // SWA_GROUP], u)) * inv
            for p, inv, (u, h) in zip(ps, inv_denoms, units)]
    for u in range(nsub):
        for j in range(SWA_HEADS // 2):
            pair = jnp.where(lo_q, outs[u * SWA_HEADS + 2 * j], outs[u * SWA_HEADS + 2 * j + 1])
            o_ref[u * blk:(u + 1) * blk, j * LANES:(j + 1) * LANES] = pair.astype(o_ref.dtype)


def _swa(qkv, sinks, *, nsub=2):
    b, s, _ = qkv.shape
    blk = SWA_BLOCK
    tq = nsub * blk
    nq = SWA_HEADS * HEAD_DIM
    kcol = nq // LANES
    vcol = kcol + 1
    prev = lambda n: jnp.maximum(nsub * n - 1, 0)
    return pl.pallas_call(
        _swa_kernel,
        out_shape=jax.ShapeDtypeStruct((b, s, nq), BF16),
        grid=(b, s // tq),
        in_specs=[
            pl.BlockSpec(memory_space=pltpu.SMEM),
            pl.BlockSpec((None, tq, nq), lambda bi, n: (bi, n, 0)),
            pl.BlockSpec((None, blk, LANES), lambda bi, n: (bi, prev(n), kcol)),
            pl.BlockSpec((None, tq, LANES), lambda bi, n: (bi, n, kcol)),
            pl.BlockSpec((None, blk, LANES), lambda bi, n: (bi, prev(n), vcol)),
            pl.BlockSpec((None, tq, LANES), lambda bi, n: (bi, n, vcol)),
        ],
        out_specs=pl.BlockSpec((None, tq, nq), lambda bi, n: (bi, n, 0)),
        compiler_params=_params("parallel", "arbitrary"),
        name="swa",
    )(sinks, qkv, qkv, qkv, qkv, qkv)


def _tri_inverse_minus_eye(lows, ri, ci):
    same = lambda w: (ri ^ ci) < w
    base = 8
    x = [jnp.where(same(base), -low, 0.0) for low in lows]
    p2 = [_dot_bf(xi, xi) for xi in x]
    e = [xi + pi + _dot_bf(xi, pi) for xi, pi in zip(x, p2)]
    p4 = [_dot_bf(pi, pi) for pi in p2]
    e = [ei + pi + _dot_bf(ei, pi) for ei, pi in zip(e, p4)]
    w = base * 2
    while w <= RWKV_CHUNK:
        off = [jnp.where(same(w) & ~same(w // 2), low, 0.0) for low in lows]
        wm = [oi + _dot_bf(ei, oi) for ei, oi in zip(e, off)]
        e = [ei - wi - _dot_bf(wi, ei) for ei, wi in zip(e, wm)]
        w *= 2
    return e


def _rwkv_kernel(h_ref, mu_ref, w0_ref, w2_ref, a0_ref, a2_ref, g2_ref, kk_ref, ka_ref,
                 rk_ref, lnw_ref, lnb_ref, o_ref, state_ref, last_ref):
    c = RWKV_CHUNK
    tb = h_ref.shape[0]
    nchunk = tb // c
    dim = o_ref.shape[-1]
    npair = dim // LANES

    @pl.when(pl.program_id(1) == 0)
    def _():
        state_ref[...] = jnp.zeros_like(state_ref)
        last_ref[...] = jnp.zeros_like(last_ref)

    h = h_ref[...]
    row = lax.broadcasted_iota(jnp.int32, h.shape, 0)
    shifted = jnp.where(row == 0, last_ref[...], pltpu.roll(h, 1, 0))
    last_ref[...] = h[tb - 1:tb, :]
    hs = h + (shifted - h) * mu_ref[...]
    r = hs[:, 0:dim]
    k = hs[:, dim:2 * dim]
    v = hs[:, 2 * dim:3 * dim]
    xwa = hs[:, 3 * dim:3 * dim + LANES]
    xg = hs[:, 3 * dim + LANES:3 * dim + 2 * LANES]

    wl = w0_ref[...] + _dot_x3(jnp.tanh(xwa), w2_ref[...])
    logw = -jax.nn.sigmoid(wl) * float(np.exp(-0.5))
    a = jax.nn.sigmoid(a0_ref[...] + _dot_x3(xwa, a2_ref[...]))
    gate = _dot_bf(jax.nn.sigmoid(xg), g2_ref[...])

    ri = lax.broadcasted_iota(jnp.int32, (LANES, LANES), 0)
    ci = lax.broadcasted_iota(jnp.int32, (LANES, LANES), 1)
    ones_bd = ((ri ^ ci) < HEAD_DIM).astype(BF16)
    ones_bd2 = jnp.concatenate([ones_bd, ones_bd], axis=0)

    def head_sum(x):
        cols = []
        for j in range(npair):
            hi, lo = _hi_lo(x[:, j * LANES:(j + 1) * LANES])
            cols.append(_dot(jnp.concatenate([hi, lo], axis=1), ones_bd2))
        return jnp.concatenate(cols, axis=1)

    kk = k * kk_ref[...]
    kk = kk * jnp.minimum(lax.rsqrt(head_sum(kk * kk)), 1.0 / L2_EPS)
    k2 = k * (1.0 + (a - 1.0) * ka_ref[...])
    bvec = kk * a

    ti = lax.broadcasted_iota(jnp.int32, (tb, tb), 0)
    si = lax.broadcasted_iota(jnp.int32, (tb, tb), 1)
    tri = (((ti ^ si) < c) & (si <= ti)).astype(BF16)
    cum = _dot(jnp.concatenate([tri] * 3, axis=1), jnp.concatenate(_split3(logw), axis=0))
    e_neg = jnp.exp(-cum)
    alpha = kk * jnp.exp(cum - logw)
    beta = bvec * e_neg
    kappa = k2 * e_neg
    rho = r * jnp.exp(cum)
    cum_end = jnp.concatenate(
        [jnp.broadcast_to(cum[(n + 1) * c - 1:(n + 1) * c, :], (c, dim)) for n in range(nchunk)], axis=0)
    to_end = jnp.exp(cum_end - cum)
    beta_e = bvec * to_end
    kappa_e = k2 * to_end
    w_end = jnp.exp(cum_end)

    same_head = (ri ^ ci) < c
    strict = same_head & (ci < ri)
    incl = same_head & (ci <= ri)
    eye = ri == ci
    lo = lax.broadcasted_iota(jnp.int32, (c, LANES), 1) < HEAD_DIM

    def stack(x2):
        return jnp.concatenate([jnp.where(lo, x2, 0.0), jnp.where(lo, 0.0, x2)], axis=0)

    units = [(n, j) for n in range(nchunk) for j in range(npair)]
    blk = lambda x, n, j: x[n * c:(n + 1) * c, j * LANES:(j + 1) * LANES]
    a_s = [stack(blk(alpha, n, j)) for n, j in units]
    rho_s = [stack(blk(rho, n, j)) for n, j in units]
    v_s = [stack(blk(v, n, j)) for n, j in units]
    ends = [jnp.concatenate([stack(blk(beta_e, n, j)), stack(blk(kappa_e, n, j))], axis=0) for n, j in units]
    bk = [jnp.concatenate([blk(beta, n, j)] * 2 + [blk(kappa, n, j)] * 2, axis=0) for n, j in units]
    sc = [_dot_bf(jnp.concatenate([ai, ri_], axis=0), bi, NT) for ai, ri_, bi in zip(a_s, rho_s, bk)]
    l_ab = [jnp.where(strict, s[:LANES, :LANES], 0.0) for s in sc]
    l_ak = [jnp.where(strict, s[:LANES, LANES:], 0.0) for s in sc]
    r_b = [jnp.where(incl, s[LANES:, :LANES], 0.0) for s in sc]
    r_k = [jnp.where(incl, s[LANES:, LANES:], 0.0) for s in sc]
    e_inv = _tri_inverse_minus_eye(l_ab, ri, ci)
    lkv = [_dot_bf(li, vi) for li, vi in zip(l_ak, v_s)]
    p_m = [-(ai + _dot_bf(ei, ai)) for ei, ai in zip(e_inv, a_s)]
    q_m = [-(xi + _dot_bf(ei, xi)) for ei, xi in zip(e_inv, lkv)]
    m_m = [jnp.where(eye, blk(w_end, n, j)[:1, :], 0.0) + _dot_bf(pi, ei[:LANES], TN)
           for (n, j), pi, ei in zip(units, p_m, ends)]
    n_m = [_dot_bf(jnp.concatenate([qi, vi], axis=0), ei, TN) for qi, vi, ei in zip(q_m, v_s, ends)]
    g_m = [ri_ + _dot_bf(rb, pi) for ri_, rb, pi in zip(rho_s, r_b, p_m)]
    h_m = [_dot_bf(jnp.concatenate([rb, rk], axis=1), jnp.concatenate([qi, vi], axis=0))
           for rb, rk, qi, vi in zip(r_b, r_k, q_m, v_s)]

    state = [state_ref[j] for j in range(npair)]
    ys = []
    for n in range(nchunk):
        idx = [n * npair + j for j in range(npair)]
        nxt = [_dot_bf(state[j], m_m[i]) + n_m[i] for j, i in enumerate(idx)]
        y = [_dot_bf(g_m[i], state[j], NT) + h_m[i] for j, i in enumerate(idx)]
        ys.append(jnp.concatenate([yi[:c] + yi[c:] for yi in y], axis=1))
        state = nxt
    for j in range(npair):
        state_ref[j] = state[j]
    y = jnp.concatenate(ys, axis=0)

    mean = head_sum(y) * (1.0 / HEAD_DIM)
    d = y - mean
    var = head_sum(d * d) * (1.0 / HEAD_DIM)
    y = d * lax.rsqrt(var + GN_EPS) * lnw_ref[...] + lnb_ref[...]
    y = y + head_sum(r * k2 * rk_ref[...]) * v
    o_ref[...] = (y * gate).astype(o_ref.dtype)


def _rwkv(hb, mu, w0, w2p, a0, a2p, g2, k_k, k_a, r_k, ln_w, ln_b, *, tb=4 * RWKV_CHUNK):
    b, s, cols = hb.shape
    dim = w0.shape[-1]
    c = tb
    row = lambda x: x.reshape(1, -1)
    vec = lambda n: pl.BlockSpec((1, n), lambda bi, t: (0, 0))
    mat = lambda m: pl.BlockSpec(m.shape, lambda bi, t: (0, 0))
    return pl.pallas_call(
        _rwkv_kernel,
        out_shape=jax.ShapeDtypeStruct((b, s, dim), BF16),
        grid=(b, s // c),
        in_specs=[
            pl.BlockSpec((None, c, cols), lambda bi, t: (bi, t, 0)),
            vec(cols), vec(dim), mat(w2p), vec(dim), mat(a2p), mat(g2),
            vec(dim), vec(dim), vec(dim), vec(dim), vec(dim),
        ],
        out_specs=pl.BlockSpec((None, c, dim), lambda bi, t: (bi, t, 0)),
        scratch_shapes=[pltpu.VMEM((dim // LANES, LANES, LANES), F32), pltpu.VMEM((1, cols), F32)],
        compiler_params=_params("parallel", "arbitrary"),
        name="rwkv7",
    )(hb, row(mu), row(w0), w2p, row(a0), a2p, g2, row(k_k), row(k_a), row(r_k), row(ln_w), row(ln_b))


FOX_AUG = 3
FOX_BLOCK = 512


def _fox_in_kernel(x_ref, g_ref, w_ref, wvt_ref, bf_ref, sel_ref, qa_ref, ka_ref, va_ref, carry_ref):
    tc = x_ref.shape[0]
    nheads = qa_ref.shape[-1] // LANES
    dim = nheads * HEAD_DIM
    scale = HEAD_DIM ** -0.5 * LOG2E

    @pl.when(pl.program_id(1) == 0)
    def _():
        carry_ref[...] = jnp.zeros_like(carry_ref)

    hn = _rms(x_ref[...], g_ref[...]).astype(BF16)
    z = _dot(hn, w_ref[:, 2 * dim:]) + bf_ref[...]
    proj_q = _dot(hn, w_ref[:, :dim])
    logf = jnp.minimum(z, 0.0) - jnp.log(1.0 + jnp.exp(-jnp.abs(z)))
    lane = lax.broadcasted_iota(jnp.int32, z.shape, 1)
    logf = jnp.where(lane < nheads, logf, 0.0)
    ti = lax.broadcasted_iota(jnp.int32, (tc, tc), 0)
    si = lax.broadcasted_iota(jnp.int32, (tc, tc), 1)
    cg = _sel_dot((si <= ti).astype(BF16), logf) + carry_ref[...]
    carry_ref[...] = cg[tc - 1:tc, :]

    hi, mid, low = _split3(cg * LOG2E)
    pieces = (hi.astype(F32) + pltpu.roll(mid.astype(F32), nheads, 1)
              + pltpu.roll(low.astype(F32), 2 * nheads, 1)
              + (lane == FOX_AUG * nheads).astype(F32)).astype(BF16)
    q_c = _dot(pieces, sel_ref[0])
    k_c = _dot(pieces, sel_ref[1])

    lane = lax.broadcasted_iota(jnp.int32, (tc, LANES), 1)
    lo = lane < HEAD_DIM

    def assemble(proj, c_aug, o_ref):
        for j in range(nheads // 2):
            x = proj[:, j * LANES:(j + 1) * LANES]
            xr = pltpu.roll(x, HEAD_DIM, 1)
            for e, xe in ((0, x), (1, xr)):
                hs = slice((2 * j + e) * LANES, (2 * j + e + 1) * LANES)
                o_ref[:, hs] = jnp.where(lo, xe, c_aug[:, hs]).astype(BF16)

    proj_k = _dot(hn, w_ref[:, dim:2 * dim])
    assemble(proj_q * scale, q_c, qa_ref)
    v_t = _dot(wvt_ref[...], hn, NT)
    assemble(proj_k, k_c, ka_ref)
    one_rows = (lax.broadcasted_iota(jnp.int32, (LANES - HEAD_DIM, tc), 0) == 0).astype(F32)
    for h in range(nheads):
        va_ref[h, 0] = jnp.concatenate([v_t[h * HEAD_DIM:(h + 1) * HEAD_DIM], one_rows], axis=0).astype(BF16)


def _fox_select_matrices(nheads):
    sel = np.zeros((2, LANES, nheads * LANES), np.float32)
    one_row = FOX_AUG * nheads
    for h in range(nheads):
        for i in range(FOX_AUG):
            sel[0, i * nheads + h, h * LANES + HEAD_DIM + i] = 1.0
            sel[0, one_row, h * LANES + HEAD_DIM + FOX_AUG + i] = 1.0
            sel[1, one_row, h * LANES + HEAD_DIM + i] = 1.0
            sel[1, i * nheads + h, h * LANES + HEAD_DIM + FOX_AUG + i] = -1.0
    return jnp.asarray(sel, BF16)


def _fox_in(x, g, w_in, b_f, *, batch, tc):
    t, d = x.shape
    s = t // batch
    nheads = b_f.shape[-1]
    assert FOX_AUG * nheads < LANES
    dim = nheads * HEAD_DIM
    w = jnp.concatenate([w_in[:, :2 * dim], jnp.pad(w_in[:, 3 * dim:], ((0, 0), (0, LANES - nheads)))],
                        axis=1).astype(BF16)
    w_vt = w_in[:, 2 * dim:3 * dim].T.astype(BF16)
    bf = jnp.zeros((1, LANES), F32).at[0, :nheads].set(b_f)
    sel = _fox_select_matrices(nheads)
    wide = nheads * LANES
    nt = s // tc
    resident = pl.Buffered(1)
    return pl.pallas_call(
        _fox_in_kernel,
        out_shape=(jax.ShapeDtypeStruct((batch, s, wide), BF16),
                   jax.ShapeDtypeStruct((batch, s, wide), BF16),
                   jax.ShapeDtypeStruct((batch, nheads, nt, LANES, tc), BF16)),
        grid=(batch, nt),
        in_specs=[
            pl.BlockSpec((tc, d), lambda bi, ti: (bi * nt + ti, 0)),
            pl.BlockSpec((1, d), lambda bi, ti: (0, 0)),
            pl.BlockSpec(w.shape, lambda bi, ti: (0, 0), pipeline_mode=resident),
            pl.BlockSpec(w_vt.shape, lambda bi, ti: (0, 0), pipeline_mode=resident),
            pl.BlockSpec((1, LANES), lambda bi, ti: (0, 0)),
            pl.BlockSpec(sel.shape, lambda bi, ti: (0, 0, 0), pipeline_mode=resident),
        ],
        out_specs=(pl.BlockSpec((None, tc, wide), lambda bi, ti: (bi, ti, 0)),
                   pl.BlockSpec((None, tc, wide), lambda bi, ti: (bi, ti, 0)),
                   pl.BlockSpec((None, nheads, 1, LANES, tc), lambda bi, ti: (bi, 0, ti, 0, 0))),
        scratch_shapes=[pltpu.VMEM((1, LANES), F32)],
        compiler_params=_params("parallel", "arbitrary"),
        name="fox_in",
    )(x, g.reshape(1, d), w, w_vt, bf, sel)


def _fox_attn_kernel(q_ref, k_ref, v_ref, o_ref, m_ref, acc_ref, sa_ref, sb_ref):
    tq = o_ref.shape[0] // 2
    nh = q_ref.shape[1] // LANES
    step = pl.program_id(2)
    last_q = 2 * pl.num_programs(2) - 1
    keys = lax.broadcasted_iota(jnp.int32, (tq, tq), 0)
    queries = lax.broadcasted_iota(jnp.int32, (tq, tq), 1)

    def logits_into(dst_ref, qb, kb, heads=None):
        q0 = pl.multiple_of(qb * tq, tq)
        k0 = pl.multiple_of(kb * tq, tq)
        for e in range(nh) if heads is None else heads:
            hs = slice(e * LANES, (e + 1) * LANES)
            dst_ref[e] = _dot(k_ref[pl.ds(k0, tq), hs], q_ref[pl.ds(q0, tq), hs], NT)

    def consume(src_ref, kb, diagonal, after_head=None, before_head=None):
        for e in range(nh):
            if before_head is not None:
                before_head(e)
            s = src_ref[e]
            if diagonal:
                s = jnp.where(keys <= queries, s, -jnp.inf)
            m_prev = m_ref[e]
            m_new = jnp.maximum(m_prev, jnp.max(s, axis=0, keepdims=True))
            p = jnp.exp2(s - m_new).astype(BF16)
            acc_ref[e] = jnp.exp2(m_prev - m_new) * acc_ref[e] + _dot(v_ref[e, kb], p)
            m_ref[e] = m_new
            if after_head is not None:
                after_head(e)

    @pl.when(step == 0)
    def _():
        logits_into(sa_ref, 0, 0)

    for half in range(2):
        qi = 2 * step + half
        m_ref[...] = jnp.full_like(m_ref, -jnp.inf)
        acc_ref[...] = jnp.zeros_like(acc_ref)

        def two_blocks(t, carry, qi=qi):
            kb = 2 * t
            consume(sa_ref, kb, False, before_head=lambda e: logits_into(sb_ref, qi, kb + 1, heads=(e,)),
                    after_head=lambda e: logits_into(sa_ref, qi, kb + 2, heads=(e,)))
            consume(sb_ref, kb + 1, False)
            return carry

        lax.fori_loop(0, step, two_blocks, 0)
        if half == 0:
            consume(sa_ref, qi, True, after_head=lambda e, qi=qi: logits_into(sa_ref, qi + 1, 0, heads=(e,)))
        else:
            nxt = jnp.minimum(qi + 1, last_q)
            logits_into(sb_ref, qi, qi)
            consume(sa_ref, qi - 1, False, after_head=lambda e, nxt=nxt: logits_into(sa_ref, nxt, 0, heads=(e,)))
            consume(sb_ref, qi, True)

        for pair in range(nh // 2):
            outs = []
            for e in (2 * pair, 2 * pair + 1):
                acc = acc_ref[e]
                outs.append(acc[:HEAD_DIM] / acc[HEAD_DIM:HEAD_DIM + 1])
            o_ref[half * tq:(half + 1) * tq, pair * LANES:(pair + 1) * LANES] = (
                jnp.concatenate(outs, axis=0).T.astype(o_ref.dtype))


def _fox_attn(q_aug, k_aug, v_aug, *, tq, nh=4):
    b, s, wide = q_aug.shape
    nheads = wide // LANES
    assert (s // tq) % 2 == 0
    return pl.pallas_call(
        _fox_attn_kernel,
        out_shape=jax.ShapeDtypeStruct((b, s, nheads * HEAD_DIM), BF16),
        grid=(b, nheads // nh, s // (2 * tq)),
        in_specs=[
            pl.BlockSpec((None, s, nh * LANES), lambda bi, g, qi: (bi, 0, g)),
            pl.BlockSpec((None, s, nh * LANES), lambda bi, g, qi: (bi, 0, g)),
            pl.BlockSpec((None, nh, s // tq, LANES, tq), lambda bi, g, qi: (bi, g, 0, 0, 0)),
        ],
        out_specs=pl.BlockSpec((None, 2 * tq, nh * HEAD_DIM), lambda bi, g, qi: (bi, qi, g)),
        scratch_shapes=[pltpu.VMEM((nh, 1, tq), F32), pltpu.VMEM((nh, LANES, tq), F32),
                        pltpu.VMEM((nh, tq, tq), F32), pltpu.VMEM((nh, tq, tq), F32)],
        compiler_params=_params("parallel", "parallel", "arbitrary"),
        name="fox_attn",
    )(q_aug, k_aug, v_aug)


def kernel(x, p, ffn1_norm, ffn1_w_gu, ffn1_w_down, mix_norm, ffn2_norm, ffn2_w_gu, ffn2_w_down, ple_norm, ple_w_gate, ple_w_proj, even_w_in, even_w_out, swa_sinks, rwkv_mu, rwkv_w0, rwkv_w2, rwkv_a0, rwkv_a2, rwkv_g2, rwkv_k_k, rwkv_k_a, rwkv_r_k, rwkv_ln_w, rwkv_ln_b, fox_w_in, fox_b_f, fox_w_out, final_norm):
    b, s, d = x.shape
    depth = p.shape[0]
    t = b * s
    bf = lambda w: w.astype(BF16)
    swa_q = SWA_HEADS * HEAD_DIM
    swa_cols = swa_q + 2 * (SWA_HEADS // SWA_GROUP) * HEAD_DIM
    rwkv_dim = rwkv_w0.shape[-1]
    lora = rwkv_w2.shape[1]
    fox_heads = fox_b_f.shape[-1]
    fox_dim = fox_heads * HEAD_DIM

    x = x.reshape(t, d)
    for i in range(depth):
        j = i // 2
        if i % 2 == 0:
            widths = (swa_cols, even_w_in.shape[-1] - swa_cols)
            x, qkv, hb = _ffn(x, ffn1_norm[i], ffn1_w_gu, ffn1_w_down, i, proj=(mix_norm[i], even_w_in, j, widths))
            ya = _swa(qkv.reshape(b, s, swa_cols), swa_sinks[j])
            zeros = jnp.zeros((lora, rwkv_dim), F32)
            w2p = jnp.concatenate([rwkv_w2[j], zeros], axis=0)
            a2p = jnp.concatenate([zeros, rwkv_a2[j]], axis=0)
            yb = _rwkv(hb.reshape(b, s, -1), rwkv_mu[j], rwkv_w0[j], w2p, rwkv_a0[j], a2p,
                       bf(rwkv_g2[j]), rwkv_k_k[j], rwkv_k_a[j], rwkv_r_k[j].reshape(-1),
                       rwkv_ln_w[j], rwkv_ln_b[j])
            mixed = ([ya.reshape(t, swa_q), yb.reshape(t, rwkv_dim)], even_w_out, j)
        else:
            (x,) = _ffn(x, ffn1_norm[i], ffn1_w_gu, ffn1_w_down, i)
            q_aug, k_aug, v_aug = _fox_in(x, mix_norm[i], fox_w_in[j], fox_b_f[j], batch=b, tc=FOX_BLOCK)
            yc = _fox_attn(q_aug, k_aug, v_aug, tq=FOX_BLOCK)
            mixed = ([yc.reshape(t, fox_dim)], fox_w_out, j)
        x = _post_mix(*mixed, x, ffn2_norm[i], ffn2_w_gu, ffn2_w_down, ple_norm[i], ple_w_gate,
                      p.reshape(depth, t, -1), ple_w_proj, final_norm, i, final=(i == depth - 1))
    return x.reshape(b, s, d)
```

```python
import functools

import jax
import jax.numpy as jnp
import numpy as np
from jax import lax
from jax.experimental import pallas as pl
from jax.experimental.pallas import tpu as pltpu

F32 = jnp.float32
BF16 = jnp.bfloat16

LANES = 128
HEAD_DIM = 64
SWA_HEADS = 8
SWA_GROUP = 4
SWA_BLOCK = 128
RWKV_CHUNK = 64
NORM_EPS = 1e-6
GN_EPS = 64e-5
L2_EPS = 1e-12
LOG2E = float(np.log2(np.e))
VMEM_LIMIT = 56 * 1024 * 1024

NN = (((1,), (0,)), ((), ()))
NT = (((1,), (1,)), ((), ()))
TN = (((0,), (0,)), ((), ()))


def _dot(a, b, dims=NN):
    return lax.dot_general(a, b, dims, preferred_element_type=F32)


def _dot_bf(a, b, dims=NN):
    return _dot(a.astype(BF16), b.astype(BF16), dims)


def _hi_lo(x):
    hi = x.astype(BF16)
    lo = (x - hi.astype(F32)).astype(BF16)
    return hi, lo


def _dot_x3(a, b, dims=NN):
    ah, al = _hi_lo(a)
    bh, bl = _hi_lo(b)
    return _dot(ah, bh, dims) + (_dot(ah, bl, dims) + _dot(al, bh, dims))


def _split3(x):
    hi = x.astype(BF16)
    r1 = x - hi.astype(F32)
    mid = r1.astype(BF16)
    lo = (r1 - mid.astype(F32)).astype(BF16)
    return hi, mid, lo


def _sel_dot(sel, x):
    hi, mid, lo = _split3(x)
    return _dot(sel, hi) + (_dot(sel, mid) + _dot(sel, lo))


def _rms(x, g):
    ms = jnp.mean(x * x, axis=-1, keepdims=True)
    return x * lax.rsqrt(ms + NORM_EPS) * g


def _params(*sem):
    return pltpu.CompilerParams(dimension_semantics=sem, vmem_limit_bytes=VMEM_LIMIT)


LOAD_STEPS = 16


def _weight_chunk(w, layer, rows, block=0):
    chunk = rows // LOAD_STEPS
    assert chunk * LOAD_STEPS == rows and chunk % 16 == 0, (rows, chunk)
    return pl.BlockSpec((None, chunk, w.shape[2]),
                        lambda i: (layer, block * LOAD_STEPS + jnp.minimum(i, LOAD_STEPS - 1), 0))


def _stash(step, chunk_ref, copy_ref):
    rows = chunk_ref.shape[0]
    copy_ref[pl.ds(pl.multiple_of(step * rows, rows), rows), :] = chunk_ref[...].astype(BF16)


def _row_tile(tm, n):
    return pl.BlockSpec((tm, n), lambda i: (jnp.maximum(i - LOAD_STEPS, 0), 0))


def _swiglu_half_step(x, g, wgu_ref, wd_ref):
    dff = wd_ref.shape[0]
    hn = _rms(x, g).astype(BF16)
    gate = _dot(hn, wgu_ref[:, :dff])
    up = _dot(hn, wgu_ref[:, dff:])
    act = (gate * jax.nn.sigmoid(gate) * up).astype(BF16)
    return x + 0.5 * _dot(act, wd_ref[...])


def _ffn_kernel(n_proj, x_ref, g_ref, wgu_ref, wd_ref, *refs):
    n_w = 3 if n_proj else 2
    copies = refs[len(refs) - n_w:]
    refs = refs[:len(refs) - n_w]
    step = pl.program_id(0)

    @pl.when(step < LOAD_STEPS)
    def _():
        _stash(step, wgu_ref, copies[0])
        _stash(step, wd_ref, copies[1])
        if n_proj:
            _stash(step, refs[1], copies[2])

    @pl.when(step >= LOAD_STEPS)
    def _():
        x = _swiglu_half_step(x_ref[...], g_ref[...], copies[0], copies[1])
        if n_proj:
            gm_ref, o_ref, proj_refs = refs[0], refs[2], refs[3:]
            proj = _dot(_rms(x, gm_ref[...]).astype(BF16), copies[2][...])
            col = 0
            for p_ref in proj_refs:
                p_ref[...] = proj[:, col:col + p_ref.shape[1]]
                col += p_ref.shape[1]
        else:
            o_ref = refs[0]
        o_ref[...] = x


def _ffn(x, g, w_gu, w_down, layer, proj=None, *, tm=512):
    t, d = x.shape
    dff = w_down.shape[1]
    vec = pl.BlockSpec((1, d), lambda i: (0, 0))
    in_specs = [_row_tile(tm, d), vec, _weight_chunk(w_gu, layer, d), _weight_chunk(w_down, layer, dff)]
    args = [x, g.reshape(1, d), w_gu, w_down]
    scratch = [pltpu.VMEM((d, 2 * dff), BF16), pltpu.VMEM((dff, d), BF16)]
    out_shape = [jax.ShapeDtypeStruct((t, d), F32)]
    out_specs = [_row_tile(tm, d)]
    widths = ()
    if proj is not None:
        gm, w, w_layer, widths = proj
        assert sum(widths) == w.shape[2]
        in_specs += [vec, _weight_chunk(w, w_layer, d)]
        args += [gm.reshape(1, d), w]
        scratch += [pltpu.VMEM(w.shape[1:], BF16)]
        out_shape += [jax.ShapeDtypeStruct((t, n), F32) for n in widths]
        out_specs += [_row_tile(tm, n) for n in widths]
    return pl.pallas_call(
        functools.partial(_ffn_kernel, len(widths)),
        out_shape=out_shape,
        grid=(LOAD_STEPS + t // tm,),
        in_specs=in_specs,
        out_specs=out_specs,
        scratch_shapes=scratch,
        compiler_params=_params("arbitrary"),
        name="ffn",
    )(*args)


def _post_mix_kernel(n_in, final, *refs):
    a_refs, wo_refs = refs[:n_in], refs[n_in:2 * n_in]
    (x_ref, g2_ref, wgu_ref, wd_ref, gp_ref, wpg_ref, p_ref, wpp_ref, fn_ref, o_ref) = refs[2 * n_in:2 * n_in + 10]
    copies = refs[2 * n_in + 10:]
    wo_copies, (wgu_c, wd_c, wpg_c, wpp_c) = copies[:n_in], copies[n_in:]
    step = pl.program_id(0)

    @pl.when(step < LOAD_STEPS)
    def _():
        for src, dst in zip(wo_refs + (wgu_ref, wd_ref, wpg_ref, wpp_ref), wo_copies + (wgu_c, wd_c, wpg_c, wpp_c)):
            _stash(step, src, dst)

    @pl.when(step >= LOAD_STEPS)
    def _():
        x = x_ref[...]
        for a_ref, w_c in zip(a_refs, wo_copies):
            x = x + _dot(a_ref[...], w_c[...])
        x = _swiglu_half_step(x, g2_ref[...], wgu_c, wd_c)
        gate = jax.nn.sigmoid(_dot(_rms(x, gp_ref[...]).astype(BF16), wpg_c[...]))
        x = x + gate * _dot(p_ref[...].astype(BF16), wpp_c[...])
        if final:
            x = _rms(x, fn_ref[...])
        o_ref[...] = x


def _post_mix(a_list, w_out, out_layer, x, g2, w_gu, w_down, gp, w_gate, p, w_proj, final_g, layer, *, final,
              tm=512):
    t, d = x.shape
    dff = w_down.shape[1]
    vec = pl.BlockSpec((1, d), lambda i: (0, 0))
    width = a_list[0].shape[1]
    assert all(a.shape[1] == width for a in a_list)
    in_specs = [_row_tile(tm, width) for _ in a_list]
    in_specs += [_weight_chunk(w_out, out_layer, width, block=k) for k in range(len(a_list))]
    in_specs += [_row_tile(tm, d), vec, _weight_chunk(w_gu, layer, d), _weight_chunk(w_down, layer, dff), vec,
                 _weight_chunk(w_gate, layer, d),
                 pl.BlockSpec((None, tm, p.shape[2]), lambda i: (layer, jnp.maximum(i - LOAD_STEPS, 0), 0)),
                 _weight_chunk(w_proj, layer, w_proj.shape[1]), vec]
    scratch = [pltpu.VMEM((width, d), BF16) for _ in a_list]
    scratch += [pltpu.VMEM((d, 2 * dff), BF16), pltpu.VMEM((dff, d), BF16), pltpu.VMEM((d, d), BF16),
                pltpu.VMEM(w_proj.shape[1:], BF16)]
    return pl.pallas_call(
        functools.partial(_post_mix_kernel, len(a_list), final),
        out_shape=jax.ShapeDtypeStruct((t, d), F32),
        grid=(LOAD_STEPS + t // tm,),
        in_specs=in_specs,
        out_specs=_row_tile(tm, d),
        scratch_shapes=scratch,
        compiler_params=_params("arbitrary"),
        name="post_mix",
    )(*a_list, *([w_out] * len(a_list)), x, g2.reshape(1, d), w_gu, w_down, gp.reshape(1, d), w_gate, p,
      w_proj, final_g.reshape(1, d))


def _swa_kernel(sink_ref, q_ref, kp_ref, kc_ref, vp_ref, vc_ref, o_ref):
    n = pl.program_id(1)
    blk = SWA_BLOCK
    nsub = q_ref.shape[0] // blk
    scale = HEAD_DIM ** -0.5
    k = jnp.concatenate([kp_ref[...], kc_ref[...]], axis=0)
    v = jnp.concatenate([vp_ref[...], vc_ref[...]], axis=0)
    kr = pltpu.roll(k, HEAD_DIM, 1)
    vr = pltpu.roll(v, HEAD_DIM, 1)
    lo_kv = lax.broadcasted_iota(jnp.int32, k.shape, 1) < HEAD_DIM
    kdup = [jnp.where(lo_kv, k, kr).astype(BF16), jnp.where(lo_kv, kr, k).astype(BF16)]
    vdup = [jnp.where(lo_kv, v, vr).astype(BF16), jnp.where(lo_kv, vr, v).astype(BF16)]

    qi = lax.broadcasted_iota(jnp.int32, (blk, 2 * blk), 0)
    ki = lax.broadcasted_iota(jnp.int32, (blk, 2 * blk), 1)
    dist = qi + blk - ki
    in_window = (dist >= 0) & (dist < blk)
    distf = dist.astype(F32)
    lo_q = lax.broadcasted_iota(jnp.int32, (blk, LANES), 1) < HEAD_DIM

    units = [(u, h) for u in range(nsub) for h in range(SWA_HEADS)]
    keys_of = lambda x, u: x[u * blk:(u + 2) * blk]
    qm = []
    for u, h in units:
        j, e = divmod(h, 2)
        q2 = q_ref[u * blk:(u + 1) * blk, j * LANES:(j + 1) * LANES] * (scale * LOG2E)
        qm.append(jnp.where(lo_q if e == 0 else ~lo_q, q2, 0.0).astype(BF16))
    logits = [_dot(qi_, keys_of(kdup[h // SWA_GROUP], u), NT) for qi_, (u, h) in zip(qm, units)]
    ps, inv_denoms = [], []
    for s, (u, h) in zip(logits, units):
        valid = in_window & ((n > 0) | (ki >= blk)) if u == 0 else in_window
        slope = 2.0 ** (-8.0 * (h + 1) / SWA_HEADS) * LOG2E
        s = jnp.where(valid, s - slope * distf, -jnp.inf)
        sink = sink_ref[h] * LOG2E
        m = jnp.maximum(jnp.max(s, axis=-1, keepdims=True), sink)
        p = jnp.exp2(s - m)
        inv_denoms.append(1.0 / (jnp.sum(p, axis=-1, keepdims=True) + jnp.exp2(sink - m)))
        ps.append(p.astype(BF16))
    outs = [_dot(p, keys_of(vdup[h // SWA_GROUP], u)) * inv
            for p, inv, (u, h) in zip(ps, inv_denoms, units)]
    for u in range(nsub):
        for j in range(SWA_HEADS // 2):
            pair = jnp.where(lo_q, outs[u * SWA_HEADS + 2 * j], outs[u * SWA_HEADS + 2 * j + 1])
            o_ref[u * blk:(u + 1) * blk, j * LANES:(j + 1) * LANES] = pair.astype(o_ref.dtype)


def _swa(qkv, sinks, *, nsub=2):
    b, s, _ = qkv.shape
    blk = SWA_BLOCK
    tq = nsub * blk
    nq = SWA_HEADS * HEAD_DIM
    kcol = nq // LANES
    vcol = kcol + 1
    prev = lambda n: jnp.maximum(nsub * n - 1, 0)
    return pl.pallas_call(
        _swa_kernel,
        out_shape=jax.ShapeDtypeStruct((b, s, nq), BF16),
        grid=(b, s // tq),
        in_specs=[
            pl.BlockSpec(memory_space=pltpu.SMEM),
            pl.BlockSpec((None, tq, nq), lambda bi, n: (bi, n, 0)),
            pl.BlockSpec((None, blk, LANES), lambda bi, n: (bi, prev(n), kcol)),
            pl.BlockSpec((None, tq, LANES), lambda bi, n: (bi, n, kcol)),
            pl.BlockSpec((None, blk, LANES), lambda bi, n: (bi, prev(n), vcol)),
            pl.BlockSpec((None, tq, LANES), lambda bi, n: (bi, n, vcol)),
        ],
        out_specs=pl.BlockSpec((None, tq, nq), lambda bi, n: (bi, n, 0)),
        compiler_params=_params("parallel", "arbitrary"),
        name="swa",
    )(sinks, qkv, qkv, qkv, qkv, qkv)


def _tri_inverse_minus_eye(lows, ri, ci):
    same = lambda w: (ri ^ ci) < w
    base = 8
    x = [jnp.where(same(base), -low, 0.0) for low in lows]
    p2 = [_dot_bf(xi, xi) for xi in x]
    e = [xi + pi + _dot_bf(xi, pi) for xi, pi in zip(x, p2)]
    p4 = [_dot_bf(pi, pi) for pi in p2]
    e = [ei + pi + _dot_bf(ei, pi) for ei, pi in zip(e, p4)]
    w = base * 2
    while w <= RWKV_CHUNK:
        off = [jnp.where(same(w) & ~same(w // 2), low, 0.0) for low in lows]
        wm = [oi + _dot_bf(ei, oi) for ei, oi in zip(e, off)]
        e = [ei - wi - _dot_bf(wi, ei) for ei, wi in zip(e, wm)]
        w *= 2
    return e


def _rwkv_kernel(h_ref, mu_ref, w0_ref, w2_ref, a0_ref, a2_ref, g2_ref, kk_ref, ka_ref,
                 rk_ref, lnw_ref, lnb_ref, o_ref, state_ref, last_ref):
    c = RWKV_CHUNK
    tb = h_ref.shape[0]
    nchunk = tb // c
    dim = o_ref.shape[-1]
    npair = dim // LANES

    @pl.when(pl.program_id(1) == 0)
    def _():
        state_ref[...] = jnp.zeros_like(state_ref)
        last_ref[...] = jnp.zeros_like(last_ref)

    h = h_ref[...]
    row = lax.broadcasted_iota(jnp.int32, h.shape, 0)
    shifted = jnp.where(row == 0, last_ref[...], pltpu.roll(h, 1, 0))
    last_ref[...] = h[tb - 1:tb, :]
    hs = h + (shifted - h) * mu_ref[...]
    r = hs[:, 0:dim]
    k = hs[:, dim:2 * dim]
    v = hs[:, 2 * dim:3 * dim]
    xwa = hs[:, 3 * dim:3 * dim + LANES]
    xg = hs[:, 3 * dim + LANES:3 * dim + 2 * LANES]

    wl = w0_ref[...] + _dot_x3(jnp.tanh(xwa), w2_ref[...])
    logw = -jax.nn.sigmoid(wl) * float(np.exp(-0.5))
    a = jax.nn.sigmoid(a0_ref[...] + _dot_x3(xwa, a2_ref[...]))
    gate = _dot_bf(jax.nn.sigmoid(xg), g2_ref[...])

    ri = lax.broadcasted_iota(jnp.int32, (LANES, LANES), 0)
    ci = lax.broadcasted_iota(jnp.int32, (LANES, LANES), 1)
    ones_bd = ((ri ^ ci) < HEAD_DIM).astype(BF16)
    ones_bd2 = jnp.concatenate([ones_bd, ones_bd], axis=0)

    def head_sum(x):
        cols = []
        for j in range(npair):
            hi, lo = _hi_lo(x[:, j * LANES:(j + 1) * LANES])
            cols.append(_dot(jnp.concatenate([hi, lo], axis=1), ones_bd2))
        return jnp.concatenate(cols, axis=1)

    kk = k * kk_ref[...]
    kk = kk * jnp.minimum(lax.rsqrt(head_sum(kk * kk)), 1.0 / L2_EPS)
    k2 = k * (1.0 + (a - 1.0) * ka_ref[...])
    bvec = kk * a

    ti = lax.broadcasted_iota(jnp.int32, (tb, tb), 0)
    si = lax.broadcasted_iota(jnp.int32, (tb, tb), 1)
    tri = (((ti ^ si) < c) & (si <= ti)).astype(BF16)
    cum = _dot(jnp.concatenate([tri] * 3, axis=1), jnp.concatenate(_split3(logw), axis=0))
    e_neg = jnp.exp(-cum)
    alpha = kk * jnp.exp(cum - logw)
    beta = bvec * e_neg
    kappa = k2 * e_neg
    rho = r * jnp.exp(cum)
    cum_end = jnp.concatenate(
        [jnp.broadcast_to(cum[(n + 1) * c - 1:(n + 1) * c, :], (c, dim)) for n in range(nchunk)], axis=0)
    to_end = jnp.exp(cum_end - cum)
    beta_e = bvec * to_end
    kappa_e = k2 * to_end
    w_end = jnp.exp(cum_end)

    same_head = (ri ^ ci) < c
    strict = same_head & (ci < ri)
    incl = same_head & (ci <= ri)
    eye = ri == ci
    lo = lax.broadcasted_iota(jnp.int32, (c, LANES), 1) < HEAD_DIM

    def stack(x2):
        return jnp.concatenate([jnp.where(lo, x2, 0.0), jnp.where(lo, 0.0, x2)], axis=0)

    units = [(n, j) for n in range(nchunk) for j in range(npair)]
    blk = lambda x, n, j: x[n * c:(n + 1) * c, j * LANES:(j + 1) * LANES]
    a_s = [stack(blk(alpha, n, j)) for n, j in units]
    rho_s = [stack(blk(rho, n, j)) for n, j in units]
    v_s = [stack(blk(v, n, j)) for n, j in units]
    ends = [jnp.concatenate([stack(blk(beta_e, n, j)), stack(blk(kappa_e, n, j))], axis=0) for n, j in units]
    bk = [jnp.concatenate([blk(beta, n, j)] * 2 + [blk(kappa, n, j)] * 2, axis=0) for n, j in units]
    sc = [_dot_bf(jnp.concatenate([ai, ri_], axis=0), bi, NT) for ai, ri_, bi in zip(a_s, rho_s, bk)]
    l_ab = [jnp.where(strict, s[:LANES, :LANES], 0.0) for s in sc]
    l_ak = [jnp.where(strict, s[:LANES, LANES:], 0.0) for s in sc]
    r_b = [jnp.where(incl, s[LANES:, :LANES], 0.0) for s in sc]
    r_k = [jnp.where(incl, s[LANES:, LANES:], 0.0) for s in sc]
    e_inv = _tri_inverse_minus_eye(l_ab, ri, ci)
    lkv = [_dot_bf(li, vi) for li, vi in zip(l_ak, v_s)]
    p_m = [-(ai + _dot_bf(ei, ai)) for ei, ai in zip(e_inv, a_s)]
    q_m = [-(xi + _dot_bf(ei, xi)) for ei, xi in zip(e_inv, lkv)]
    m_m = [jnp.where(eye, blk(w_end, n, j)[:1, :], 0.0) + _dot_bf(pi, ei[:LANES], TN)
           for (n, j), pi, ei in zip(units, p_m, ends)]
    n_m = [_dot_bf(jnp.concatenate([qi, vi], axis=0), ei, TN) for qi, vi, ei in zip(q_m, v_s, ends)]
    g_m = [ri_ + _dot_bf(rb, pi) for ri_, rb, pi in zip(rho_s, r_b, p_m)]
    h_m = [_dot_bf(jnp.concatenate([rb, rk], axis=1), jnp.concatenate([qi, vi], axis=0))
           for rb, rk, qi, vi in zip(r_b, r_k, q_m, v_s)]

    state = [state_ref[j] for j in range(npair)]
    ys = []
    for n in range(nchunk):
        idx = [n * npair + j for j in range(npair)]
        nxt = [_dot_bf(state[j], m_m[i]) + n_m[i] for j, i in enumerate(idx)]
        y = [_dot_bf(g_m[i], state[j], NT) + h_m[i] for j, i in enumerate(idx)]
        ys.append(jnp.concatenate([yi[:c] + yi[c:] for yi in y], axis=1))
        state = nxt
    for j in range(npair):
        state_ref[j] = state[j]
    y = jnp.concatenate(ys, axis=0)

    mean = head_sum(y) * (1.0 / HEAD_DIM)
    d = y - mean
    var = head_sum(d * d) * (1.0 / HEAD_DIM)
    y = d * lax.rsqrt(var + GN_EPS) * lnw_ref[...] + lnb_ref[...]
    y = y + head_sum(r * k2 * rk_ref[...]) * v
    o_ref[...] = (y * gate).astype(o_ref.dtype)


def _rwkv(hb, mu, w0, w2p, a0, a2p, g2, k_k, k_a, r_k, ln_w, ln_b, *, tb=4 * RWKV_CHUNK):
    b, s, cols = hb.shape
    dim = w0.shape[-1]
    c = tb
    row = lambda x: x.reshape(1, -1)
    vec = lambda n: pl.BlockSpec((1, n), lambda bi, t: (0, 0))
    mat = lambda m: pl.BlockSpec(m.shape, lambda bi, t: (0, 0))
    return pl.pallas_call(
        _rwkv_kernel,
        out_shape=jax.ShapeDtypeStruct((b, s, dim), BF16),
        grid=(b, s // c),
        in_specs=[
            pl.BlockSpec((None, c, cols), lambda bi, t: (bi, t, 0)),
            vec(cols), vec(dim), mat(w2p), vec(dim), mat(a2p), mat(g2),
            vec(dim), vec(dim), vec(dim), vec(dim), vec(dim),
        ],
        out_specs=pl.BlockSpec((None, c, dim), lambda bi, t: (bi, t, 0)),
        scratch_shapes=[pltpu.VMEM((dim // LANES, LANES, LANES), F32), pltpu.VMEM((1, cols), F32)],
        compiler_params=_params("parallel", "arbitrary"),
        name="rwkv7",
    )(hb, row(mu), row(w0), w2p, row(a0), a2p, g2, row(k_k), row(k_a), row(r_k), row(ln_w), row(ln_b))


FOX_AUG = 3
FOX_BLOCK = 512


def _fox_in_kernel(x_ref, g_ref, w_ref, wvt_ref, bf_ref, sel_ref, qa_ref, ka_ref, va_ref, carry_ref):
    tc = x_ref.shape[0]
    nheads = qa_ref.shape[-1] // LANES
    dim = nheads * HEAD_DIM
    scale = HEAD_DIM ** -0.5 * LOG2E

    @pl.when(pl.program_id(1) == 0)
    def _():
        carry_ref[...] = jnp.zeros_like(carry_ref)

    hn = _rms(x_ref[...], g_ref[...]).astype(BF16)
    z = _dot(hn, w_ref[:, 2 * dim:]) + bf_ref[...]
    proj_q = _dot(hn, w_ref[:, :dim])
    logf = jnp.minimum(z, 0.0) - jnp.log(1.0 + jnp.exp(-jnp.abs(z)))
    lane = lax.broadcasted_iota(jnp.int32, z.shape, 1)
    logf = jnp.where(lane < nheads, logf, 0.0)
    ti = lax.broadcasted_iota(jnp.int32, (tc, tc), 0)
    si = lax.broadcasted_iota(jnp.int32, (tc, tc), 1)
    cg = _sel_dot((si <= ti).astype(BF16), logf) + carry_ref[...]
    carry_ref[...] = cg[tc - 1:tc, :]

    hi, mid, low = _split3(cg * LOG2E)
    pieces = (hi.astype(F32) + pltpu.roll(mid.astype(F32), nheads, 1)
              + pltpu.roll(low.astype(F32), 2 * nheads, 1)
              + (lane == FOX_AUG * nheads).astype(F32)).astype(BF16)
    q_c = _dot(pieces, sel_ref[0])
    k_c = _dot(pieces, sel_ref[1])

    lane = lax.broadcasted_iota(jnp.int32, (tc, LANES), 1)
    lo = lane < HEAD_DIM

    def assemble(proj, c_aug, o_ref):
        for j in range(nheads // 2):
            x = proj[:, j * LANES:(j + 1) * LANES]
            xr = pltpu.roll(x, HEAD_DIM, 1)
            for e, xe in ((0, x), (1, xr)):
                hs = slice((2 * j + e) * LANES, (2 * j + e + 1) * LANES)
                o_ref[:, hs] = jnp.where(lo, xe, c_aug[:, hs]).astype(BF16)

    proj_k = _dot(hn, w_ref[:, dim:2 * dim])
    assemble(proj_q * scale, q_c, qa_ref)
    v_t = _dot(wvt_ref[...], hn, NT)
    assemble(proj_k, k_c, ka_ref)
    one_rows = (lax.broadcasted_iota(jnp.int32, (LANES - HEAD_DIM, tc), 0) == 0).astype(F32)
    for h in range(nheads):
        va_ref[h, 0] = jnp.concatenate([v_t[h * HEAD_DIM:(h + 1) * HEAD_DIM], one_rows], axis=0).astype(BF16)


def _fox_select_matrices(nheads):
    sel = np.zeros((2, LANES, nheads * LANES), np.float32)
    one_row = FOX_AUG * nheads
    for h in range(nheads):
        for i in range(FOX_AUG):
            sel[0, i * nheads + h, h * LANES + HEAD_DIM + i] = 1.0
            sel[0, one_row, h * LANES + HEAD_DIM + FOX_AUG + i] = 1.0
            sel[1, one_row, h * LANES + HEAD_DIM + i] = 1.0
            sel[1, i * nheads + h, h * LANES + HEAD_DIM + FOX_AUG + i] = -1.0
    return jnp.asarray(sel, BF16)


def _fox_in(x, g, w_in, b_f, *, batch, tc):
    t, d = x.shape
    s = t // batch
    nheads = b_f.shape[-1]
    assert FOX_AUG * nheads < LANES
    dim = nheads * HEAD_DIM
    w = jnp.concatenate([w_in[:, :2 * dim], jnp.pad(w_in[:, 3 * dim:], ((0, 0), (0, LANES - nheads)))],
                        axis=1).astype(BF16)
    w_vt = w_in[:, 2 * dim:3 * dim].T.astype(BF16)
    bf = jnp.zeros((1, LANES), F32).at[0, :nheads].set(b_f)
    sel = _fox_select_matrices(nheads)
    wide = nheads * LANES
    nt = s // tc
    resident = pl.Buffered(1)
    return pl.pallas_call(
        _fox_in_kernel,
        out_shape=(jax.ShapeDtypeStruct((batch, s, wide), BF16),
                   jax.ShapeDtypeStruct((batch, s, wide), BF16),
                   jax.ShapeDtypeStruct((batch, nheads, nt, LANES, tc), BF16)),
        grid=(batch, nt),
        in_specs=[
            pl.BlockSpec((tc, d), lambda bi, ti: (bi * nt + ti, 0)),
            pl.BlockSpec((1, d), lambda bi, ti: (0, 0)),
            pl.BlockSpec(w.shape, lambda bi, ti: (0, 0), pipeline_mode=resident),
            pl.BlockSpec(w_vt.shape, lambda bi, ti: (0, 0), pipeline_mode=resident),
            pl.BlockSpec((1, LANES), lambda bi, ti: (0, 0)),
            pl.BlockSpec(sel.shape, lambda bi, ti: (0, 0, 0), pipeline_mode=resident),
        ],
        out_specs=(pl.BlockSpec((None, tc, wide), lambda bi, ti: (bi, ti, 0)),
                   pl.BlockSpec((None, tc, wide), lambda bi, ti: (bi, ti, 0)),
                   pl.BlockSpec((None, nheads, 1, LANES, tc), lambda bi, ti: (bi, 0, ti, 0, 0))),
        scratch_shapes=[pltpu.VMEM((1, LANES), F32)],
        compiler_params=_params("parallel", "arbitrary"),
        name="fox_in",
    )(x, g.reshape(1, d), w, w_vt, bf, sel)


def _fox_attn_kernel(q_ref, k_ref, v_ref, o_ref, m_ref, acc_ref, sa_ref, sb_ref):
    tq = o_ref.shape[0] // 2
    nh = q_ref.shape[1] // LANES
    step = pl.program_id(2)
    last_q = 2 * pl.num_programs(2) - 1
    keys = lax.broadcasted_iota(jnp.int32, (tq, tq), 0)
    queries = lax.broadcasted_iota(jnp.int32, (tq, tq), 1)

    def logits_into(dst_ref, qb, kb, heads=None):
        q0 = pl.multiple_of(qb * tq, tq)
        k0 = pl.multiple_of(kb * tq, tq)
        for e in range(nh) if heads is None else heads:
            hs = slice(e * LANES, (e + 1) * LANES)
            dst_ref[e] = _dot(k_ref[pl.ds(k0, tq), hs], q_ref[pl.ds(q0, tq), hs], NT)

    def consume(src_ref, kb, diagonal, after_head=None):
        for e in range(nh):
            s = src_ref[e]
            if diagonal:
                s = jnp.where(keys <= queries, s, -jnp.inf)
            m_prev = m_ref[e]
            m_new = jnp.maximum(m_prev, jnp.max(s, axis=0, keepdims=True))
            p = jnp.exp2(s - m_new).astype(BF16)
            acc_ref[e] = jnp.exp2(m_prev - m_new) * acc_ref[e] + _dot(v_ref[e, kb], p)
            m_ref[e] = m_new
            if after_head is not None:
                after_head(e)

    def refill(dst_ref, qb, kb):
        return lambda e: logits_into(dst_ref, qb, kb, heads=(e,))

    @pl.when(step == 0)
    def _():
        logits_into(sa_ref, 0, 0)
        logits_into(sb_ref, 1, 1)

    for half in range(2):
        qi = 2 * step + half
        nxt = qi + 1 if half == 0 else jnp.minimum(qi + 1, last_q)
        m_ref[...] = jnp.full_like(m_ref, -jnp.inf)
        acc_ref[...] = jnp.zeros_like(acc_ref)

        def two_blocks(t, carry, qi=qi, half=half, nxt=nxt):
            kb = 2 * t
            consume(sa_ref, kb, False, after_head=refill(sa_ref, qi, kb + 2))
            if half == 0:
                last = t == step - 1
                consume(sb_ref, kb + 1, False,
                        after_head=refill(sb_ref, jnp.where(last, nxt, qi), jnp.where(last, 1, kb + 3)))
            else:
                consume(sb_ref, kb + 1, False, after_head=refill(sb_ref, qi, kb + 3))
            return carry

        lax.fori_loop(0, step, two_blocks, 0)
        if half == 0:
            consume(sa_ref, qi, True, after_head=refill(sa_ref, nxt, 0))
        else:
            consume(sa_ref, qi - 1, False, after_head=refill(sa_ref, nxt, 0))
            consume(sb_ref, qi, True, after_head=refill(sb_ref, nxt, 1))

        for pair in range(nh // 2):
            outs = []
            for e in (2 * pair, 2 * pair + 1):
                acc = acc_ref[e]
                outs.append(acc[:HEAD_DIM] / acc[HEAD_DIM:HEAD_DIM + 1])
            o_ref[half * tq:(half + 1) * tq, pair * LANES:(pair + 1) * LANES] = (
                jnp.concatenate(outs, axis=0).T.astype(o_ref.dtype))


def _fox_attn(q_aug, k_aug, v_aug, *, tq, nh=4):
    b, s, wide = q_aug.shape
    nheads = wide // LANES
    assert (s // tq) % 2 == 0
    return pl.pallas_call(
        _fox_attn_kernel,
        out_shape=jax.ShapeDtypeStruct((b, s, nheads * HEAD_DIM), BF16),
        grid=(b, nheads // nh, s // (2 * tq)),
        in_specs=[
            pl.BlockSpec((None, s, nh * LANES), lambda bi, g, qi: (bi, 0, g)),
            pl.BlockSpec((None, s, nh * LANES), lambda bi, g, qi: (bi, 0, g)),
            pl.BlockSpec((None, nh, s // tq, LANES, tq), lambda bi, g, qi: (bi, g, 0, 0, 0)),
        ],
        out_specs=pl.BlockSpec((None, 2 * tq, nh * HEAD_DIM), lambda bi, g, qi: (bi, qi, g)),
        scratch_shapes=[pltpu.VMEM((nh, 1, tq), F32), pltpu.VMEM((nh, LANES, tq), F32),
                        pltpu.VMEM((nh, tq, tq), F32), pltpu.VMEM((nh, tq, tq), F32)],
        compiler_params=_params("parallel", "parallel", "arbitrary"),
        name="fox_attn",
    )(q_aug, k_aug, v_aug)


def kernel(x, p, ffn1_norm, ffn1_w_gu, ffn1_w_down, mix_norm, ffn2_norm, ffn2_w_gu, ffn2_w_down, ple_norm, ple_w_gate, ple_w_proj, even_w_in, even_w_out, swa_sinks, rwkv_mu, rwkv_w0, rwkv_w2, rwkv_a0, rwkv_a2, rwkv_g2, rwkv_k_k, rwkv_k_a, rwkv_r_k, rwkv_ln_w, rwkv_ln_b, fox_w_in, fox_b_f, fox_w_out, final_norm):
    b, s, d = x.shape
    depth = p.shape[0]
    t = b * s
    bf = lambda w: w.astype(BF16)
    swa_q = SWA_HEADS * HEAD_DIM
    swa_cols = swa_q + 2 * (SWA_HEADS // SWA_GROUP) * HEAD_DIM
    rwkv_dim = rwkv_w0.shape[-1]
    lora = rwkv_w2.shape[1]
    fox_heads = fox_b_f.shape[-1]
    fox_dim = fox_heads * HEAD_DIM

    x = x.reshape(t, d)
    for i in range(depth):
        j = i // 2
        if i % 2 == 0:
            widths = (swa_cols, even_w_in.shape[-1] - swa_cols)
            x, qkv, hb = _ffn(x, ffn1_norm[i], ffn1_w_gu, ffn1_w_down, i, proj=(mix_norm[i], even_w_in, j, widths))
            ya = _swa(qkv.reshape(b, s, swa_cols), swa_sinks[j])
            zeros = jnp.zeros((lora, rwkv_dim), F32)
            w2p = jnp.concatenate([rwkv_w2[j], zeros], axis=0)
            a2p = jnp.concatenate([zeros, rwkv_a2[j]], axis=0)
            yb = _rwkv(hb.reshape(b, s, -1), rwkv_mu[j], rwkv_w0[j], w2p, rwkv_a0[j], a2p,
                       bf(rwkv_g2[j]), rwkv_k_k[j], rwkv_k_a[j], rwkv_r_k[j].reshape(-1),
                       rwkv_ln_w[j], rwkv_ln_b[j])
            mixed = ([ya.reshape(t, swa_q), yb.reshape(t, rwkv_dim)], even_w_out, j)
        else:
            (x,) = _ffn(x, ffn1_norm[i], ffn1_w_gu, ffn1_w_down, i)
            q_aug, k_aug, v_aug = _fox_in(x, mix_norm[i], fox_w_in[j], fox_b_f[j], batch=b, tc=FOX_BLOCK)
            yc = _fox_attn(q_aug, k_aug, v_aug, tq=FOX_BLOCK)
            mixed = ([yc.reshape(t, fox_dim)], fox_w_out, j)
        x = _post_mix(*mixed, x, ffn2_norm[i], ffn2_w_gu, ffn2_w_down, ple_norm[i], ple_w_gate,
                      p.reshape(depth, t, -1), ple_w_proj, final_norm, i, final=(i == depth - 1))
    return x.reshape(b, s, d)
```
